```python
import jax, jax.numpy as jnp
from jax import lax
import numpy as np

D_MODEL = 1024
BATCH = 32
SEQ = 256
DEPTH = 4
DEC_BATCH = 2
DEC_SEQ = 2048
PAST_LEN = 512

GRID_W = 64
N_HEADS = 8
N_KV_HEADS = 2
HEAD_DIM = 64
ATTN_WIDTH = N_HEADS * HEAD_DIM
KV_WIDTH = N_KV_HEADS * HEAD_DIM
ROPE_AXIS_DIM = HEAD_DIM // 2
ROPE_BASE = 10000.0
Q_BLOCK = 128
POOL_WINDOWS = (2, 4, 8, 16)
N_POOL_GROUPS = 4
POOL_GROUP_DIM = 64
POOL_WIDTH = N_POOL_GROUPS * POOL_GROUP_DIM
N_FOURIER_GROUPS = 4
FOURIER_GROUP_DIM = 64
FOURIER_WIDTH = N_FOURIER_GROUPS * FOURIER_GROUP_DIM
N_BRANCHES = 3
SPLIT_POINTS = (ATTN_WIDTH,
                ATTN_WIDTH + KV_WIDTH,
                ATTN_WIDTH + 2 * KV_WIDTH,
                ATTN_WIDTH + 2 * KV_WIDTH + POOL_WIDTH,
                ATTN_WIDTH + 2 * KV_WIDTH + POOL_WIDTH + FOURIER_WIDTH)
IN_WIDTH = SPLIT_POINTS[-1] + N_BRANCHES * D_MODEL
N_EXPERT_GROUPS = 4
EXPERTS_PER_GROUP = 8
N_EXPERTS = N_EXPERT_GROUPS * EXPERTS_PER_GROUP
TOP_K = 2
EXPERT_FF = 256
MOE_BLOCK = 128
N_MOD = 6
RMS_EPS = 1e-6

kernel_name = "hybrid_diffusion_prefix_trunk_step"


def rmsnorm(x, g):
    xf = x.astype(jnp.float32)
    y = xf * lax.rsqrt(jnp.mean(xf * xf, axis=-1, keepdims=True) + RMS_EPS)
    return (y * g.astype(jnp.float32)).astype(x.dtype)


def grid_rope(L):
    rows = L // GRID_W
    row = jnp.repeat(jnp.arange(rows), GRID_W).astype(jnp.float32)
    col = jnp.tile(jnp.arange(GRID_W), rows).astype(jnp.float32)
    n_freq = ROPE_AXIS_DIM // 2
    inv = ROPE_BASE ** (-jnp.arange(n_freq, dtype=jnp.float32) * 2.0 / ROPE_AXIS_DIM)
    ang = jnp.stack([row[:, None] * inv, col[:, None] * inv], axis=1)
    return jnp.cos(ang), jnp.sin(ang)


def apply_rope(x, cos, sin):
    B, L, H, hd = x.shape
    xr = x.astype(jnp.float32).reshape(B, L, H, 2, 2, ROPE_AXIS_DIM // 2)
    x1 = xr[..., 0, :]
    x2 = xr[..., 1, :]
    c = cos[None, :, None]
    s = sin[None, :, None]
    out = jnp.stack([x1 * c - x2 * s, x2 * c + x1 * s], axis=-2)
    return out.reshape(B, L, H, hd).astype(x.dtype)


def attend(q, k, v):
    B, Lq, H, hd = q.shape
    KV = k.shape[2]
    G = H // KV
    nb = Lq // Q_BLOCK
    qb = q.reshape(B, nb, Q_BLOCK, KV, G, hd).transpose(1, 0, 2, 3, 4, 5)
    scale = hd ** -0.5

    def one_block(q_blk):
        s = jnp.einsum('bqkgd,bskd->bkgqs', q_blk, k,
                       preferred_element_type=jnp.float32) * scale
        p = jax.nn.softmax(s, axis=-1)
        return jnp.einsum('bkgqs,bskd->bqkgd', p.astype(v.dtype), v)

    o = lax.map(one_block, qb)
    return o.transpose(1, 0, 2, 3, 4, 5).reshape(B, Lq, H * hd)


def multiscale_pool(xp, w_group, scale):
    B, L, _ = xp.shape
    xg = xp.reshape(B, L, N_POOL_GROUPS, POOL_GROUP_DIM).astype(jnp.float32)
    csum = jnp.concatenate([jnp.zeros((B, 1, N_POOL_GROUPS, POOL_GROUP_DIM), jnp.float32),
                            jnp.cumsum(xg, axis=1)], axis=1)
    t = jnp.arange(L)
    outs = []
    for g, w in enumerate(POOL_WINDOWS):
        lo = jnp.clip(t - w // 2, 0, L)
        hi = jnp.clip(t + w // 2, 0, L)
        cnt = (hi - lo).astype(jnp.float32)
        window_sum = csum[:, hi, g] - csum[:, lo, g]
        outs.append(window_sum / cnt[None, :, None] - xg[:, :, g])
    pooled = jnp.stack(outs, axis=2).astype(xp.dtype)
    mixed = jnp.einsum('blgc,gcd->blgd', pooled, w_group)
    return mixed.reshape(B, L, POOL_WIDTH) * scale


def fourier_mix(xf):
    B, L, _ = xf.shape
    xg = xf.reshape(B, L, N_FOURIER_GROUPS, FOURIER_GROUP_DIM).astype(jnp.float32)
    y = jnp.fft.fft2(xg, axes=(1, 3), norm="ortho").real
    return y.reshape(B, L, FOURIER_WIDTH).astype(xf.dtype)


def mixing(h, lp, latent_ctx):
    B, L, _ = h.shape
    proj = h @ lp["w_in"]
    q, k, v, xp, xf, gts = jnp.split(proj, SPLIT_POINTS, axis=-1)
    q = rmsnorm(q.reshape(B, L, N_HEADS, HEAD_DIM), lp["q_norm"])
    k = rmsnorm(k.reshape(B, L, N_KV_HEADS, HEAD_DIM), lp["k_norm"])
    v = v.reshape(B, L, N_KV_HEADS, HEAD_DIM)
    if latent_ctx is None:
        k_all, v_all = k, v
    else:
        ctx_k, ctx_v, cos, sin = latent_ctx
        q = apply_rope(q, cos, sin)
        k_all = jnp.concatenate([ctx_k.astype(k.dtype), apply_rope(k, cos, sin)], axis=1)
        v_all = jnp.concatenate([ctx_v.astype(v.dtype), v], axis=1)
    attn = attend(q, k_all, v_all) @ lp["w_attn_out"]
    pool = multiscale_pool(xp, lp["w_pool_group"], lp["pool_scale"]) @ lp["w_pool_out"]
    four = fourier_mix(xf) @ lp["w_fourier_out"]
    g = jax.nn.sigmoid(gts.reshape(B, L, N_BRANCHES, D_MODEL))
    merged = g[:, :, 0] * attn + g[:, :, 1] * pool + g[:, :, 2] * four
    return merged @ lp["w_out"], k, v


def hierarchical_moe(h, lp):
    N, D = h.shape
    lg = (h @ lp["w_router_group"]).astype(jnp.float32) + lp["b_router_group"].astype(jnp.float32)
    p_group = jax.nn.softmax(lg, axis=-1)
    g_sel = jnp.argmax(lg, axis=-1)
    p_g_sel = jnp.take_along_axis(p_group, g_sel[:, None], axis=-1)
    le = ((h @ lp["w_router_expert"]).astype(jnp.float32)
          + lp["b_router_expert"].astype(jnp.float32)).reshape(N, N_EXPERT_GROUPS, EXPERTS_PER_GROUP)
    le_sel = jnp.take_along_axis(le, g_sel[:, None, None], axis=1)[:, 0]
    top_v, top_i = lax.top_k(le_sel, TOP_K)
    expert_idx = g_sel[:, None] * EXPERTS_PER_GROUP + top_i
    expert_w = p_g_sel * jax.nn.softmax(top_v, axis=-1)

    A = N * TOP_K
    flat_e = expert_idx.reshape(-1).astype(jnp.int32)
    flat_w = expert_w.reshape(-1)
    flat_tok = jnp.repeat(jnp.arange(N, dtype=jnp.int32), TOP_K)
    order = jnp.argsort(flat_e)
    sorted_e = flat_e[order]
    counts = jnp.bincount(flat_e, length=N_EXPERTS)
    padded = ((counts + MOE_BLOCK - 1) // MOE_BLOCK) * MOE_BLOCK
    pad_end = jnp.cumsum(padded)
    pad_start = pad_end - padded
    start = jnp.cumsum(counts) - counts
    dest = pad_start[sorted_e] + (jnp.arange(A) - start[sorted_e])
    n_blocks = (A + N_EXPERTS * (MOE_BLOCK - 1) + MOE_BLOCK - 1) // MOE_BLOCK
    rows = n_blocks * MOE_BLOCK
    row_tok = jnp.full((rows,), N, jnp.int32).at[dest].set(flat_tok[order])
    row_w = jnp.zeros((rows,), jnp.float32).at[dest].set(flat_w[order]).astype(h.dtype)
    block_expert = jnp.minimum(
        jnp.searchsorted(pad_end, jnp.arange(n_blocks) * MOE_BLOCK, side='right'), N_EXPERTS - 1)
    h_pad = jnp.concatenate([h, jnp.zeros((1, D), h.dtype)], axis=0)
    xs = h_pad[row_tok].reshape(n_blocks, MOE_BLOCK, D)
    w_gate, w_up, w_down = lp["w_expert_gate"], lp["w_expert_up"], lp["w_expert_down"]

    def expert_block(args):
        xb, e = args
        return (jax.nn.silu(xb @ w_gate[e]) * (xb @ w_up[e])) @ w_down[e]

    y = lax.map(expert_block, (xs, block_expert)).reshape(rows, D) * row_w[:, None]
    return jax.ops.segment_sum(y, row_tok, num_segments=N + 1)[:N]


def trunk_layer(x, mod, lp, latent_ctx):
    sh1, sc1, g1, sh2, sc2, g2 = jnp.split(mod.astype(x.dtype), N_MOD, axis=-1)
    h = rmsnorm(x, lp["norm_mix_pre"]) * (1 + sc1) + sh1
    mix, k, v = mixing(h, lp, latent_ctx)
    x = x + g1 * rmsnorm(mix, lp["norm_mix_post"])
    B, L, D = x.shape
    h = rmsnorm(x, lp["norm_ffn_pre"]) * (1 + sc2) + sh2
    ffn = hierarchical_moe(h.reshape(B * L, D), lp).reshape(B, L, D)
    x = x + g2 * rmsnorm(ffn, lp["norm_ffn_post"])
    return x, k, v


def setup_inputs(seed: int = 0) -> dict:
    key = jax.random.key(seed)
    ks = iter(jax.random.split(key, 40))
    f32 = jnp.float32

    def nrm(shape, s):
        return jax.random.normal(next(ks), shape, f32) * s

    def gain(shape):
        return 1.0 + 0.1 * jax.random.normal(next(ks), shape, f32)

    D = D_MODEL
    return {
        "x_prompt": nrm((BATCH, SEQ, D), 1.0),
        "x_sample": nrm((DEC_BATCH, DEC_SEQ, D), 1.0),
        "cache_k": nrm((DEC_BATCH, DEPTH, PAST_LEN, N_KV_HEADS, HEAD_DIM), 1.0),
        "cache_v": nrm((DEC_BATCH, DEPTH, PAST_LEN, N_KV_HEADS, HEAD_DIM), 1.0),
        "c": nrm((DEC_BATCH, D), 1.0),
        "c_ctx": nrm((D,), 1.0),
        "w_ada": nrm((DEPTH, D, N_MOD * D), 0.2 * D ** -0.5),
        "b_ada": nrm((DEPTH, N_MOD * D), 0.02),
        "norm_mix_pre": gain((DEPTH, D)),
        "norm_mix_post": gain((DEPTH, D)),
        "norm_ffn_pre": gain((DEPTH, D)),
        "norm_ffn_post": gain((DEPTH, D)),
        "w_in": nrm((DEPTH, D, IN_WIDTH), D ** -0.5),
        "q_norm": gain((DEPTH, HEAD_DIM)),
        "k_norm": gain((DEPTH, HEAD_DIM)),
        "w_attn_out": nrm((DEPTH, ATTN_WIDTH, D), ATTN_WIDTH ** -0.5),
        "w_pool_group": nrm((DEPTH, N_POOL_GROUPS, POOL_GROUP_DIM, POOL_GROUP_DIM), POOL_GROUP_DIM ** -0.5),
        "pool_scale": gain((DEPTH, POOL_WIDTH)),
        "w_pool_out": nrm((DEPTH, POOL_WIDTH, D), POOL_WIDTH ** -0.5),
        "w_fourier_out": nrm((DEPTH, FOURIER_WIDTH, D), FOURIER_WIDTH ** -0.5),
        "w_out": nrm((DEPTH, D, D), D ** -0.5),
        "w_router_group": nrm((DEPTH, D, N_EXPERT_GROUPS), D ** -0.5),
        "b_router_group": nrm((DEPTH, N_EXPERT_GROUPS), 0.01),
        "w_router_expert": nrm((DEPTH, D, N_EXPERTS), D ** -0.5),
        "b_router_expert": nrm((DEPTH, N_EXPERTS), 0.01),
        "w_expert_gate": nrm((DEPTH, N_EXPERTS, D, EXPERT_FF), D ** -0.5),
        "w_expert_up": nrm((DEPTH, N_EXPERTS, D, EXPERT_FF), D ** -0.5),
        "w_expert_down": nrm((DEPTH, N_EXPERTS, EXPERT_FF, D), EXPERT_FF ** -0.5),
    }


def reference(x_prompt, x_sample, cache_k, cache_v, c, c_ctx, w_ada, b_ada,
              norm_mix_pre, norm_mix_post, norm_ffn_pre, norm_ffn_post, w_in, q_norm, k_norm,
              w_attn_out, w_pool_group, pool_scale, w_pool_out, w_fourier_out, w_out,
              w_router_group, b_router_group, w_router_expert, b_router_expert,
              w_expert_gate, w_expert_up, w_expert_down):
    cos, sin = grid_rope(x_sample.shape[1])
    silu_ctx = jax.nn.silu(c_ctx)
    silu_c = jax.nn.silu(c)
    xp_stream = x_prompt
    xs_stream = x_sample
    new_k, new_v = [], []
    for l in range(DEPTH):
        lp = dict(w_in=w_in[l], q_norm=q_norm[l], k_norm=k_norm[l], w_attn_out=w_attn_out[l],
                  w_pool_group=w_pool_group[l], pool_scale=pool_scale[l], w_pool_out=w_pool_out[l],
                  w_fourier_out=w_fourier_out[l], w_out=w_out[l],
                  norm_mix_pre=norm_mix_pre[l], norm_mix_post=norm_mix_post[l],
                  norm_ffn_pre=norm_ffn_pre[l], norm_ffn_post=norm_ffn_post[l],
                  w_router_group=w_router_group[l], b_router_group=b_router_group[l],
                  w_router_expert=w_router_expert[l], b_router_expert=b_router_expert[l],
                  w_expert_gate=w_expert_gate[l], w_expert_up=w_expert_up[l],
                  w_expert_down=w_expert_down[l])
        mod_ctx = (silu_ctx @ w_ada[l] + b_ada[l])[None, None, :]
        xp_stream, k_ctx, v_ctx = trunk_layer(xp_stream, mod_ctx, lp, None)
        new_k.append(k_ctx)
        new_v.append(v_ctx)
        mod_lat = (silu_c @ w_ada[l] + b_ada[l])[:, None, :]
        xs_stream, _, _ = trunk_layer(xs_stream, mod_lat, lp,
                                      (cache_k[:, l], cache_v[:, l], cos, sin))
    new_cache_k = jnp.stack(new_k, axis=1)
    new_cache_v = jnp.stack(new_v, axis=1)
    return (xp_stream, xs_stream, new_cache_k, new_cache_v)
```

```python
import functools
import math

import numpy as np
import jax
import jax.numpy as jnp
from jax import lax
from jax.experimental import pallas as pl
from jax.experimental.pallas import tpu as pltpu

F32 = jnp.float32
BF16 = jnp.bfloat16

D_MODEL = 1024
BATCH = 32
SEQ = 256
DEPTH = 4
DEC_BATCH = 2
DEC_SEQ = 2048
PAST_LEN = 512
GRID_W = 64
N_HEADS = 8
N_KV_HEADS = 2
HEAD_DIM = 64
KV_GROUP = N_HEADS // N_KV_HEADS
ATTN_WIDTH = N_HEADS * HEAD_DIM
KV_WIDTH = N_KV_HEADS * HEAD_DIM
ROPE_AXIS_DIM = HEAD_DIM // 2
ROPE_BASE = 10000.0
POOL_WINDOWS = (2, 4, 8, 16)
POOL_WIDTH = 256
POOL_GROUP_DIM = 64
FOURIER_WIDTH = 256
FOURIER_GROUP_DIM = 64
N_BRANCHES = 3
OFF_K = ATTN_WIDTH
OFF_P = ATTN_WIDTH + 2 * KV_WIDTH
OFF_G = OFF_P + POOL_WIDTH + FOURIER_WIDTH
IN_WIDTH = OFF_G + N_BRANCHES * D_MODEL
N_EXPERT_GROUPS = 4
EXPERTS_PER_GROUP = 8
N_EXPERTS = N_EXPERT_GROUPS * EXPERTS_PER_GROUP
TOP_K = 2
EXPERT_FF = 256
N_MOD = 6
RMS_EPS = 1e-6

N_P = BATCH * SEQ
N_S = DEC_BATCH * DEC_SEQ
N_TOK = N_P + N_S
LANES = 128
MOD_ROWS = 8
POOL_PAD = 16

TM = 512
N_TILES = N_TOK // TM
P_TILES = N_P // TM
S_TILES_PER_SEQ = DEC_SEQ // TM
TQ_S = 128
TF_S = 512
EBLK = 256
N_EBLK = (N_TOK * TOP_K + N_EXPERTS * (EBLK - 1) + EBLK - 1) // EBLK
XS_ROWS = N_EBLK * EBLK
MOD_NT = 1536


def _dot(a, b):
    return jnp.dot(a, b, preferred_element_type=F32)


def _rms(x):
    return x * lax.rsqrt(jnp.mean(x * x, axis=-1, keepdims=True) + RMS_EPS)


def _mod_row(i):
    return jnp.where(i < P_TILES, 0, 1 + (i - P_TILES) // S_TILES_PER_SEQ)


def _rope_block(i):
    return jnp.where(i < P_TILES, S_TILES_PER_SEQ, (i - P_TILES) % S_TILES_PER_SEQ)


def _mod_kernel(c_ref, w_ref, b_ref, o_ref):
    c = c_ref[...]
    s = (c * jax.nn.sigmoid(c)).astype(BF16)
    o_ref[...] = _dot(s, w_ref[...].astype(BF16)) + b_ref[...]


def _mod_call(c_all, w_ada, b_ada):
    nt = (N_MOD * D_MODEL) // MOD_NT
    return pl.pallas_call(
        _mod_kernel,
        grid=(DEPTH, nt),
        in_specs=[
            pl.BlockSpec((MOD_ROWS, D_MODEL), lambda l, j: (0, 0)),
            pl.BlockSpec((None, D_MODEL, MOD_NT), lambda l, j: (l, 0, j)),
            pl.BlockSpec((None, 1, MOD_NT), lambda l, j: (l, 0, j)),
        ],
        out_specs=pl.BlockSpec((None, MOD_ROWS, MOD_NT), lambda l, j: (l, 0, j)),
        out_shape=jax.ShapeDtypeStruct((DEPTH, MOD_ROWS, N_MOD * D_MODEL), F32),
        name="adaln_mod",
    )(c_all, w_ada, b_ada.reshape(DEPTH, 1, N_MOD * D_MODEL))


def _proj_kernel(x_ref, mod_ref, gpre_ref, w_ref, qg_ref, kg_ref, cos_ref, s1_ref, s2_ref,
                 avg_ref, dft_ref, q_ref, k_ref, v_ref, xp_ref, xc_ref, xs_ref, g_ref):
    mod = mod_ref[...]
    sh1 = mod[:, 0:D_MODEL]
    sc1 = mod[:, D_MODEL:2 * D_MODEL]
    h = (_rms(x_ref[...]) * gpre_ref[...]) * (1.0 + sc1) + sh1
    hb = h.astype(BF16)

    cos = cos_ref[...]
    s1 = s1_ref[...]
    s2 = s2_ref[...]

    def rope(t):
        return (t * cos + pltpu.roll(t, LANES - ROPE_AXIS_DIM // 2, 1) * s1
                + pltpu.roll(t, ROPE_AXIS_DIM // 2, 1) * s2)

    avg = avg_ref[...]
    for c in range(ATTN_WIDTH // LANES):
        lo = c * LANES
        q = _dot(hb, w_ref[:, lo:lo + LANES])
        ms = _dot((q * q).astype(BF16), avg)
        q = q * lax.rsqrt(ms + RMS_EPS) * qg_ref[...]
        q_ref[:, lo:lo + LANES] = rope(q).astype(BF16)

    kv = _dot(hb, w_ref[:, OFF_K:OFF_K + 2 * KV_WIDTH])
    k = kv[:, 0:KV_WIDTH]
    ms = _dot((k * k).astype(BF16), avg)
    k = k * lax.rsqrt(ms + RMS_EPS) * kg_ref[...]
    k_ref[...] = rope(k)
    v_ref[...] = kv[:, KV_WIDTH:]

    pf = _dot(hb, w_ref[:, OFF_P:OFF_G])
    xp_ref[...] = pf[:, 0:POOL_WIDTH]
    cs = _dot(pf[:, POOL_WIDTH:].astype(BF16), dft_ref[...])
    xc_ref[...] = cs[:, 0:FOURIER_WIDTH].astype(BF16)
    xs_ref[...] = cs[:, FOURIER_WIDTH:].astype(BF16)

    for c in range(N_BRANCHES):
        lo = OFF_G + c * D_MODEL
        g_ref[:, c * D_MODEL:(c + 1) * D_MODEL] = _dot(hb, w_ref[:, lo:lo + D_MODEL]).astype(BF16)


def _proj_call(l, x, mods, gpre, w_in, qg, kg, cos_t, s1_t, s2_t, avg, dft):
    row = lambda i: (i, 0)
    const2 = lambda i: (0, 0)
    per_layer = lambda i: (l, 0, 0)
    rope_spec = pl.BlockSpec((TM, LANES), lambda i: (_rope_block(i), 0))
    return pl.pallas_call(
        _proj_kernel,
        grid=(N_TILES,),
        in_specs=[
            pl.BlockSpec((TM, D_MODEL), row),
            pl.BlockSpec((None, None, 1, N_MOD * D_MODEL), lambda i: (l, _mod_row(i), 0, 0)),
            pl.BlockSpec((None, 1, D_MODEL), per_layer),
            pl.BlockSpec((None, D_MODEL, IN_WIDTH), per_layer),
            pl.BlockSpec((None, 1, LANES), per_layer),
            pl.BlockSpec((None, 1, LANES), per_layer),
            rope_spec, rope_spec, rope_spec,
            pl.BlockSpec((LANES, LANES), const2),
            pl.BlockSpec((FOURIER_WIDTH, 2 * FOURIER_WIDTH), const2),
        ],
        out_specs=[
            pl.BlockSpec((TM, ATTN_WIDTH), row),
            pl.BlockSpec((TM, KV_WIDTH), row),
            pl.BlockSpec((TM, KV_WIDTH), row),
            pl.BlockSpec((TM, POOL_WIDTH), row),
            pl.BlockSpec((TM, FOURIER_WIDTH), row),
            pl.BlockSpec((TM, FOURIER_WIDTH), row),
            pl.BlockSpec((TM, N_BRANCHES * D_MODEL), row),
        ],
        out_shape=[
            jax.ShapeDtypeStruct((N_TOK, ATTN_WIDTH), BF16),
            jax.ShapeDtypeStruct((N_TOK, KV_WIDTH), F32),
            jax.ShapeDtypeStruct((N_TOK, KV_WIDTH), F32),
            jax.ShapeDtypeStruct((N_TOK, POOL_WIDTH), F32),
            jax.ShapeDtypeStruct((N_TOK, FOURIER_WIDTH), BF16),
            jax.ShapeDtypeStruct((N_TOK, FOURIER_WIDTH), BF16),
            jax.ShapeDtypeStruct((N_TOK, N_BRANCHES * D_MODEL), BF16),
        ],
        name="proj",
    )(x, mods, gpre, w_in, qg, kg, cos_t, s1_t, s2_t, avg, dft)


def _attn_kernel(*refs, n_parts, tq):
    q_ref = refs[0]
    kv_refs = refs[1:1 + 2 * n_parts]
    o_ref = refs[-1]
    scale = HEAD_DIM ** -0.5
    outs = []
    for j in range(N_KV_HEADS):
        lo = j * HEAD_DIM
        qs = jnp.concatenate(
            [q_ref[:, (KV_GROUP * j + g) * HEAD_DIM:(KV_GROUP * j + g + 1) * HEAD_DIM]
             for g in range(KV_GROUP)], axis=0)
        scores = []
        for p in range(n_parts):
            kp = kv_refs[2 * p][:, lo:lo + HEAD_DIM].astype(BF16)
            scores.append(lax.dot_general(qs, kp, (((1,), (1,)), ((), ())),
                                          preferred_element_type=F32))
        m = jnp.max(scores[0], axis=1, keepdims=True)
        for s in scores[1:]:
            m = jnp.maximum(m, jnp.max(s, axis=1, keepdims=True))
        acc = None
        den = None
        for p in range(n_parts):
            e = jnp.exp((scores[p] - m) * scale)
            vp = kv_refs[2 * p + 1][:, lo:lo + HEAD_DIM].astype(BF16)
            pv = _dot(e.astype(BF16), vp)
            es = jnp.sum(e, axis=1, keepdims=True)
            acc = pv if acc is None else acc + pv
            den = es if den is None else den + es
        o = acc / den
        outs.extend(o[g * tq:(g + 1) * tq] for g in range(KV_GROUP))
    o_ref[...] = jnp.concatenate(outs, axis=1).astype(BF16)


def _attn_prompt_call(q, k, v):
    blk = lambda b: (b, 0)
    return pl.pallas_call(
        functools.partial(_attn_kernel, n_parts=1, tq=SEQ),
        grid=(BATCH,),
        in_specs=[
            pl.BlockSpec((SEQ, ATTN_WIDTH), blk),
            pl.BlockSpec((SEQ, KV_WIDTH), blk),
            pl.BlockSpec((SEQ, KV_WIDTH), blk),
        ],
        out_specs=pl.BlockSpec((SEQ, ATTN_WIDTH), blk),
        out_shape=jax.ShapeDtypeStruct((N_P, ATTN_WIDTH), BF16),
        name="attn_context",
    )(q, k, v)


def _attn_sample_call(l, q, k, v, cache_k, cache_v):
    nq = DEC_SEQ // TQ_S
    qrow = lambda b, i: (N_P // TQ_S + b * nq + i, 0)
    seq = lambda b, i: (N_P // DEC_SEQ + b, 0)
    cache = lambda b, i: (b, l, 0, 0)
    return pl.pallas_call(
        functools.partial(_attn_kernel, n_parts=2, tq=TQ_S),
        grid=(DEC_BATCH, nq),
        in_specs=[
            pl.BlockSpec((TQ_S, ATTN_WIDTH), qrow),
            pl.BlockSpec((None, None, PAST_LEN, KV_WIDTH), cache),
            pl.BlockSpec((None, None, PAST_LEN, KV_WIDTH), cache),
            pl.BlockSpec((DEC_SEQ, KV_WIDTH), seq),
            pl.BlockSpec((DEC_SEQ, KV_WIDTH), seq),
        ],
        out_specs=pl.BlockSpec((TQ_S, ATTN_WIDTH), lambda b, i: (b * nq + i, 0)),
        out_shape=jax.ShapeDtypeStruct((N_S, ATTN_WIDTH), BF16),
        name="attn_latent",
    )(q, cache_k, cache_v, k, v)


def _pool_kernel(xp_ref, bdw_ref, sc_ref, o_ref, pad_ref, *, seq_len):
    half = POOL_WIDTH // 2
    zeros = jnp.zeros((POOL_PAD, POOL_WIDTH), F32)
    pad_ref[0:POOL_PAD, :] = zeros
    pad_ref[POOL_PAD + seq_len:, :] = zeros
    pad_ref[POOL_PAD:POOL_PAD + seq_len, :] = xp_ref[...]
    chunk = min(seq_len, 256)
    lane = lax.broadcasted_iota(jnp.int32, (chunk, half), 1)
    first = lane < POOL_GROUP_DIM
    for c in range(seq_len // chunk):
        base = c * chunk
        t = lax.broadcasted_iota(jnp.int32, (chunk, half), 0) + base

        def sh(j, lo):
            return pad_ref[POOL_PAD + base + j:POOL_PAD + base + j + chunk, lo:lo + half]

        def cnt(w):
            return (jnp.minimum(t + w // 2, seq_len) - jnp.maximum(t - w // 2, 0)).astype(F32)

        xa = sh(0, 0)
        w2 = sh(-1, 0) + xa
        w4 = w2 + sh(-2, 0) + sh(1, 0)
        xb = sh(0, half)
        w8 = xb
        for j in (-4, -3, -2, -1, 1, 2, 3):
            w8 = w8 + sh(j, half)
        w16 = w8
        for j in (-8, -7, -6, -5, 4, 5, 6, 7):
            w16 = w16 + sh(j, half)
        pa = jnp.where(first, w2 / cnt(2), w4 / cnt(4)) - xa
        pb = jnp.where(first, w8 / cnt(8), w16 / cnt(16)) - xb
        pooled = jnp.concatenate([pa, pb], axis=1).astype(BF16)
        o_ref[base:base + chunk, :] = (_dot(pooled, bdw_ref[...]) * sc_ref[...]).astype(BF16)


def _pool_call(l, xp, bdw, scale, seq_len, n_seq, blk0):
    per_layer = lambda b: (l, 0, 0)
    return pl.pallas_call(
        functools.partial(_pool_kernel, seq_len=seq_len),
        grid=(n_seq,),
        in_specs=[
            pl.BlockSpec((seq_len, POOL_WIDTH), lambda b: (blk0 + b, 0)),
            pl.BlockSpec((None, POOL_WIDTH, POOL_WIDTH), per_layer),
            pl.BlockSpec((None, 1, POOL_WIDTH), per_layer),
        ],
        out_specs=pl.BlockSpec((seq_len, POOL_WIDTH), lambda b: (b, 0)),
        out_shape=jax.ShapeDtypeStruct((n_seq * seq_len, POOL_WIDTH), BF16),
        scratch_shapes=[pltpu.VMEM((seq_len + 2 * POOL_PAD, POOL_WIDTH), F32)],
        name="pool_%d" % seq_len,
    )(xp, bdw, scale)


def _fourier_kernel(c_ref, s_ref, xc_ref, xs_ref, o_ref, *, scale):
    y = _dot(c_ref[...], xc_ref[...]) - _dot(s_ref[...], xs_ref[...])
    o_ref[...] = (y * scale).astype(BF16)


def _fourier_prompt_call(cmat, smat, xc, xs):
    blk = lambda b: (b, 0)
    const2 = lambda b: (0, 0)
    return pl.pallas_call(
        functools.partial(_fourier_kernel, scale=1.0 / math.sqrt(SEQ * FOURIER_GROUP_DIM)),
        grid=(BATCH,),
        in_specs=[
            pl.BlockSpec((SEQ, SEQ), const2),
            pl.BlockSpec((SEQ, SEQ), const2),
            pl.BlockSpec((SEQ, FOURIER_WIDTH), blk),
            pl.BlockSpec((SEQ, FOURIER_WIDTH), blk),
        ],
        out_specs=pl.BlockSpec((SEQ, FOURIER_WIDTH), blk),
        out_shape=jax.ShapeDtypeStruct((N_P, FOURIER_WIDTH), BF16),
        name="fourier_context",
    )(cmat, smat, xc, xs)


def _fourier_sample_call(cmat, smat, xc, xs):
    nt = DEC_SEQ // TF_S
    rows = lambda b, i: (i, 0)
    seq = lambda b, i: (N_P // DEC_SEQ + b, 0)
    out = lambda b, i: (b * nt + i, 0)
    return pl.pallas_call(
        functools.partial(_fourier_kernel, scale=1.0 / math.sqrt(DEC_SEQ * FOURIER_GROUP_DIM)),
        grid=(DEC_BATCH, nt),
        in_specs=[
            pl.BlockSpec((TF_S, DEC_SEQ), rows),
            pl.BlockSpec((TF_S, DEC_SEQ), rows),
            pl.BlockSpec((DEC_SEQ, FOURIER_WIDTH), seq),
            pl.BlockSpec((DEC_SEQ, FOURIER_WIDTH), seq),
        ],
        out_specs=pl.BlockSpec((TF_S, FOURIER_WIDTH), out),
        out_shape=jax.ShapeDtypeStruct((N_S, FOURIER_WIDTH), BF16),
        name="fourier_latent",
    )(cmat, smat, xc, xs)


def _merge_kernel(x_ref, attn_p_ref, attn_s_ref, pool_p_ref, pool_s_ref, four_p_ref, four_s_ref,
                  g_ref, mod_ref, gpost_ref, gffn_ref,
                  wa_ref, wp_ref, wf_ref, wo_ref, wr_ref, br_ref, tri_ref,
                  x1_ref, h2_ref, mi_ref, mw_ref, cnt_ref, carry_ref):
    i = pl.program_id(0)

    @pl.when(i == 0)
    def _():
        carry_ref[...] = jnp.zeros_like(carry_ref)

    mod = mod_ref[...]
    g1 = mod[:, 2 * D_MODEL:3 * D_MODEL]
    sh2 = mod[:, 3 * D_MODEL:4 * D_MODEL]
    sc2 = mod[:, 4 * D_MODEL:5 * D_MODEL]

    def gate(c):
        return jax.nn.sigmoid(g_ref[:, c * D_MODEL:(c + 1) * D_MODEL].astype(F32))

    def branch(p_ref, s_ref):
        return jnp.where(i < P_TILES, p_ref[...], s_ref[...])

    merged = gate(0) * _dot(branch(attn_p_ref, attn_s_ref), wa_ref[...])
    merged = merged + gate(1) * _dot(branch(pool_p_ref, pool_s_ref), wp_ref[...])
    merged = merged + gate(2) * _dot(branch(four_p_ref, four_s_ref), wf_ref[...])
    mix = _dot(merged.astype(BF16), wo_ref[...])
    x1 = x_ref[...] + g1 * (_rms(mix) * gpost_ref[...])
    x1_ref[...] = x1
    h2 = (_rms(x1) * gffn_ref[...]) * (1.0 + sc2) + sh2
    h2_ref[...] = h2

    logits = _dot(h2.astype(BF16), wr_ref[...]) + br_ref[...]
    lane = lax.broadcasted_iota(jnp.int32, (TM, LANES), 1)
    lanef = lane.astype(F32)
    neg = jnp.float32(-3e38)
    big = jnp.float32(1e9)
    is_g = (lane >= N_EXPERTS) & (lane < N_EXPERTS + N_EXPERT_GROUPS)
    lg = jnp.where(is_g, logits, neg)
    gmax = jnp.max(lg, axis=1, keepdims=True)
    g_sel = jnp.min(jnp.where(lg == gmax, lanef - N_EXPERTS, big), axis=1, keepdims=True)
    p_g = 1.0 / jnp.sum(jnp.where(is_g, jnp.exp(logits - gmax), 0.0), axis=1, keepdims=True)
    grp = lax.shift_right_logical(lane, int(math.log2(EXPERTS_PER_GROUP))).astype(F32)
    in_grp = (lane < N_EXPERTS) & (grp == g_sel)
    le = jnp.where(in_grp, logits, neg)
    v1 = jnp.max(le, axis=1, keepdims=True)
    i1 = jnp.min(jnp.where(le == v1, lanef, big), axis=1, keepdims=True)
    le2 = jnp.where(lanef == i1, neg, le)
    v2 = jnp.max(le2, axis=1, keepdims=True)
    i2 = jnp.min(jnp.where(le2 == v2, lanef, big), axis=1, keepdims=True)
    e21 = jnp.exp(v2 - v1)
    w1 = p_g / (1.0 + e21)
    w2 = p_g * e21 / (1.0 + e21)

    oh1 = (lanef == i1).astype(F32)
    oh2 = (lanef == i2).astype(F32)
    ohb = (oh1 + oh2).astype(BF16)
    before = _dot(tri_ref[...], ohb) + carry_ref[0:1, :]
    r1 = jnp.sum(before * oh1, axis=1, keepdims=True)
    r2 = jnp.sum(before * oh2, axis=1, keepdims=True)
    carry_ref[...] = carry_ref[...] + _dot(jnp.ones((MOD_ROWS, TM), BF16), ohb)
    cnt_ref[...] = carry_ref[...]

    meta = jnp.where(lane == 0, i1, jnp.where(lane == 1, i2, jnp.where(lane == 2, r1, r2)))
    mi_ref[...] = meta.astype(jnp.int32)
    mw_ref[...] = jnp.where(lane == 0, w1, w2)


def _merge_call(l, x, attn, pool, four, gates, mods, gpost, gffn, wa, wp, wf, wo, wr, br, tri):
    row = lambda i: (i, 0)
    prow = lambda i: (jnp.minimum(i, P_TILES - 1), 0)
    srow = lambda i: (jnp.maximum(i - P_TILES, 0), 0)
    const2 = lambda i: (0, 0)
    per_layer = lambda i: (l, 0, 0)
    return pl.pallas_call(
        _merge_kernel,
        grid=(N_TILES,),
        in_specs=[
            pl.BlockSpec((TM, D_MODEL), row),
            pl.BlockSpec((TM, ATTN_WIDTH), prow),
            pl.BlockSpec((TM, ATTN_WIDTH), srow),
            pl.BlockSpec((TM, POOL_WIDTH), prow),
            pl.BlockSpec((TM, POOL_WIDTH), srow),
            pl.BlockSpec((TM, FOURIER_WIDTH), prow),
            pl.BlockSpec((TM, FOURIER_WIDTH), srow),
            pl.BlockSpec((TM, N_BRANCHES * D_MODEL), row),
            pl.BlockSpec((None, None, 1, N_MOD * D_MODEL), lambda i: (l, _mod_row(i), 0, 0)),
            pl.BlockSpec((None, 1, D_MODEL), per_layer),
            pl.BlockSpec((None, 1, D_MODEL), per_layer),
            pl.BlockSpec((None, ATTN_WIDTH, D_MODEL), per_layer),
            pl.BlockSpec((None, POOL_WIDTH, D_MODEL), per_layer),
            pl.BlockSpec((None, FOURIER_WIDTH, D_MODEL), per_layer),
            pl.BlockSpec((None, D_MODEL, D_MODEL), per_layer),
            pl.BlockSpec((None, D_MODEL, LANES), per_layer),
            pl.BlockSpec((None, 1, LANES), per_layer),
            pl.BlockSpec((TM, TM), const2),
        ],
        out_specs=[
            pl.BlockSpec((TM, D_MODEL), row),
            pl.BlockSpec((TM, D_MODEL), row),
            pl.BlockSpec((TM, LANES), row),
            pl.BlockSpec((TM, LANES), row),
            pl.BlockSpec((MOD_ROWS, LANES), const2),
        ],
        out_shape=[
            jax.ShapeDtypeStruct((N_TOK, D_MODEL), F32),
            jax.ShapeDtypeStruct((N_TOK, D_MODEL), F32),
            jax.ShapeDtypeStruct((N_TOK, LANES), jnp.int32),
            jax.ShapeDtypeStruct((N_TOK, LANES), F32),
            jax.ShapeDtypeStruct((MOD_ROWS, LANES), F32),
        ],
        scratch_shapes=[pltpu.VMEM((MOD_ROWS, LANES), F32)],
        compiler_params=pltpu.CompilerParams(dimension_semantics=("arbitrary",)),
        name="merge_router",
    )(x, attn[0], attn[1], pool[0], pool[1], four[0], four[1], gates, mods, gpost, gffn,
      wa, wp, wf, wo, wr, br, tri)


def _row_copy(src_ref, src_row, dst_ref, dst_row, sem):
    return pltpu.make_async_copy(src_ref.at[pl.ds(src_row, 1)], dst_ref.at[pl.ds(dst_row, 1)], sem)


def _scatter_kernel(d1_ref, d2_ref, h_ref, xs_in_ref, xs_ref, sem):
    del xs_in_ref
    base = pl.program_id(0) * TM

    def start(r, carry):
        _row_copy(h_ref, r, xs_ref, d1_ref[base + r], sem).start()
        _row_copy(h_ref, r, xs_ref, d2_ref[base + r], sem).start()
        return carry

    lax.fori_loop(0, TM, start, 0)

    def wait(r, carry):
        _row_copy(h_ref, 0, xs_ref, 0, sem).wait()
        _row_copy(h_ref, 0, xs_ref, 0, sem).wait()
        return carry

    lax.fori_loop(0, TM, wait, 0)


def _scatter_call(d1, d2, h2, xs):
    return pl.pallas_call(
        _scatter_kernel,
        grid_spec=pltpu.PrefetchScalarGridSpec(
            num_scalar_prefetch=2,
            grid=(N_TILES,),
            in_specs=[
                pl.BlockSpec((TM, D_MODEL), lambda i, d1, d2: (i, 0)),
                pl.BlockSpec(memory_space=pl.ANY),
            ],
            out_specs=pl.BlockSpec(memory_space=pl.ANY),
            scratch_shapes=[pltpu.SemaphoreType.DMA(())],
        ),
        out_shape=jax.ShapeDtypeStruct((XS_ROWS, D_MODEL), F32),
        input_output_aliases={3: 0},
        compiler_params=pltpu.CompilerParams(dimension_semantics=("arbitrary",)),
        name="moe_scatter",
    )(d1, d2, h2, xs)


def _expert_kernel(be_ref, nu_ref, x_ref, wg_ref, wu_ref, wd_ref, y_ref):
    b = pl.program_id(0)

    @pl.when(b < nu_ref[0])
    def _():
        xb = x_ref[...].astype(BF16)
        g = _dot(xb, wg_ref[...].astype(BF16))
        u = _dot(xb, wu_ref[...].astype(BF16))
        a = (g * jax.nn.sigmoid(g)) * u
        y_ref[...] = _dot(a.astype(BF16), wd_ref[...].astype(BF16))

    @pl.when(b >= nu_ref[0])
    def _():
        y_ref[...] = jnp.zeros_like(y_ref)


def _expert_call(l, block_expert, n_used, xs, w_gate, w_up, w_down):
    row = lambda b, be, nu: (b, 0)
    wsel = lambda b, be, nu: (l, be[b], 0, 0)
    return pl.pallas_call(
        _expert_kernel,
        grid_spec=pltpu.PrefetchScalarGridSpec(
            num_scalar_prefetch=2,
            grid=(N_EBLK,),
            in_specs=[
                pl.BlockSpec((EBLK, D_MODEL), row),
                pl.BlockSpec((None, None, D_MODEL, EXPERT_FF), wsel),
                pl.BlockSpec((None, None, D_MODEL, EXPERT_FF), wsel),
                pl.BlockSpec((None, None, EXPERT_FF, D_MODEL), wsel),
            ],
            out_specs=pl.BlockSpec((EBLK, D_MODEL), row),
        ),
        out_shape=jax.ShapeDtypeStruct((XS_ROWS, D_MODEL), F32),
        compiler_params=pltpu.CompilerParams(dimension_semantics=("arbitrary",)),
        name="moe_experts",
    )(block_expert, n_used, xs, w_gate, w_up, w_down)


def _combine_kernel(d1_ref, d2_ref, y_ref, x1_ref, mw_ref, mod_ref, gpost_ref, o_ref, ybuf, sem):
    base = pl.program_id(0) * TM

    def start(r, carry):
        _row_copy(y_ref, d1_ref[base + r], ybuf.at[0], r, sem).start()
        _row_copy(y_ref, d2_ref[base + r], ybuf.at[1], r, sem).start()
        return carry

    lax.fori_loop(0, TM, start, 0)

    def wait(r, carry):
        _row_copy(y_ref, 0, ybuf.at[0], 0, sem).wait()
        _row_copy(y_ref, 0, ybuf.at[1], 0, sem).wait()
        return carry

    lax.fori_loop(0, TM, wait, 0)

    mw = mw_ref[...]
    ffn = mw[:, 0:1] * ybuf[0] + mw[:, 1:2] * ybuf[1]
    g2 = mod_ref[...][:, 5 * D_MODEL:6 * D_MODEL]
    o_ref[...] = x1_ref[...] + g2 * (_rms(ffn) * gpost_ref[...])


def _combine_call(l, d1, d2, y, x1, mw, mods, gpost):
    return pl.pallas_call(
        _combine_kernel,
        grid_spec=pltpu.PrefetchScalarGridSpec(
            num_scalar_prefetch=2,
            grid=(N_TILES,),
            in_specs=[
                pl.BlockSpec(memory_space=pl.ANY),
                pl.BlockSpec((TM, D_MODEL), lambda i, d1, d2: (i, 0)),
                pl.BlockSpec((TM, LANES), lambda i, d1, d2: (i, 0)),
                pl.BlockSpec((None, None, 1, N_MOD * D_MODEL),
                             lambda i, d1, d2: (l, _mod_row(i), 0, 0)),
                pl.BlockSpec((None, 1, D_MODEL), lambda i, d1, d2: (l, 0, 0)),
            ],
            out_specs=pl.BlockSpec((TM, D_MODEL), lambda i, d1, d2: (i, 0)),
            scratch_shapes=[pltpu.VMEM((TOP_K, TM, D_MODEL), F32), pltpu.SemaphoreType.DMA(())],
        ),
        out_shape=jax.ShapeDtypeStruct((N_TOK, D_MODEL), F32),
        compiler_params=pltpu.CompilerParams(dimension_semantics=("arbitrary",)),
        name="moe_combine",
    )(d1, d2, y, x1, mw, mods, gpost)


def _dft_mats(n):
    k = np.arange(n, dtype=np.int64)
    ang = 2.0 * np.pi * ((k[:, None] * k[None, :]) % n).astype(np.float64) / n
    return np.cos(ang), np.sin(ang)


def _block_diag(m, reps):
    n = m.shape[0]
    out = np.zeros((n * reps, n * reps), m.dtype)
    for r in range(reps):
        out[r * n:(r + 1) * n, r * n:(r + 1) * n] = m
    return out


def _rope_tables():
    t = np.arange(DEC_SEQ)
    pos = np.stack([t // GRID_W, t % GRID_W], axis=1).astype(np.float64)
    n_freq = ROPE_AXIS_DIM // 2
    inv = ROPE_BASE ** (-np.arange(n_freq, dtype=np.float64) * 2.0 / ROPE_AXIS_DIM)
    ang = pos[:, :, None] * inv[None, None, :]
    cos = np.cos(ang)
    sin = np.sin(ang)
    zero = np.zeros_like(sin[:, 0])
    cos_h = np.concatenate([cos[:, 0], cos[:, 0], cos[:, 1], cos[:, 1]], axis=1)
    s1_h = np.concatenate([-sin[:, 0], zero, -sin[:, 1], zero], axis=1)
    s2_h = np.concatenate([zero, sin[:, 0], zero, sin[:, 1]], axis=1)
    reps = LANES // HEAD_DIM

    def table(a, ident):
        a = np.tile(a, (1, reps))
        pad = np.full((TM, LANES), ident, np.float64)
        return jnp.asarray(np.concatenate([a, pad], axis=0), F32)

    return table(cos_h, 1.0), table(s1_h, 0.0), table(s2_h, 0.0)


def kernel(x_prompt, x_sample, cache_k, cache_v, c, c_ctx, w_ada, b_ada, norm_mix_pre,
           norm_mix_post, norm_ffn_pre, norm_ffn_post, w_in, q_norm, k_norm, w_attn_out,
           w_pool_group, pool_scale, w_pool_out, w_fourier_out, w_out, w_router_group,
           b_router_group, w_router_expert, b_router_expert, w_expert_gate, w_expert_up,
           w_expert_down):
    cos_t, s1_t, s2_t = _rope_tables()
    avg = jnp.asarray(_block_diag(np.full((HEAD_DIM, HEAD_DIM), 1.0 / HEAD_DIM), LANES // HEAD_DIM), BF16)
    c64, s64 = _dft_mats(FOURIER_GROUP_DIM)
    n_fg = FOURIER_WIDTH // FOURIER_GROUP_DIM
    dft_ch = jnp.asarray(np.concatenate([_block_diag(c64, n_fg), _block_diag(s64, n_fg)], axis=1), BF16)
    cp, sp = _dft_mats(SEQ)
    cp, sp = jnp.asarray(cp, BF16), jnp.asarray(sp, BF16)
    cl, sl = _dft_mats(DEC_SEQ)
    cl, sl = jnp.asarray(cl, BF16), jnp.asarray(sl, BF16)
    tri = jnp.asarray(np.tril(np.ones((TM, TM)), -1), BF16)

    w_in_b = w_in.astype(BF16)
    wa_b = w_attn_out.astype(BF16)
    wp_b = w_pool_out.astype(BF16)
    wf_b = w_fourier_out.astype(BF16)
    wo_b = w_out.astype(BF16)
    pad_r = jnp.zeros((DEPTH, D_MODEL, LANES - N_EXPERTS - N_EXPERT_GROUPS), F32)
    wr_b = jnp.concatenate([w_router_expert, w_router_group, pad_r], axis=2).astype(BF16)
    br = jnp.concatenate([b_router_expert, b_router_group,
                          jnp.zeros((DEPTH, LANES - N_EXPERTS - N_EXPERT_GROUPS), F32)],
                         axis=1).reshape(DEPTH, 1, LANES)
    n_pg = POOL_WIDTH // POOL_GROUP_DIM
    bdw = jnp.zeros((DEPTH, POOL_WIDTH, POOL_WIDTH), F32)
    for g in range(n_pg):
        lo = g * POOL_GROUP_DIM
        bdw = bdw.at[:, lo:lo + POOL_GROUP_DIM, lo:lo + POOL_GROUP_DIM].set(w_pool_group[:, g])
    bdw = bdw.astype(BF16)
    pscale = pool_scale.reshape(DEPTH, 1, POOL_WIDTH)
    qg = jnp.tile(q_norm, (1, LANES // HEAD_DIM)).reshape(DEPTH, 1, LANES)
    kg = jnp.tile(k_norm, (1, LANES // HEAD_DIM)).reshape(DEPTH, 1, LANES)
    gpre = norm_mix_pre.reshape(DEPTH, 1, D_MODEL)
    gpost = norm_mix_post.reshape(DEPTH, 1, D_MODEL)
    gffn = norm_ffn_pre.reshape(DEPTH, 1, D_MODEL)
    gfpost = norm_ffn_post.reshape(DEPTH, 1, D_MODEL)
    ck = cache_k.reshape(DEC_BATCH, DEPTH, PAST_LEN, KV_WIDTH)
    cv = cache_v.reshape(DEC_BATCH, DEPTH, PAST_LEN, KV_WIDTH)

    c_all = jnp.concatenate([c_ctx[None, :], c, jnp.zeros((MOD_ROWS - 1 - DEC_BATCH, D_MODEL), F32)], axis=0)
    mods = _mod_call(c_all, w_ada, b_ada).reshape(DEPTH, MOD_ROWS, 1, N_MOD * D_MODEL)

    x = jnp.concatenate([x_prompt.reshape(N_P, D_MODEL), x_sample.reshape(N_S, D_MODEL)], axis=0)
    xs_buf = jnp.zeros((XS_ROWS, D_MODEL), F32)
    new_k, new_v = [], []
    for l in range(DEPTH):
        q, k, v, xp, xc, xsn, gates = _proj_call(l, x, mods, gpre, w_in_b, qg, kg,
                                                 cos_t, s1_t, s2_t, avg, dft_ch)
        new_k.append(k[:N_P].reshape(BATCH, SEQ, N_KV_HEADS, HEAD_DIM))
        new_v.append(v[:N_P].reshape(BATCH, SEQ, N_KV_HEADS, HEAD_DIM))
        attn = (_attn_prompt_call(q, k, v), _attn_sample_call(l, q, k, v, ck, cv))
        pool = (_pool_call(l, xp, bdw, pscale, SEQ, BATCH, 0),
                _pool_call(l, xp, bdw, pscale, DEC_SEQ, DEC_BATCH, N_P // DEC_SEQ))
        four = (_fourier_prompt_call(cp, sp, xc, xsn), _fourier_sample_call(cl, sl, xc, xsn))
        x1, h2, mi, mw, cnt = _merge_call(l, x, attn, pool, four, gates, mods, gpost, gffn,
                                          wa_b, wp_b, wf_b, wo_b, wr_b, br, tri)
        counts = cnt[0, :N_EXPERTS].astype(jnp.int32)
        padded = ((counts + EBLK - 1) // EBLK) * EBLK
        pad_end = jnp.cumsum(padded)
        pad_start = pad_end - padded
        d1 = pad_start[mi[:, 0]] + mi[:, 2]
        d2 = pad_start[mi[:, 1]] + mi[:, 3]
        block_expert = jnp.minimum(
            jnp.searchsorted(pad_end, jnp.arange(N_EBLK, dtype=jnp.int32) * EBLK, side='right'),
            N_EXPERTS - 1).astype(jnp.int32)
        n_used = (pad_end[-1:] // EBLK).astype(jnp.int32)
        xs_buf = _scatter_call(d1, d2, h2, xs_buf)
        y = _expert_call(l, block_expert, n_used, xs_buf, w_expert_gate, w_expert_up, w_expert_down)
        x = _combine_call(l, d1, d2, y, x1, mw, mods, gfpost)

    y_prompt = x[:N_P].reshape(BATCH, SEQ, D_MODEL)
    y_sample = x[N_P:].reshape(DEC_BATCH, DEC_SEQ, D_MODEL)
    return (y_prompt, y_sample, jnp.stack(new_k, axis=1), jnp.stack(new_v, axis=1))
```

```python
import functools
import math

import numpy as np
import jax
import jax.numpy as jnp
from jax import lax
from jax.experimental import pallas as pl
from jax.experimental.pallas import tpu as pltpu

F32 = jnp.float32
BF16 = jnp.bfloat16

D_MODEL = 1024
BATCH = 32
SEQ = 256
DEPTH = 4
DEC_BATCH = 2
DEC_SEQ = 2048
PAST_LEN = 512
GRID_W = 64
N_HEADS = 8
N_KV_HEADS = 2
HEAD_DIM = 64
KV_GROUP = N_HEADS // N_KV_HEADS
ATTN_WIDTH = N_HEADS * HEAD_DIM
KV_WIDTH = N_KV_HEADS * HEAD_DIM
ROPE_AXIS_DIM = HEAD_DIM // 2
ROPE_BASE = 10000.0
POOL_WINDOWS = (2, 4, 8, 16)
POOL_WIDTH = 256
POOL_GROUP_DIM = 64
FOURIER_WIDTH = 256
FOURIER_GROUP_DIM = 64
N_BRANCHES = 3
OFF_K = ATTN_WIDTH
OFF_P = ATTN_WIDTH + 2 * KV_WIDTH
OFF_G = OFF_P + POOL_WIDTH + FOURIER_WIDTH
IN_WIDTH = OFF_G + N_BRANCHES * D_MODEL
N_EXPERT_GROUPS = 4
EXPERTS_PER_GROUP = 8
N_EXPERTS = N_EXPERT_GROUPS * EXPERTS_PER_GROUP
TOP_K = 2
EXPERT_FF = 256
N_MOD = 6
RMS_EPS = 1e-6

N_P = BATCH * SEQ
N_S = DEC_BATCH * DEC_SEQ
N_TOK = N_P + N_S
LANES = 128
MOD_ROWS = 8
POOL_PAD = 16

TM = 512
N_TILES = N_TOK // TM
P_TILES = N_P // TM
S_TILES_PER_SEQ = DEC_SEQ // TM
TQ_S = 128
TF_S = 512
EBLK = 256
RUN_ALIGN = 8
RUN_SHIFT = 3
RUN_BITS = 7
assert RUN_ALIGN << (RUN_BITS - 1) == TM
SORT_ROWS = TOP_K * TM + N_EXPERTS * RUN_ALIGN
N_RUNS = N_TILES * N_EXPERTS
N_EBLK = (N_TOK * TOP_K + N_RUNS * (RUN_ALIGN - 1) + N_EXPERTS * (EBLK - 1) + EBLK - 1) // EBLK
XS_ROWS = N_EBLK * EBLK
MOD_NT = 1536


def _dot(a, b):
    return jnp.dot(a, b, preferred_element_type=F32)


def _rms(x):
    return x * lax.rsqrt(jnp.mean(x * x, axis=-1, keepdims=True) + RMS_EPS)


def _mod_row(i):
    return jnp.where(i < P_TILES, 0, 1 + (i - P_TILES) // S_TILES_PER_SEQ)


def _rope_block(i):
    return jnp.where(i < P_TILES, S_TILES_PER_SEQ, (i - P_TILES) % S_TILES_PER_SEQ)


def _mod_kernel(c_ref, w_ref, b_ref, o_ref):
    c = c_ref[...]
    s = (c * jax.nn.sigmoid(c)).astype(BF16)
    o_ref[...] = _dot(s, w_ref[...].astype(BF16)) + b_ref[...]


def _mod_call(c_all, w_ada, b_ada):
    nt = (N_MOD * D_MODEL) // MOD_NT
    return pl.pallas_call(
        _mod_kernel,
        grid=(DEPTH, nt),
        in_specs=[
            pl.BlockSpec((MOD_ROWS, D_MODEL), lambda l, j: (0, 0)),
            pl.BlockSpec((None, D_MODEL, MOD_NT), lambda l, j: (l, 0, j)),
            pl.BlockSpec((None, 1, MOD_NT), lambda l, j: (l, 0, j)),
        ],
        out_specs=pl.BlockSpec((None, MOD_ROWS, MOD_NT), lambda l, j: (l, 0, j)),
        out_shape=jax.ShapeDtypeStruct((DEPTH, MOD_ROWS, N_MOD * D_MODEL), F32),
        name="adaln_mod",
    )(c_all, w_ada, b_ada.reshape(DEPTH, 1, N_MOD * D_MODEL))


def _proj_kernel(x_ref, mod_ref, gpre_ref, w_ref, qg_ref, kg_ref, cos_ref, s1_ref, s2_ref,
                 avg_ref, dft_ref, q_ref, k_ref, v_ref, xp_ref, xc_ref, xs_ref, g_ref):
    mod = mod_ref[...]
    sh1 = mod[:, 0:D_MODEL]
    sc1 = mod[:, D_MODEL:2 * D_MODEL]
    h = (_rms(x_ref[...]) * gpre_ref[...]) * (1.0 + sc1) + sh1
    hb = h.astype(BF16)

    cos = cos_ref[...]
    s1 = s1_ref[...]
    s2 = s2_ref[...]

    def rope(t):
        return (t * cos + pltpu.roll(t, LANES - ROPE_AXIS_DIM // 2, 1) * s1
                + pltpu.roll(t, ROPE_AXIS_DIM // 2, 1) * s2)

    avg = avg_ref[...]
    for c in range(ATTN_WIDTH // LANES):
        lo = c * LANES
        q = _dot(hb, w_ref[:, lo:lo + LANES])
        ms = _dot((q * q).astype(BF16), avg)
        q = q * lax.rsqrt(ms + RMS_EPS) * qg_ref[...]
        q_ref[:, lo:lo + LANES] = rope(q).astype(BF16)

    kv = _dot(hb, w_ref[:, OFF_K:OFF_K + 2 * KV_WIDTH])
    k = kv[:, 0:KV_WIDTH]
    ms = _dot((k * k).astype(BF16), avg)
    k = k * lax.rsqrt(ms + RMS_EPS) * kg_ref[...]
    k_ref[...] = rope(k)
    v_ref[...] = kv[:, KV_WIDTH:]

    pf = _dot(hb, w_ref[:, OFF_P:OFF_G])
    xp_ref[...] = pf[:, 0:POOL_WIDTH]
    cs = _dot(pf[:, POOL_WIDTH:].astype(BF16), dft_ref[...])
    xc_ref[...] = cs[:, 0:FOURIER_WIDTH].astype(BF16)
    xs_ref[...] = cs[:, FOURIER_WIDTH:].astype(BF16)

    for c in range(N_BRANCHES):
        lo = OFF_G + c * D_MODEL
        g_ref[:, c * D_MODEL:(c + 1) * D_MODEL] = _dot(hb, w_ref[:, lo:lo + D_MODEL]).astype(BF16)


def _proj_call(l, x, mods, gpre, w_in, qg, kg, cos_t, s1_t, s2_t, avg, dft):
    row = lambda i: (i, 0)
    const2 = lambda i: (0, 0)
    per_layer = lambda i: (l, 0, 0)
    rope_spec = pl.BlockSpec((TM, LANES), lambda i: (_rope_block(i), 0))
    return pl.pallas_call(
        _proj_kernel,
        grid=(N_TILES,),
        in_specs=[
            pl.BlockSpec((TM, D_MODEL), row),
            pl.BlockSpec((None, None, 1, N_MOD * D_MODEL), lambda i: (l, _mod_row(i), 0, 0)),
            pl.BlockSpec((None, 1, D_MODEL), per_layer),
            pl.BlockSpec((None, D_MODEL, IN_WIDTH), per_layer),
            pl.BlockSpec((None, 1, LANES), per_layer),
            pl.BlockSpec((None, 1, LANES), per_layer),
            rope_spec, rope_spec, rope_spec,
            pl.BlockSpec((LANES, LANES), const2),
            pl.BlockSpec((FOURIER_WIDTH, 2 * FOURIER_WIDTH), const2),
        ],
        out_specs=[
            pl.BlockSpec((TM, ATTN_WIDTH), row),
            pl.BlockSpec((TM, KV_WIDTH), row),
            pl.BlockSpec((TM, KV_WIDTH), row),
            pl.BlockSpec((TM, POOL_WIDTH), row),
            pl.BlockSpec((TM, FOURIER_WIDTH), row),
            pl.BlockSpec((TM, FOURIER_WIDTH), row),
            pl.BlockSpec((TM, N_BRANCHES * D_MODEL), row),
        ],
        out_shape=[
            jax.ShapeDtypeStruct((N_TOK, ATTN_WIDTH), BF16),
            jax.ShapeDtypeStruct((N_TOK, KV_WIDTH), F32),
            jax.ShapeDtypeStruct((N_TOK, KV_WIDTH), F32),
            jax.ShapeDtypeStruct((N_TOK, POOL_WIDTH), F32),
            jax.ShapeDtypeStruct((N_TOK, FOURIER_WIDTH), BF16),
            jax.ShapeDtypeStruct((N_TOK, FOURIER_WIDTH), BF16),
            jax.ShapeDtypeStruct((N_TOK, N_BRANCHES * D_MODEL), BF16),
        ],
        name="proj",
    )(x, mods, gpre, w_in, qg, kg, cos_t, s1_t, s2_t, avg, dft)


def _attn_kernel(*refs, n_parts, tq):
    q_ref = refs[0]
    kv_refs = refs[1:1 + 2 * n_parts]
    o_ref = refs[-1]
    scale = HEAD_DIM ** -0.5
    outs = []
    for j in range(N_KV_HEADS):
        lo = j * HEAD_DIM
        qs = jnp.concatenate(
            [q_ref[:, (KV_GROUP * j + g) * HEAD_DIM:(KV_GROUP * j + g + 1) * HEAD_DIM]
             for g in range(KV_GROUP)], axis=0)
        scores = []
        for p in range(n_parts):
            kp = kv_refs[2 * p][:, lo:lo + HEAD_DIM].astype(BF16)
            scores.append(lax.dot_general(qs, kp, (((1,), (1,)), ((), ())),
                                          preferred_element_type=F32))
        m = jnp.max(scores[0], axis=1, keepdims=True)
        for s in scores[1:]:
            m = jnp.maximum(m, jnp.max(s, axis=1, keepdims=True))
        acc = None
        den = None
        for p in range(n_parts):
            e = jnp.exp((scores[p] - m) * scale)
            vp = kv_refs[2 * p + 1][:, lo:lo + HEAD_DIM].astype(BF16)
            pv = _dot(e.astype(BF16), vp)
            es = jnp.sum(e, axis=1, keepdims=True)
            acc = pv if acc is None else acc + pv
            den = es if den is None else den + es
        o = acc / den
        outs.extend(o[g * tq:(g + 1) * tq] for g in range(KV_GROUP))
    o_ref[...] = jnp.concatenate(outs, axis=1).astype(BF16)


def _attn_prompt_call(q, k, v):
    blk = lambda b: (b, 0)
    return pl.pallas_call(
        functools.partial(_attn_kernel, n_parts=1, tq=SEQ),
        grid=(BATCH,),
        in_specs=[
            pl.BlockSpec((SEQ, ATTN_WIDTH), blk),
            pl.BlockSpec((SEQ, KV_WIDTH), blk),
            pl.BlockSpec((SEQ, KV_WIDTH), blk),
        ],
        out_specs=pl.BlockSpec((SEQ, ATTN_WIDTH), blk),
        out_shape=jax.ShapeDtypeStruct((N_P, ATTN_WIDTH), BF16),
        name="attn_context",
    )(q, k, v)


def _attn_sample_call(l, q, k, v, cache_k, cache_v):
    nq = DEC_SEQ // TQ_S
    qrow = lambda b, i: (N_P // TQ_S + b * nq + i, 0)
    seq = lambda b, i: (N_P // DEC_SEQ + b, 0)
    cache = lambda b, i: (b, l, 0, 0)
    return pl.pallas_call(
        functools.partial(_attn_kernel, n_parts=2, tq=TQ_S),
        grid=(DEC_BATCH, nq),
        in_specs=[
            pl.BlockSpec((TQ_S, ATTN_WIDTH), qrow),
            pl.BlockSpec((None, None, PAST_LEN, KV_WIDTH), cache),
            pl.BlockSpec((None, None, PAST_LEN, KV_WIDTH), cache),
            pl.BlockSpec((DEC_SEQ, KV_WIDTH), seq),
            pl.BlockSpec((DEC_SEQ, KV_WIDTH), seq),
        ],
        out_specs=pl.BlockSpec((TQ_S, ATTN_WIDTH), lambda b, i: (b * nq + i, 0)),
        out_shape=jax.ShapeDtypeStruct((N_S, ATTN_WIDTH), BF16),
        name="attn_latent",
    )(q, cache_k, cache_v, k, v)


def _pool_kernel(xp_ref, bdw_ref, sc_ref, o_ref, pad_ref, *, seq_len):
    half = POOL_WIDTH // 2
    zeros = jnp.zeros((POOL_PAD, POOL_WIDTH), F32)
    pad_ref[0:POOL_PAD, :] = zeros
    pad_ref[POOL_PAD + seq_len:, :] = zeros
    pad_ref[POOL_PAD:POOL_PAD + seq_len, :] = xp_ref[...]
    chunk = min(seq_len, 256)
    lane = lax.broadcasted_iota(jnp.int32, (chunk, half), 1)
    first = lane < POOL_GROUP_DIM
    for c in range(seq_len // chunk):
        base = c * chunk
        t = lax.broadcasted_iota(jnp.int32, (chunk, half), 0) + base

        def sh(j, lo):
            return pad_ref[POOL_PAD + base + j:POOL_PAD + base + j + chunk, lo:lo + half]

        def cnt(w):
            return (jnp.minimum(t + w // 2, seq_len) - jnp.maximum(t - w // 2, 0)).astype(F32)

        xa = sh(0, 0)
        w2 = sh(-1, 0) + xa
        w4 = w2 + sh(-2, 0) + sh(1, 0)
        xb = sh(0, half)
        w8 = xb
        for j in (-4, -3, -2, -1, 1, 2, 3):
            w8 = w8 + sh(j, half)
        w16 = w8
        for j in (-8, -7, -6, -5, 4, 5, 6, 7):
            w16 = w16 + sh(j, half)
        pa = jnp.where(first, w2 / cnt(2), w4 / cnt(4)) - xa
        pb = jnp.where(first, w8 / cnt(8), w16 / cnt(16)) - xb
        pooled = jnp.concatenate([pa, pb], axis=1).astype(BF16)
        o_ref[base:base + chunk, :] = (_dot(pooled, bdw_ref[...]) * sc_ref[...]).astype(BF16)


def _pool_call(l, xp, bdw, scale, seq_len, n_seq, blk0):
    per_layer = lambda b: (l, 0, 0)
    return pl.pallas_call(
        functools.partial(_pool_kernel, seq_len=seq_len),
        grid=(n_seq,),
        in_specs=[
            pl.BlockSpec((seq_len, POOL_WIDTH), lambda b: (blk0 + b, 0)),
            pl.BlockSpec((None, POOL_WIDTH, POOL_WIDTH), per_layer),
            pl.BlockSpec((None, 1, POOL_WIDTH), per_layer),
        ],
        out_specs=pl.BlockSpec((seq_len, POOL_WIDTH), lambda b: (b, 0)),
        out_shape=jax.ShapeDtypeStruct((n_seq * seq_len, POOL_WIDTH), BF16),
        scratch_shapes=[pltpu.VMEM((seq_len + 2 * POOL_PAD, POOL_WIDTH), F32)],
        name="pool_%d" % seq_len,
    )(xp, bdw, scale)


def _fourier_kernel(c_ref, s_ref, xc_ref, xs_ref, o_ref, *, scale):
    y = _dot(c_ref[...], xc_ref[...]) - _dot(s_ref[...], xs_ref[...])
    o_ref[...] = (y * scale).astype(BF16)


def _fourier_prompt_call(cmat, smat, xc, xs):
    blk = lambda b: (b, 0)
    const2 = lambda b: (0, 0)
    return pl.pallas_call(
        functools.partial(_fourier_kernel, scale=1.0 / math.sqrt(SEQ * FOURIER_GROUP_DIM)),
        grid=(BATCH,),
        in_specs=[
            pl.BlockSpec((SEQ, SEQ), const2),
            pl.BlockSpec((SEQ, SEQ), const2),
            pl.BlockSpec((SEQ, FOURIER_WIDTH), blk),
            pl.BlockSpec((SEQ, FOURIER_WIDTH), blk),
        ],
        out_specs=pl.BlockSpec((SEQ, FOURIER_WIDTH), blk),
        out_shape=jax.ShapeDtypeStruct((N_P, FOURIER_WIDTH), BF16),
        name="fourier_context",
    )(cmat, smat, xc, xs)


def _fourier_sample_call(cmat, smat, xc, xs):
    nt = DEC_SEQ // TF_S
    rows = lambda b, i: (i, 0)
    seq = lambda b, i: (N_P // DEC_SEQ + b, 0)
    out = lambda b, i: (b * nt + i, 0)
    return pl.pallas_call(
        functools.partial(_fourier_kernel, scale=1.0 / math.sqrt(DEC_SEQ * FOURIER_GROUP_DIM)),
        grid=(DEC_BATCH, nt),
        in_specs=[
            pl.BlockSpec((TF_S, DEC_SEQ), rows),
            pl.BlockSpec((TF_S, DEC_SEQ), rows),
            pl.BlockSpec((DEC_SEQ, FOURIER_WIDTH), seq),
            pl.BlockSpec((DEC_SEQ, FOURIER_WIDTH), seq),
        ],
        out_specs=pl.BlockSpec((TF_S, FOURIER_WIDTH), out),
        out_shape=jax.ShapeDtypeStruct((N_S, FOURIER_WIDTH), BF16),
        name="fourier_latent",
    )(cmat, smat, xc, xs)


def _merge_kernel(x_ref, attn_p_ref, attn_s_ref, pool_p_ref, pool_s_ref, four_p_ref, four_s_ref,
                  g_ref, mod_ref, gpost_ref, gffn_ref,
                  wa_ref, wp_ref, wf_ref, wo_ref, wr_ref, br_ref, tri_ref, upper_ref,
                  x1_ref, h2_ref, pos_ref, post_ref, mw_ref, cnt_ref):
    i = pl.program_id(0)
    mod = mod_ref[...]
    g1 = mod[:, 2 * D_MODEL:3 * D_MODEL]
    sh2 = mod[:, 3 * D_MODEL:4 * D_MODEL]
    sc2 = mod[:, 4 * D_MODEL:5 * D_MODEL]

    def gate(c):
        return jax.nn.sigmoid(g_ref[:, c * D_MODEL:(c + 1) * D_MODEL].astype(F32))

    def branch(p_ref, s_ref):
        return jnp.where(i < P_TILES, p_ref[...], s_ref[...])

    merged = gate(0) * _dot(branch(attn_p_ref, attn_s_ref), wa_ref[...])
    merged = merged + gate(1) * _dot(branch(pool_p_ref, pool_s_ref), wp_ref[...])
    merged = merged + gate(2) * _dot(branch(four_p_ref, four_s_ref), wf_ref[...])
    mix = _dot(merged.astype(BF16), wo_ref[...])
    x1 = x_ref[...] + g1 * (_rms(mix) * gpost_ref[...])
    x1_ref[...] = x1
    h2 = (_rms(x1) * gffn_ref[...]) * (1.0 + sc2) + sh2
    h2b = h2.astype(BF16)
    h2_ref[...] = h2b

    logits = _dot(h2b, wr_ref[...]) + br_ref[...]
    lane = lax.broadcasted_iota(jnp.int32, (TM, LANES), 1)
    lanef = lane.astype(F32)
    neg = jnp.float32(-3e38)
    big = jnp.float32(1e9)
    is_g = (lane >= N_EXPERTS) & (lane < N_EXPERTS + N_EXPERT_GROUPS)
    lg = jnp.where(is_g, logits, neg)
    gmax = jnp.max(lg, axis=1, keepdims=True)
    g_sel = jnp.min(jnp.where(lg == gmax, lanef - N_EXPERTS, big), axis=1, keepdims=True)
    p_g = 1.0 / jnp.sum(jnp.where(is_g, jnp.exp(logits - gmax), 0.0), axis=1, keepdims=True)
    grp = lax.shift_right_logical(lane, int(math.log2(EXPERTS_PER_GROUP))).astype(F32)
    in_grp = (lane < N_EXPERTS) & (grp == g_sel)
    le = jnp.where(in_grp, logits, neg)
    v1 = jnp.max(le, axis=1, keepdims=True)
    i1 = jnp.min(jnp.where(le == v1, lanef, big), axis=1, keepdims=True)
    le2 = jnp.where(lanef == i1, neg, le)
    v2 = jnp.max(le2, axis=1, keepdims=True)
    i2 = jnp.min(jnp.where(le2 == v2, lanef, big), axis=1, keepdims=True)
    e21 = jnp.exp(v2 - v1)
    w1 = p_g / (1.0 + e21)
    w2 = p_g * e21 / (1.0 + e21)

    oh1 = (lanef == i1).astype(F32)
    oh2 = (lanef == i2).astype(F32)
    ohb = (oh1 + oh2).astype(BF16)
    before = _dot(tri_ref[...], ohb)
    cnt = _dot(jnp.ones((MOD_ROWS, TM), BF16), ohb)
    cnt_pad = (lax.shift_right_logical(cnt.astype(jnp.int32) + (RUN_ALIGN - 1), RUN_SHIFT)
               * RUN_ALIGN).astype(F32)
    run_off = _dot(cnt_pad.astype(BF16), upper_ref[...])
    slot = run_off[0:1, :] + before
    p1 = jnp.sum(slot * oh1, axis=1, keepdims=True)
    p2 = jnp.sum(slot * oh2, axis=1, keepdims=True)
    cnt_ref[...] = cnt

    pos = jnp.where(lane == 0, p1, jnp.where(lane == 1, p2, 0.0))
    pos_ref[...] = pos.astype(jnp.int32)
    post_ref[...] = pos.T[0:MOD_ROWS, :].astype(jnp.int32)
    mw_ref[...] = jnp.where(lane == 0, w1, w2)


def _merge_call(l, x, attn, pool, four, gates, mods, gpost, gffn, wa, wp, wf, wo, wr, br, tri,
                upper):
    row = lambda i: (i, 0)
    prow = lambda i: (jnp.minimum(i, P_TILES - 1), 0)
    srow = lambda i: (jnp.maximum(i - P_TILES, 0), 0)
    const2 = lambda i: (0, 0)
    per_layer = lambda i: (l, 0, 0)
    return pl.pallas_call(
        _merge_kernel,
        grid=(N_TILES,),
        in_specs=[
            pl.BlockSpec((TM, D_MODEL), row),
            pl.BlockSpec((TM, ATTN_WIDTH), prow),
            pl.BlockSpec((TM, ATTN_WIDTH), srow),
            pl.BlockSpec((TM, POOL_WIDTH), prow),
            pl.BlockSpec((TM, POOL_WIDTH), srow),
            pl.BlockSpec((TM, FOURIER_WIDTH), prow),
            pl.BlockSpec((TM, FOURIER_WIDTH), srow),
            pl.BlockSpec((TM, N_BRANCHES * D_MODEL), row),
            pl.BlockSpec((None, None, 1, N_MOD * D_MODEL), lambda i: (l, _mod_row(i), 0, 0)),
            pl.BlockSpec((None, 1, D_MODEL), per_layer),
            pl.BlockSpec((None, 1, D_MODEL), per_layer),
            pl.BlockSpec((None, ATTN_WIDTH, D_MODEL), per_layer),
            pl.BlockSpec((None, POOL_WIDTH, D_MODEL), per_layer),
            pl.BlockSpec((None, FOURIER_WIDTH, D_MODEL), per_layer),
            pl.BlockSpec((None, D_MODEL, D_MODEL), per_layer),
            pl.BlockSpec((None, D_MODEL, LANES), per_layer),
            pl.BlockSpec((None, 1, LANES), per_layer),
            pl.BlockSpec((TM, TM), const2),
            pl.BlockSpec((LANES, LANES), const2),
        ],
        out_specs=[
            pl.BlockSpec((TM, D_MODEL), row),
            pl.BlockSpec((TM, D_MODEL), row),
            pl.BlockSpec((TM, LANES), row),
            pl.BlockSpec((MOD_ROWS, TM), lambda i: (0, i)),
            pl.BlockSpec((TM, LANES), row),
            pl.BlockSpec((MOD_ROWS, LANES), row),
        ],
        out_shape=[
            jax.ShapeDtypeStruct((N_TOK, D_MODEL), F32),
            jax.ShapeDtypeStruct((N_TOK, D_MODEL), BF16),
            jax.ShapeDtypeStruct((N_TOK, LANES), jnp.int32),
            jax.ShapeDtypeStruct((MOD_ROWS, N_TOK), jnp.int32),
            jax.ShapeDtypeStruct((N_TOK, LANES), F32),
            jax.ShapeDtypeStruct((N_TILES * MOD_ROWS, LANES), F32),
        ],
        name="merge_router",
    )(x, attn[0], attn[1], pool[0], pool[1], four[0], four[1], gates, mods, gpost, gffn,
      wa, wp, wf, wo, wr, br, tri, upper)


def _run_copies(cnt, src_ref, src_off, dst_ref, dst_off, sem, wait):
    off = jnp.int32(0)
    for b in reversed(range(RUN_BITS)):
        size = RUN_ALIGN << b
        take = lax.shift_right_logical(cnt, b) & 1

        @pl.when(take == 1)
        def _(off=off, size=size):
            cp = pltpu.make_async_copy(
                src_ref.at[pl.ds(pl.multiple_of(src_off + off, RUN_ALIGN), size)],
                dst_ref.at[pl.ds(pl.multiple_of(dst_off + off, RUN_ALIGN), size)], sem)
            if wait:
                cp.wait()
            else:
                cp.start()

        off = off + take * size


def _tile_runs(tile, cnt_ref, fn):
    def body(e, carry):
        r = tile * N_EXPERTS + e
        fn(r, cnt_ref[r])
        return carry

    lax.fori_loop(0, N_EXPERTS, body, 0)


def _dispatch_kernel(cnt_ref, toff_ref, xoff_ref, h_ref, post_ref, xs_in_ref, xs_ref, sorted_ref, sem):
    del xs_in_ref
    i = pl.program_id(0)
    slot = lax.rem(i, 2)
    rows = lax.broadcasted_iota(jnp.int32, (SORT_ROWS, TM), 0)
    p = post_ref[...]
    perm = jnp.where(rows == p[0:1, :], 1.0, jnp.where(rows == p[1:2, :], 1.0, 0.0)).astype(BF16)
    sorted_ref[slot] = _dot(perm, h_ref[...])

    def copies(tile, slot, wait):
        _tile_runs(tile, cnt_ref, lambda r, c: _run_copies(
            c, sorted_ref.at[slot], toff_ref[r], xs_ref, xoff_ref[r], sem.at[slot], wait))

    @pl.when(i > 0)
    def _():
        copies(i - 1, 1 - slot, True)

    copies(i, slot, False)

    @pl.when(i == pl.num_programs(0) - 1)
    def _():
        copies(i, slot, True)


def _dispatch_call(cnt, toff, xoff, h2, post, xs):
    return pl.pallas_call(
        _dispatch_kernel,
        grid_spec=pltpu.PrefetchScalarGridSpec(
            num_scalar_prefetch=3,
            grid=(N_TILES,),
            in_specs=[
                pl.BlockSpec((TM, D_MODEL), lambda i, *_: (i, 0)),
                pl.BlockSpec((MOD_ROWS, TM), lambda i, *_: (0, i)),
                pl.BlockSpec(memory_space=pl.ANY),
            ],
            out_specs=pl.BlockSpec(memory_space=pl.ANY),
            scratch_shapes=[pltpu.VMEM((2, SORT_ROWS, D_MODEL), F32), pltpu.SemaphoreType.DMA((2,))],
        ),
        out_shape=jax.ShapeDtypeStruct((XS_ROWS, D_MODEL), F32),
        input_output_aliases={5: 0},
        compiler_params=pltpu.CompilerParams(dimension_semantics=("arbitrary",)),
        name="moe_dispatch",
    )(cnt, toff, xoff, h2, post, xs)


def _expert_kernel(be_ref, nu_ref, x_ref, wg_ref, wu_ref, wd_ref, y_ref):
    b = pl.program_id(0)

    @pl.when(b < nu_ref[0])
    def _():
        xb = x_ref[...].astype(BF16)
        g = _dot(xb, wg_ref[...].astype(BF16))
        u = _dot(xb, wu_ref[...].astype(BF16))
        a = (g * jax.nn.sigmoid(g)) * u
        y_ref[...] = _dot(a.astype(BF16), wd_ref[...].astype(BF16))

    @pl.when(b >= nu_ref[0])
    def _():
        y_ref[...] = jnp.zeros_like(y_ref)


def _expert_call(l, block_expert, n_used, xs, w_gate, w_up, w_down):
    row = lambda b, be, nu: (b, 0)
    wsel = lambda b, be, nu: (l, be[b], 0, 0)
    return pl.pallas_call(
        _expert_kernel,
        grid_spec=pltpu.PrefetchScalarGridSpec(
            num_scalar_prefetch=2,
            grid=(N_EBLK,),
            in_specs=[
                pl.BlockSpec((EBLK, D_MODEL), row),
                pl.BlockSpec((None, None, D_MODEL, EXPERT_FF), wsel),
                pl.BlockSpec((None, None, D_MODEL, EXPERT_FF), wsel),
                pl.BlockSpec((None, None, EXPERT_FF, D_MODEL), wsel),
            ],
            out_specs=pl.BlockSpec((EBLK, D_MODEL), row),
        ),
        out_shape=jax.ShapeDtypeStruct((XS_ROWS, D_MODEL), F32),
        compiler_params=pltpu.CompilerParams(dimension_semantics=("arbitrary",)),
        name="moe_experts",
    )(block_expert, n_used, xs, w_gate, w_up, w_down)


def _combine_kernel(cnt_ref, toff_ref, xoff_ref, y_ref, x1_ref, pos_ref, mw_ref, mod_ref, gpost_ref,
                    o_ref, ybuf, sem):
    i = pl.program_id(0)
    slot = lax.rem(i, 2)

    def copies(tile, slot, wait):
        _tile_runs(tile, cnt_ref, lambda r, c: _run_copies(
            c, y_ref, xoff_ref[r], ybuf.at[slot], toff_ref[r], sem.at[slot], wait))

    @pl.when(i == 0)
    def _():
        ybuf[...] = jnp.zeros_like(ybuf)
        copies(0, 0, False)

    @pl.when(i + 1 < pl.num_programs(0))
    def _():
        copies(i + 1, 1 - slot, False)

    copies(i, slot, True)

    pos = pos_ref[...]
    mw = mw_ref[...]
    cols = lax.broadcasted_iota(jnp.int32, (TM, SORT_ROWS), 1)
    qw = (jnp.where(cols == pos[:, 0:1], mw[:, 0:1], 0.0)
          + jnp.where(cols == pos[:, 1:2], mw[:, 1:2], 0.0)).astype(BF16)
    ffn = _dot(qw, ybuf[slot].astype(BF16))
    g2 = mod_ref[...][:, 5 * D_MODEL:6 * D_MODEL]
    o_ref[...] = x1_ref[...] + g2 * (_rms(ffn) * gpost_ref[...])


def _combine_call(l, cnt, toff, xoff, y, x1, pos, mw, mods, gpost):
    row = lambda i, *_: (i, 0)
    return pl.pallas_call(
        _combine_kernel,
        grid_spec=pltpu.PrefetchScalarGridSpec(
            num_scalar_prefetch=3,
            grid=(N_TILES,),
            in_specs=[
                pl.BlockSpec(memory_space=pl.ANY),
                pl.BlockSpec((TM, D_MODEL), row),
                pl.BlockSpec((TM, LANES), row),
                pl.BlockSpec((TM, LANES), row),
                pl.BlockSpec((None, None, 1, N_MOD * D_MODEL), lambda i, *_: (l, _mod_row(i), 0, 0)),
                pl.BlockSpec((None, 1, D_MODEL), lambda i, *_: (l, 0, 0)),
            ],
            out_specs=pl.BlockSpec((TM, D_MODEL), row),
            scratch_shapes=[pltpu.VMEM((2, SORT_ROWS, D_MODEL), F32), pltpu.SemaphoreType.DMA((2,))],
        ),
        out_shape=jax.ShapeDtypeStruct((N_TOK, D_MODEL), F32),
        compiler_params=pltpu.CompilerParams(dimension_semantics=("arbitrary",)),
        name="moe_combine",
    )(cnt, toff, xoff, y, x1, pos, mw, mods, gpost)


def _dft_mats(n):
    k = np.arange(n, dtype=np.int64)
    ang = 2.0 * np.pi * ((k[:, None] * k[None, :]) % n).astype(np.float64) / n
    return np.cos(ang), np.sin(ang)


def _block_diag(m, reps):
    n = m.shape[0]
    out = np.zeros((n * reps, n * reps), m.dtype)
    for r in range(reps):
        out[r * n:(r + 1) * n, r * n:(r + 1) * n] = m
    return out


def _rope_tables():
    t = np.arange(DEC_SEQ)
    pos = np.stack([t // GRID_W, t % GRID_W], axis=1).astype(np.float64)
    n_freq = ROPE_AXIS_DIM // 2
    inv = ROPE_BASE ** (-np.arange(n_freq, dtype=np.float64) * 2.0 / ROPE_AXIS_DIM)
    ang = pos[:, :, None] * inv[None, None, :]
    cos = np.cos(ang)
    sin = np.sin(ang)
    zero = np.zeros_like(sin[:, 0])
    cos_h = np.concatenate([cos[:, 0], cos[:, 0], cos[:, 1], cos[:, 1]], axis=1)
    s1_h = np.concatenate([-sin[:, 0], zero, -sin[:, 1], zero], axis=1)
    s2_h = np.concatenate([zero, sin[:, 0], zero, sin[:, 1]], axis=1)
    reps = LANES // HEAD_DIM

    def table(a, ident):
        a = np.tile(a, (1, reps))
        pad = np.full((TM, LANES), ident, np.float64)
        return jnp.asarray(np.concatenate([a, pad], axis=0), F32)

    return table(cos_h, 1.0), table(s1_h, 0.0), table(s2_h, 0.0)


def kernel(x_prompt, x_sample, cache_k, cache_v, c, c_ctx, w_ada, b_ada, norm_mix_pre,
           norm_mix_post, norm_ffn_pre, norm_ffn_post, w_in, q_norm, k_norm, w_attn_out,
           w_pool_group, pool_scale, w_pool_out, w_fourier_out, w_out, w_router_group,
           b_router_group, w_router_expert, b_router_expert, w_expert_gate, w_expert_up,
           w_expert_down):
    cos_t, s1_t, s2_t = _rope_tables()
    avg = jnp.asarray(_block_diag(np.full((HEAD_DIM, HEAD_DIM), 1.0 / HEAD_DIM), LANES // HEAD_DIM), BF16)
    c64, s64 = _dft_mats(FOURIER_GROUP_DIM)
    n_fg = FOURIER_WIDTH // FOURIER_GROUP_DIM
    dft_ch = jnp.asarray(np.concatenate([_block_diag(c64, n_fg), _block_diag(s64, n_fg)], axis=1), BF16)
    cp, sp = _dft_mats(SEQ)
    cp, sp = jnp.asarray(cp, BF16), jnp.asarray(sp, BF16)
    cl, sl = _dft_mats(DEC_SEQ)
    cl, sl = jnp.asarray(cl, BF16), jnp.asarray(sl, BF16)
    tri = jnp.asarray(np.tril(np.ones((TM, TM)), -1), BF16)
    upper = jnp.asarray(np.triu(np.ones((LANES, LANES)), 1), BF16)

    w_in_b = w_in.astype(BF16)
    wa_b = w_attn_out.astype(BF16)
    wp_b = w_pool_out.astype(BF16)
    wf_b = w_fourier_out.astype(BF16)
    wo_b = w_out.astype(BF16)
    pad_r = jnp.zeros((DEPTH, D_MODEL, LANES - N_EXPERTS - N_EXPERT_GROUPS), F32)
    wr_b = jnp.concatenate([w_router_expert, w_router_group, pad_r], axis=2).astype(BF16)
    br = jnp.concatenate([b_router_expert, b_router_group,
                          jnp.zeros((DEPTH, LANES - N_EXPERTS - N_EXPERT_GROUPS), F32)],
                         axis=1).reshape(DEPTH, 1, LANES)
    n_pg = POOL_WIDTH // POOL_GROUP_DIM
    bdw = jnp.zeros((DEPTH, POOL_WIDTH, POOL_WIDTH), F32)
    for g in range(n_pg):
        lo = g * POOL_GROUP_DIM
        bdw = bdw.at[:, lo:lo + POOL_GROUP_DIM, lo:lo + POOL_GROUP_DIM].set(w_pool_group[:, g])
    bdw = bdw.astype(BF16)
    pscale = pool_scale.reshape(DEPTH, 1, POOL_WIDTH)
    qg = jnp.tile(q_norm, (1, LANES // HEAD_DIM)).reshape(DEPTH, 1, LANES)
    kg = jnp.tile(k_norm, (1, LANES // HEAD_DIM)).reshape(DEPTH, 1, LANES)
    gpre = norm_mix_pre.reshape(DEPTH, 1, D_MODEL)
    gpost = norm_mix_post.reshape(DEPTH, 1, D_MODEL)
    gffn = norm_ffn_pre.reshape(DEPTH, 1, D_MODEL)
    gfpost = norm_ffn_post.reshape(DEPTH, 1, D_MODEL)
    ck = cache_k.reshape(DEC_BATCH, DEPTH, PAST_LEN, KV_WIDTH)
    cv = cache_v.reshape(DEC_BATCH, DEPTH, PAST_LEN, KV_WIDTH)

    c_all = jnp.concatenate([c_ctx[None, :], c, jnp.zeros((MOD_ROWS - 1 - DEC_BATCH, D_MODEL), F32)], axis=0)
    mods = _mod_call(c_all, w_ada, b_ada).reshape(DEPTH, MOD_ROWS, 1, N_MOD * D_MODEL)

    x = jnp.concatenate([x_prompt.reshape(N_P, D_MODEL), x_sample.reshape(N_S, D_MODEL)], axis=0)
    xs_buf = jnp.zeros((XS_ROWS, D_MODEL), F32)
    new_k, new_v = [], []
    for l in range(DEPTH):
        q, k, v, xp, xc, xsn, gates = _proj_call(l, x, mods, gpre, w_in_b, qg, kg,
                                                 cos_t, s1_t, s2_t, avg, dft_ch)
        new_k.append(k[:N_P].reshape(BATCH, SEQ, N_KV_HEADS, HEAD_DIM))
        new_v.append(v[:N_P].reshape(BATCH, SEQ, N_KV_HEADS, HEAD_DIM))
        attn = (_attn_prompt_call(q, k, v), _attn_sample_call(l, q, k, v, ck, cv))
        pool = (_pool_call(l, xp, bdw, pscale, SEQ, BATCH, 0),
                _pool_call(l, xp, bdw, pscale, DEC_SEQ, DEC_BATCH, N_P // DEC_SEQ))
        four = (_fourier_prompt_call(cp, sp, xc, xsn), _fourier_sample_call(cl, sl, xc, xsn))
        x1, h2, pos, post, mw, cnt = _merge_call(l, x, attn, pool, four, gates, mods, gpost, gffn,
                                                 wa_b, wp_b, wf_b, wo_b, wr_b, br, tri, upper)
        runs = cnt.reshape(N_TILES, MOD_ROWS, LANES)[:, 0, :N_EXPERTS].astype(jnp.int32)
        runs = ((runs + RUN_ALIGN - 1) // RUN_ALIGN) * RUN_ALIGN
        tile_off = jnp.cumsum(runs, axis=1) - runs
        rows_e = jnp.sum(runs, axis=0)
        padded = ((rows_e + EBLK - 1) // EBLK) * EBLK
        pad_end = jnp.cumsum(padded)
        xs_off = (pad_end - padded)[None, :] + jnp.cumsum(runs, axis=0) - runs
        block_expert = jnp.minimum(
            jnp.searchsorted(pad_end, jnp.arange(N_EBLK, dtype=jnp.int32) * EBLK, side='right'),
            N_EXPERTS - 1).astype(jnp.int32)
        n_used = (pad_end[-1:] // EBLK).astype(jnp.int32)
        run_cnt = (runs // RUN_ALIGN).reshape(N_RUNS)
        tile_off = tile_off.reshape(N_RUNS)
        xs_off = xs_off.reshape(N_RUNS)
        xs_buf = _dispatch_call(run_cnt, tile_off, xs_off, h2, post, xs_buf)
        y = _expert_call(l, block_expert, n_used, xs_buf, w_expert_gate, w_expert_up, w_expert_down)
        x = _combine_call(l, run_cnt, tile_off, xs_off, y, x1, pos, mw, mods, gfpost)

    y_prompt = x[:N_P].reshape(BATCH, SEQ, D_MODEL)
    y_sample = x[N_P:].reshape(DEC_BATCH, DEC_SEQ, D_MODEL)
    return (y_prompt, y_sample, jnp.stack(new_k, axis=1), jnp.stack(new_v, axis=1))
```

```python
import functools
import math

import numpy as np
import jax
import jax.numpy as jnp
from jax import lax
from jax.experimental import pallas as pl
from jax.experimental.pallas import tpu as pltpu

F32 = jnp.float32
BF16 = jnp.bfloat16

D_MODEL = 1024
BATCH = 32
SEQ = 256
DEPTH = 4
DEC_BATCH = 2
DEC_SEQ = 2048
PAST_LEN = 512
GRID_W = 64
N_HEADS = 8
N_KV_HEADS = 2
HEAD_DIM = 64
KV_GROUP = N_HEADS // N_KV_HEADS
ATTN_WIDTH = N_HEADS * HEAD_DIM
KV_WIDTH = N_KV_HEADS * HEAD_DIM
ROPE_AXIS_DIM = HEAD_DIM // 2
ROPE_BASE = 10000.0
POOL_WINDOWS = (2, 4, 8, 16)
POOL_WIDTH = 256
POOL_GROUP_DIM = 64
FOURIER_WIDTH = 256
FOURIER_GROUP_DIM = 64
N_BRANCHES = 3
OFF_K = ATTN_WIDTH
OFF_P = ATTN_WIDTH + 2 * KV_WIDTH
OFF_G = OFF_P + POOL_WIDTH + FOURIER_WIDTH
IN_WIDTH = OFF_G + N_BRANCHES * D_MODEL
N_EXPERT_GROUPS = 4
EXPERTS_PER_GROUP = 8
N_EXPERTS = N_EXPERT_GROUPS * EXPERTS_PER_GROUP
TOP_K = 2
EXPERT_FF = 256
N_MOD = 6
RMS_EPS = 1e-6

N_P = BATCH * SEQ
N_S = DEC_BATCH * DEC_SEQ
N_TOK = N_P + N_S
LANES = 128
MOD_ROWS = 8
POOL_PAD = 16

TM = 512
N_TILES = N_TOK // TM
P_TILES = N_P // TM
S_TILES_PER_SEQ = DEC_SEQ // TM
TQ_S = 128
TF_S = 512
EBLK = 256
RUN_ALIGN = 8
RUN_SHIFT = 3
RUN_BITS = 7
assert RUN_ALIGN << (RUN_BITS - 1) == TM
SORT_ROWS = TOP_K * TM + N_EXPERTS * RUN_ALIGN
N_RUNS = N_TILES * N_EXPERTS
N_EBLK = (N_TOK * TOP_K + N_RUNS * (RUN_ALIGN - 1) + N_EXPERTS * (EBLK - 1) + EBLK - 1) // EBLK
XS_ROWS = N_EBLK * EBLK
MOD_NT = 1536


def _dot(a, b):
    return jnp.dot(a, b, preferred_element_type=F32)


def _rms(x):
    return x * lax.rsqrt(jnp.mean(x * x, axis=-1, keepdims=True) + RMS_EPS)


def _mod_row(i):
    return jnp.where(i < P_TILES, 0, 1 + (i - P_TILES) // S_TILES_PER_SEQ)


def _rope_block(i):
    return jnp.where(i < P_TILES, S_TILES_PER_SEQ, (i - P_TILES) % S_TILES_PER_SEQ)


def _mod_kernel(c_ref, w_ref, b_ref, o_ref):
    c = c_ref[...]
    s = (c * jax.nn.sigmoid(c)).astype(BF16)
    o_ref[...] = _dot(s, w_ref[...].astype(BF16)) + b_ref[...]


def _mod_call(c_all, w_ada, b_ada):
    nt = (N_MOD * D_MODEL) // MOD_NT
    return pl.pallas_call(
        _mod_kernel,
        grid=(DEPTH, nt),
        in_specs=[
            pl.BlockSpec((MOD_ROWS, D_MODEL), lambda l, j: (0, 0)),
            pl.BlockSpec((None, D_MODEL, MOD_NT), lambda l, j: (l, 0, j)),
            pl.BlockSpec((None, 1, MOD_NT), lambda l, j: (l, 0, j)),
        ],
        out_specs=pl.BlockSpec((None, MOD_ROWS, MOD_NT), lambda l, j: (l, 0, j)),
        out_shape=jax.ShapeDtypeStruct((DEPTH, MOD_ROWS, N_MOD * D_MODEL), F32),
        name="adaln_mod",
    )(c_all, w_ada, b_ada.reshape(DEPTH, 1, N_MOD * D_MODEL))


def _proj_kernel(x_ref, mod_ref, gpre_ref, w_ref, qg_ref, kg_ref, cos_ref, s1_ref, s2_ref,
                 avg_ref, dft_ref, q_ref, k_ref, v_ref, xp_ref, xc_ref, xs_ref, g_ref):
    mod = mod_ref[...]
    sh1 = mod[:, 0:D_MODEL]
    sc1 = mod[:, D_MODEL:2 * D_MODEL]
    h = (_rms(x_ref[...]) * gpre_ref[...]) * (1.0 + sc1) + sh1
    hb = h.astype(BF16)

    cos = cos_ref[...]
    s1 = s1_ref[...]
    s2 = s2_ref[...]

    def rope(t):
        return (t * cos + pltpu.roll(t, LANES - ROPE_AXIS_DIM // 2, 1) * s1
                + pltpu.roll(t, ROPE_AXIS_DIM // 2, 1) * s2)

    avg = avg_ref[...]
    for c in range(ATTN_WIDTH // LANES):
        lo = c * LANES
        q = _dot(hb, w_ref[:, lo:lo + LANES])
        ms = _dot((q * q).astype(BF16), avg)
        q = q * lax.rsqrt(ms + RMS_EPS) * qg_ref[...]
        q_ref[:, lo:lo + LANES] = rope(q).astype(BF16)

    kv = _dot(hb, w_ref[:, OFF_K:OFF_K + 2 * KV_WIDTH])
    k = kv[:, 0:KV_WIDTH]
    ms = _dot((k * k).astype(BF16), avg)
    k = k * lax.rsqrt(ms + RMS_EPS) * kg_ref[...]
    k_ref[...] = rope(k)
    v_ref[...] = kv[:, KV_WIDTH:]

    pf = _dot(hb, w_ref[:, OFF_P:OFF_G])
    xp_ref[...] = pf[:, 0:POOL_WIDTH]
    cs = _dot(pf[:, POOL_WIDTH:].astype(BF16), dft_ref[...])
    xc_ref[...] = cs[:, 0:FOURIER_WIDTH].astype(BF16)
    xs_ref[...] = cs[:, FOURIER_WIDTH:].astype(BF16)

    for c in range(N_BRANCHES):
        lo = OFF_G + c * D_MODEL
        g_ref[:, c * D_MODEL:(c + 1) * D_MODEL] = _dot(hb, w_ref[:, lo:lo + D_MODEL]).astype(BF16)


def _proj_call(l, x, mods, gpre, w_in, qg, kg, cos_t, s1_t, s2_t, avg, dft):
    row = lambda i: (i, 0)
    const2 = lambda i: (0, 0)
    per_layer = lambda i: (l, 0, 0)
    rope_spec = pl.BlockSpec((TM, LANES), lambda i: (_rope_block(i), 0))
    return pl.pallas_call(
        _proj_kernel,
        grid=(N_TILES,),
        in_specs=[
            pl.BlockSpec((TM, D_MODEL), row),
            pl.BlockSpec((None, None, 1, N_MOD * D_MODEL), lambda i: (l, _mod_row(i), 0, 0)),
            pl.BlockSpec((None, 1, D_MODEL), per_layer),
            pl.BlockSpec((None, D_MODEL, IN_WIDTH), per_layer),
            pl.BlockSpec((None, 1, LANES), per_layer),
            pl.BlockSpec((None, 1, LANES), per_layer),
            rope_spec, rope_spec, rope_spec,
            pl.BlockSpec((LANES, LANES), const2),
            pl.BlockSpec((FOURIER_WIDTH, 2 * FOURIER_WIDTH), const2),
        ],
        out_specs=[
            pl.BlockSpec((TM, ATTN_WIDTH), row),
            pl.BlockSpec((TM, KV_WIDTH), row),
            pl.BlockSpec((TM, KV_WIDTH), row),
            pl.BlockSpec((TM, POOL_WIDTH), row),
            pl.BlockSpec((TM, FOURIER_WIDTH), row),
            pl.BlockSpec((TM, FOURIER_WIDTH), row),
            pl.BlockSpec((TM, N_BRANCHES * D_MODEL), row),
        ],
        out_shape=[
            jax.ShapeDtypeStruct((N_TOK, ATTN_WIDTH), BF16),
            jax.ShapeDtypeStruct((N_TOK, KV_WIDTH), F32),
            jax.ShapeDtypeStruct((N_TOK, KV_WIDTH), F32),
            jax.ShapeDtypeStruct((N_TOK, POOL_WIDTH), F32),
            jax.ShapeDtypeStruct((N_TOK, FOURIER_WIDTH), BF16),
            jax.ShapeDtypeStruct((N_TOK, FOURIER_WIDTH), BF16),
            jax.ShapeDtypeStruct((N_TOK, N_BRANCHES * D_MODEL), BF16),
        ],
        name="proj",
    )(x, mods, gpre, w_in, qg, kg, cos_t, s1_t, s2_t, avg, dft)


def _attn_kernel(*refs, n_parts, tq):
    q_ref = refs[0]
    kv_refs = refs[1:1 + 2 * n_parts]
    o_ref = refs[-1]
    scale = HEAD_DIM ** -0.5
    outs = []
    for j in range(N_KV_HEADS):
        lo = j * HEAD_DIM
        qs = jnp.concatenate(
            [q_ref[:, (KV_GROUP * j + g) * HEAD_DIM:(KV_GROUP * j + g + 1) * HEAD_DIM]
             for g in range(KV_GROUP)], axis=0)
        scores = []
        for p in range(n_parts):
            kp = kv_refs[2 * p][:, lo:lo + HEAD_DIM].astype(BF16)
            scores.append(lax.dot_general(qs, kp, (((1,), (1,)), ((), ())),
                                          preferred_element_type=F32))
        m = jnp.max(scores[0], axis=1, keepdims=True)
        for s in scores[1:]:
            m = jnp.maximum(m, jnp.max(s, axis=1, keepdims=True))
        acc = None
        den = None
        for p in range(n_parts):
            e = jnp.exp((scores[p] - m) * scale)
            vp = kv_refs[2 * p + 1][:, lo:lo + HEAD_DIM].astype(BF16)
            pv = _dot(e.astype(BF16), vp)
            es = jnp.sum(e, axis=1, keepdims=True)
            acc = pv if acc is None else acc + pv
            den = es if den is None else den + es
        o = acc / den
        outs.extend(o[g * tq:(g + 1) * tq] for g in range(KV_GROUP))
    o_ref[...] = jnp.concatenate(outs, axis=1).astype(BF16)


def _attn_prompt_call(q, k, v):
    blk = lambda b: (b, 0)
    return pl.pallas_call(
        functools.partial(_attn_kernel, n_parts=1, tq=SEQ),
        grid=(BATCH,),
        in_specs=[
            pl.BlockSpec((SEQ, ATTN_WIDTH), blk),
            pl.BlockSpec((SEQ, KV_WIDTH), blk),
            pl.BlockSpec((SEQ, KV_WIDTH), blk),
        ],
        out_specs=pl.BlockSpec((SEQ, ATTN_WIDTH), blk),
        out_shape=jax.ShapeDtypeStruct((N_P, ATTN_WIDTH), BF16),
        name="attn_context",
    )(q, k, v)


def _attn_sample_call(l, q, k, v, cache_k, cache_v):
    nq = DEC_SEQ // TQ_S
    qrow = lambda b, i: (N_P // TQ_S + b * nq + i, 0)
    seq = lambda b, i: (N_P // DEC_SEQ + b, 0)
    cache = lambda b, i: (b, l, 0, 0)
    return pl.pallas_call(
        functools.partial(_attn_kernel, n_parts=2, tq=TQ_S),
        grid=(DEC_BATCH, nq),
        in_specs=[
            pl.BlockSpec((TQ_S, ATTN_WIDTH), qrow),
            pl.BlockSpec((None, None, PAST_LEN, KV_WIDTH), cache),
            pl.BlockSpec((None, None, PAST_LEN, KV_WIDTH), cache),
            pl.BlockSpec((DEC_SEQ, KV_WIDTH), seq),
            pl.BlockSpec((DEC_SEQ, KV_WIDTH), seq),
        ],
        out_specs=pl.BlockSpec((TQ_S, ATTN_WIDTH), lambda b, i: (b * nq + i, 0)),
        out_shape=jax.ShapeDtypeStruct((N_S, ATTN_WIDTH), BF16),
        name="attn_latent",
    )(q, cache_k, cache_v, k, v)


def _pool_kernel(xp_ref, bdw_ref, sc_ref, o_ref, pad_ref, *, seq_len):
    half = POOL_WIDTH // 2
    zeros = jnp.zeros((POOL_PAD, POOL_WIDTH), F32)
    pad_ref[0:POOL_PAD, :] = zeros
    pad_ref[POOL_PAD + seq_len:, :] = zeros
    pad_ref[POOL_PAD:POOL_PAD + seq_len, :] = xp_ref[...]
    chunk = min(seq_len, 256)
    lane = lax.broadcasted_iota(jnp.int32, (chunk, half), 1)
    first = lane < POOL_GROUP_DIM
    for c in range(seq_len // chunk):
        base = c * chunk
        t = lax.broadcasted_iota(jnp.int32, (chunk, half), 0) + base

        def sh(j, lo):
            return pad_ref[POOL_PAD + base + j:POOL_PAD + base + j + chunk, lo:lo + half]

        def cnt(w):
            return (jnp.minimum(t + w // 2, seq_len) - jnp.maximum(t - w // 2, 0)).astype(F32)

        xa = sh(0, 0)
        w2 = sh(-1, 0) + xa
        w4 = w2 + sh(-2, 0) + sh(1, 0)
        xb = sh(0, half)
        w8 = xb
        for j in (-4, -3, -2, -1, 1, 2, 3):
            w8 = w8 + sh(j, half)
        w16 = w8
        for j in (-8, -7, -6, -5, 4, 5, 6, 7):
            w16 = w16 + sh(j, half)
        pa = jnp.where(first, w2 / cnt(2), w4 / cnt(4)) - xa
        pb = jnp.where(first, w8 / cnt(8), w16 / cnt(16)) - xb
        pooled = jnp.concatenate([pa, pb], axis=1).astype(BF16)
        o_ref[base:base + chunk, :] = (_dot(pooled, bdw_ref[...]) * sc_ref[...]).astype(BF16)


def _pool_call(l, xp, bdw, scale, seq_len, n_seq, blk0):
    per_layer = lambda b: (l, 0, 0)
    return pl.pallas_call(
        functools.partial(_pool_kernel, seq_len=seq_len),
        grid=(n_seq,),
        in_specs=[
            pl.BlockSpec((seq_len, POOL_WIDTH), lambda b: (blk0 + b, 0)),
            pl.BlockSpec((None, POOL_WIDTH, POOL_WIDTH), per_layer),
            pl.BlockSpec((None, 1, POOL_WIDTH), per_layer),
        ],
        out_specs=pl.BlockSpec((seq_len, POOL_WIDTH), lambda b: (b, 0)),
        out_shape=jax.ShapeDtypeStruct((n_seq * seq_len, POOL_WIDTH), BF16),
        scratch_shapes=[pltpu.VMEM((seq_len + 2 * POOL_PAD, POOL_WIDTH), F32)],
        name="pool_%d" % seq_len,
    )(xp, bdw, scale)


def _fourier_kernel(c_ref, s_ref, xc_ref, xs_ref, o_ref, *, scale):
    y = _dot(c_ref[...], xc_ref[...]) - _dot(s_ref[...], xs_ref[...])
    o_ref[...] = (y * scale).astype(BF16)


def _fourier_prompt_call(cmat, smat, xc, xs):
    blk = lambda b: (b, 0)
    const2 = lambda b: (0, 0)
    return pl.pallas_call(
        functools.partial(_fourier_kernel, scale=1.0 / math.sqrt(SEQ * FOURIER_GROUP_DIM)),
        grid=(BATCH,),
        in_specs=[
            pl.BlockSpec((SEQ, SEQ), const2),
            pl.BlockSpec((SEQ, SEQ), const2),
            pl.BlockSpec((SEQ, FOURIER_WIDTH), blk),
            pl.BlockSpec((SEQ, FOURIER_WIDTH), blk),
        ],
        out_specs=pl.BlockSpec((SEQ, FOURIER_WIDTH), blk),
        out_shape=jax.ShapeDtypeStruct((N_P, FOURIER_WIDTH), BF16),
        name="fourier_context",
    )(cmat, smat, xc, xs)


def _fourier_sample_call(cmat, smat, xc, xs):
    nt = DEC_SEQ // TF_S
    rows = lambda b, i: (i, 0)
    seq = lambda b, i: (N_P // DEC_SEQ + b, 0)
    out = lambda b, i: (b * nt + i, 0)
    return pl.pallas_call(
        functools.partial(_fourier_kernel, scale=1.0 / math.sqrt(DEC_SEQ * FOURIER_GROUP_DIM)),
        grid=(DEC_BATCH, nt),
        in_specs=[
            pl.BlockSpec((TF_S, DEC_SEQ), rows),
            pl.BlockSpec((TF_S, DEC_SEQ), rows),
            pl.BlockSpec((DEC_SEQ, FOURIER_WIDTH), seq),
            pl.BlockSpec((DEC_SEQ, FOURIER_WIDTH), seq),
        ],
        out_specs=pl.BlockSpec((TF_S, FOURIER_WIDTH), out),
        out_shape=jax.ShapeDtypeStruct((N_S, FOURIER_WIDTH), BF16),
        name="fourier_latent",
    )(cmat, smat, xc, xs)


def _merge_kernel(x_ref, attn_p_ref, attn_s_ref, pool_p_ref, pool_s_ref, four_p_ref, four_s_ref,
                  g_ref, mod_ref, gpost_ref, gffn_ref,
                  wa_ref, wp_ref, wf_ref, wo_ref, wr_ref, br_ref, tri_ref, upper_ref,
                  x1_ref, h2_ref, pos_ref, post_ref, mw_ref, cnt_ref):
    i = pl.program_id(0)
    mod = mod_ref[...]
    g1 = mod[:, 2 * D_MODEL:3 * D_MODEL]
    sh2 = mod[:, 3 * D_MODEL:4 * D_MODEL]
    sc2 = mod[:, 4 * D_MODEL:5 * D_MODEL]

    def gate(c):
        return jax.nn.sigmoid(g_ref[:, c * D_MODEL:(c + 1) * D_MODEL].astype(F32))

    def branch(p_ref, s_ref):
        return jnp.where(i < P_TILES, p_ref[...], s_ref[...])

    merged = gate(0) * _dot(branch(attn_p_ref, attn_s_ref), wa_ref[...])
    merged = merged + gate(1) * _dot(branch(pool_p_ref, pool_s_ref), wp_ref[...])
    merged = merged + gate(2) * _dot(branch(four_p_ref, four_s_ref), wf_ref[...])
    mix = _dot(merged.astype(BF16), wo_ref[...])
    x1 = x_ref[...] + g1 * (_rms(mix) * gpost_ref[...])
    x1_ref[...] = x1
    h2 = (_rms(x1) * gffn_ref[...]) * (1.0 + sc2) + sh2
    h2b = h2.astype(BF16)
    h2_ref[...] = h2b

    logits = _dot(h2b, wr_ref[...]) + br_ref[...]
    lane = lax.broadcasted_iota(jnp.int32, (TM, LANES), 1)
    lanef = lane.astype(F32)
    neg = jnp.float32(-3e38)
    big = jnp.float32(1e9)
    is_g = (lane >= N_EXPERTS) & (lane < N_EXPERTS + N_EXPERT_GROUPS)
    lg = jnp.where(is_g, logits, neg)
    gmax = jnp.max(lg, axis=1, keepdims=True)
    g_sel = jnp.min(jnp.where(lg == gmax, lanef - N_EXPERTS, big), axis=1, keepdims=True)
    p_g = 1.0 / jnp.sum(jnp.where(is_g, jnp.exp(logits - gmax), 0.0), axis=1, keepdims=True)
    grp = lax.shift_right_logical(lane, int(math.log2(EXPERTS_PER_GROUP))).astype(F32)
    in_grp = (lane < N_EXPERTS) & (grp == g_sel)
    le = jnp.where(in_grp, logits, neg)
    v1 = jnp.max(le, axis=1, keepdims=True)
    i1 = jnp.min(jnp.where(le == v1, lanef, big), axis=1, keepdims=True)
    le2 = jnp.where(lanef == i1, neg, le)
    v2 = jnp.max(le2, axis=1, keepdims=True)
    i2 = jnp.min(jnp.where(le2 == v2, lanef, big), axis=1, keepdims=True)
    e21 = jnp.exp(v2 - v1)
    w1 = p_g / (1.0 + e21)
    w2 = p_g * e21 / (1.0 + e21)

    oh1 = (lanef == i1).astype(F32)
    oh2 = (lanef == i2).astype(F32)
    ohb = (oh1 + oh2).astype(BF16)
    before = _dot(tri_ref[...], ohb)
    cnt = _dot(jnp.ones((MOD_ROWS, TM), BF16), ohb)
    cnt_pad = (lax.shift_right_logical(cnt.astype(jnp.int32) + (RUN_ALIGN - 1), RUN_SHIFT)
               * RUN_ALIGN).astype(F32)
    run_off = _dot(cnt_pad.astype(BF16), upper_ref[...])
    slot = run_off[0:1, :] + before
    p1 = jnp.sum(slot * oh1, axis=1, keepdims=True)
    p2 = jnp.sum(slot * oh2, axis=1, keepdims=True)
    cnt_ref[...] = cnt

    pos = jnp.where(lane == 0, p1, jnp.where(lane == 1, p2, 0.0))
    pos_ref[...] = pos.astype(jnp.int32)
    post_ref[...] = pos.T[0:MOD_ROWS, :].astype(jnp.int32)
    mw_ref[...] = jnp.where(lane == 0, w1, w2)


def _merge_call(l, x, attn, pool, four, gates, mods, gpost, gffn, wa, wp, wf, wo, wr, br, tri,
                upper):
    row = lambda i: (i, 0)
    prow = lambda i: (jnp.minimum(i, P_TILES - 1), 0)
    srow = lambda i: (jnp.maximum(i - P_TILES, 0), 0)
    const2 = lambda i: (0, 0)
    per_layer = lambda i: (l, 0, 0)
    return pl.pallas_call(
        _merge_kernel,
        grid=(N_TILES,),
        in_specs=[
            pl.BlockSpec((TM, D_MODEL), row),
            pl.BlockSpec((TM, ATTN_WIDTH), prow),
            pl.BlockSpec((TM, ATTN_WIDTH), srow),
            pl.BlockSpec((TM, POOL_WIDTH), prow),
            pl.BlockSpec((TM, POOL_WIDTH), srow),
            pl.BlockSpec((TM, FOURIER_WIDTH), prow),
            pl.BlockSpec((TM, FOURIER_WIDTH), srow),
            pl.BlockSpec((TM, N_BRANCHES * D_MODEL), row),
            pl.BlockSpec((None, None, 1, N_MOD * D_MODEL), lambda i: (l, _mod_row(i), 0, 0)),
            pl.BlockSpec((None, 1, D_MODEL), per_layer),
            pl.BlockSpec((None, 1, D_MODEL), per_layer),
            pl.BlockSpec((None, ATTN_WIDTH, D_MODEL), per_layer),
            pl.BlockSpec((None, POOL_WIDTH, D_MODEL), per_layer),
            pl.BlockSpec((None, FOURIER_WIDTH, D_MODEL), per_layer),
            pl.BlockSpec((None, D_MODEL, D_MODEL), per_layer),
            pl.BlockSpec((None, D_MODEL, LANES), per_layer),
            pl.BlockSpec((None, 1, LANES), per_layer),
            pl.BlockSpec((TM, TM), const2),
            pl.BlockSpec((LANES, LANES), const2),
        ],
        out_specs=[
            pl.BlockSpec((TM, D_MODEL), row),
            pl.BlockSpec((TM, D_MODEL), row),
            pl.BlockSpec((TM, LANES), row),
            pl.BlockSpec((MOD_ROWS, TM), lambda i: (0, i)),
            pl.BlockSpec((TM, LANES), row),
            pl.BlockSpec((MOD_ROWS, LANES), row),
        ],
        out_shape=[
            jax.ShapeDtypeStruct((N_TOK, D_MODEL), F32),
            jax.ShapeDtypeStruct((N_TOK, D_MODEL), BF16),
            jax.ShapeDtypeStruct((N_TOK, LANES), jnp.int32),
            jax.ShapeDtypeStruct((MOD_ROWS, N_TOK), jnp.int32),
            jax.ShapeDtypeStruct((N_TOK, LANES), F32),
            jax.ShapeDtypeStruct((N_TILES * MOD_ROWS, LANES), F32),
        ],
        name="merge_router",
    )(x, attn[0], attn[1], pool[0], pool[1], four[0], four[1], gates, mods, gpost, gffn,
      wa, wp, wf, wo, wr, br, tri, upper)


HALF = D_MODEL // 2
U32 = jnp.uint32
HI_MASK = 0xFFFF0000


def _pack_rows(x):
    lo = lax.bitcast_convert_type(x[:, :HALF], U32)
    hi = lax.bitcast_convert_type(x[:, HALF:], U32)
    return lax.shift_right_logical(lo, U32(16)) | (hi & U32(HI_MASK))


def _unpack_rows(u):
    lo = lax.bitcast_convert_type(lax.shift_left(u, U32(16)), F32)
    hi = lax.bitcast_convert_type(u & U32(HI_MASK), F32)
    return jnp.concatenate([lo.astype(BF16), hi.astype(BF16)], axis=1)


def _run_copies(cnt, src_ref, src_off, dst_ref, dst_off, sem, wait):
    off = jnp.int32(0)
    for b in reversed(range(RUN_BITS)):
        size = RUN_ALIGN << b
        take = lax.shift_right_logical(cnt, b) & 1

        @pl.when(take == 1)
        def _(off=off, size=size):
            cp = pltpu.make_async_copy(
                src_ref.at[pl.ds(pl.multiple_of(src_off + off, RUN_ALIGN), size)],
                dst_ref.at[pl.ds(pl.multiple_of(dst_off + off, RUN_ALIGN), size)], sem)
            if wait:
                cp.wait()
            else:
                cp.start()

        off = off + take * size


def _tile_runs(tile, cnt_ref, fn):
    def body(e, carry):
        r = tile * N_EXPERTS + e
        fn(r, cnt_ref[r])
        return carry

    lax.fori_loop(0, N_EXPERTS, body, 0)


def _dispatch_kernel(cnt_ref, toff_ref, xoff_ref, h_ref, post_ref, xs_in_ref, xs_ref, sorted_ref, sem):
    del xs_in_ref
    i = pl.program_id(0)
    slot = lax.rem(i, 2)
    rows = lax.broadcasted_iota(jnp.int32, (SORT_ROWS, TM), 0)
    p = post_ref[...]
    perm = jnp.where(rows == p[0:1, :], 1.0, jnp.where(rows == p[1:2, :], 1.0, 0.0)).astype(BF16)
    sorted_ref[slot] = _pack_rows(_dot(perm, h_ref[...]))

    def copies(tile, slot, wait):
        _tile_runs(tile, cnt_ref, lambda r, c: _run_copies(
            c, sorted_ref.at[slot], toff_ref[r], xs_ref, xoff_ref[r], sem.at[slot], wait))

    @pl.when(i > 0)
    def _():
        copies(i - 1, 1 - slot, True)

    copies(i, slot, False)

    @pl.when(i == pl.num_programs(0) - 1)
    def _():
        copies(i, slot, True)


def _dispatch_call(cnt, toff, xoff, h2, post, xs):
    return pl.pallas_call(
        _dispatch_kernel,
        grid_spec=pltpu.PrefetchScalarGridSpec(
            num_scalar_prefetch=3,
            grid=(N_TILES,),
            in_specs=[
                pl.BlockSpec((TM, D_MODEL), lambda i, *_: (i, 0)),
                pl.BlockSpec((MOD_ROWS, TM), lambda i, *_: (0, i)),
                pl.BlockSpec(memory_space=pl.ANY),
            ],
            out_specs=pl.BlockSpec(memory_space=pl.ANY),
            scratch_shapes=[pltpu.VMEM((2, SORT_ROWS, HALF), U32), pltpu.SemaphoreType.DMA((2,))],
        ),
        out_shape=jax.ShapeDtypeStruct((XS_ROWS, HALF), U32),
        input_output_aliases={5: 0},
        compiler_params=pltpu.CompilerParams(dimension_semantics=("arbitrary",)),
        name="moe_dispatch",
    )(cnt, toff, xoff, h2, post, xs)


def _expert_kernel(be_ref, nu_ref, x_ref, wg_ref, wu_ref, wd_ref, y_ref):
    b = pl.program_id(0)

    @pl.when(b < nu_ref[0])
    def _():
        xb = _unpack_rows(x_ref[...])
        g = _dot(xb, wg_ref[...].astype(BF16))
        u = _dot(xb, wu_ref[...].astype(BF16))
        a = (g * jax.nn.sigmoid(g)) * u
        y = _dot(a.astype(BF16), wd_ref[...].astype(BF16))
        y_ref[...] = _pack_rows(y.astype(BF16).astype(F32))

    @pl.when(b >= nu_ref[0])
    def _():
        y_ref[...] = jnp.zeros_like(y_ref)


def _expert_call(l, block_expert, n_used, xs, w_gate, w_up, w_down):
    row = lambda b, be, nu: (b, 0)
    used_row = lambda b, be, nu: (jnp.minimum(b, nu[0] - 1), 0)
    wsel = lambda b, be, nu: (l, be[b], 0, 0)
    return pl.pallas_call(
        _expert_kernel,
        grid_spec=pltpu.PrefetchScalarGridSpec(
            num_scalar_prefetch=2,
            grid=(N_EBLK,),
            in_specs=[
                pl.BlockSpec((EBLK, HALF), used_row),
                pl.BlockSpec((None, None, D_MODEL, EXPERT_FF), wsel),
                pl.BlockSpec((None, None, D_MODEL, EXPERT_FF), wsel),
                pl.BlockSpec((None, None, EXPERT_FF, D_MODEL), wsel),
            ],
            out_specs=pl.BlockSpec((EBLK, HALF), row),
        ),
        out_shape=jax.ShapeDtypeStruct((XS_ROWS, HALF), U32),
        compiler_params=pltpu.CompilerParams(dimension_semantics=("arbitrary",)),
        name="moe_experts",
    )(block_expert, n_used, xs, w_gate, w_up, w_down)


def _combine_kernel(cnt_ref, toff_ref, xoff_ref, y_ref, x1_ref, pos_ref, mw_ref, mod_ref, gpost_ref,
                    o_ref, ybuf, sem):
    i = pl.program_id(0)
    slot = lax.rem(i, 2)

    def copies(tile, slot, wait):
        _tile_runs(tile, cnt_ref, lambda r, c: _run_copies(
            c, y_ref, xoff_ref[r], ybuf.at[slot], toff_ref[r], sem.at[slot], wait))

    @pl.when(i == 0)
    def _():
        ybuf[...] = jnp.zeros_like(ybuf)
        copies(0, 0, False)

    @pl.when(i + 1 < pl.num_programs(0))
    def _():
        copies(i + 1, 1 - slot, False)

    copies(i, slot, True)

    pos = pos_ref[...]
    mw = mw_ref[...]
    cols = lax.broadcasted_iota(jnp.int32, (TM, SORT_ROWS), 1)
    qw = (jnp.where(cols == pos[:, 0:1], mw[:, 0:1], 0.0)
          + jnp.where(cols == pos[:, 1:2], mw[:, 1:2], 0.0)).astype(BF16)
    ffn = _dot(qw, _unpack_rows(ybuf[slot]))
    g2 = mod_ref[...][:, 5 * D_MODEL:6 * D_MODEL]
    o_ref[...] = x1_ref[...] + g2 * (_rms(ffn) * gpost_ref[...])


def _combine_call(l, cnt, toff, xoff, y, x1, pos, mw, mods, gpost):
    row = lambda i, *_: (i, 0)
    return pl.pallas_call(
        _combine_kernel,
        grid_spec=pltpu.PrefetchScalarGridSpec(
            num_scalar_prefetch=3,
            grid=(N_TILES,),
            in_specs=[
                pl.BlockSpec(memory_space=pl.ANY),
                pl.BlockSpec((TM, D_MODEL), row),
                pl.BlockSpec((TM, LANES), row),
                pl.BlockSpec((TM, LANES), row),
                pl.BlockSpec((None, None, 1, N_MOD * D_MODEL), lambda i, *_: (l, _mod_row(i), 0, 0)),
                pl.BlockSpec((None, 1, D_MODEL), lambda i, *_: (l, 0, 0)),
            ],
            out_specs=pl.BlockSpec((TM, D_MODEL), row),
            scratch_shapes=[pltpu.VMEM((2, SORT_ROWS, HALF), U32), pltpu.SemaphoreType.DMA((2,))],
        ),
        out_shape=jax.ShapeDtypeStruct((N_TOK, D_MODEL), F32),
        compiler_params=pltpu.CompilerParams(dimension_semantics=("arbitrary",)),
        name="moe_combine",
    )(cnt, toff, xoff, y, x1, pos, mw, mods, gpost)


def _dft_mats(n):
    k = np.arange(n, dtype=np.int64)
    ang = 2.0 * np.pi * ((k[:, None] * k[None, :]) % n).astype(np.float64) / n
    return np.cos(ang), np.sin(ang)


def _block_diag(m, reps):
    n = m.shape[0]
    out = np.zeros((n * reps, n * reps), m.dtype)
    for r in range(reps):
        out[r * n:(r + 1) * n, r * n:(r + 1) * n] = m
    return out


def _rope_tables():
    t = np.arange(DEC_SEQ)
    pos = np.stack([t // GRID_W, t % GRID_W], axis=1).astype(np.float64)
    n_freq = ROPE_AXIS_DIM // 2
    inv = ROPE_BASE ** (-np.arange(n_freq, dtype=np.float64) * 2.0 / ROPE_AXIS_DIM)
    ang = pos[:, :, None] * inv[None, None, :]
    cos = np.cos(ang)
    sin = np.sin(ang)
    zero = np.zeros_like(sin[:, 0])
    cos_h = np.concatenate([cos[:, 0], cos[:, 0], cos[:, 1], cos[:, 1]], axis=1)
    s1_h = np.concatenate([-sin[:, 0], zero, -sin[:, 1], zero], axis=1)
    s2_h = np.concatenate([zero, sin[:, 0], zero, sin[:, 1]], axis=1)
    reps = LANES // HEAD_DIM

    def table(a, ident):
        a = np.tile(a, (1, reps))
        pad = np.full((TM, LANES), ident, np.float64)
        return jnp.asarray(np.concatenate([a, pad], axis=0), F32)

    return table(cos_h, 1.0), table(s1_h, 0.0), table(s2_h, 0.0)


def kernel(x_prompt, x_sample, cache_k, cache_v, c, c_ctx, w_ada, b_ada, norm_mix_pre,
           norm_mix_post, norm_ffn_pre, norm_ffn_post, w_in, q_norm, k_norm, w_attn_out,
           w_pool_group, pool_scale, w_pool_out, w_fourier_out, w_out, w_router_group,
           b_router_group, w_router_expert, b_router_expert, w_expert_gate, w_expert_up,
           w_expert_down):
    cos_t, s1_t, s2_t = _rope_tables()
    avg = jnp.asarray(_block_diag(np.full((HEAD_DIM, HEAD_DIM), 1.0 / HEAD_DIM), LANES // HEAD_DIM), BF16)
    c64, s64 = _dft_mats(FOURIER_GROUP_DIM)
    n_fg = FOURIER_WIDTH // FOURIER_GROUP_DIM
    dft_ch = jnp.asarray(np.concatenate([_block_diag(c64, n_fg), _block_diag(s64, n_fg)], axis=1), BF16)
    cp, sp = _dft_mats(SEQ)
    cp, sp = jnp.asarray(cp, BF16), jnp.asarray(sp, BF16)
    cl, sl = _dft_mats(DEC_SEQ)
    cl, sl = jnp.asarray(cl, BF16), jnp.asarray(sl, BF16)
    tri = jnp.asarray(np.tril(np.ones((TM, TM)), -1), BF16)
    upper = jnp.asarray(np.triu(np.ones((LANES, LANES)), 1), BF16)

    w_in_b = w_in.astype(BF16)
    wa_b = w_attn_out.astype(BF16)
    wp_b = w_pool_out.astype(BF16)
    wf_b = w_fourier_out.astype(BF16)
    wo_b = w_out.astype(BF16)
    pad_r = jnp.zeros((DEPTH, D_MODEL, LANES - N_EXPERTS - N_EXPERT_GROUPS), F32)
    wr_b = jnp.concatenate([w_router_expert, w_router_group, pad_r], axis=2).astype(BF16)
    br = jnp.concatenate([b_router_expert, b_router_group,
                          jnp.zeros((DEPTH, LANES - N_EXPERTS - N_EXPERT_GROUPS), F32)],
                         axis=1).reshape(DEPTH, 1, LANES)
    n_pg = POOL_WIDTH // POOL_GROUP_DIM
    bdw = jnp.zeros((DEPTH, POOL_WIDTH, POOL_WIDTH), F32)
    for g in range(n_pg):
        lo = g * POOL_GROUP_DIM
        bdw = bdw.at[:, lo:lo + POOL_GROUP_DIM, lo:lo + POOL_GROUP_DIM].set(w_pool_group[:, g])
    bdw = bdw.astype(BF16)
    pscale = pool_scale.reshape(DEPTH, 1, POOL_WIDTH)
    qg = jnp.tile(q_norm, (1, LANES // HEAD_DIM)).reshape(DEPTH, 1, LANES)
    kg = jnp.tile(k_norm, (1, LANES // HEAD_DIM)).reshape(DEPTH, 1, LANES)
    gpre = norm_mix_pre.reshape(DEPTH, 1, D_MODEL)
    gpost = norm_mix_post.reshape(DEPTH, 1, D_MODEL)
    gffn = norm_ffn_pre.reshape(DEPTH, 1, D_MODEL)
    gfpost = norm_ffn_post.reshape(DEPTH, 1, D_MODEL)
    ck = cache_k.reshape(DEC_BATCH, DEPTH, PAST_LEN, KV_WIDTH)
    cv = cache_v.reshape(DEC_BATCH, DEPTH, PAST_LEN, KV_WIDTH)

    c_all = jnp.concatenate([c_ctx[None, :], c, jnp.zeros((MOD_ROWS - 1 - DEC_BATCH, D_MODEL), F32)], axis=0)
    mods = _mod_call(c_all, w_ada, b_ada).reshape(DEPTH, MOD_ROWS, 1, N_MOD * D_MODEL)

    x = jnp.concatenate([x_prompt.reshape(N_P, D_MODEL), x_sample.reshape(N_S, D_MODEL)], axis=0)
    xs_buf = jnp.zeros((XS_ROWS, HALF), U32)
    new_k, new_v = [], []
    for l in range(DEPTH):
        q, k, v, xp, xc, xsn, gates = _proj_call(l, x, mods, gpre, w_in_b, qg, kg,
                                                 cos_t, s1_t, s2_t, avg, dft_ch)
        new_k.append(k[:N_P].reshape(BATCH, SEQ, N_KV_HEADS, HEAD_DIM))
        new_v.append(v[:N_P].reshape(BATCH, SEQ, N_KV_HEADS, HEAD_DIM))
        attn = (_attn_prompt_call(q, k, v), _attn_sample_call(l, q, k, v, ck, cv))
        pool = (_pool_call(l, xp, bdw, pscale, SEQ, BATCH, 0),
                _pool_call(l, xp, bdw, pscale, DEC_SEQ, DEC_BATCH, N_P // DEC_SEQ))
        four = (_fourier_prompt_call(cp, sp, xc, xsn), _fourier_sample_call(cl, sl, xc, xsn))
        x1, h2, pos, post, mw, cnt = _merge_call(l, x, attn, pool, four, gates, mods, gpost, gffn,
                                                 wa_b, wp_b, wf_b, wo_b, wr_b, br, tri, upper)
        runs = cnt.reshape(N_TILES, MOD_ROWS, LANES)[:, 0, :N_EXPERTS].astype(jnp.int32)
        runs = ((runs + RUN_ALIGN - 1) // RUN_ALIGN) * RUN_ALIGN
        tile_off = jnp.cumsum(runs, axis=1) - runs
        rows_e = jnp.sum(runs, axis=0)
        padded = ((rows_e + EBLK - 1) // EBLK) * EBLK
        pad_end = jnp.cumsum(padded)
        xs_off = (pad_end - padded)[None, :] + jnp.cumsum(runs, axis=0) - runs
        block_expert = jnp.minimum(
            jnp.searchsorted(pad_end, jnp.arange(N_EBLK, dtype=jnp.int32) * EBLK, side='right'),
            N_EXPERTS - 1).astype(jnp.int32)
        n_used = (pad_end[-1:] // EBLK).astype(jnp.int32)
        run_cnt = (runs // RUN_ALIGN).reshape(N_RUNS)
        tile_off = tile_off.reshape(N_RUNS)
        xs_off = xs_off.reshape(N_RUNS)
        xs_buf = _dispatch_call(run_cnt, tile_off, xs_off, h2, post, xs_buf)
        y = _expert_call(l, block_expert, n_used, xs_buf, w_expert_gate, w_expert_up, w_expert_down)
        x = _combine_call(l, run_cnt, tile_off, xs_off, y, x1, pos, mw, mods, gfpost)

    y_prompt = x[:N_P].reshape(BATCH, SEQ, D_MODEL)
    y_sample = x[N_P:].reshape(DEC_BATCH, DEC_SEQ, D_MODEL)
    return (y_prompt, y_sample, jnp.stack(new_k, axis=1), jnp.stack(new_v, axis=1))
```

```python
import functools
import math

import numpy as np
import jax
import jax.numpy as jnp
from jax import lax
from jax.experimental import pallas as pl
from jax.experimental.pallas import tpu as pltpu

F32 = jnp.float32
BF16 = jnp.bfloat16

D_MODEL = 1024
BATCH = 32
SEQ = 256
DEPTH = 4
DEC_BATCH = 2
DEC_SEQ = 2048
PAST_LEN = 512
GRID_W = 64
N_HEADS = 8
N_KV_HEADS = 2
HEAD_DIM = 64
KV_GROUP = N_HEADS // N_KV_HEADS
ATTN_WIDTH = N_HEADS * HEAD_DIM
KV_WIDTH = N_KV_HEADS * HEAD_DIM
ROPE_AXIS_DIM = HEAD_DIM // 2
ROPE_BASE = 10000.0
POOL_WINDOWS = (2, 4, 8, 16)
POOL_WIDTH = 256
POOL_GROUP_DIM = 64
FOURIER_WIDTH = 256
FOURIER_GROUP_DIM = 64
N_BRANCHES = 3
OFF_K = ATTN_WIDTH
OFF_P = ATTN_WIDTH + 2 * KV_WIDTH
OFF_G = OFF_P + POOL_WIDTH + FOURIER_WIDTH
IN_WIDTH = OFF_G + N_BRANCHES * D_MODEL
N_EXPERT_GROUPS = 4
EXPERTS_PER_GROUP = 8
N_EXPERTS = N_EXPERT_GROUPS * EXPERTS_PER_GROUP
TOP_K = 2
EXPERT_FF = 256
N_MOD = 6
RMS_EPS = 1e-6
QK_SCALE = HEAD_DIM ** -0.5 * math.log2(math.e)

N_P = BATCH * SEQ
N_S = DEC_BATCH * DEC_SEQ
N_TOK = N_P + N_S
LANES = 128
MOD_ROWS = 8
POOL_PAD = 16

TM = 512
N_TILES = N_TOK // TM
P_TILES = N_P // TM
S_TILES_PER_SEQ = DEC_SEQ // TM
TQ_S = 128
TF_S = 512
EBLK = 512
RUN_ALIGN = 8
RUN_SHIFT = 3
RUN_BITS = 7
assert RUN_ALIGN << (RUN_BITS - 1) == TM
SORT_ROWS = TOP_K * TM + N_EXPERTS * RUN_ALIGN
N_RUNS = N_TILES * N_EXPERTS
N_EBLK = (N_TOK * TOP_K + N_RUNS * (RUN_ALIGN - 1) + N_EXPERTS * (EBLK - 1) + EBLK - 1) // EBLK
XS_ROWS = N_EBLK * EBLK
MOD_NT = 1536


def _dot(a, b):
    return jnp.dot(a, b, preferred_element_type=F32)


def _rms(x):
    return x * lax.rsqrt(jnp.mean(x * x, axis=-1, keepdims=True) + RMS_EPS)


def _mod_row(i):
    return jnp.where(i < P_TILES, 0, 1 + (i - P_TILES) // S_TILES_PER_SEQ)


def _rope_block(i):
    return jnp.where(i < P_TILES, S_TILES_PER_SEQ, (i - P_TILES) % S_TILES_PER_SEQ)


def _mod_kernel(c_ref, w_ref, b_ref, o_ref):
    c = c_ref[...]
    s = (c * jax.nn.sigmoid(c)).astype(BF16)
    o_ref[...] = _dot(s, w_ref[...].astype(BF16)) + b_ref[...]


def _mod_call(c_all, w_ada, b_ada):
    nt = (N_MOD * D_MODEL) // MOD_NT
    return pl.pallas_call(
        _mod_kernel,
        grid=(DEPTH, nt),
        in_specs=[
            pl.BlockSpec((MOD_ROWS, D_MODEL), lambda l, j: (0, 0)),
            pl.BlockSpec((None, D_MODEL, MOD_NT), lambda l, j: (l, 0, j)),
            pl.BlockSpec((None, 1, MOD_NT), lambda l, j: (l, 0, j)),
        ],
        out_specs=pl.BlockSpec((None, MOD_ROWS, MOD_NT), lambda l, j: (l, 0, j)),
        out_shape=jax.ShapeDtypeStruct((DEPTH, MOD_ROWS, N_MOD * D_MODEL), F32),
        name="adaln_mod",
    )(c_all, w_ada, b_ada.reshape(DEPTH, 1, N_MOD * D_MODEL))


def _prenorm(x, mod, gain):
    return (_rms(x) * gain) * (1.0 + mod[:, D_MODEL:2 * D_MODEL]) + mod[:, 0:D_MODEL]


def _prenorm_kernel(x_ref, mod_ref, gpre_ref, hb_ref):
    hb_ref[...] = _prenorm(x_ref[...], mod_ref[...], gpre_ref[...]).astype(BF16)


def _prenorm_call(l, x, mods, gpre):
    return pl.pallas_call(
        _prenorm_kernel,
        grid=(N_TILES,),
        in_specs=[
            pl.BlockSpec((TM, D_MODEL), lambda i: (i, 0)),
            pl.BlockSpec((None, None, 1, 2 * D_MODEL), lambda i: (l, _mod_row(i), 0, 0)),
            pl.BlockSpec((None, 1, D_MODEL), lambda i: (l, 0, 0)),
        ],
        out_specs=pl.BlockSpec((TM, D_MODEL), lambda i: (i, 0)),
        out_shape=jax.ShapeDtypeStruct((N_TOK, D_MODEL), BF16),
        name="prenorm",
    )(x, mods, gpre)


def _proj_kernel(hb_ref, w_ref, qg_ref, kg_ref, cos_ref, s1_ref, s2_ref,
                 avg_ref, dft_ref, q_ref, k_ref, v_ref, xp_ref, xc_ref, xs_ref, g_ref):
    hb = hb_ref[...]

    cos = cos_ref[...]
    s1 = s1_ref[...]
    s2 = s2_ref[...]

    def rope(t):
        return (t * cos + pltpu.roll(t, LANES - ROPE_AXIS_DIM // 2, 1) * s1
                + pltpu.roll(t, ROPE_AXIS_DIM // 2, 1) * s2)

    avg = avg_ref[...]
    for c in range(ATTN_WIDTH // LANES):
        lo = c * LANES
        q = _dot(hb, w_ref[:, lo:lo + LANES])
        ms = _dot((q * q).astype(BF16), avg)
        q = q * lax.rsqrt(ms + RMS_EPS) * qg_ref[...]
        q_ref[:, lo:lo + LANES] = (rope(q) * QK_SCALE).astype(BF16)

    kv = _dot(hb, w_ref[:, OFF_K:OFF_K + 2 * KV_WIDTH])
    k = kv[:, 0:KV_WIDTH]
    ms = _dot((k * k).astype(BF16), avg)
    k = k * lax.rsqrt(ms + RMS_EPS) * kg_ref[...]
    k_ref[...] = rope(k)
    v_ref[...] = kv[:, KV_WIDTH:]

    pf = _dot(hb, w_ref[:, OFF_P:OFF_G])
    xp_ref[...] = pf[:, 0:POOL_WIDTH]
    cs = _dot(pf[:, POOL_WIDTH:].astype(BF16), dft_ref[...])
    xc_ref[...] = cs[:, 0:FOURIER_WIDTH].astype(BF16)
    xs_ref[...] = cs[:, FOURIER_WIDTH:].astype(BF16)

    for c in range(N_BRANCHES):
        lo = OFF_G + c * D_MODEL
        g_ref[:, c * D_MODEL:(c + 1) * D_MODEL] = _dot(hb, w_ref[:, lo:lo + D_MODEL]).astype(BF16)


def _proj_call(l, hb, w_in, qg, kg, cos_t, s1_t, s2_t, avg, dft):
    row = lambda i: (i, 0)
    const2 = lambda i: (0, 0)
    per_layer = lambda i: (l, 0, 0)
    rope_spec = pl.BlockSpec((TM, LANES), lambda i: (_rope_block(i), 0))
    return pl.pallas_call(
        _proj_kernel,
        grid=(N_TILES,),
        in_specs=[
            pl.BlockSpec((TM, D_MODEL), row),
            pl.BlockSpec((None, D_MODEL, IN_WIDTH), per_layer),
            pl.BlockSpec((None, 1, LANES), per_layer),
            pl.BlockSpec((None, 1, LANES), per_layer),
            rope_spec, rope_spec, rope_spec,
            pl.BlockSpec((LANES, LANES), const2),
            pl.BlockSpec((FOURIER_WIDTH, 2 * FOURIER_WIDTH), const2),
        ],
        out_specs=[
            pl.BlockSpec((TM, ATTN_WIDTH), row),
            pl.BlockSpec((TM, KV_WIDTH), row),
            pl.BlockSpec((TM, KV_WIDTH), row),
            pl.BlockSpec((TM, POOL_WIDTH), row),
            pl.BlockSpec((TM, FOURIER_WIDTH), row),
            pl.BlockSpec((TM, FOURIER_WIDTH), row),
            pl.BlockSpec((TM, N_BRANCHES * D_MODEL), row),
        ],
        out_shape=[
            jax.ShapeDtypeStruct((N_TOK, ATTN_WIDTH), BF16),
            jax.ShapeDtypeStruct((N_TOK, KV_WIDTH), F32),
            jax.ShapeDtypeStruct((N_TOK, KV_WIDTH), F32),
            jax.ShapeDtypeStruct((N_TOK, POOL_WIDTH), F32),
            jax.ShapeDtypeStruct((N_TOK, FOURIER_WIDTH), BF16),
            jax.ShapeDtypeStruct((N_TOK, FOURIER_WIDTH), BF16),
            jax.ShapeDtypeStruct((N_TOK, N_BRANCHES * D_MODEL), BF16),
        ],
        name="proj",
    )(hb, w_in, qg, kg, cos_t, s1_t, s2_t, avg, dft)


def _attn_kernel(*refs, n_parts, tq):
    q_ref = refs[0]
    kv_refs = refs[1:1 + 2 * n_parts]
    o_ref = refs[-1]
    outs = []
    for j in range(N_KV_HEADS):
        lo = j * HEAD_DIM
        qs = jnp.concatenate(
            [q_ref[:, (KV_GROUP * j + g) * HEAD_DIM:(KV_GROUP * j + g + 1) * HEAD_DIM]
             for g in range(KV_GROUP)], axis=0)
        scores = []
        for p in range(n_parts):
            kp = kv_refs[2 * p][:, lo:lo + HEAD_DIM].astype(BF16)
            scores.append(lax.dot_general(qs, kp, (((1,), (1,)), ((), ())),
                                          preferred_element_type=F32))
        m = jnp.max(scores[0], axis=1, keepdims=True)
        for s in scores[1:]:
            m = jnp.maximum(m, jnp.max(s, axis=1, keepdims=True))
        acc = None
        den = None
        for p in range(n_parts):
            e = jnp.exp2(scores[p] - m)
            vp = kv_refs[2 * p + 1][:, lo:lo + HEAD_DIM].astype(BF16)
            pv = _dot(e.astype(BF16), vp)
            es = jnp.sum(e, axis=1, keepdims=True)
            acc = pv if acc is None else acc + pv
            den = es if den is None else den + es
        o = acc / den
        outs.extend(o[g * tq:(g + 1) * tq] for g in range(KV_GROUP))
    o_ref[...] = jnp.concatenate(outs, axis=1).astype(BF16)


def _attn_prompt_call(q, k, v):
    blk = lambda b: (b, 0)
    return pl.pallas_call(
        functools.partial(_attn_kernel, n_parts=1, tq=SEQ),
        grid=(BATCH,),
        in_specs=[
            pl.BlockSpec((SEQ, ATTN_WIDTH), blk),
            pl.BlockSpec((SEQ, KV_WIDTH), blk),
            pl.BlockSpec((SEQ, KV_WIDTH), blk),
        ],
        out_specs=pl.BlockSpec((SEQ, ATTN_WIDTH), blk),
        out_shape=jax.ShapeDtypeStruct((N_P, ATTN_WIDTH), BF16),
        name="attn_context",
    )(q, k, v)


def _attn_sample_call(l, q, k, v, cache_k, cache_v):
    nq = DEC_SEQ // TQ_S
    qrow = lambda b, i: (N_P // TQ_S + b * nq + i, 0)
    seq = lambda b, i: (N_P // DEC_SEQ + b, 0)
    cache = lambda b, i: (b, l, 0, 0)
    return pl.pallas_call(
        functools.partial(_attn_kernel, n_parts=2, tq=TQ_S),
        grid=(DEC_BATCH, nq),
        in_specs=[
            pl.BlockSpec((TQ_S, ATTN_WIDTH), qrow),
            pl.BlockSpec((None, None, PAST_LEN, KV_WIDTH), cache),
            pl.BlockSpec((None, None, PAST_LEN, KV_WIDTH), cache),
            pl.BlockSpec((DEC_SEQ, KV_WIDTH), seq),
            pl.BlockSpec((DEC_SEQ, KV_WIDTH), seq),
        ],
        out_specs=pl.BlockSpec((TQ_S, ATTN_WIDTH), lambda b, i: (b * nq + i, 0)),
        out_shape=jax.ShapeDtypeStruct((N_S, ATTN_WIDTH), BF16),
        name="attn_latent",
    )(q, cache_k, cache_v, k, v)


def _pool_kernel(xp_ref, bdw_ref, sc_ref, o_ref, pad_ref, *, seq_len):
    half = POOL_WIDTH // 2
    zeros = jnp.zeros((POOL_PAD, POOL_WIDTH), F32)
    pad_ref[0:POOL_PAD, :] = zeros
    pad_ref[POOL_PAD + seq_len:, :] = zeros
    pad_ref[POOL_PAD:POOL_PAD + seq_len, :] = xp_ref[...]
    chunk = min(seq_len, 256)
    lane = lax.broadcasted_iota(jnp.int32, (chunk, half), 1)
    first = lane < POOL_GROUP_DIM
    for c in range(seq_len // chunk):
        base = c * chunk
        t = lax.broadcasted_iota(jnp.int32, (chunk, half), 0) + base

        def sh(j, lo):
            return pad_ref[POOL_PAD + base + j:POOL_PAD + base + j + chunk, lo:lo + half]

        def cnt(w):
            return (jnp.minimum(t + w // 2, seq_len) - jnp.maximum(t - w // 2, 0)).astype(F32)

        xa = sh(0, 0)
        w2 = sh(-1, 0) + xa
        w4 = w2 + sh(-2, 0) + sh(1, 0)
        xb = sh(0, half)
        w8 = xb
        for j in (-4, -3, -2, -1, 1, 2, 3):
            w8 = w8 + sh(j, half)
        w16 = w8
        for j in (-8, -7, -6, -5, 4, 5, 6, 7):
            w16 = w16 + sh(j, half)
        pa = jnp.where(first, w2 / cnt(2), w4 / cnt(4)) - xa
        pb = jnp.where(first, w8 / cnt(8), w16 / cnt(16)) - xb
        pooled = jnp.concatenate([pa, pb], axis=1).astype(BF16)
        o_ref[base:base + chunk, :] = (_dot(pooled, bdw_ref[...]) * sc_ref[...]).astype(BF16)


def _pool_call(l, xp, bdw, scale, seq_len, n_seq, blk0):
    per_layer = lambda b: (l, 0, 0)
    return pl.pallas_call(
        functools.partial(_pool_kernel, seq_len=seq_len),
        grid=(n_seq,),
        in_specs=[
            pl.BlockSpec((seq_len, POOL_WIDTH), lambda b: (blk0 + b, 0)),
            pl.BlockSpec((None, POOL_WIDTH, POOL_WIDTH), per_layer),
            pl.BlockSpec((None, 1, POOL_WIDTH), per_layer),
        ],
        out_specs=pl.BlockSpec((seq_len, POOL_WIDTH), lambda b: (b, 0)),
        out_shape=jax.ShapeDtypeStruct((n_seq * seq_len, POOL_WIDTH), BF16),
        scratch_shapes=[pltpu.VMEM((seq_len + 2 * POOL_PAD, POOL_WIDTH), F32)],
        name="pool_%d" % seq_len,
    )(xp, bdw, scale)


def _fourier_kernel(c_ref, s_ref, xc_ref, xs_ref, o_ref, *, scale):
    y = _dot(c_ref[...], xc_ref[...]) - _dot(s_ref[...], xs_ref[...])
    o_ref[...] = (y * scale).astype(BF16)


def _fourier_prompt_call(cmat, smat, xc, xs):
    blk = lambda b: (b, 0)
    const2 = lambda b: (0, 0)
    return pl.pallas_call(
        functools.partial(_fourier_kernel, scale=1.0 / math.sqrt(SEQ * FOURIER_GROUP_DIM)),
        grid=(BATCH,),
        in_specs=[
            pl.BlockSpec((SEQ, SEQ), const2),
            pl.BlockSpec((SEQ, SEQ), const2),
            pl.BlockSpec((SEQ, FOURIER_WIDTH), blk),
            pl.BlockSpec((SEQ, FOURIER_WIDTH), blk),
        ],
        out_specs=pl.BlockSpec((SEQ, FOURIER_WIDTH), blk),
        out_shape=jax.ShapeDtypeStruct((N_P, FOURIER_WIDTH), BF16),
        name="fourier_context",
    )(cmat, smat, xc, xs)


def _fourier_sample_call(cmat, smat, xc, xs):
    nt = DEC_SEQ // TF_S
    rows = lambda b, i: (i, 0)
    seq = lambda b, i: (N_P // DEC_SEQ + b, 0)
    out = lambda b, i: (b * nt + i, 0)
    return pl.pallas_call(
        functools.partial(_fourier_kernel, scale=1.0 / math.sqrt(DEC_SEQ * FOURIER_GROUP_DIM)),
        grid=(DEC_BATCH, nt),
        in_specs=[
            pl.BlockSpec((TF_S, DEC_SEQ), rows),
            pl.BlockSpec((TF_S, DEC_SEQ), rows),
            pl.BlockSpec((DEC_SEQ, FOURIER_WIDTH), seq),
            pl.BlockSpec((DEC_SEQ, FOURIER_WIDTH), seq),
        ],
        out_specs=pl.BlockSpec((TF_S, FOURIER_WIDTH), out),
        out_shape=jax.ShapeDtypeStruct((N_S, FOURIER_WIDTH), BF16),
        name="fourier_latent",
    )(cmat, smat, xc, xs)


def _merge_kernel(x_ref, attn_p_ref, attn_s_ref, pool_p_ref, pool_s_ref, four_p_ref, four_s_ref,
                  g_ref, mod_ref, gpost_ref, gffn_ref,
                  wa_ref, wp_ref, wf_ref, wo_ref, wr_ref, br_ref, tri_ref, upper_ref,
                  x1_ref, h2_ref, pos_ref, post_ref, mw_ref, cnt_ref):
    i = pl.program_id(0)
    mod = mod_ref[...]
    g1 = mod[:, 2 * D_MODEL:3 * D_MODEL]
    sh2 = mod[:, 3 * D_MODEL:4 * D_MODEL]
    sc2 = mod[:, 4 * D_MODEL:5 * D_MODEL]

    def gate(c):
        return jax.nn.sigmoid(g_ref[:, c * D_MODEL:(c + 1) * D_MODEL].astype(F32))

    def branch(p_ref, s_ref):
        return jnp.where(i < P_TILES, p_ref[...], s_ref[...])

    merged = gate(0) * _dot(branch(attn_p_ref, attn_s_ref), wa_ref[...])
    merged = merged + gate(1) * _dot(branch(pool_p_ref, pool_s_ref), wp_ref[...])
    merged = merged + gate(2) * _dot(branch(four_p_ref, four_s_ref), wf_ref[...])
    mix = _dot(merged.astype(BF16), wo_ref[...])
    x1 = x_ref[...] + g1 * (_rms(mix) * gpost_ref[...])
    x1_ref[...] = x1
    h2 = (_rms(x1) * gffn_ref[...]) * (1.0 + sc2) + sh2
    h2b = h2.astype(BF16)
    h2_ref[...] = h2b

    logits = _dot(h2b, wr_ref[...]) + br_ref[...]
    lane = lax.broadcasted_iota(jnp.int32, (TM, LANES), 1)
    lanef = lane.astype(F32)
    neg = jnp.float32(-3e38)
    big = jnp.float32(1e9)
    is_g = (lane >= N_EXPERTS) & (lane < N_EXPERTS + N_EXPERT_GROUPS)
    lg = jnp.where(is_g, logits, neg)
    gmax = jnp.max(lg, axis=1, keepdims=True)
    g_sel = jnp.min(jnp.where(lg == gmax, lanef - N_EXPERTS, big), axis=1, keepdims=True)
    p_g = 1.0 / jnp.sum(jnp.where(is_g, jnp.exp(logits - gmax), 0.0), axis=1, keepdims=True)
    grp = lax.shift_right_logical(lane, int(math.log2(EXPERTS_PER_GROUP))).astype(F32)
    in_grp = (lane < N_EXPERTS) & (grp == g_sel)
    le = jnp.where(in_grp, logits, neg)
    v1 = jnp.max(le, axis=1, keepdims=True)
    i1 = jnp.min(jnp.where(le == v1, lanef, big), axis=1, keepdims=True)
    le2 = jnp.where(lanef == i1, neg, le)
    v2 = jnp.max(le2, axis=1, keepdims=True)
    i2 = jnp.min(jnp.where(le2 == v2, lanef, big), axis=1, keepdims=True)
    e21 = jnp.exp(v2 - v1)
    w1 = p_g / (1.0 + e21)
    w2 = p_g * e21 / (1.0 + e21)

    oh1 = (lanef == i1).astype(F32)
    oh2 = (lanef == i2).astype(F32)
    ohb = (oh1 + oh2).astype(BF16)
    before = _dot(tri_ref[...], ohb)
    cnt = _dot(jnp.ones((MOD_ROWS, TM), BF16), ohb)
    cnt_pad = (lax.shift_right_logical(cnt.astype(jnp.int32) + (RUN_ALIGN - 1), RUN_SHIFT)
               * RUN_ALIGN).astype(F32)
    run_off = _dot(cnt_pad.astype(BF16), upper_ref[...])
    slot = run_off[0:1, :] + before
    p1 = jnp.sum(slot * oh1, axis=1, keepdims=True)
    p2 = jnp.sum(slot * oh2, axis=1, keepdims=True)
    cnt_ref[...] = cnt

    pos = jnp.where(lane == 0, p1, jnp.where(lane == 1, p2, 0.0))
    pos_ref[...] = pos.astype(jnp.int32)
    post_ref[...] = pos.T[0:MOD_ROWS, :].astype(jnp.int32)
    mw_ref[...] = jnp.where(lane == 0, w1, w2)


def _merge_call(l, x, attn, pool, four, gates, mods, gpost, gffn, wa, wp, wf, wo, wr, br, tri,
                upper):
    row = lambda i: (i, 0)
    prow = lambda i: (jnp.minimum(i, P_TILES - 1), 0)
    srow = lambda i: (jnp.maximum(i - P_TILES, 0), 0)
    const2 = lambda i: (0, 0)
    per_layer = lambda i: (l, 0, 0)
    return pl.pallas_call(
        _merge_kernel,
        grid=(N_TILES,),
        in_specs=[
            pl.BlockSpec((TM, D_MODEL), row),
            pl.BlockSpec((TM, ATTN_WIDTH), prow),
            pl.BlockSpec((TM, ATTN_WIDTH), srow),
            pl.BlockSpec((TM, POOL_WIDTH), prow),
            pl.BlockSpec((TM, POOL_WIDTH), srow),
            pl.BlockSpec((TM, FOURIER_WIDTH), prow),
            pl.BlockSpec((TM, FOURIER_WIDTH), srow),
            pl.BlockSpec((TM, N_BRANCHES * D_MODEL), row),
            pl.BlockSpec((None, None, 1, N_MOD * D_MODEL), lambda i: (l, _mod_row(i), 0, 0)),
            pl.BlockSpec((None, 1, D_MODEL), per_layer),
            pl.BlockSpec((None, 1, D_MODEL), per_layer),
            pl.BlockSpec((None, ATTN_WIDTH, D_MODEL), per_layer),
            pl.BlockSpec((None, POOL_WIDTH, D_MODEL), per_layer),
            pl.BlockSpec((None, FOURIER_WIDTH, D_MODEL), per_layer),
            pl.BlockSpec((None, D_MODEL, D_MODEL), per_layer),
            pl.BlockSpec((None, D_MODEL, LANES), per_layer),
            pl.BlockSpec((None, 1, LANES), per_layer),
            pl.BlockSpec((TM, TM), const2),
            pl.BlockSpec((LANES, LANES), const2),
        ],
        out_specs=[
            pl.BlockSpec((TM, D_MODEL), row),
            pl.BlockSpec((TM, D_MODEL), row),
            pl.BlockSpec((TM, LANES), row),
            pl.BlockSpec((MOD_ROWS, TM), lambda i: (0, i)),
            pl.BlockSpec((TM, LANES), row),
            pl.BlockSpec((MOD_ROWS, LANES), row),
        ],
        out_shape=[
            jax.ShapeDtypeStruct((N_TOK, D_MODEL), F32),
            jax.ShapeDtypeStruct((N_TOK, D_MODEL), BF16),
            jax.ShapeDtypeStruct((N_TOK, LANES), jnp.int32),
            jax.ShapeDtypeStruct((MOD_ROWS, N_TOK), jnp.int32),
            jax.ShapeDtypeStruct((N_TOK, LANES), F32),
            jax.ShapeDtypeStruct((N_TILES * MOD_ROWS, LANES), F32),
        ],
        name="merge_router",
    )(x, attn[0], attn[1], pool[0], pool[1], four[0], four[1], gates, mods, gpost, gffn,
      wa, wp, wf, wo, wr, br, tri, upper)


HALF = D_MODEL // 2
U32 = jnp.uint32
HI_MASK = 0xFFFF0000


def _pack_rows(x):
    lo = lax.bitcast_convert_type(x[:, :HALF], U32)
    hi = lax.bitcast_convert_type(x[:, HALF:], U32)
    return lax.shift_right_logical(lo, U32(16)) | (hi & U32(HI_MASK))


def _unpack_rows(u):
    lo = lax.bitcast_convert_type(lax.shift_left(u, U32(16)), F32)
    hi = lax.bitcast_convert_type(u & U32(HI_MASK), F32)
    return jnp.concatenate([lo.astype(BF16), hi.astype(BF16)], axis=1)


def _run_copies(cnt, src_ref, src_off, dst_ref, dst_off, sem, wait):
    off = jnp.int32(0)
    for b in reversed(range(RUN_BITS)):
        size = RUN_ALIGN << b
        take = lax.shift_right_logical(cnt, b) & 1

        @pl.when(take == 1)
        def _(off=off, size=size):
            cp = pltpu.make_async_copy(
                src_ref.at[pl.ds(pl.multiple_of(src_off + off, RUN_ALIGN), size)],
                dst_ref.at[pl.ds(pl.multiple_of(dst_off + off, RUN_ALIGN), size)], sem)
            if wait:
                cp.wait()
            else:
                cp.start()

        off = off + take * size


def _tile_runs(tile, cnt_ref, fn):
    def body(e, carry):
        r = tile * N_EXPERTS + e
        fn(r, cnt_ref[r])
        return carry

    lax.fori_loop(0, N_EXPERTS, body, 0)


def _dispatch_kernel(cnt_ref, toff_ref, xoff_ref, h_ref, post_ref, xs_in_ref, xs_ref, sorted_ref, sem):
    del xs_in_ref
    i = pl.program_id(0)
    slot = lax.rem(i, 2)
    rows = lax.broadcasted_iota(jnp.int32, (SORT_ROWS, TM), 0)
    p = post_ref[...]
    perm = jnp.where(rows == p[0:1, :], 1.0, jnp.where(rows == p[1:2, :], 1.0, 0.0)).astype(BF16)
    sorted_ref[slot] = _pack_rows(_dot(perm, h_ref[...]))

    def copies(tile, slot, wait):
        _tile_runs(tile, cnt_ref, lambda r, c: _run_copies(
            c, sorted_ref.at[slot], toff_ref[r], xs_ref, xoff_ref[r], sem.at[slot], wait))

    @pl.when(i > 0)
    def _():
        copies(i - 1, 1 - slot, True)

    copies(i, slot, False)

    @pl.when(i == pl.num_programs(0) - 1)
    def _():
        copies(i, slot, True)


def _dispatch_call(cnt, toff, xoff, h2, post, xs):
    return pl.pallas_call(
        _dispatch_kernel,
        grid_spec=pltpu.PrefetchScalarGridSpec(
            num_scalar_prefetch=3,
            grid=(N_TILES,),
            in_specs=[
                pl.BlockSpec((TM, D_MODEL), lambda i, *_: (i, 0)),
                pl.BlockSpec((MOD_ROWS, TM), lambda i, *_: (0, i)),
                pl.BlockSpec(memory_space=pl.ANY),
            ],
            out_specs=pl.BlockSpec(memory_space=pl.ANY),
            scratch_shapes=[pltpu.VMEM((2, SORT_ROWS, HALF), U32), pltpu.SemaphoreType.DMA((2,))],
        ),
        out_shape=jax.ShapeDtypeStruct((XS_ROWS, HALF), U32),
        input_output_aliases={5: 0},
        compiler_params=pltpu.CompilerParams(dimension_semantics=("arbitrary",)),
        name="moe_dispatch",
    )(cnt, toff, xoff, h2, post, xs)


def _expert_kernel(be_ref, nu_ref, x_ref, wg_ref, wu_ref, wd_ref, y_ref):
    b = pl.program_id(0)

    @pl.when(b < nu_ref[0])
    def _():
        xb = _unpack_rows(x_ref[...])
        g = _dot(xb, wg_ref[...].astype(BF16))
        u = _dot(xb, wu_ref[...].astype(BF16))
        a = (g * jax.nn.sigmoid(g)) * u
        y = _dot(a.astype(BF16), wd_ref[...].astype(BF16))
        y_ref[...] = _pack_rows(y.astype(BF16).astype(F32))

    @pl.when(b >= nu_ref[0])
    def _():
        y_ref[...] = jnp.zeros_like(y_ref)


def _expert_call(l, block_expert, n_used, xs, w_gate, w_up, w_down):
    row = lambda b, be, nu: (b, 0)
    used_row = lambda b, be, nu: (jnp.minimum(b, nu[0] - 1), 0)
    wsel = lambda b, be, nu: (l, be[b], 0, 0)
    return pl.pallas_call(
        _expert_kernel,
        grid_spec=pltpu.PrefetchScalarGridSpec(
            num_scalar_prefetch=2,
            grid=(N_EBLK,),
            in_specs=[
                pl.BlockSpec((EBLK, HALF), used_row),
                pl.BlockSpec((None, None, D_MODEL, EXPERT_FF), wsel),
                pl.BlockSpec((None, None, D_MODEL, EXPERT_FF), wsel),
                pl.BlockSpec((None, None, EXPERT_FF, D_MODEL), wsel),
            ],
            out_specs=pl.BlockSpec((EBLK, HALF), row),
        ),
        out_shape=jax.ShapeDtypeStruct((XS_ROWS, HALF), U32),
        compiler_params=pltpu.CompilerParams(dimension_semantics=("arbitrary",)),
        name="moe_experts",
    )(block_expert, n_used, xs, w_gate, w_up, w_down)


def _combine_kernel(cnt_ref, toff_ref, xoff_ref, y_ref, x1_ref, pos_ref, mw_ref, g2_ref, gpost_ref,
                    *rest, has_next):
    if has_next:
        nmod_ref, ngpre_ref, o_ref, hb_ref, ybuf, sem = rest
    else:
        o_ref, ybuf, sem = rest
    i = pl.program_id(0)
    slot = lax.rem(i, 2)

    def copies(tile, slot, wait):
        _tile_runs(tile, cnt_ref, lambda r, c: _run_copies(
            c, y_ref, xoff_ref[r], ybuf.at[slot], toff_ref[r], sem.at[slot], wait))

    @pl.when(i == 0)
    def _():
        ybuf[...] = jnp.zeros_like(ybuf)
        copies(0, 0, False)

    @pl.when(i + 1 < pl.num_programs(0))
    def _():
        copies(i + 1, 1 - slot, False)

    copies(i, slot, True)

    pos = pos_ref[...]
    mw = mw_ref[...]
    cols = lax.broadcasted_iota(jnp.int32, (TM, SORT_ROWS), 1)
    qw = (jnp.where(cols == pos[:, 0:1], mw[:, 0:1], 0.0)
          + jnp.where(cols == pos[:, 1:2], mw[:, 1:2], 0.0)).astype(BF16)
    ffn = _dot(qw, _unpack_rows(ybuf[slot]))
    x2 = x1_ref[...] + g2_ref[...] * (_rms(ffn) * gpost_ref[...])
    o_ref[...] = x2
    if has_next:
        hb_ref[...] = _prenorm(x2, nmod_ref[...], ngpre_ref[...]).astype(BF16)


def _combine_call(l, cnt, toff, xoff, y, x1, pos, mw, mods, gpost, gpre):
    has_next = l + 1 < DEPTH
    row = lambda i, *_: (i, 0)
    in_specs = [
        pl.BlockSpec(memory_space=pl.ANY),
        pl.BlockSpec((TM, D_MODEL), row),
        pl.BlockSpec((TM, LANES), row),
        pl.BlockSpec((TM, LANES), row),
        pl.BlockSpec((None, None, 1, D_MODEL), lambda i, *_: (l, _mod_row(i), 0, N_MOD - 1)),
        pl.BlockSpec((None, 1, D_MODEL), lambda i, *_: (l, 0, 0)),
    ]
    args = [cnt, toff, xoff, y, x1, pos, mw, mods, gpost]
    out_specs = [pl.BlockSpec((TM, D_MODEL), row)]
    out_shape = [jax.ShapeDtypeStruct((N_TOK, D_MODEL), F32)]
    if has_next:
        in_specs += [
            pl.BlockSpec((None, None, 1, 2 * D_MODEL), lambda i, *_: (l + 1, _mod_row(i), 0, 0)),
            pl.BlockSpec((None, 1, D_MODEL), lambda i, *_: (l + 1, 0, 0)),
        ]
        args += [mods, gpre]
        out_specs.append(pl.BlockSpec((TM, D_MODEL), row))
        out_shape.append(jax.ShapeDtypeStruct((N_TOK, D_MODEL), BF16))
    return pl.pallas_call(
        functools.partial(_combine_kernel, has_next=has_next),
        grid_spec=pltpu.PrefetchScalarGridSpec(
            num_scalar_prefetch=3,
            grid=(N_TILES,),
            in_specs=in_specs,
            out_specs=out_specs,
            scratch_shapes=[pltpu.VMEM((2, SORT_ROWS, HALF), U32), pltpu.SemaphoreType.DMA((2,))],
        ),
        out_shape=out_shape,
        compiler_params=pltpu.CompilerParams(dimension_semantics=("arbitrary",)),
        name="moe_combine",
    )(*args)


def _dft_mats(n):
    k = np.arange(n, dtype=np.int64)
    ang = 2.0 * np.pi * ((k[:, None] * k[None, :]) % n).astype(np.float64) / n
    return np.cos(ang), np.sin(ang)


def _block_diag(m, reps):
    n = m.shape[0]
    out = np.zeros((n * reps, n * reps), m.dtype)
    for r in range(reps):
        out[r * n:(r + 1) * n, r * n:(r + 1) * n] = m
    return out


def _rope_tables():
    t = np.arange(DEC_SEQ)
    pos = np.stack([t // GRID_W, t % GRID_W], axis=1).astype(np.float64)
    n_freq = ROPE_AXIS_DIM // 2
    inv = ROPE_BASE ** (-np.arange(n_freq, dtype=np.float64) * 2.0 / ROPE_AXIS_DIM)
    ang = pos[:, :, None] * inv[None, None, :]
    cos = np.cos(ang)
    sin = np.sin(ang)
    zero = np.zeros_like(sin[:, 0])
    cos_h = np.concatenate([cos[:, 0], cos[:, 0], cos[:, 1], cos[:, 1]], axis=1)
    s1_h = np.concatenate([-sin[:, 0], zero, -sin[:, 1], zero], axis=1)
    s2_h = np.concatenate([zero, sin[:, 0], zero, sin[:, 1]], axis=1)
    reps = LANES // HEAD_DIM

    def table(a, ident):
        a = np.tile(a, (1, reps))
        pad = np.full((TM, LANES), ident, np.float64)
        return jnp.asarray(np.concatenate([a, pad], axis=0), F32)

    return table(cos_h, 1.0), table(s1_h, 0.0), table(s2_h, 0.0)


def kernel(x_prompt, x_sample, cache_k, cache_v, c, c_ctx, w_ada, b_ada, norm_mix_pre,
           norm_mix_post, norm_ffn_pre, norm_ffn_post, w_in, q_norm, k_norm, w_attn_out,
           w_pool_group, pool_scale, w_pool_out, w_fourier_out, w_out, w_router_group,
           b_router_group, w_router_expert, b_router_expert, w_expert_gate, w_expert_up,
           w_expert_down):
    cos_t, s1_t, s2_t = _rope_tables()
    avg = jnp.asarray(_block_diag(np.full((HEAD_DIM, HEAD_DIM), 1.0 / HEAD_DIM), LANES // HEAD_DIM), BF16)
    c64, s64 = _dft_mats(FOURIER_GROUP_DIM)
    n_fg = FOURIER_WIDTH // FOURIER_GROUP_DIM
    dft_ch = jnp.asarray(np.concatenate([_block_diag(c64, n_fg), _block_diag(s64, n_fg)], axis=1), BF16)
    cp, sp = _dft_mats(SEQ)
    cp, sp = jnp.asarray(cp, BF16), jnp.asarray(sp, BF16)
    cl, sl = _dft_mats(DEC_SEQ)
    cl, sl = jnp.asarray(cl, BF16), jnp.asarray(sl, BF16)
    tri = jnp.asarray(np.tril(np.ones((TM, TM)), -1), BF16)
    upper = jnp.asarray(np.triu(np.ones((LANES, LANES)), 1), BF16)

    w_in_b = w_in.astype(BF16)
    wa_b = w_attn_out.astype(BF16)
    wp_b = w_pool_out.astype(BF16)
    wf_b = w_fourier_out.astype(BF16)
    wo_b = w_out.astype(BF16)
    pad_r = jnp.zeros((DEPTH, D_MODEL, LANES - N_EXPERTS - N_EXPERT_GROUPS), F32)
    wr_b = jnp.concatenate([w_router_expert, w_router_group, pad_r], axis=2).astype(BF16)
    br = jnp.concatenate([b_router_expert, b_router_group,
                          jnp.zeros((DEPTH, LANES - N_EXPERTS - N_EXPERT_GROUPS), F32)],
                         axis=1).reshape(DEPTH, 1, LANES)
    n_pg = POOL_WIDTH // POOL_GROUP_DIM
    bdw = jnp.zeros((DEPTH, POOL_WIDTH, POOL_WIDTH), F32)
    for g in range(n_pg):
        lo = g * POOL_GROUP_DIM
        bdw = bdw.at[:, lo:lo + POOL_GROUP_DIM, lo:lo + POOL_GROUP_DIM].set(w_pool_group[:, g])
    bdw = bdw.astype(BF16)
    pscale = pool_scale.reshape(DEPTH, 1, POOL_WIDTH)
    qg = jnp.tile(q_norm, (1, LANES // HEAD_DIM)).reshape(DEPTH, 1, LANES)
    kg = jnp.tile(k_norm, (1, LANES // HEAD_DIM)).reshape(DEPTH, 1, LANES)
    gpre = norm_mix_pre.reshape(DEPTH, 1, D_MODEL)
    gpost = norm_mix_post.reshape(DEPTH, 1, D_MODEL)
    gffn = norm_ffn_pre.reshape(DEPTH, 1, D_MODEL)
    gfpost = norm_ffn_post.reshape(DEPTH, 1, D_MODEL)
    ck = cache_k.reshape(DEC_BATCH, DEPTH, PAST_LEN, KV_WIDTH)
    cv = cache_v.reshape(DEC_BATCH, DEPTH, PAST_LEN, KV_WIDTH)

    c_all = jnp.concatenate([c_ctx[None, :], c, jnp.zeros((MOD_ROWS - 1 - DEC_BATCH, D_MODEL), F32)], axis=0)
    mods = _mod_call(c_all, w_ada, b_ada).reshape(DEPTH, MOD_ROWS, 1, N_MOD * D_MODEL)

    x = jnp.concatenate([x_prompt.reshape(N_P, D_MODEL), x_sample.reshape(N_S, D_MODEL)], axis=0)
    xs_buf = jnp.zeros((XS_ROWS, HALF), U32)
    new_k, new_v = [], []
    hb = _prenorm_call(0, x, mods, gpre)
    for l in range(DEPTH):
        q, k, v, xp, xc, xsn, gates = _proj_call(l, hb, w_in_b, qg, kg, cos_t, s1_t, s2_t, avg, dft_ch)
        new_k.append(k[:N_P].reshape(BATCH, SEQ, N_KV_HEADS, HEAD_DIM))
        new_v.append(v[:N_P].reshape(BATCH, SEQ, N_KV_HEADS, HEAD_DIM))
        attn = (_attn_prompt_call(q, k, v), _attn_sample_call(l, q, k, v, ck, cv))
        pool = (_pool_call(l, xp, bdw, pscale, SEQ, BATCH, 0),
                _pool_call(l, xp, bdw, pscale, DEC_SEQ, DEC_BATCH, N_P // DEC_SEQ))
        four = (_fourier_prompt_call(cp, sp, xc, xsn), _fourier_sample_call(cl, sl, xc, xsn))
        x1, h2, pos, post, mw, cnt = _merge_call(l, x, attn, pool, four, gates, mods, gpost, gffn,
                                                 wa_b, wp_b, wf_b, wo_b, wr_b, br, tri, upper)
        runs = cnt.reshape(N_TILES, MOD_ROWS, LANES)[:, 0, :N_EXPERTS].astype(jnp.int32)
        runs = ((runs + RUN_ALIGN - 1) // RUN_ALIGN) * RUN_ALIGN
        tile_off = jnp.cumsum(runs, axis=1) - runs
        rows_e = jnp.sum(runs, axis=0)
        padded = ((rows_e + EBLK - 1) // EBLK) * EBLK
        pad_end = jnp.cumsum(padded)
        xs_off = (pad_end - padded)[None, :] + jnp.cumsum(runs, axis=0) - runs
        block_expert = jnp.minimum(
            jnp.searchsorted(pad_end, jnp.arange(N_EBLK, dtype=jnp.int32) * EBLK, side='right'),
            N_EXPERTS - 1).astype(jnp.int32)
        n_used = (pad_end[-1:] // EBLK).astype(jnp.int32)
        run_cnt = (runs // RUN_ALIGN).reshape(N_RUNS)
        tile_off = tile_off.reshape(N_RUNS)
        xs_off = xs_off.reshape(N_RUNS)
        xs_buf = _dispatch_call(run_cnt, tile_off, xs_off, h2, post, xs_buf)
        y = _expert_call(l, block_expert, n_used, xs_buf, w_expert_gate, w_expert_up, w_expert_down)
        outs = _combine_call(l, run_cnt, tile_off, xs_off, y, x1, pos, mw, mods, gfpost, gpre)
        x = outs[0]
        hb = outs[-1]

    y_prompt = x[:N_P].reshape(BATCH, SEQ, D_MODEL)
    y_sample = x[N_P:].reshape(DEC_BATCH, DEC_SEQ, D_MODEL)
    return (y_prompt, y_sample, jnp.stack(new_k, axis=1), jnp.stack(new_v, axis=1))
```

```python
import functools
import math

import numpy as np
import jax
import jax.numpy as jnp
from jax import lax
from jax.experimental import pallas as pl
from jax.experimental.pallas import tpu as pltpu

F32 = jnp.float32
BF16 = jnp.bfloat16

D_MODEL = 1024
BATCH = 32
SEQ = 256
DEPTH = 4
DEC_BATCH = 2
DEC_SEQ = 2048
PAST_LEN = 512
GRID_W = 64
N_HEADS = 8
N_KV_HEADS = 2
HEAD_DIM = 64
KV_GROUP = N_HEADS // N_KV_HEADS
ATTN_WIDTH = N_HEADS * HEAD_DIM
KV_WIDTH = N_KV_HEADS * HEAD_DIM
ROPE_AXIS_DIM = HEAD_DIM // 2
ROPE_BASE = 10000.0
POOL_WINDOWS = (2, 4, 8, 16)
POOL_WIDTH = 256
POOL_GROUP_DIM = 64
FOURIER_WIDTH = 256
FOURIER_GROUP_DIM = 64
N_BRANCHES = 3
OFF_K = ATTN_WIDTH
OFF_P = ATTN_WIDTH + 2 * KV_WIDTH
OFF_G = OFF_P + POOL_WIDTH + FOURIER_WIDTH
IN_WIDTH = OFF_G + N_BRANCHES * D_MODEL
N_EXPERT_GROUPS = 4
EXPERTS_PER_GROUP = 8
N_EXPERTS = N_EXPERT_GROUPS * EXPERTS_PER_GROUP
TOP_K = 2
EXPERT_FF = 256
N_MOD = 6
RMS_EPS = 1e-6
QK_SCALE = HEAD_DIM ** -0.5 * math.log2(math.e)

N_P = BATCH * SEQ
N_S = DEC_BATCH * DEC_SEQ
N_TOK = N_P + N_S
LANES = 128
MOD_ROWS = 8
POOL_PAD = 16

TM = 512
N_TILES = N_TOK // TM
P_TILES = N_P // TM
S_TILES_PER_SEQ = DEC_SEQ // TM
TQ_S = 128
KEY_CHUNK = 512
TF_S = 512
EBLK = 512
RUN_ALIGN = 8
RUN_SHIFT = 3
RUN_BITS = 7
assert RUN_ALIGN << (RUN_BITS - 1) == TM
SORT_ROWS = TOP_K * TM + N_EXPERTS * RUN_ALIGN
TILE_BITS = 8
assert SORT_ROWS < RUN_ALIGN << TILE_BITS
N_RUNS = N_TILES * N_EXPERTS
N_EBLK = (N_TOK * TOP_K + N_RUNS * (RUN_ALIGN - 1) + N_EXPERTS * (EBLK - 1) + EBLK - 1) // EBLK
XS_ROWS = N_EBLK * EBLK
MOD_NT = 1536


def _dot(a, b):
    return jnp.dot(a, b, preferred_element_type=F32)


def _rms(x):
    return x * lax.rsqrt(jnp.mean(x * x, axis=-1, keepdims=True) + RMS_EPS)


def _mod_row(i):
    return jnp.where(i < P_TILES, 0, 1 + (i - P_TILES) // S_TILES_PER_SEQ)


def _rope_block(i):
    return jnp.where(i < P_TILES, S_TILES_PER_SEQ, (i - P_TILES) % S_TILES_PER_SEQ)


def _mod_kernel(c_ref, w_ref, b_ref, o_ref):
    c = c_ref[...]
    s = (c * jax.nn.sigmoid(c)).astype(BF16)
    o_ref[...] = _dot(s, w_ref[...].astype(BF16)) + b_ref[...]


def _mod_call(c_all, w_ada, b_ada):
    nt = (N_MOD * D_MODEL) // MOD_NT
    return pl.pallas_call(
        _mod_kernel,
        grid=(DEPTH, nt),
        in_specs=[
            pl.BlockSpec((MOD_ROWS, D_MODEL), lambda l, j: (0, 0)),
            pl.BlockSpec((None, D_MODEL, MOD_NT), lambda l, j: (l, 0, j)),
            pl.BlockSpec((None, 1, MOD_NT), lambda l, j: (l, 0, j)),
        ],
        out_specs=pl.BlockSpec((None, MOD_ROWS, MOD_NT), lambda l, j: (l, 0, j)),
        out_shape=jax.ShapeDtypeStruct((DEPTH, MOD_ROWS, N_MOD * D_MODEL), F32),
        name="adaln_mod",
    )(c_all, w_ada, b_ada.reshape(DEPTH, 1, N_MOD * D_MODEL))


def _prenorm(x, mod, gain):
    return (_rms(x) * gain) * (1.0 + mod[:, D_MODEL:2 * D_MODEL]) + mod[:, 0:D_MODEL]


def _prenorm_kernel(x_ref, mod_ref, gpre_ref, hb_ref):
    hb_ref[...] = _prenorm(x_ref[...], mod_ref[...], gpre_ref[...]).astype(BF16)


def _prenorm_call(l, x, mods, gpre):
    return pl.pallas_call(
        _prenorm_kernel,
        grid=(N_TILES,),
        in_specs=[
            pl.BlockSpec((TM, D_MODEL), lambda i: (i, 0)),
            pl.BlockSpec((None, None, 1, 2 * D_MODEL), lambda i: (l, _mod_row(i), 0, 0)),
            pl.BlockSpec((None, 1, D_MODEL), lambda i: (l, 0, 0)),
        ],
        out_specs=pl.BlockSpec((TM, D_MODEL), lambda i: (i, 0)),
        out_shape=jax.ShapeDtypeStruct((N_TOK, D_MODEL), BF16),
        name="prenorm",
    )(x, mods, gpre)


def _proj_kernel(hb_ref, w_ref, qg_ref, kg_ref, cos_ref, s1_ref, s2_ref,
                 avg_ref, dft_ref, q_ref, k_ref, v_ref, xp_ref, xc_ref, xs_ref, g_ref):
    hb = hb_ref[...]

    cos = cos_ref[...]
    s1 = s1_ref[...]
    s2 = s2_ref[...]

    def rope(t):
        return (t * cos + pltpu.roll(t, LANES - ROPE_AXIS_DIM // 2, 1) * s1
                + pltpu.roll(t, ROPE_AXIS_DIM // 2, 1) * s2)

    avg = avg_ref[...]
    for c in range(ATTN_WIDTH // LANES):
        lo = c * LANES
        q = _dot(hb, w_ref[:, lo:lo + LANES])
        ms = _dot((q * q).astype(BF16), avg)
        q = q * lax.rsqrt(ms + RMS_EPS) * qg_ref[...]
        q_ref[:, lo:lo + LANES] = (rope(q) * QK_SCALE).astype(BF16)

    kv = _dot(hb, w_ref[:, OFF_K:OFF_K + 2 * KV_WIDTH])
    k = kv[:, 0:KV_WIDTH]
    ms = _dot((k * k).astype(BF16), avg)
    k = k * lax.rsqrt(ms + RMS_EPS) * kg_ref[...]
    k_ref[...] = rope(k)
    v_ref[...] = kv[:, KV_WIDTH:]

    pf = _dot(hb, w_ref[:, OFF_P:OFF_G])
    xp_ref[...] = pf[:, 0:POOL_WIDTH]
    cs = _dot(pf[:, POOL_WIDTH:].astype(BF16), dft_ref[...])
    xc_ref[...] = cs[:, 0:FOURIER_WIDTH].astype(BF16)
    xs_ref[...] = cs[:, FOURIER_WIDTH:].astype(BF16)

    for c in range(N_BRANCHES):
        lo = OFF_G + c * D_MODEL
        g_ref[:, c * D_MODEL:(c + 1) * D_MODEL] = _dot(hb, w_ref[:, lo:lo + D_MODEL]).astype(BF16)


def _proj_call(l, hb, w_in, qg, kg, cos_t, s1_t, s2_t, avg, dft):
    row = lambda i: (i, 0)
    const2 = lambda i: (0, 0)
    per_layer = lambda i: (l, 0, 0)
    rope_spec = pl.BlockSpec((TM, LANES), lambda i: (_rope_block(i), 0))
    return pl.pallas_call(
        _proj_kernel,
        grid=(N_TILES,),
        in_specs=[
            pl.BlockSpec((TM, D_MODEL), row),
            pl.BlockSpec((None, D_MODEL, IN_WIDTH), per_layer),
            pl.BlockSpec((None, 1, LANES), per_layer),
            pl.BlockSpec((None, 1, LANES), per_layer),
            rope_spec, rope_spec, rope_spec,
            pl.BlockSpec((LANES, LANES), const2),
            pl.BlockSpec((FOURIER_WIDTH, 2 * FOURIER_WIDTH), const2),
        ],
        out_specs=[
            pl.BlockSpec((TM, ATTN_WIDTH), row),
            pl.BlockSpec((TM, KV_WIDTH), row),
            pl.BlockSpec((TM, KV_WIDTH), row),
            pl.BlockSpec((TM, POOL_WIDTH), row),
            pl.BlockSpec((TM, FOURIER_WIDTH), row),
            pl.BlockSpec((TM, FOURIER_WIDTH), row),
            pl.BlockSpec((TM, N_BRANCHES * D_MODEL), row),
        ],
        out_shape=[
            jax.ShapeDtypeStruct((N_TOK, ATTN_WIDTH), BF16),
            jax.ShapeDtypeStruct((N_TOK, KV_WIDTH), F32),
            jax.ShapeDtypeStruct((N_TOK, KV_WIDTH), F32),
            jax.ShapeDtypeStruct((N_TOK, POOL_WIDTH), F32),
            jax.ShapeDtypeStruct((N_TOK, FOURIER_WIDTH), BF16),
            jax.ShapeDtypeStruct((N_TOK, FOURIER_WIDTH), BF16),
            jax.ShapeDtypeStruct((N_TOK, N_BRANCHES * D_MODEL), BF16),
        ],
        name="proj",
    )(hb, w_in, qg, kg, cos_t, s1_t, s2_t, avg, dft)


def _attn_kernel(*refs, n_parts, tq):
    q_ref = refs[0]
    kv_refs = refs[1:1 + 2 * n_parts]
    o_ref = refs[-1]
    single_chunk = n_parts == 1 and kv_refs[0].shape[0] <= KEY_CHUNK
    outs = []
    for j in range(N_KV_HEADS):
        lo = j * HEAD_DIM
        qs = jnp.concatenate(
            [q_ref[:, (KV_GROUP * j + g) * HEAD_DIM:(KV_GROUP * j + g + 1) * HEAD_DIM]
             for g in range(KV_GROUP)], axis=0)
        m = acc = None
        for p in range(n_parts):
            k_ref, v_ref = kv_refs[2 * p], kv_refs[2 * p + 1]
            for c0 in range(0, k_ref.shape[0], KEY_CHUNK):
                c1 = min(c0 + KEY_CHUNK, k_ref.shape[0])
                kc = k_ref[c0:c1, lo:lo + HEAD_DIM].astype(BF16)
                s = lax.dot_general(qs, kc, (((1,), (1,)), ((), ())), preferred_element_type=F32)
                mc = jnp.max(s, axis=1, keepdims=True)
                m_new = mc if m is None else jnp.maximum(m, mc)
                e = jnp.exp2(s - m_new)
                vc = v_ref[c0:c1, lo:lo + HEAD_DIM].astype(BF16)
                if single_chunk:
                    den = jnp.sum(e, axis=1, keepdims=True)
                else:
                    vc = jnp.concatenate([vc, jnp.ones((c1 - c0, HEAD_DIM), BF16)], axis=1)
                pv = _dot(e.astype(BF16), vc)
                acc = pv if m is None else acc * jnp.exp2(m - m_new) + pv
                m = m_new
        o = acc / den if single_chunk else acc[:, 0:HEAD_DIM] / acc[:, HEAD_DIM:2 * HEAD_DIM]
        outs.extend(o[g * tq:(g + 1) * tq] for g in range(KV_GROUP))
    o_ref[...] = jnp.concatenate(outs, axis=1).astype(BF16)


def _attn_prompt_call(q, k, v):
    blk = lambda b: (b, 0)
    return pl.pallas_call(
        functools.partial(_attn_kernel, n_parts=1, tq=SEQ),
        grid=(BATCH,),
        in_specs=[
            pl.BlockSpec((SEQ, ATTN_WIDTH), blk),
            pl.BlockSpec((SEQ, KV_WIDTH), blk),
            pl.BlockSpec((SEQ, KV_WIDTH), blk),
        ],
        out_specs=pl.BlockSpec((SEQ, ATTN_WIDTH), blk),
        out_shape=jax.ShapeDtypeStruct((N_P, ATTN_WIDTH), BF16),
        name="attn_context",
    )(q, k, v)


def _attn_sample_call(l, q, k, v, cache_k, cache_v):
    nq = DEC_SEQ // TQ_S
    qrow = lambda b, i: (N_P // TQ_S + b * nq + i, 0)
    seq = lambda b, i: (N_P // DEC_SEQ + b, 0)
    cache = lambda b, i: (b, l, 0, 0)
    return pl.pallas_call(
        functools.partial(_attn_kernel, n_parts=2, tq=TQ_S),
        grid=(DEC_BATCH, nq),
        in_specs=[
            pl.BlockSpec((TQ_S, ATTN_WIDTH), qrow),
            pl.BlockSpec((None, None, PAST_LEN, KV_WIDTH), cache),
            pl.BlockSpec((None, None, PAST_LEN, KV_WIDTH), cache),
            pl.BlockSpec((DEC_SEQ, KV_WIDTH), seq),
            pl.BlockSpec((DEC_SEQ, KV_WIDTH), seq),
        ],
        out_specs=pl.BlockSpec((TQ_S, ATTN_WIDTH), lambda b, i: (b * nq + i, 0)),
        out_shape=jax.ShapeDtypeStruct((N_S, ATTN_WIDTH), BF16),
        name="attn_latent",
    )(q, cache_k, cache_v, k, v)


def _pool_kernel(xp_ref, bdw_ref, sc_ref, o_ref, pad_ref, *, seq_len):
    half = POOL_WIDTH // 2
    zeros = jnp.zeros((POOL_PAD, POOL_WIDTH), F32)
    pad_ref[0:POOL_PAD, :] = zeros
    pad_ref[POOL_PAD + seq_len:, :] = zeros
    pad_ref[POOL_PAD:POOL_PAD + seq_len, :] = xp_ref[...]
    chunk = min(seq_len, 256)
    lane = lax.broadcasted_iota(jnp.int32, (chunk, half), 1)
    first = lane < POOL_GROUP_DIM
    for c in range(seq_len // chunk):
        base = c * chunk
        t = lax.broadcasted_iota(jnp.int32, (chunk, half), 0) + base

        def sh(j, lo):
            return pad_ref[POOL_PAD + base + j:POOL_PAD + base + j + chunk, lo:lo + half]

        def cnt(w):
            return (jnp.minimum(t + w // 2, seq_len) - jnp.maximum(t - w // 2, 0)).astype(F32)

        xa = sh(0, 0)
        w2 = sh(-1, 0) + xa
        w4 = w2 + sh(-2, 0) + sh(1, 0)
        xb = sh(0, half)
        w8 = xb
        for j in (-4, -3, -2, -1, 1, 2, 3):
            w8 = w8 + sh(j, half)
        w16 = w8
        for j in (-8, -7, -6, -5, 4, 5, 6, 7):
            w16 = w16 + sh(j, half)
        pa = jnp.where(first, w2 / cnt(2), w4 / cnt(4)) - xa
        pb = jnp.where(first, w8 / cnt(8), w16 / cnt(16)) - xb
        pooled = jnp.concatenate([pa, pb], axis=1).astype(BF16)
        o_ref[base:base + chunk, :] = (_dot(pooled, bdw_ref[...]) * sc_ref[...]).astype(BF16)


def _pool_call(l, xp, bdw, scale, seq_len, n_seq, blk0):
    per_layer = lambda b: (l, 0, 0)
    return pl.pallas_call(
        functools.partial(_pool_kernel, seq_len=seq_len),
        grid=(n_seq,),
        in_specs=[
            pl.BlockSpec((seq_len, POOL_WIDTH), lambda b: (blk0 + b, 0)),
            pl.BlockSpec((None, POOL_WIDTH, POOL_WIDTH), per_layer),
            pl.BlockSpec((None, 1, POOL_WIDTH), per_layer),
        ],
        out_specs=pl.BlockSpec((seq_len, POOL_WIDTH), lambda b: (b, 0)),
        out_shape=jax.ShapeDtypeStruct((n_seq * seq_len, POOL_WIDTH), BF16),
        scratch_shapes=[pltpu.VMEM((seq_len + 2 * POOL_PAD, POOL_WIDTH), F32)],
        name="pool_%d" % seq_len,
    )(xp, bdw, scale)


def _fourier_kernel(c_ref, s_ref, xc_ref, xs_ref, o_ref, *, scale):
    y = _dot(c_ref[...], xc_ref[...]) - _dot(s_ref[...], xs_ref[...])
    o_ref[...] = (y * scale).astype(BF16)


def _fourier_prompt_call(cmat, smat, xc, xs):
    blk = lambda b: (b, 0)
    const2 = lambda b: (0, 0)
    return pl.pallas_call(
        functools.partial(_fourier_kernel, scale=1.0 / math.sqrt(SEQ * FOURIER_GROUP_DIM)),
        grid=(BATCH,),
        in_specs=[
            pl.BlockSpec((SEQ, SEQ), const2),
            pl.BlockSpec((SEQ, SEQ), const2),
            pl.BlockSpec((SEQ, FOURIER_WIDTH), blk),
            pl.BlockSpec((SEQ, FOURIER_WIDTH), blk),
        ],
        out_specs=pl.BlockSpec((SEQ, FOURIER_WIDTH), blk),
        out_shape=jax.ShapeDtypeStruct((N_P, FOURIER_WIDTH), BF16),
        name="fourier_context",
    )(cmat, smat, xc, xs)


def _fourier_sample_call(cmat, smat, xc, xs):
    nt = DEC_SEQ // TF_S
    rows = lambda b, i: (i, 0)
    seq = lambda b, i: (N_P // DEC_SEQ + b, 0)
    out = lambda b, i: (b * nt + i, 0)
    return pl.pallas_call(
        functools.partial(_fourier_kernel, scale=1.0 / math.sqrt(DEC_SEQ * FOURIER_GROUP_DIM)),
        grid=(DEC_BATCH, nt),
        in_specs=[
            pl.BlockSpec((TF_S, DEC_SEQ), rows),
            pl.BlockSpec((TF_S, DEC_SEQ), rows),
            pl.BlockSpec((DEC_SEQ, FOURIER_WIDTH), seq),
            pl.BlockSpec((DEC_SEQ, FOURIER_WIDTH), seq),
        ],
        out_specs=pl.BlockSpec((TF_S, FOURIER_WIDTH), out),
        out_shape=jax.ShapeDtypeStruct((N_S, FOURIER_WIDTH), BF16),
        name="fourier_latent",
    )(cmat, smat, xc, xs)


def _merge_kernel(x_ref, attn_p_ref, attn_s_ref, pool_p_ref, pool_s_ref, four_p_ref, four_s_ref,
                  g_ref, mod_ref, gpost_ref, gffn_ref,
                  wa_ref, wp_ref, wf_ref, wo_ref, wr_ref, br_ref, tri_ref, upper_ref,
                  x1_ref, h2_ref, pos_ref, post_ref, mw_ref, cnt_ref):
    i = pl.program_id(0)
    mod = mod_ref[...]
    g1 = mod[:, 2 * D_MODEL:3 * D_MODEL]
    sh2 = mod[:, 3 * D_MODEL:4 * D_MODEL]
    sc2 = mod[:, 4 * D_MODEL:5 * D_MODEL]

    def gate(c):
        return jax.nn.sigmoid(g_ref[:, c * D_MODEL:(c + 1) * D_MODEL].astype(F32))

    def branch(p_ref, s_ref):
        return jnp.where(i < P_TILES, p_ref[...], s_ref[...])

    merged = gate(0) * _dot(branch(attn_p_ref, attn_s_ref), wa_ref[...])
    merged = merged + gate(1) * _dot(branch(pool_p_ref, pool_s_ref), wp_ref[...])
    merged = merged + gate(2) * _dot(branch(four_p_ref, four_s_ref), wf_ref[...])
    mix = _dot(merged.astype(BF16), wo_ref[...])
    x1 = x_ref[...] + g1 * (_rms(mix) * gpost_ref[...])
    x1_ref[...] = x1
    h2 = (_rms(x1) * gffn_ref[...]) * (1.0 + sc2) + sh2
    h2b = h2.astype(BF16)
    h2_ref[...] = h2b

    logits = _dot(h2b, wr_ref[...]) + br_ref[...]
    lane = lax.broadcasted_iota(jnp.int32, (TM, LANES), 1)
    lanef = lane.astype(F32)
    neg = jnp.float32(-3e38)
    big = jnp.float32(1e9)
    is_g = (lane >= N_EXPERTS) & (lane < N_EXPERTS + N_EXPERT_GROUPS)
    lg = jnp.where(is_g, logits, neg)
    gmax = jnp.max(lg, axis=1, keepdims=True)
    g_sel = jnp.min(jnp.where(lg == gmax, lanef - N_EXPERTS, big), axis=1, keepdims=True)
    p_g = 1.0 / jnp.sum(jnp.where(is_g, jnp.exp(logits - gmax), 0.0), axis=1, keepdims=True)
    grp = lax.shift_right_logical(lane, int(math.log2(EXPERTS_PER_GROUP))).astype(F32)
    in_grp = (lane < N_EXPERTS) & (grp == g_sel)
    le = jnp.where(in_grp, logits, neg)
    v1 = jnp.max(le, axis=1, keepdims=True)
    i1 = jnp.min(jnp.where(le == v1, lanef, big), axis=1, keepdims=True)
    le2 = jnp.where(lanef == i1, neg, le)
    v2 = jnp.max(le2, axis=1, keepdims=True)
    i2 = jnp.min(jnp.where(le2 == v2, lanef, big), axis=1, keepdims=True)
    e21 = jnp.exp(v2 - v1)
    w1 = p_g / (1.0 + e21)
    w2 = p_g * e21 / (1.0 + e21)

    oh1 = (lanef == i1).astype(F32)
    oh2 = (lanef == i2).astype(F32)
    ohb = (oh1 + oh2).astype(BF16)
    before = _dot(tri_ref[...], ohb)
    cnt = _dot(jnp.ones((MOD_ROWS, TM), BF16), ohb)
    cnt_pad = (lax.shift_right_logical(cnt.astype(jnp.int32) + (RUN_ALIGN - 1), RUN_SHIFT)
               * RUN_ALIGN).astype(F32)
    run_off = _dot(cnt_pad.astype(BF16), upper_ref[...])
    slot = run_off[0:1, :] + before
    p1 = jnp.sum(slot * oh1, axis=1, keepdims=True)
    p2 = jnp.sum(slot * oh2, axis=1, keepdims=True)
    cnt_ref[...] = cnt

    pos = jnp.where(lane == 0, p1, jnp.where(lane == 1, p2, 0.0))
    pos_ref[...] = pos.astype(jnp.int32)
    post_ref[...] = pos.T[0:MOD_ROWS, :].astype(jnp.int32)
    mw_ref[...] = jnp.where(lane == 0, w1, w2)


def _merge_call(l, x, attn, pool, four, gates, mods, gpost, gffn, wa, wp, wf, wo, wr, br, tri,
                upper):
    row = lambda i: (i, 0)
    prow = lambda i: (jnp.minimum(i, P_TILES - 1), 0)
    srow = lambda i: (jnp.maximum(i - P_TILES, 0), 0)
    const2 = lambda i: (0, 0)
    per_layer = lambda i: (l, 0, 0)
    return pl.pallas_call(
        _merge_kernel,
        grid=(N_TILES,),
        in_specs=[
            pl.BlockSpec((TM, D_MODEL), row),
            pl.BlockSpec((TM, ATTN_WIDTH), prow),
            pl.BlockSpec((TM, ATTN_WIDTH), srow),
            pl.BlockSpec((TM, POOL_WIDTH), prow),
            pl.BlockSpec((TM, POOL_WIDTH), srow),
            pl.BlockSpec((TM, FOURIER_WIDTH), prow),
            pl.BlockSpec((TM, FOURIER_WIDTH), srow),
            pl.BlockSpec((TM, N_BRANCHES * D_MODEL), row),
            pl.BlockSpec((None, None, 1, N_MOD * D_MODEL), lambda i: (l, _mod_row(i), 0, 0)),
            pl.BlockSpec((None, 1, D_MODEL), per_layer),
            pl.BlockSpec((None, 1, D_MODEL), per_layer),
            pl.BlockSpec((None, ATTN_WIDTH, D_MODEL), per_layer),
            pl.BlockSpec((None, POOL_WIDTH, D_MODEL), per_layer),
            pl.BlockSpec((None, FOURIER_WIDTH, D_MODEL), per_layer),
            pl.BlockSpec((None, D_MODEL, D_MODEL), per_layer),
            pl.BlockSpec((None, D_MODEL, LANES), per_layer),
            pl.BlockSpec((None, 1, LANES), per_layer),
            pl.BlockSpec((TM, TM), const2),
            pl.BlockSpec((LANES, LANES), const2),
        ],
        out_specs=[
            pl.BlockSpec((TM, D_MODEL), row),
            pl.BlockSpec((TM, D_MODEL), row),
            pl.BlockSpec((TM, LANES), row),
            pl.BlockSpec((MOD_ROWS, TM), lambda i: (0, i)),
            pl.BlockSpec((TM, LANES), row),
            pl.BlockSpec((MOD_ROWS, LANES), row),
        ],
        out_shape=[
            jax.ShapeDtypeStruct((N_TOK, D_MODEL), F32),
            jax.ShapeDtypeStruct((N_TOK, D_MODEL), BF16),
            jax.ShapeDtypeStruct((N_TOK, LANES), jnp.int32),
            jax.ShapeDtypeStruct((MOD_ROWS, N_TOK), jnp.int32),
            jax.ShapeDtypeStruct((N_TOK, LANES), F32),
            jax.ShapeDtypeStruct((N_TILES * MOD_ROWS, LANES), F32),
        ],
        name="merge_router",
    )(x, attn[0], attn[1], pool[0], pool[1], four[0], four[1], gates, mods, gpost, gffn,
      wa, wp, wf, wo, wr, br, tri, upper)


HALF = D_MODEL // 2
U32 = jnp.uint32
HI_MASK = 0xFFFF0000


def _pack_rows(x):
    lo = lax.bitcast_convert_type(x[:, :HALF], U32)
    hi = lax.bitcast_convert_type(x[:, HALF:], U32)
    return lax.shift_right_logical(lo, U32(16)) | (hi & U32(HI_MASK))


def _unpack_rows(u):
    lo = lax.bitcast_convert_type(lax.shift_left(u, U32(16)), F32)
    hi = lax.bitcast_convert_type(u & U32(HI_MASK), F32)
    return jnp.concatenate([lo.astype(BF16), hi.astype(BF16)], axis=1)


def _run_copies(cnt, src_ref, src_off, dst_ref, dst_off, sem, wait, bits=RUN_BITS):
    off = jnp.int32(0)
    for b in reversed(range(bits)):
        size = RUN_ALIGN << b
        take = lax.shift_right_logical(cnt, b) & 1

        @pl.when(take == 1)
        def _(off=off, size=size):
            cp = pltpu.make_async_copy(
                src_ref.at[pl.ds(pl.multiple_of(src_off + off, RUN_ALIGN), size)],
                dst_ref.at[pl.ds(pl.multiple_of(dst_off + off, RUN_ALIGN), size)], sem)
            if wait:
                cp.wait()
            else:
                cp.start()

        off = off + take * size


def _tile_runs(tile, cnt_ref, fn):
    def body(e, carry):
        r = tile * N_EXPERTS + e
        fn(r, cnt_ref[r])
        return carry

    lax.fori_loop(0, N_EXPERTS, body, 0)


def _dispatch_kernel(cnt_ref, toff_ref, xoff_ref, tsum_ref, h_ref, post_ref, xs_in_ref, xs_ref,
                     sorted_ref, sem):
    del xs_in_ref
    i = pl.program_id(0)
    slot = lax.rem(i, 2)
    rows = lax.broadcasted_iota(jnp.int32, (SORT_ROWS, TM), 0)
    p = post_ref[...]
    perm = jnp.where(rows == p[0:1, :], 1.0, jnp.where(rows == p[1:2, :], 1.0, 0.0)).astype(BF16)
    sorted_ref[slot] = _pack_rows(_dot(perm, h_ref[...]))

    def wait(tile, slot):
        _run_copies(tsum_ref[tile], sorted_ref.at[slot], 0, xs_ref, 0, sem.at[slot], True, TILE_BITS)

    @pl.when(i > 0)
    def _():
        wait(i - 1, 1 - slot)

    _tile_runs(i, cnt_ref, lambda r, c: _run_copies(
        c, sorted_ref.at[slot], toff_ref[r], xs_ref, xoff_ref[r], sem.at[slot], False))

    @pl.when(i == pl.num_programs(0) - 1)
    def _():
        wait(i, slot)


def _dispatch_call(cnt, toff, xoff, tsum, h2, post, xs):
    return pl.pallas_call(
        _dispatch_kernel,
        grid_spec=pltpu.PrefetchScalarGridSpec(
            num_scalar_prefetch=4,
            grid=(N_TILES,),
            in_specs=[
                pl.BlockSpec((TM, D_MODEL), lambda i, *_: (i, 0)),
                pl.BlockSpec((MOD_ROWS, TM), lambda i, *_: (0, i)),
                pl.BlockSpec(memory_space=pl.ANY),
            ],
            out_specs=pl.BlockSpec(memory_space=pl.ANY),
            scratch_shapes=[pltpu.VMEM((2, SORT_ROWS, HALF), U32), pltpu.SemaphoreType.DMA((2,))],
        ),
        out_shape=jax.ShapeDtypeStruct((XS_ROWS, HALF), U32),
        input_output_aliases={6: 0},
        compiler_params=pltpu.CompilerParams(dimension_semantics=("arbitrary",)),
        name="moe_dispatch",
    )(cnt, toff, xoff, tsum, h2, post, xs)


def _expert_kernel(be_ref, nu_ref, x_ref, wg_ref, wu_ref, wd_ref, y_ref):
    b = pl.program_id(0)

    @pl.when(b < nu_ref[0])
    def _():
        xb = _unpack_rows(x_ref[...])
        g = _dot(xb, wg_ref[...].astype(BF16))
        u = _dot(xb, wu_ref[...].astype(BF16))
        a = (g * jax.nn.sigmoid(g)) * u
        y = _dot(a.astype(BF16), wd_ref[...].astype(BF16))
        y_ref[...] = _pack_rows(y.astype(BF16).astype(F32))

    @pl.when(b >= nu_ref[0])
    def _():
        y_ref[...] = jnp.zeros_like(y_ref)


def _expert_call(l, block_expert, n_used, xs, w_gate, w_up, w_down):
    row = lambda b, be, nu: (b, 0)
    used_row = lambda b, be, nu: (jnp.minimum(b, nu[0] - 1), 0)
    wsel = lambda b, be, nu: (l, be[b], 0, 0)
    return pl.pallas_call(
        _expert_kernel,
        grid_spec=pltpu.PrefetchScalarGridSpec(
            num_scalar_prefetch=2,
            grid=(N_EBLK,),
            in_specs=[
                pl.BlockSpec((EBLK, HALF), used_row),
                pl.BlockSpec((None, None, D_MODEL, EXPERT_FF), wsel),
                pl.BlockSpec((None, None, D_MODEL, EXPERT_FF), wsel),
                pl.BlockSpec((None, None, EXPERT_FF, D_MODEL), wsel),
            ],
            out_specs=pl.BlockSpec((EBLK, HALF), row),
        ),
        out_shape=jax.ShapeDtypeStruct((XS_ROWS, HALF), U32),
        compiler_params=pltpu.CompilerParams(dimension_semantics=("arbitrary",)),
        name="moe_experts",
    )(block_expert, n_used, xs, w_gate, w_up, w_down)


def _combine_kernel(cnt_ref, toff_ref, xoff_ref, tsum_ref, y_ref, x1_ref, pos_ref, mw_ref, g2_ref,
                    gpost_ref, *rest, has_next):
    if has_next:
        nmod_ref, ngpre_ref, o_ref, hb_ref, ybuf, sem = rest
    else:
        o_ref, ybuf, sem = rest
    i = pl.program_id(0)
    slot = lax.rem(i, 2)

    def start(tile, slot):
        _tile_runs(tile, cnt_ref, lambda r, c: _run_copies(
            c, y_ref, xoff_ref[r], ybuf.at[slot], toff_ref[r], sem.at[slot], False))

    @pl.when(i == 0)
    def _():
        ybuf[...] = jnp.zeros_like(ybuf)
        start(0, 0)

    @pl.when(i + 1 < pl.num_programs(0))
    def _():
        start(i + 1, 1 - slot)

    _run_copies(tsum_ref[i], y_ref, 0, ybuf.at[slot], 0, sem.at[slot], True, TILE_BITS)

    pos = pos_ref[...]
    mw = mw_ref[...]
    cols = lax.broadcasted_iota(jnp.int32, (TM, SORT_ROWS), 1)
    qw = (jnp.where(cols == pos[:, 0:1], mw[:, 0:1], 0.0)
          + jnp.where(cols == pos[:, 1:2], mw[:, 1:2], 0.0)).astype(BF16)
    ffn = _dot(qw, _unpack_rows(ybuf[slot]))
    x2 = x1_ref[...] + g2_ref[...] * (_rms(ffn) * gpost_ref[...])
    o_ref[...] = x2
    if has_next:
        hb_ref[...] = _prenorm(x2, nmod_ref[...], ngpre_ref[...]).astype(BF16)


def _combine_call(l, cnt, toff, xoff, tsum, y, x1, pos, mw, mods, gpost, gpre):
    has_next = l + 1 < DEPTH
    row = lambda i, *_: (i, 0)
    in_specs = [
        pl.BlockSpec(memory_space=pl.ANY),
        pl.BlockSpec((TM, D_MODEL), row),
        pl.BlockSpec((TM, LANES), row),
        pl.BlockSpec((TM, LANES), row),
        pl.BlockSpec((None, None, 1, D_MODEL), lambda i, *_: (l, _mod_row(i), 0, N_MOD - 1)),
        pl.BlockSpec((None, 1, D_MODEL), lambda i, *_: (l, 0, 0)),
    ]
    args = [cnt, toff, xoff, tsum, y, x1, pos, mw, mods, gpost]
    out_specs = [pl.BlockSpec((TM, D_MODEL), row)]
    out_shape = [jax.ShapeDtypeStruct((N_TOK, D_MODEL), F32)]
    if has_next:
        in_specs += [
            pl.BlockSpec((None, None, 1, 2 * D_MODEL), lambda i, *_: (l + 1, _mod_row(i), 0, 0)),
            pl.BlockSpec((None, 1, D_MODEL), lambda i, *_: (l + 1, 0, 0)),
        ]
        args += [mods, gpre]
        out_specs.append(pl.BlockSpec((TM, D_MODEL), row))
        out_shape.append(jax.ShapeDtypeStruct((N_TOK, D_MODEL), BF16))
    return pl.pallas_call(
        functools.partial(_combine_kernel, has_next=has_next),
        grid_spec=pltpu.PrefetchScalarGridSpec(
            num_scalar_prefetch=4,
            grid=(N_TILES,),
            in_specs=in_specs,
            out_specs=out_specs,
            scratch_shapes=[pltpu.VMEM((2, SORT_ROWS, HALF), U32), pltpu.SemaphoreType.DMA((2,))],
        ),
        out_shape=out_shape,
        compiler_params=pltpu.CompilerParams(dimension_semantics=("arbitrary",)),
        name="moe_combine",
    )(*args)


def _dft_mats(n):
    k = np.arange(n, dtype=np.int64)
    ang = 2.0 * np.pi * ((k[:, None] * k[None, :]) % n).astype(np.float64) / n
    return np.cos(ang), np.sin(ang)


def _block_diag(m, reps):
    n = m.shape[0]
    out = np.zeros((n * reps, n * reps), m.dtype)
    for r in range(reps):
        out[r * n:(r + 1) * n, r * n:(r + 1) * n] = m
    return out


def _rope_tables():
    t = np.arange(DEC_SEQ)
    pos = np.stack([t // GRID_W, t % GRID_W], axis=1).astype(np.float64)
    n_freq = ROPE_AXIS_DIM // 2
    inv = ROPE_BASE ** (-np.arange(n_freq, dtype=np.float64) * 2.0 / ROPE_AXIS_DIM)
    ang = pos[:, :, None] * inv[None, None, :]
    cos = np.cos(ang)
    sin = np.sin(ang)
    zero = np.zeros_like(sin[:, 0])
    cos_h = np.concatenate([cos[:, 0], cos[:, 0], cos[:, 1], cos[:, 1]], axis=1)
    s1_h = np.concatenate([-sin[:, 0], zero, -sin[:, 1], zero], axis=1)
    s2_h = np.concatenate([zero, sin[:, 0], zero, sin[:, 1]], axis=1)
    reps = LANES // HEAD_DIM

    def table(a, ident):
        a = np.tile(a, (1, reps))
        pad = np.full((TM, LANES), ident, np.float64)
        return jnp.asarray(np.concatenate([a, pad], axis=0), F32)

    return table(cos_h, 1.0), table(s1_h, 0.0), table(s2_h, 0.0)


def kernel(x_prompt, x_sample, cache_k, cache_v, c, c_ctx, w_ada, b_ada, norm_mix_pre,
           norm_mix_post, norm_ffn_pre, norm_ffn_post, w_in, q_norm, k_norm, w_attn_out,
           w_pool_group, pool_scale, w_pool_out, w_fourier_out, w_out, w_router_group,
           b_router_group, w_router_expert, b_router_expert, w_expert_gate, w_expert_up,
           w_expert_down):
    cos_t, s1_t, s2_t = _rope_tables()
    avg = jnp.asarray(_block_diag(np.full((HEAD_DIM, HEAD_DIM), 1.0 / HEAD_DIM), LANES // HEAD_DIM), BF16)
    c64, s64 = _dft_mats(FOURIER_GROUP_DIM)
    n_fg = FOURIER_WIDTH // FOURIER_GROUP_DIM
    dft_ch = jnp.asarray(np.concatenate([_block_diag(c64, n_fg), _block_diag(s64, n_fg)], axis=1), BF16)
    cp, sp = _dft_mats(SEQ)
    cp, sp = jnp.asarray(cp, BF16), jnp.asarray(sp, BF16)
    cl, sl = _dft_mats(DEC_SEQ)
    cl, sl = jnp.asarray(cl, BF16), jnp.asarray(sl, BF16)
    tri = jnp.asarray(np.tril(np.ones((TM, TM)), -1), BF16)
    upper = jnp.asarray(np.triu(np.ones((LANES, LANES)), 1), BF16)

    w_in_b = w_in.astype(BF16)
    wa_b = w_attn_out.astype(BF16)
    wp_b = w_pool_out.astype(BF16)
    wf_b = w_fourier_out.astype(BF16)
    wo_b = w_out.astype(BF16)
    pad_r = jnp.zeros((DEPTH, D_MODEL, LANES - N_EXPERTS - N_EXPERT_GROUPS), F32)
    wr_b = jnp.concatenate([w_router_expert, w_router_group, pad_r], axis=2).astype(BF16)
    br = jnp.concatenate([b_router_expert, b_router_group,
                          jnp.zeros((DEPTH, LANES - N_EXPERTS - N_EXPERT_GROUPS), F32)],
                         axis=1).reshape(DEPTH, 1, LANES)
    n_pg = POOL_WIDTH // POOL_GROUP_DIM
    bdw = jnp.zeros((DEPTH, POOL_WIDTH, POOL_WIDTH), F32)
    for g in range(n_pg):
        lo = g * POOL_GROUP_DIM
        bdw = bdw.at[:, lo:lo + POOL_GROUP_DIM, lo:lo + POOL_GROUP_DIM].set(w_pool_group[:, g])
    bdw = bdw.astype(BF16)
    pscale = pool_scale.reshape(DEPTH, 1, POOL_WIDTH)
    qg = jnp.tile(q_norm, (1, LANES // HEAD_DIM)).reshape(DEPTH, 1, LANES)
    kg = jnp.tile(k_norm, (1, LANES // HEAD_DIM)).reshape(DEPTH, 1, LANES)
    gpre = norm_mix_pre.reshape(DEPTH, 1, D_MODEL)
    gpost = norm_mix_post.reshape(DEPTH, 1, D_MODEL)
    gffn = norm_ffn_pre.reshape(DEPTH, 1, D_MODEL)
    gfpost = norm_ffn_post.reshape(DEPTH, 1, D_MODEL)
    ck = cache_k.reshape(DEC_BATCH, DEPTH, PAST_LEN, KV_WIDTH)
    cv = cache_v.reshape(DEC_BATCH, DEPTH, PAST_LEN, KV_WIDTH)

    c_all = jnp.concatenate([c_ctx[None, :], c, jnp.zeros((MOD_ROWS - 1 - DEC_BATCH, D_MODEL), F32)], axis=0)
    mods = _mod_call(c_all, w_ada, b_ada).reshape(DEPTH, MOD_ROWS, 1, N_MOD * D_MODEL)

    x = jnp.concatenate([x_prompt.reshape(N_P, D_MODEL), x_sample.reshape(N_S, D_MODEL)], axis=0)
    xs_buf = jnp.zeros((XS_ROWS, HALF), U32)
    new_k, new_v = [], []
    hb = _prenorm_call(0, x, mods, gpre)
    for l in range(DEPTH):
        q, k, v, xp, xc, xsn, gates = _proj_call(l, hb, w_in_b, qg, kg, cos_t, s1_t, s2_t, avg, dft_ch)
        new_k.append(k[:N_P].reshape(BATCH, SEQ, N_KV_HEADS, HEAD_DIM))
        new_v.append(v[:N_P].reshape(BATCH, SEQ, N_KV_HEADS, HEAD_DIM))
        attn = (_attn_prompt_call(q, k, v), _attn_sample_call(l, q, k, v, ck, cv))
        pool = (_pool_call(l, xp, bdw, pscale, SEQ, BATCH, 0),
                _pool_call(l, xp, bdw, pscale, DEC_SEQ, DEC_BATCH, N_P // DEC_SEQ))
        four = (_fourier_prompt_call(cp, sp, xc, xsn), _fourier_sample_call(cl, sl, xc, xsn))
        x1, h2, pos, post, mw, cnt = _merge_call(l, x, attn, pool, four, gates, mods, gpost, gffn,
                                                 wa_b, wp_b, wf_b, wo_b, wr_b, br, tri, upper)
        runs = cnt.reshape(N_TILES, MOD_ROWS, LANES)[:, 0, :N_EXPERTS].astype(jnp.int32)
        runs = ((runs + RUN_ALIGN - 1) // RUN_ALIGN) * RUN_ALIGN
        tile_off = jnp.cumsum(runs, axis=1) - runs
        rows_e = jnp.sum(runs, axis=0)
        padded = ((rows_e + EBLK - 1) // EBLK) * EBLK
        pad_end = jnp.cumsum(padded)
        xs_off = (pad_end - padded)[None, :] + jnp.cumsum(runs, axis=0) - runs
        block_expert = jnp.minimum(
            jnp.searchsorted(pad_end, jnp.arange(N_EBLK, dtype=jnp.int32) * EBLK, side='right'),
            N_EXPERTS - 1).astype(jnp.int32)
        n_used = (pad_end[-1:] // EBLK).astype(jnp.int32)
        run_cnt = (runs // RUN_ALIGN).reshape(N_RUNS)
        tile_off = tile_off.reshape(N_RUNS)
        xs_off = xs_off.reshape(N_RUNS)
        tile_cnt = jnp.sum(runs, axis=1) // RUN_ALIGN
        xs_buf = _dispatch_call(run_cnt, tile_off, xs_off, tile_cnt, h2, post, xs_buf)
        y = _expert_call(l, block_expert, n_used, xs_buf, w_expert_gate, w_expert_up, w_expert_down)
        outs = _combine_call(l, run_cnt, tile_off, xs_off, tile_cnt, y, x1, pos, mw, mods, gfpost,
                             gpre)
        x = outs[0]
        hb = outs[-1]

    y_prompt = x[:N_P].reshape(BATCH, SEQ, D_MODEL)
    y_sample = x[N_P:].reshape(DEC_BATCH, DEC_SEQ, D_MODEL)
    return (y_prompt, y_sample, jnp.stack(new_k, axis=1), jnp.stack(new_v, axis=1))
```

```python
import functools
import math

import numpy as np
import jax
import jax.numpy as jnp
from jax import lax
from jax.experimental import pallas as pl
from jax.experimental.pallas import tpu as pltpu

F32 = jnp.float32
BF16 = jnp.bfloat16

D_MODEL = 1024
BATCH = 32
SEQ = 256
DEPTH = 4
DEC_BATCH = 2
DEC_SEQ = 2048
PAST_LEN = 512
GRID_W = 64
N_HEADS = 8
N_KV_HEADS = 2
HEAD_DIM = 64
KV_GROUP = N_HEADS // N_KV_HEADS
ATTN_WIDTH = N_HEADS * HEAD_DIM
KV_WIDTH = N_KV_HEADS * HEAD_DIM
ROPE_AXIS_DIM = HEAD_DIM // 2
ROPE_BASE = 10000.0
POOL_WINDOWS = (2, 4, 8, 16)
POOL_WIDTH = 256
POOL_GROUP_DIM = 64
FOURIER_WIDTH = 256
FOURIER_GROUP_DIM = 64
N_BRANCHES = 3
OFF_K = ATTN_WIDTH
OFF_P = ATTN_WIDTH + 2 * KV_WIDTH
OFF_G = OFF_P + POOL_WIDTH + FOURIER_WIDTH
IN_WIDTH = OFF_G + N_BRANCHES * D_MODEL
N_EXPERT_GROUPS = 4
EXPERTS_PER_GROUP = 8
N_EXPERTS = N_EXPERT_GROUPS * EXPERTS_PER_GROUP
TOP_K = 2
EXPERT_FF = 256
N_MOD = 6
RMS_EPS = 1e-6
QK_SCALE = HEAD_DIM ** -0.5 * math.log2(math.e)

N_P = BATCH * SEQ
N_S = DEC_BATCH * DEC_SEQ
N_TOK = N_P + N_S
LANES = 128
MOD_ROWS = 8
POOL_PAD = 16

TM = 512
N_TILES = N_TOK // TM
P_TILES = N_P // TM
S_TILES_PER_SEQ = DEC_SEQ // TM
TQ_S = 128
KEY_CHUNK = 512
TF_S = 512
EBLK = 512
RUN_ALIGN = 8
RUN_SHIFT = 3
RUN_BITS = 7
assert RUN_ALIGN << (RUN_BITS - 1) == TM
SORT_ROWS = TOP_K * TM + N_EXPERTS * RUN_ALIGN
DISPATCH_SLOTS = 3
assert N_TILES >= DISPATCH_SLOTS
TILE_BITS = 8
assert SORT_ROWS < RUN_ALIGN << TILE_BITS
N_RUNS = N_TILES * N_EXPERTS
N_EBLK = (N_TOK * TOP_K + N_RUNS * (RUN_ALIGN - 1) + N_EXPERTS * (EBLK - 1) + EBLK - 1) // EBLK
XS_ROWS = N_EBLK * EBLK
MOD_NT = 1536


def _dot(a, b):
    return jnp.dot(a, b, preferred_element_type=F32)


def _rms(x):
    return x * lax.rsqrt(jnp.mean(x * x, axis=-1, keepdims=True) + RMS_EPS)


def _mod_row(i):
    return jnp.where(i < P_TILES, 0, 1 + (i - P_TILES) // S_TILES_PER_SEQ)


def _rope_block(i):
    return jnp.where(i < P_TILES, S_TILES_PER_SEQ, (i - P_TILES) % S_TILES_PER_SEQ)


def _mod_kernel(c_ref, w_ref, b_ref, o_ref):
    c = c_ref[...]
    s = (c * jax.nn.sigmoid(c)).astype(BF16)
    o_ref[...] = _dot(s, w_ref[...].astype(BF16)) + b_ref[...]


def _mod_call(c_all, w_ada, b_ada):
    nt = (N_MOD * D_MODEL) // MOD_NT
    return pl.pallas_call(
        _mod_kernel,
        grid=(DEPTH, nt),
        in_specs=[
            pl.BlockSpec((MOD_ROWS, D_MODEL), lambda l, j: (0, 0)),
            pl.BlockSpec((None, D_MODEL, MOD_NT), lambda l, j: (l, 0, j)),
            pl.BlockSpec((None, 1, MOD_NT), lambda l, j: (l, 0, j)),
        ],
        out_specs=pl.BlockSpec((None, MOD_ROWS, MOD_NT), lambda l, j: (l, 0, j)),
        out_shape=jax.ShapeDtypeStruct((DEPTH, MOD_ROWS, N_MOD * D_MODEL), F32),
        name="adaln_mod",
    )(c_all, w_ada, b_ada.reshape(DEPTH, 1, N_MOD * D_MODEL))


def _prenorm(x, mod, gain):
    return (_rms(x) * gain) * (1.0 + mod[:, D_MODEL:2 * D_MODEL]) + mod[:, 0:D_MODEL]


def _prenorm_kernel(x_ref, mod_ref, gpre_ref, hb_ref):
    hb_ref[...] = _prenorm(x_ref[...], mod_ref[...], gpre_ref[...]).astype(BF16)


def _prenorm_call(l, x, mods, gpre):
    return pl.pallas_call(
        _prenorm_kernel,
        grid=(N_TILES,),
        in_specs=[
            pl.BlockSpec((TM, D_MODEL), lambda i: (i, 0)),
            pl.BlockSpec((None, None, 1, 2 * D_MODEL), lambda i: (l, _mod_row(i), 0, 0)),
            pl.BlockSpec((None, 1, D_MODEL), lambda i: (l, 0, 0)),
        ],
        out_specs=pl.BlockSpec((TM, D_MODEL), lambda i: (i, 0)),
        out_shape=jax.ShapeDtypeStruct((N_TOK, D_MODEL), BF16),
        name="prenorm",
    )(x, mods, gpre)


def _proj_kernel(hb_ref, w_ref, qg_ref, kg_ref, cos_ref, s1_ref, s2_ref,
                 avg_ref, dft_ref, q_ref, k_ref, v_ref, xp_ref, xc_ref, xs_ref, g_ref):
    hb = hb_ref[...]

    cos = cos_ref[...]
    s1 = s1_ref[...]
    s2 = s2_ref[...]

    def rope(t):
        return (t * cos + pltpu.roll(t, LANES - ROPE_AXIS_DIM // 2, 1) * s1
                + pltpu.roll(t, ROPE_AXIS_DIM // 2, 1) * s2)

    avg = avg_ref[...]
    for c in range(ATTN_WIDTH // LANES):
        lo = c * LANES
        q = _dot(hb, w_ref[:, lo:lo + LANES])
        ms = _dot((q * q).astype(BF16), avg)
        q = q * lax.rsqrt(ms + RMS_EPS) * qg_ref[...]
        q_ref[:, lo:lo + LANES] = (rope(q) * QK_SCALE).astype(BF16)

    kv = _dot(hb, w_ref[:, OFF_K:OFF_K + 2 * KV_WIDTH])
    k = kv[:, 0:KV_WIDTH]
    ms = _dot((k * k).astype(BF16), avg)
    k = k * lax.rsqrt(ms + RMS_EPS) * kg_ref[...]
    k_ref[...] = rope(k)
    v_ref[...] = kv[:, KV_WIDTH:]

    pf = _dot(hb, w_ref[:, OFF_P:OFF_G])
    xp_ref[...] = pf[:, 0:POOL_WIDTH]
    cs = _dot(pf[:, POOL_WIDTH:].astype(BF16), dft_ref[...])
    xc_ref[...] = cs[:, 0:FOURIER_WIDTH].astype(BF16)
    xs_ref[...] = cs[:, FOURIER_WIDTH:].astype(BF16)

    for c in range(N_BRANCHES):
        lo = OFF_G + c * D_MODEL
        g_ref[:, c * D_MODEL:(c + 1) * D_MODEL] = _dot(hb, w_ref[:, lo:lo + D_MODEL]).astype(BF16)


def _proj_call(l, hb, w_in, qg, kg, cos_t, s1_t, s2_t, avg, dft):
    row = lambda i: (i, 0)
    const2 = lambda i: (0, 0)
    per_layer = lambda i: (l, 0, 0)
    rope_spec = pl.BlockSpec((TM, LANES), lambda i: (_rope_block(i), 0))
    return pl.pallas_call(
        _proj_kernel,
        grid=(N_TILES,),
        in_specs=[
            pl.BlockSpec((TM, D_MODEL), row),
            pl.BlockSpec((None, D_MODEL, IN_WIDTH), per_layer),
            pl.BlockSpec((None, 1, LANES), per_layer),
            pl.BlockSpec((None, 1, LANES), per_layer),
            rope_spec, rope_spec, rope_spec,
            pl.BlockSpec((LANES, LANES), const2),
            pl.BlockSpec((FOURIER_WIDTH, 2 * FOURIER_WIDTH), const2),
        ],
        out_specs=[
            pl.BlockSpec((TM, ATTN_WIDTH), row),
            pl.BlockSpec((TM, KV_WIDTH), row),
            pl.BlockSpec((TM, KV_WIDTH), row),
            pl.BlockSpec((TM, POOL_WIDTH), row),
            pl.BlockSpec((TM, FOURIER_WIDTH), row),
            pl.BlockSpec((TM, FOURIER_WIDTH), row),
            pl.BlockSpec((TM, N_BRANCHES * D_MODEL), row),
        ],
        out_shape=[
            jax.ShapeDtypeStruct((N_TOK, ATTN_WIDTH), BF16),
            jax.ShapeDtypeStruct((N_TOK, KV_WIDTH), F32),
            jax.ShapeDtypeStruct((N_TOK, KV_WIDTH), F32),
            jax.ShapeDtypeStruct((N_TOK, POOL_WIDTH), F32),
            jax.ShapeDtypeStruct((N_TOK, FOURIER_WIDTH), BF16),
            jax.ShapeDtypeStruct((N_TOK, FOURIER_WIDTH), BF16),
            jax.ShapeDtypeStruct((N_TOK, N_BRANCHES * D_MODEL), BF16),
        ],
        name="proj",
    )(hb, w_in, qg, kg, cos_t, s1_t, s2_t, avg, dft)


def _attn_kernel(*refs, n_parts, tq):
    q_ref = refs[0]
    kv_refs = refs[1:1 + 2 * n_parts]
    o_ref = refs[-1]
    single_chunk = n_parts == 1 and kv_refs[0].shape[0] <= KEY_CHUNK
    outs = []
    for j in range(N_KV_HEADS):
        lo = j * HEAD_DIM
        qs = jnp.concatenate(
            [q_ref[:, (KV_GROUP * j + g) * HEAD_DIM:(KV_GROUP * j + g + 1) * HEAD_DIM]
             for g in range(KV_GROUP)], axis=0)
        m = acc = None
        for p in range(n_parts):
            k_ref, v_ref = kv_refs[2 * p], kv_refs[2 * p + 1]
            for c0 in range(0, k_ref.shape[0], KEY_CHUNK):
                c1 = min(c0 + KEY_CHUNK, k_ref.shape[0])
                kc = k_ref[c0:c1, lo:lo + HEAD_DIM].astype(BF16)
                s = lax.dot_general(qs, kc, (((1,), (1,)), ((), ())), preferred_element_type=F32)
                mc = jnp.max(s, axis=1, keepdims=True)
                m_new = mc if m is None else jnp.maximum(m, mc)
                e = jnp.exp2(s - m_new)
                vc = v_ref[c0:c1, lo:lo + HEAD_DIM].astype(BF16)
                if single_chunk:
                    den = jnp.sum(e, axis=1, keepdims=True)
                else:
                    vc = jnp.concatenate([vc, jnp.ones((c1 - c0, HEAD_DIM), BF16)], axis=1)
                pv = _dot(e.astype(BF16), vc)
                acc = pv if m is None else acc * jnp.exp2(m - m_new) + pv
                m = m_new
        o = acc / den if single_chunk else acc[:, 0:HEAD_DIM] / acc[:, HEAD_DIM:2 * HEAD_DIM]
        outs.extend(o[g * tq:(g + 1) * tq] for g in range(KV_GROUP))
    o_ref[...] = jnp.concatenate(outs, axis=1).astype(BF16)


def _attn_prompt_call(q, k, v):
    blk = lambda b: (b, 0)
    return pl.pallas_call(
        functools.partial(_attn_kernel, n_parts=1, tq=SEQ),
        grid=(BATCH,),
        in_specs=[
            pl.BlockSpec((SEQ, ATTN_WIDTH), blk),
            pl.BlockSpec((SEQ, KV_WIDTH), blk),
            pl.BlockSpec((SEQ, KV_WIDTH), blk),
        ],
        out_specs=pl.BlockSpec((SEQ, ATTN_WIDTH), blk),
        out_shape=jax.ShapeDtypeStruct((N_P, ATTN_WIDTH), BF16),
        name="attn_context",
    )(q, k, v)


def _attn_sample_call(l, q, k, v, cache_k, cache_v):
    nq = DEC_SEQ // TQ_S
    qrow = lambda b, i: (N_P // TQ_S + b * nq + i, 0)
    seq = lambda b, i: (N_P // DEC_SEQ + b, 0)
    cache = lambda b, i: (b, l, 0, 0)
    return pl.pallas_call(
        functools.partial(_attn_kernel, n_parts=2, tq=TQ_S),
        grid=(DEC_BATCH, nq),
        in_specs=[
            pl.BlockSpec((TQ_S, ATTN_WIDTH), qrow),
            pl.BlockSpec((None, None, PAST_LEN, KV_WIDTH), cache),
            pl.BlockSpec((None, None, PAST_LEN, KV_WIDTH), cache),
            pl.BlockSpec((DEC_SEQ, KV_WIDTH), seq),
            pl.BlockSpec((DEC_SEQ, KV_WIDTH), seq),
        ],
        out_specs=pl.BlockSpec((TQ_S, ATTN_WIDTH), lambda b, i: (b * nq + i, 0)),
        out_shape=jax.ShapeDtypeStruct((N_S, ATTN_WIDTH), BF16),
        name="attn_latent",
    )(q, cache_k, cache_v, k, v)


def _pool_kernel(xp_ref, bdw_ref, sc_ref, o_ref, pad_ref, *, seq_len):
    half = POOL_WIDTH // 2
    zeros = jnp.zeros((POOL_PAD, POOL_WIDTH), F32)
    pad_ref[0:POOL_PAD, :] = zeros
    pad_ref[POOL_PAD + seq_len:, :] = zeros
    pad_ref[POOL_PAD:POOL_PAD + seq_len, :] = xp_ref[...]
    chunk = min(seq_len, 256)
    lane = lax.broadcasted_iota(jnp.int32, (chunk, half), 1)
    first = lane < POOL_GROUP_DIM
    for c in range(seq_len // chunk):
        base = c * chunk
        t = lax.broadcasted_iota(jnp.int32, (chunk, half), 0) + base

        def sh(j, lo):
            return pad_ref[POOL_PAD + base + j:POOL_PAD + base + j + chunk, lo:lo + half]

        def cnt(w):
            return (jnp.minimum(t + w // 2, seq_len) - jnp.maximum(t - w // 2, 0)).astype(F32)

        xa = sh(0, 0)
        w2 = sh(-1, 0) + xa
        w4 = w2 + sh(-2, 0) + sh(1, 0)
        xb = sh(0, half)
        w8 = xb
        for j in (-4, -3, -2, -1, 1, 2, 3):
            w8 = w8 + sh(j, half)
        w16 = w8
        for j in (-8, -7, -6, -5, 4, 5, 6, 7):
            w16 = w16 + sh(j, half)
        pa = jnp.where(first, w2 / cnt(2), w4 / cnt(4)) - xa
        pb = jnp.where(first, w8 / cnt(8), w16 / cnt(16)) - xb
        pooled = jnp.concatenate([pa, pb], axis=1).astype(BF16)
        o_ref[base:base + chunk, :] = (_dot(pooled, bdw_ref[...]) * sc_ref[...]).astype(BF16)


def _pool_call(l, xp, bdw, scale, seq_len, n_seq, blk0):
    per_layer = lambda b: (l, 0, 0)
    return pl.pallas_call(
        functools.partial(_pool_kernel, seq_len=seq_len),
        grid=(n_seq,),
        in_specs=[
            pl.BlockSpec((seq_len, POOL_WIDTH), lambda b: (blk0 + b, 0)),
            pl.BlockSpec((None, POOL_WIDTH, POOL_WIDTH), per_layer),
            pl.BlockSpec((None, 1, POOL_WIDTH), per_layer),
        ],
        out_specs=pl.BlockSpec((seq_len, POOL_WIDTH), lambda b: (b, 0)),
        out_shape=jax.ShapeDtypeStruct((n_seq * seq_len, POOL_WIDTH), BF16),
        scratch_shapes=[pltpu.VMEM((seq_len + 2 * POOL_PAD, POOL_WIDTH), F32)],
        name="pool_%d" % seq_len,
    )(xp, bdw, scale)


def _fourier_kernel(c_ref, s_ref, xc_ref, xs_ref, o_ref, *, scale):
    y = _dot(c_ref[...], xc_ref[...]) - _dot(s_ref[...], xs_ref[...])
    o_ref[...] = (y * scale).astype(BF16)


def _fourier_prompt_call(cmat, smat, xc, xs):
    blk = lambda b: (b, 0)
    const2 = lambda b: (0, 0)
    return pl.pallas_call(
        functools.partial(_fourier_kernel, scale=1.0 / math.sqrt(SEQ * FOURIER_GROUP_DIM)),
        grid=(BATCH,),
        in_specs=[
            pl.BlockSpec((SEQ, SEQ), const2),
            pl.BlockSpec((SEQ, SEQ), const2),
            pl.BlockSpec((SEQ, FOURIER_WIDTH), blk),
            pl.BlockSpec((SEQ, FOURIER_WIDTH), blk),
        ],
        out_specs=pl.BlockSpec((SEQ, FOURIER_WIDTH), blk),
        out_shape=jax.ShapeDtypeStruct((N_P, FOURIER_WIDTH), BF16),
        name="fourier_context",
    )(cmat, smat, xc, xs)


def _fourier_sample_call(cmat, smat, xc, xs):
    nt = DEC_SEQ // TF_S
    rows = lambda b, i: (i, 0)
    seq = lambda b, i: (N_P // DEC_SEQ + b, 0)
    out = lambda b, i: (b * nt + i, 0)
    return pl.pallas_call(
        functools.partial(_fourier_kernel, scale=1.0 / math.sqrt(DEC_SEQ * FOURIER_GROUP_DIM)),
        grid=(DEC_BATCH, nt),
        in_specs=[
            pl.BlockSpec((TF_S, DEC_SEQ), rows),
            pl.BlockSpec((TF_S, DEC_SEQ), rows),
            pl.BlockSpec((DEC_SEQ, FOURIER_WIDTH), seq),
            pl.BlockSpec((DEC_SEQ, FOURIER_WIDTH), seq),
        ],
        out_specs=pl.BlockSpec((TF_S, FOURIER_WIDTH), out),
        out_shape=jax.ShapeDtypeStruct((N_S, FOURIER_WIDTH), BF16),
        name="fourier_latent",
    )(cmat, smat, xc, xs)


def _merge_kernel(x_ref, attn_p_ref, attn_s_ref, pool_p_ref, pool_s_ref, four_p_ref, four_s_ref,
                  g_ref, mod_ref, gpost_ref, gffn_ref,
                  wa_ref, wp_ref, wf_ref, wo_ref, wr_ref, br_ref, tri_ref, upper_ref,
                  x1_ref, h2_ref, pos_ref, post_ref, mw_ref, cnt_ref):
    i = pl.program_id(0)
    mod = mod_ref[...]
    g1 = mod[:, 2 * D_MODEL:3 * D_MODEL]
    sh2 = mod[:, 3 * D_MODEL:4 * D_MODEL]
    sc2 = mod[:, 4 * D_MODEL:5 * D_MODEL]

    def gate(c):
        return 0.5 * jnp.tanh(0.5 * g_ref[:, c * D_MODEL:(c + 1) * D_MODEL].astype(F32)) + 0.5

    def branch(p_ref, s_ref):
        return jnp.where(i < P_TILES, p_ref[...], s_ref[...])

    merged = gate(0) * _dot(branch(attn_p_ref, attn_s_ref), wa_ref[...])
    merged = merged + gate(1) * _dot(branch(pool_p_ref, pool_s_ref), wp_ref[...])
    merged = merged + gate(2) * _dot(branch(four_p_ref, four_s_ref), wf_ref[...])
    mix = _dot(merged.astype(BF16), wo_ref[...])
    x1 = x_ref[...] + g1 * (_rms(mix) * gpost_ref[...])
    x1_ref[...] = x1
    h2 = (_rms(x1) * gffn_ref[...]) * (1.0 + sc2) + sh2
    h2b = h2.astype(BF16)
    h2_ref[...] = h2b

    logits = _dot(h2b, wr_ref[...]) + br_ref[...]
    lane = lax.broadcasted_iota(jnp.int32, (TM, LANES), 1)
    lanef = lane.astype(F32)
    neg = jnp.float32(-3e38)
    big = jnp.float32(1e9)
    is_g = (lane >= N_EXPERTS) & (lane < N_EXPERTS + N_EXPERT_GROUPS)
    lg = jnp.where(is_g, logits, neg)
    gmax = jnp.max(lg, axis=1, keepdims=True)
    g_sel = jnp.min(jnp.where(lg == gmax, lanef - N_EXPERTS, big), axis=1, keepdims=True)
    p_g = 1.0 / jnp.sum(jnp.where(is_g, jnp.exp(logits - gmax), 0.0), axis=1, keepdims=True)
    grp = lax.shift_right_logical(lane, int(math.log2(EXPERTS_PER_GROUP))).astype(F32)
    in_grp = (lane < N_EXPERTS) & (grp == g_sel)
    le = jnp.where(in_grp, logits, neg)
    v1 = jnp.max(le, axis=1, keepdims=True)
    i1 = jnp.min(jnp.where(le == v1, lanef, big), axis=1, keepdims=True)
    le2 = jnp.where(lanef == i1, neg, le)
    v2 = jnp.max(le2, axis=1, keepdims=True)
    i2 = jnp.min(jnp.where(le2 == v2, lanef, big), axis=1, keepdims=True)
    e21 = jnp.exp(v2 - v1)
    w1 = p_g / (1.0 + e21)
    w2 = p_g * e21 / (1.0 + e21)

    oh1 = (lanef == i1).astype(F32)
    oh2 = (lanef == i2).astype(F32)
    ohb = (oh1 + oh2).astype(BF16)
    before = _dot(tri_ref[...], ohb)
    cnt = _dot(jnp.ones((MOD_ROWS, TM), BF16), ohb)
    cnt_pad = (lax.shift_right_logical(cnt.astype(jnp.int32) + (RUN_ALIGN - 1), RUN_SHIFT)
               * RUN_ALIGN).astype(F32)
    run_off = _dot(cnt_pad.astype(BF16), upper_ref[...])
    slot = run_off[0:1, :] + before
    p1 = jnp.sum(slot * oh1, axis=1, keepdims=True)
    p2 = jnp.sum(slot * oh2, axis=1, keepdims=True)
    cnt_ref[...] = cnt

    pos = jnp.where(lane == 0, p1, jnp.where(lane == 1, p2, 0.0))
    pos_ref[...] = pos.astype(jnp.int32)
    post_ref[...] = pos.T[0:MOD_ROWS, :].astype(jnp.int32)
    mw_ref[...] = jnp.where(lane == 0, w1, w2)


def _merge_call(l, x, attn, pool, four, gates, mods, gpost, gffn, wa, wp, wf, wo, wr, br, tri,
                upper):
    row = lambda i: (i, 0)
    prow = lambda i: (jnp.minimum(i, P_TILES - 1), 0)
    srow = lambda i: (jnp.maximum(i - P_TILES, 0), 0)
    const2 = lambda i: (0, 0)
    per_layer = lambda i: (l, 0, 0)
    return pl.pallas_call(
        _merge_kernel,
        grid=(N_TILES,),
        in_specs=[
            pl.BlockSpec((TM, D_MODEL), row),
            pl.BlockSpec((TM, ATTN_WIDTH), prow),
            pl.BlockSpec((TM, ATTN_WIDTH), srow),
            pl.BlockSpec((TM, POOL_WIDTH), prow),
            pl.BlockSpec((TM, POOL_WIDTH), srow),
            pl.BlockSpec((TM, FOURIER_WIDTH), prow),
            pl.BlockSpec((TM, FOURIER_WIDTH), srow),
            pl.BlockSpec((TM, N_BRANCHES * D_MODEL), row),
            pl.BlockSpec((None, None, 1, N_MOD * D_MODEL), lambda i: (l, _mod_row(i), 0, 0)),
            pl.BlockSpec((None, 1, D_MODEL), per_layer),
            pl.BlockSpec((None, 1, D_MODEL), per_layer),
            pl.BlockSpec((None, ATTN_WIDTH, D_MODEL), per_layer),
            pl.BlockSpec((None, POOL_WIDTH, D_MODEL), per_layer),
            pl.BlockSpec((None, FOURIER_WIDTH, D_MODEL), per_layer),
            pl.BlockSpec((None, D_MODEL, D_MODEL), per_layer),
            pl.BlockSpec((None, D_MODEL, LANES), per_layer),
            pl.BlockSpec((None, 1, LANES), per_layer),
            pl.BlockSpec((TM, TM), const2),
            pl.BlockSpec((LANES, LANES), const2),
        ],
        out_specs=[
            pl.BlockSpec((TM, D_MODEL), row),
            pl.BlockSpec((TM, D_MODEL), row),
            pl.BlockSpec((TM, LANES), row),
            pl.BlockSpec((MOD_ROWS, TM), lambda i: (0, i)),
            pl.BlockSpec((TM, LANES), row),
            pl.BlockSpec((MOD_ROWS, LANES), row),
        ],
        out_shape=[
            jax.ShapeDtypeStruct((N_TOK, D_MODEL), F32),
            jax.ShapeDtypeStruct((N_TOK, D_MODEL), BF16),
            jax.ShapeDtypeStruct((N_TOK, LANES), jnp.int32),
            jax.ShapeDtypeStruct((MOD_ROWS, N_TOK), jnp.int32),
            jax.ShapeDtypeStruct((N_TOK, LANES), F32),
            jax.ShapeDtypeStruct((N_TILES * MOD_ROWS, LANES), F32),
        ],
        name="merge_router",
    )(x, attn[0], attn[1], pool[0], pool[1], four[0], four[1], gates, mods, gpost, gffn,
      wa, wp, wf, wo, wr, br, tri, upper)


HALF = D_MODEL // 2
U32 = jnp.uint32
HI_MASK = 0xFFFF0000


def _pack_rows(x):
    lo = lax.bitcast_convert_type(x[:, :HALF], U32)
    hi = lax.bitcast_convert_type(x[:, HALF:], U32)
    return lax.shift_right_logical(lo, U32(16)) | (hi & U32(HI_MASK))


def _unpack_rows(u):
    lo = lax.bitcast_convert_type(lax.shift_left(u, U32(16)), F32)
    hi = lax.bitcast_convert_type(u & U32(HI_MASK), F32)
    return jnp.concatenate([lo.astype(BF16), hi.astype(BF16)], axis=1)


def _run_copies(cnt, src_ref, src_off, dst_ref, dst_off, sem, wait, bits=RUN_BITS):
    off = jnp.int32(0)
    for b in reversed(range(bits)):
        size = RUN_ALIGN << b
        take = lax.shift_right_logical(cnt, b) & 1

        @pl.when(take == 1)
        def _(off=off, size=size):
            cp = pltpu.make_async_copy(
                src_ref.at[pl.ds(pl.multiple_of(src_off + off, RUN_ALIGN), size)],
                dst_ref.at[pl.ds(pl.multiple_of(dst_off + off, RUN_ALIGN), size)], sem)
            if wait:
                cp.wait()
            else:
                cp.start()

        off = off + take * size


def _tile_runs(tile, cnt_ref, fn):
    def body(e, carry):
        r = tile * N_EXPERTS + e
        fn(r, cnt_ref[r])
        return carry

    lax.fori_loop(0, N_EXPERTS, body, 0)


def _dispatch_kernel(cnt_ref, toff_ref, xoff_ref, tsum_ref, h_ref, post_ref, xs_in_ref, xs_ref,
                     sorted_ref, sem):
    del xs_in_ref
    i = pl.program_id(0)
    slot = lax.rem(i, DISPATCH_SLOTS)
    rows = lax.broadcasted_iota(jnp.int32, (SORT_ROWS, TM), 0)
    p = post_ref[...]
    perm = jnp.where(rows == p[0:1, :], 1.0, jnp.where(rows == p[1:2, :], 1.0, 0.0)).astype(BF16)
    sorted_ref[slot] = _pack_rows(_dot(perm, h_ref[...]))

    def wait(tile, slot):
        _run_copies(tsum_ref[tile], sorted_ref.at[slot], 0, xs_ref, 0, sem.at[slot], True, TILE_BITS)

    lag = DISPATCH_SLOTS - 1

    @pl.when(i >= lag)
    def _():
        wait(i - lag, lax.rem(i + 1, DISPATCH_SLOTS))

    _tile_runs(i, cnt_ref, lambda r, c: _run_copies(
        c, sorted_ref.at[slot], toff_ref[r], xs_ref, xoff_ref[r], sem.at[slot], False))

    @pl.when(i == pl.num_programs(0) - 1)
    def _():
        for back in reversed(range(lag)):
            wait(i - back, lax.rem(i - back, DISPATCH_SLOTS))


def _dispatch_call(cnt, toff, xoff, tsum, h2, post, xs):
    return pl.pallas_call(
        _dispatch_kernel,
        grid_spec=pltpu.PrefetchScalarGridSpec(
            num_scalar_prefetch=4,
            grid=(N_TILES,),
            in_specs=[
                pl.BlockSpec((TM, D_MODEL), lambda i, *_: (i, 0)),
                pl.BlockSpec((MOD_ROWS, TM), lambda i, *_: (0, i)),
                pl.BlockSpec(memory_space=pl.ANY),
            ],
            out_specs=pl.BlockSpec(memory_space=pl.ANY),
            scratch_shapes=[pltpu.VMEM((DISPATCH_SLOTS, SORT_ROWS, HALF), U32),
                            pltpu.SemaphoreType.DMA((DISPATCH_SLOTS,))],
        ),
        out_shape=jax.ShapeDtypeStruct((XS_ROWS, HALF), U32),
        input_output_aliases={6: 0},
        compiler_params=pltpu.CompilerParams(dimension_semantics=("arbitrary",)),
        name="moe_dispatch",
    )(cnt, toff, xoff, tsum, h2, post, xs)


def _expert_kernel(bstart_ref, bcnt_ref, nu_ref, wg_ref, wu_ref, wd_ref, xs_ref, y_ref,
                   wgb, wub, wdb, xbuf, ybuf, xsem, ysem):
    e = pl.program_id(0)
    n_used = nu_ref[0]

    def x_copy(g, slot):
        rows = pl.ds(pl.multiple_of(g * EBLK, EBLK), EBLK)
        return pltpu.make_async_copy(xs_ref.at[rows], xbuf.at[slot], xsem.at[slot])

    def y_copy(g, slot):
        rows = pl.ds(pl.multiple_of(g * EBLK, EBLK), EBLK)
        return pltpu.make_async_copy(ybuf.at[slot], y_ref.at[rows], ysem.at[slot])

    @pl.when(e == 0)
    def _():
        x_copy(0, 0).start()

    wgb[...] = wg_ref[...].astype(BF16)
    wub[...] = wu_ref[...].astype(BF16)
    wdb[...] = wd_ref[...].astype(BF16)
    first = bstart_ref[e]

    def block(j, carry):
        g = first + j
        slot = lax.rem(g, 2)
        x_copy(g, slot).wait()

        @pl.when(g + 1 < n_used)
        def _():
            x_copy(g + 1, 1 - slot).start()

        xb = _unpack_rows(xbuf[slot])
        gate = _dot(xb, wgb[...])
        up = _dot(xb, wub[...])
        act = (gate * jax.nn.sigmoid(gate)) * up
        y = _dot(act.astype(BF16), wdb[...])

        @pl.when(g >= 2)
        def _():
            y_copy(g - 2, slot).wait()

        ybuf[slot] = _pack_rows(y.astype(BF16).astype(F32))
        y_copy(g, slot).start()
        return carry

    lax.fori_loop(0, bcnt_ref[e], block, 0)

    @pl.when(e == pl.num_programs(0) - 1)
    def _():
        @pl.when(n_used >= 2)
        def _():
            y_copy(n_used - 2, lax.rem(n_used, 2)).wait()

        y_copy(n_used - 1, lax.rem(n_used - 1, 2)).wait()
        ybuf[0] = jnp.zeros((EBLK, HALF), U32)

        def tail(g, carry):
            y_copy(g, 0).start()
            y_copy(g, 0).wait()
            return carry

        lax.fori_loop(n_used, N_EBLK, tail, 0)


def _expert_call(l, blk_start, blk_cnt, n_used, xs, w_gate, w_up, w_down):
    wsel = lambda e, *_: (l, e, 0, 0)
    return pl.pallas_call(
        _expert_kernel,
        grid_spec=pltpu.PrefetchScalarGridSpec(
            num_scalar_prefetch=3,
            grid=(N_EXPERTS,),
            in_specs=[
                pl.BlockSpec((None, None, D_MODEL, EXPERT_FF), wsel),
                pl.BlockSpec((None, None, D_MODEL, EXPERT_FF), wsel),
                pl.BlockSpec((None, None, EXPERT_FF, D_MODEL), wsel),
                pl.BlockSpec(memory_space=pl.ANY),
            ],
            out_specs=pl.BlockSpec(memory_space=pl.ANY),
            scratch_shapes=[
                pltpu.VMEM((D_MODEL, EXPERT_FF), BF16),
                pltpu.VMEM((D_MODEL, EXPERT_FF), BF16),
                pltpu.VMEM((EXPERT_FF, D_MODEL), BF16),
                pltpu.VMEM((2, EBLK, HALF), U32),
                pltpu.VMEM((2, EBLK, HALF), U32),
                pltpu.SemaphoreType.DMA((2,)),
                pltpu.SemaphoreType.DMA((2,)),
            ],
        ),
        out_shape=jax.ShapeDtypeStruct((XS_ROWS, HALF), U32),
        compiler_params=pltpu.CompilerParams(dimension_semantics=("arbitrary",)),
        name="moe_experts",
    )(blk_start, blk_cnt, n_used, w_gate, w_up, w_down, xs)


def _combine_kernel(cnt_ref, toff_ref, xoff_ref, tsum_ref, y_ref, x1_ref, pos_ref, mw_ref, g2_ref,
                    gpost_ref, *rest, has_next):
    if has_next:
        nmod_ref, ngpre_ref, o_ref, hb_ref, ybuf, sem = rest
    else:
        o_ref, ybuf, sem = rest
    i = pl.program_id(0)
    slot = lax.rem(i, 2)

    def start(tile, slot):
        _tile_runs(tile, cnt_ref, lambda r, c: _run_copies(
            c, y_ref, xoff_ref[r], ybuf.at[slot], toff_ref[r], sem.at[slot], False))

    @pl.when(i == 0)
    def _():
        ybuf[...] = jnp.zeros_like(ybuf)
        start(0, 0)

    @pl.when(i + 1 < pl.num_programs(0))
    def _():
        start(i + 1, 1 - slot)

    _run_copies(tsum_ref[i], y_ref, 0, ybuf.at[slot], 0, sem.at[slot], True, TILE_BITS)

    pos = pos_ref[...]
    mw = mw_ref[...]
    cols = lax.broadcasted_iota(jnp.int32, (TM, SORT_ROWS), 1)
    qw = (jnp.where(cols == pos[:, 0:1], mw[:, 0:1], 0.0)
          + jnp.where(cols == pos[:, 1:2], mw[:, 1:2], 0.0)).astype(BF16)
    ffn = _dot(qw, _unpack_rows(ybuf[slot]))
    x2 = x1_ref[...] + g2_ref[...] * (_rms(ffn) * gpost_ref[...])
    o_ref[...] = x2
    if has_next:
        hb_ref[...] = _prenorm(x2, nmod_ref[...], ngpre_ref[...]).astype(BF16)


def _combine_call(l, cnt, toff, xoff, tsum, y, x1, pos, mw, mods, gpost, gpre):
    has_next = l + 1 < DEPTH
    row = lambda i, *_: (i, 0)
    in_specs = [
        pl.BlockSpec(memory_space=pl.ANY),
        pl.BlockSpec((TM, D_MODEL), row),
        pl.BlockSpec((TM, LANES), row),
        pl.BlockSpec((TM, LANES), row),
        pl.BlockSpec((None, None, 1, D_MODEL), lambda i, *_: (l, _mod_row(i), 0, N_MOD - 1)),
        pl.BlockSpec((None, 1, D_MODEL), lambda i, *_: (l, 0, 0)),
    ]
    args = [cnt, toff, xoff, tsum, y, x1, pos, mw, mods, gpost]
    out_specs = [pl.BlockSpec((TM, D_MODEL), row)]
    out_shape = [jax.ShapeDtypeStruct((N_TOK, D_MODEL), F32)]
    if has_next:
        in_specs += [
            pl.BlockSpec((None, None, 1, 2 * D_MODEL), lambda i, *_: (l + 1, _mod_row(i), 0, 0)),
            pl.BlockSpec((None, 1, D_MODEL), lambda i, *_: (l + 1, 0, 0)),
        ]
        args += [mods, gpre]
        out_specs.append(pl.BlockSpec((TM, D_MODEL), row))
        out_shape.append(jax.ShapeDtypeStruct((N_TOK, D_MODEL), BF16))
    return pl.pallas_call(
        functools.partial(_combine_kernel, has_next=has_next),
        grid_spec=pltpu.PrefetchScalarGridSpec(
            num_scalar_prefetch=4,
            grid=(N_TILES,),
            in_specs=in_specs,
            out_specs=out_specs,
            scratch_shapes=[pltpu.VMEM((2, SORT_ROWS, HALF), U32), pltpu.SemaphoreType.DMA((2,))],
        ),
        out_shape=out_shape,
        compiler_params=pltpu.CompilerParams(dimension_semantics=("arbitrary",)),
        name="moe_combine",
    )(*args)


def _dft_mats(n):
    k = np.arange(n, dtype=np.int64)
    ang = 2.0 * np.pi * ((k[:, None] * k[None, :]) % n).astype(np.float64) / n
    return np.cos(ang), np.sin(ang)


def _block_diag(m, reps):
    n = m.shape[0]
    out = np.zeros((n * reps, n * reps), m.dtype)
    for r in range(reps):
        out[r * n:(r + 1) * n, r * n:(r + 1) * n] = m
    return out


def _rope_tables():
    t = np.arange(DEC_SEQ)
    pos = np.stack([t // GRID_W, t % GRID_W], axis=1).astype(np.float64)
    n_freq = ROPE_AXIS_DIM // 2
    inv = ROPE_BASE ** (-np.arange(n_freq, dtype=np.float64) * 2.0 / ROPE_AXIS_DIM)
    ang = pos[:, :, None] * inv[None, None, :]
    cos = np.cos(ang)
    sin = np.sin(ang)
    zero = np.zeros_like(sin[:, 0])
    cos_h = np.concatenate([cos[:, 0], cos[:, 0], cos[:, 1], cos[:, 1]], axis=1)
    s1_h = np.concatenate([-sin[:, 0], zero, -sin[:, 1], zero], axis=1)
    s2_h = np.concatenate([zero, sin[:, 0], zero, sin[:, 1]], axis=1)
    reps = LANES // HEAD_DIM

    def table(a, ident):
        a = np.tile(a, (1, reps))
        pad = np.full((TM, LANES), ident, np.float64)
        return jnp.asarray(np.concatenate([a, pad], axis=0), F32)

    return table(cos_h, 1.0), table(s1_h, 0.0), table(s2_h, 0.0)


def kernel(x_prompt, x_sample, cache_k, cache_v, c, c_ctx, w_ada, b_ada, norm_mix_pre,
           norm_mix_post, norm_ffn_pre, norm_ffn_post, w_in, q_norm, k_norm, w_attn_out,
           w_pool_group, pool_scale, w_pool_out, w_fourier_out, w_out, w_router_group,
           b_router_group, w_router_expert, b_router_expert, w_expert_gate, w_expert_up,
           w_expert_down):
    cos_t, s1_t, s2_t = _rope_tables()
    avg = jnp.asarray(_block_diag(np.full((HEAD_DIM, HEAD_DIM), 1.0 / HEAD_DIM), LANES // HEAD_DIM), BF16)
    c64, s64 = _dft_mats(FOURIER_GROUP_DIM)
    n_fg = FOURIER_WIDTH // FOURIER_GROUP_DIM
    dft_ch = jnp.asarray(np.concatenate([_block_diag(c64, n_fg), _block_diag(s64, n_fg)], axis=1), BF16)
    cp, sp = _dft_mats(SEQ)
    cp, sp = jnp.asarray(cp, BF16), jnp.asarray(sp, BF16)
    cl, sl = _dft_mats(DEC_SEQ)
    cl, sl = jnp.asarray(cl, BF16), jnp.asarray(sl, BF16)
    tri = jnp.asarray(np.tril(np.ones((TM, TM)), -1), BF16)
    upper = jnp.asarray(np.triu(np.ones((LANES, LANES)), 1), BF16)

    w_in_b = w_in.astype(BF16)
    wa_b = w_attn_out.astype(BF16)
    wp_b = w_pool_out.astype(BF16)
    wf_b = w_fourier_out.astype(BF16)
    wo_b = w_out.astype(BF16)
    pad_r = jnp.zeros((DEPTH, D_MODEL, LANES - N_EXPERTS - N_EXPERT_GROUPS), F32)
    wr_b = jnp.concatenate([w_router_expert, w_router_group, pad_r], axis=2).astype(BF16)
    br = jnp.concatenate([b_router_expert, b_router_group,
                          jnp.zeros((DEPTH, LANES - N_EXPERTS - N_EXPERT_GROUPS), F32)],
                         axis=1).reshape(DEPTH, 1, LANES)
    n_pg = POOL_WIDTH // POOL_GROUP_DIM
    bdw = jnp.zeros((DEPTH, POOL_WIDTH, POOL_WIDTH), F32)
    for g in range(n_pg):
        lo = g * POOL_GROUP_DIM
        bdw = bdw.at[:, lo:lo + POOL_GROUP_DIM, lo:lo + POOL_GROUP_DIM].set(w_pool_group[:, g])
    bdw = bdw.astype(BF16)
    pscale = pool_scale.reshape(DEPTH, 1, POOL_WIDTH)
    qg = jnp.tile(q_norm, (1, LANES // HEAD_DIM)).reshape(DEPTH, 1, LANES)
    kg = jnp.tile(k_norm, (1, LANES // HEAD_DIM)).reshape(DEPTH, 1, LANES)
    gpre = norm_mix_pre.reshape(DEPTH, 1, D_MODEL)
    gpost = norm_mix_post.reshape(DEPTH, 1, D_MODEL)
    gffn = norm_ffn_pre.reshape(DEPTH, 1, D_MODEL)
    gfpost = norm_ffn_post.reshape(DEPTH, 1, D_MODEL)
    ck = cache_k.reshape(DEC_BATCH, DEPTH, PAST_LEN, KV_WIDTH)
    cv = cache_v.reshape(DEC_BATCH, DEPTH, PAST_LEN, KV_WIDTH)

    c_all = jnp.concatenate([c_ctx[None, :], c, jnp.zeros((MOD_ROWS - 1 - DEC_BATCH, D_MODEL), F32)], axis=0)
    mods = _mod_call(c_all, w_ada, b_ada).reshape(DEPTH, MOD_ROWS, 1, N_MOD * D_MODEL)

    x = jnp.concatenate([x_prompt.reshape(N_P, D_MODEL), x_sample.reshape(N_S, D_MODEL)], axis=0)
    xs_buf = jnp.zeros((XS_ROWS, HALF), U32)
    new_k, new_v = [], []
    hb = _prenorm_call(0, x, mods, gpre)
    for l in range(DEPTH):
        q, k, v, xp, xc, xsn, gates = _proj_call(l, hb, w_in_b, qg, kg, cos_t, s1_t, s2_t, avg, dft_ch)
        new_k.append(k[:N_P].reshape(BATCH, SEQ, N_KV_HEADS, HEAD_DIM))
        new_v.append(v[:N_P].reshape(BATCH, SEQ, N_KV_HEADS, HEAD_DIM))
        attn = (_attn_prompt_call(q, k, v), _attn_sample_call(l, q, k, v, ck, cv))
        pool = (_pool_call(l, xp, bdw, pscale, SEQ, BATCH, 0),
                _pool_call(l, xp, bdw, pscale, DEC_SEQ, DEC_BATCH, N_P // DEC_SEQ))
        four = (_fourier_prompt_call(cp, sp, xc, xsn), _fourier_sample_call(cl, sl, xc, xsn))
        x1, h2, pos, post, mw, cnt = _merge_call(l, x, attn, pool, four, gates, mods, gpost, gffn,
                                                 wa_b, wp_b, wf_b, wo_b, wr_b, br, tri, upper)
        runs = cnt.reshape(N_TILES, MOD_ROWS, LANES)[:, 0, :N_EXPERTS].astype(jnp.int32)
        runs = ((runs + RUN_ALIGN - 1) // RUN_ALIGN) * RUN_ALIGN
        tile_off = jnp.cumsum(runs, axis=1) - runs
        rows_e = jnp.sum(runs, axis=0)
        padded = ((rows_e + EBLK - 1) // EBLK) * EBLK
        pad_end = jnp.cumsum(padded)
        xs_off = (pad_end - padded)[None, :] + jnp.cumsum(runs, axis=0) - runs
        blk_cnt = padded // EBLK
        blk_start = (pad_end - padded) // EBLK
        n_used = pad_end[-1:] // EBLK
        run_cnt = (runs // RUN_ALIGN).reshape(N_RUNS)
        tile_off = tile_off.reshape(N_RUNS)
        xs_off = xs_off.reshape(N_RUNS)
        tile_cnt = jnp.sum(runs, axis=1) // RUN_ALIGN
        xs_buf = _dispatch_call(run_cnt, tile_off, xs_off, tile_cnt, h2, post, xs_buf)
        y = _expert_call(l, blk_start, blk_cnt, n_used, xs_buf,
                         w_expert_gate, w_expert_up, w_expert_down)
        outs = _combine_call(l, run_cnt, tile_off, xs_off, tile_cnt, y, x1, pos, mw, mods, gfpost,
                             gpre)
        x = outs[0]
        hb = outs[-1]

    y_prompt = x[:N_P].reshape(BATCH, SEQ, D_MODEL)
    y_sample = x[N_P:].reshape(DEC_BATCH, DEC_SEQ, D_MODEL)
    return (y_prompt, y_sample, jnp.stack(new_k, axis=1), jnp.stack(new_v, axis=1))
```

```python
import functools
import math

import numpy as np
import jax
import jax.numpy as jnp
from jax import lax
from jax.experimental import pallas as pl
from jax.experimental.pallas import tpu as pltpu

F32 = jnp.float32
BF16 = jnp.bfloat16

D_MODEL = 1024
BATCH = 32
SEQ = 256
DEPTH = 4
DEC_BATCH = 2
DEC_SEQ = 2048
PAST_LEN = 512
GRID_W = 64
N_HEADS = 8
N_KV_HEADS = 2
HEAD_DIM = 64
KV_GROUP = N_HEADS // N_KV_HEADS
ATTN_WIDTH = N_HEADS * HEAD_DIM
KV_WIDTH = N_KV_HEADS * HEAD_DIM
ROPE_AXIS_DIM = HEAD_DIM // 2
ROPE_BASE = 10000.0
POOL_WINDOWS = (2, 4, 8, 16)
POOL_WIDTH = 256
POOL_GROUP_DIM = 64
FOURIER_WIDTH = 256
FOURIER_GROUP_DIM = 64
N_BRANCHES = 3
OFF_K = ATTN_WIDTH
OFF_P = ATTN_WIDTH + 2 * KV_WIDTH
OFF_G = OFF_P + POOL_WIDTH + FOURIER_WIDTH
IN_WIDTH = OFF_G + N_BRANCHES * D_MODEL
N_EXPERT_GROUPS = 4
EXPERTS_PER_GROUP = 8
N_EXPERTS = N_EXPERT_GROUPS * EXPERTS_PER_GROUP
TOP_K = 2
EXPERT_FF = 256
N_MOD = 6
RMS_EPS = 1e-6
QK_SCALE = HEAD_DIM ** -0.5 * math.log2(math.e)

N_P = BATCH * SEQ
N_S = DEC_BATCH * DEC_SEQ
N_TOK = N_P + N_S
LANES = 128
MOD_ROWS = 8
POOL_PAD = 16

TM = 512
N_TILES = N_TOK // TM
P_TILES = N_P // TM
S_TILES_PER_SEQ = DEC_SEQ // TM
CTX_SEQ_PER_STEP = 4
TQ_S = 128
KEY_CHUNK = 512
TF_S = 512
EBLK = 512
RUN_ALIGN = 8
RUN_SHIFT = 3
RUN_BITS = 7
assert RUN_ALIGN << (RUN_BITS - 1) == TM
SORT_ROWS = TOP_K * TM + N_EXPERTS * RUN_ALIGN
DISPATCH_SLOTS = 3
assert N_TILES >= DISPATCH_SLOTS
TILE_BITS = 8
assert SORT_ROWS < RUN_ALIGN << TILE_BITS
N_RUNS = N_TILES * N_EXPERTS
N_EBLK = (N_TOK * TOP_K + N_RUNS * (RUN_ALIGN - 1) + N_EXPERTS * (EBLK - 1) + EBLK - 1) // EBLK
XS_ROWS = N_EBLK * EBLK
MOD_NT = 1536


def _dot(a, b):
    return jnp.dot(a, b, preferred_element_type=F32)


def _rms(x):
    return x * lax.rsqrt(jnp.mean(x * x, axis=-1, keepdims=True) + RMS_EPS)


def _mod_row(i):
    return jnp.where(i < P_TILES, 0, 1 + (i - P_TILES) // S_TILES_PER_SEQ)


def _rope_block(i):
    return jnp.where(i < P_TILES, S_TILES_PER_SEQ, (i - P_TILES) % S_TILES_PER_SEQ)


def _mod_kernel(c_ref, w_ref, b_ref, o_ref):
    c = c_ref[...]
    s = (c * jax.nn.sigmoid(c)).astype(BF16)
    o_ref[...] = _dot(s, w_ref[...].astype(BF16)) + b_ref[...]


def _mod_call(c_all, w_ada, b_ada):
    nt = (N_MOD * D_MODEL) // MOD_NT
    return pl.pallas_call(
        _mod_kernel,
        grid=(DEPTH, nt),
        in_specs=[
            pl.BlockSpec((MOD_ROWS, D_MODEL), lambda l, j: (0, 0)),
            pl.BlockSpec((None, D_MODEL, MOD_NT), lambda l, j: (l, 0, j)),
            pl.BlockSpec((None, 1, MOD_NT), lambda l, j: (l, 0, j)),
        ],
        out_specs=pl.BlockSpec((None, MOD_ROWS, MOD_NT), lambda l, j: (l, 0, j)),
        out_shape=jax.ShapeDtypeStruct((DEPTH, MOD_ROWS, N_MOD * D_MODEL), F32),
        name="adaln_mod",
    )(c_all, w_ada, b_ada.reshape(DEPTH, 1, N_MOD * D_MODEL))


def _prenorm(x, mod, gain):
    return (_rms(x) * gain) * (1.0 + mod[:, D_MODEL:2 * D_MODEL]) + mod[:, 0:D_MODEL]


def _prenorm_kernel(x_ref, mod_ref, gpre_ref, hb_ref):
    hb_ref[...] = _prenorm(x_ref[...], mod_ref[...], gpre_ref[...]).astype(BF16)


def _prenorm_call(l, x, mods, gpre):
    return pl.pallas_call(
        _prenorm_kernel,
        grid=(N_TILES,),
        in_specs=[
            pl.BlockSpec((TM, D_MODEL), lambda i: (i, 0)),
            pl.BlockSpec((None, None, 1, 2 * D_MODEL), lambda i: (l, _mod_row(i), 0, 0)),
            pl.BlockSpec((None, 1, D_MODEL), lambda i: (l, 0, 0)),
        ],
        out_specs=pl.BlockSpec((TM, D_MODEL), lambda i: (i, 0)),
        out_shape=jax.ShapeDtypeStruct((N_TOK, D_MODEL), BF16),
        name="prenorm",
    )(x, mods, gpre)


def _proj_kernel(hb_ref, w_ref, qg_ref, kg_ref, cos_ref, s1_ref, s2_ref,
                 avg_ref, dft_ref, q_ref, k_ref, v_ref, xp_ref, xc_ref, xs_ref, g_ref):
    hb = hb_ref[...]

    cos = cos_ref[...]
    s1 = s1_ref[...]
    s2 = s2_ref[...]

    def rope(t):
        return (t * cos + pltpu.roll(t, LANES - ROPE_AXIS_DIM // 2, 1) * s1
                + pltpu.roll(t, ROPE_AXIS_DIM // 2, 1) * s2)

    avg = avg_ref[...]
    for c in range(ATTN_WIDTH // LANES):
        lo = c * LANES
        q = _dot(hb, w_ref[:, lo:lo + LANES])
        ms = _dot((q * q).astype(BF16), avg)
        q = q * lax.rsqrt(ms + RMS_EPS) * qg_ref[...]
        q_ref[:, lo:lo + LANES] = (rope(q) * QK_SCALE).astype(BF16)

    kv = _dot(hb, w_ref[:, OFF_K:OFF_K + 2 * KV_WIDTH])
    k = kv[:, 0:KV_WIDTH]
    ms = _dot((k * k).astype(BF16), avg)
    k = k * lax.rsqrt(ms + RMS_EPS) * kg_ref[...]
    k_ref[...] = rope(k)
    v_ref[...] = kv[:, KV_WIDTH:]

    pf = _dot(hb, w_ref[:, OFF_P:OFF_G])
    xp_ref[...] = pf[:, 0:POOL_WIDTH]
    cs = _dot(pf[:, POOL_WIDTH:].astype(BF16), dft_ref[...])
    xc_ref[...] = cs[:, 0:FOURIER_WIDTH].astype(BF16)
    xs_ref[...] = cs[:, FOURIER_WIDTH:].astype(BF16)

    for c in range(N_BRANCHES):
        lo = OFF_G + c * D_MODEL
        g_ref[:, c * D_MODEL:(c + 1) * D_MODEL] = _dot(hb, w_ref[:, lo:lo + D_MODEL]).astype(BF16)


def _proj_call(l, hb, w_in, qg, kg, cos_t, s1_t, s2_t, avg, dft):
    row = lambda i: (i, 0)
    const2 = lambda i: (0, 0)
    per_layer = lambda i: (l, 0, 0)
    rope_spec = pl.BlockSpec((TM, LANES), lambda i: (_rope_block(i), 0))
    return pl.pallas_call(
        _proj_kernel,
        grid=(N_TILES,),
        in_specs=[
            pl.BlockSpec((TM, D_MODEL), row),
            pl.BlockSpec((None, D_MODEL, IN_WIDTH), per_layer),
            pl.BlockSpec((None, 1, LANES), per_layer),
            pl.BlockSpec((None, 1, LANES), per_layer),
            rope_spec, rope_spec, rope_spec,
            pl.BlockSpec((LANES, LANES), const2),
            pl.BlockSpec((FOURIER_WIDTH, 2 * FOURIER_WIDTH), const2),
        ],
        out_specs=[
            pl.BlockSpec((TM, ATTN_WIDTH), row),
            pl.BlockSpec((TM, KV_WIDTH), row),
            pl.BlockSpec((TM, KV_WIDTH), row),
            pl.BlockSpec((TM, POOL_WIDTH), row),
            pl.BlockSpec((TM, FOURIER_WIDTH), row),
            pl.BlockSpec((TM, FOURIER_WIDTH), row),
            pl.BlockSpec((TM, N_BRANCHES * D_MODEL), row),
        ],
        out_shape=[
            jax.ShapeDtypeStruct((N_TOK, ATTN_WIDTH), BF16),
            jax.ShapeDtypeStruct((N_TOK, KV_WIDTH), F32),
            jax.ShapeDtypeStruct((N_TOK, KV_WIDTH), F32),
            jax.ShapeDtypeStruct((N_TOK, POOL_WIDTH), F32),
            jax.ShapeDtypeStruct((N_TOK, FOURIER_WIDTH), BF16),
            jax.ShapeDtypeStruct((N_TOK, FOURIER_WIDTH), BF16),
            jax.ShapeDtypeStruct((N_TOK, N_BRANCHES * D_MODEL), BF16),
        ],
        name="proj",
    )(hb, w_in, qg, kg, cos_t, s1_t, s2_t, avg, dft)


def _attn_kernel(*refs, n_parts, tq):
    q_ref = refs[0]
    kv_refs = refs[1:1 + 2 * n_parts]
    o_ref = refs[-1]
    single_chunk = n_parts == 1 and kv_refs[0].shape[0] <= KEY_CHUNK
    outs = []
    for j in range(N_KV_HEADS):
        lo = j * HEAD_DIM
        qs = jnp.concatenate(
            [q_ref[:, (KV_GROUP * j + g) * HEAD_DIM:(KV_GROUP * j + g + 1) * HEAD_DIM]
             for g in range(KV_GROUP)], axis=0)
        m = acc = None
        for p in range(n_parts):
            k_ref, v_ref = kv_refs[2 * p], kv_refs[2 * p + 1]
            for c0 in range(0, k_ref.shape[0], KEY_CHUNK):
                c1 = min(c0 + KEY_CHUNK, k_ref.shape[0])
                kc = k_ref[c0:c1, lo:lo + HEAD_DIM].astype(BF16)
                s = lax.dot_general(qs, kc, (((1,), (1,)), ((), ())), preferred_element_type=F32)
                mc = jnp.max(s, axis=1, keepdims=True)
                m_new = mc if m is None else jnp.maximum(m, mc)
                e = jnp.exp2(s - m_new)
                vc = v_ref[c0:c1, lo:lo + HEAD_DIM].astype(BF16)
                if single_chunk:
                    den = jnp.sum(e, axis=1, keepdims=True)
                else:
                    vc = jnp.concatenate([vc, jnp.ones((c1 - c0, HEAD_DIM), BF16)], axis=1)
                pv = _dot(e.astype(BF16), vc)
                acc = pv if m is None else acc * jnp.exp2(m - m_new) + pv
                m = m_new
        o = acc / den if single_chunk else acc[:, 0:HEAD_DIM] / acc[:, HEAD_DIM:2 * HEAD_DIM]
        outs.extend(o[g * tq:(g + 1) * tq] for g in range(KV_GROUP))
    o_ref[...] = jnp.concatenate(outs, axis=1).astype(BF16)


def _attn_prompt_call(q, k, v):
    blk = lambda b: (b, 0)
    return pl.pallas_call(
        functools.partial(_attn_kernel, n_parts=1, tq=SEQ),
        grid=(BATCH,),
        in_specs=[
            pl.BlockSpec((SEQ, ATTN_WIDTH), blk),
            pl.BlockSpec((SEQ, KV_WIDTH), blk),
            pl.BlockSpec((SEQ, KV_WIDTH), blk),
        ],
        out_specs=pl.BlockSpec((SEQ, ATTN_WIDTH), blk),
        out_shape=jax.ShapeDtypeStruct((N_P, ATTN_WIDTH), BF16),
        name="attn_context",
    )(q, k, v)


def _attn_sample_call(l, q, k, v, cache_k, cache_v):
    nq = DEC_SEQ // TQ_S
    qrow = lambda b, i: (N_P // TQ_S + b * nq + i, 0)
    seq = lambda b, i: (N_P // DEC_SEQ + b, 0)
    cache = lambda b, i: (b, l, 0, 0)
    return pl.pallas_call(
        functools.partial(_attn_kernel, n_parts=2, tq=TQ_S),
        grid=(DEC_BATCH, nq),
        in_specs=[
            pl.BlockSpec((TQ_S, ATTN_WIDTH), qrow),
            pl.BlockSpec((None, None, PAST_LEN, KV_WIDTH), cache),
            pl.BlockSpec((None, None, PAST_LEN, KV_WIDTH), cache),
            pl.BlockSpec((DEC_SEQ, KV_WIDTH), seq),
            pl.BlockSpec((DEC_SEQ, KV_WIDTH), seq),
        ],
        out_specs=pl.BlockSpec((TQ_S, ATTN_WIDTH), lambda b, i: (b * nq + i, 0)),
        out_shape=jax.ShapeDtypeStruct((N_S, ATTN_WIDTH), BF16),
        name="attn_latent",
    )(q, cache_k, cache_v, k, v)


def _pool_kernel(xp_ref, bdw_ref, sc_ref, o_ref, pad_ref, *, seq_len, n_seq):
    for s in range(n_seq):
        _pool_sequence(xp_ref, bdw_ref, sc_ref, o_ref, pad_ref, s * seq_len, seq_len)


def _pool_sequence(xp_ref, bdw_ref, sc_ref, o_ref, pad_ref, row0, seq_len):
    half = POOL_WIDTH // 2
    zeros = jnp.zeros((POOL_PAD, POOL_WIDTH), F32)
    pad_ref[0:POOL_PAD, :] = zeros
    pad_ref[POOL_PAD + seq_len:, :] = zeros
    pad_ref[POOL_PAD:POOL_PAD + seq_len, :] = xp_ref[row0:row0 + seq_len, :]
    chunk = min(seq_len, 256)
    lane = lax.broadcasted_iota(jnp.int32, (chunk, half), 1)
    first = lane < POOL_GROUP_DIM
    for c in range(seq_len // chunk):
        base = c * chunk
        t = lax.broadcasted_iota(jnp.int32, (chunk, half), 0) + base

        def sh(j, lo):
            return pad_ref[POOL_PAD + base + j:POOL_PAD + base + j + chunk, lo:lo + half]

        def cnt(w):
            return (jnp.minimum(t + w // 2, seq_len) - jnp.maximum(t - w // 2, 0)).astype(F32)

        xa = sh(0, 0)
        w2 = sh(-1, 0) + xa
        w4 = w2 + sh(-2, 0) + sh(1, 0)
        xb = sh(0, half)
        w8 = xb
        for j in (-4, -3, -2, -1, 1, 2, 3):
            w8 = w8 + sh(j, half)
        w16 = w8
        for j in (-8, -7, -6, -5, 4, 5, 6, 7):
            w16 = w16 + sh(j, half)
        pa = jnp.where(first, w2 / cnt(2), w4 / cnt(4)) - xa
        pb = jnp.where(first, w8 / cnt(8), w16 / cnt(16)) - xb
        pooled = jnp.concatenate([pa, pb], axis=1).astype(BF16)
        o_ref[row0 + base:row0 + base + chunk, :] = (
            _dot(pooled, bdw_ref[...]) * sc_ref[...]).astype(BF16)


def _pool_call(l, xp, bdw, scale, seq_len, n_seq, blk0, seq_per_step):
    per_layer = lambda b: (l, 0, 0)
    rows = seq_len * seq_per_step
    return pl.pallas_call(
        functools.partial(_pool_kernel, seq_len=seq_len, n_seq=seq_per_step),
        grid=(n_seq // seq_per_step,),
        in_specs=[
            pl.BlockSpec((rows, POOL_WIDTH), lambda b: (blk0 + b, 0)),
            pl.BlockSpec((None, POOL_WIDTH, POOL_WIDTH), per_layer),
            pl.BlockSpec((None, 1, POOL_WIDTH), per_layer),
        ],
        out_specs=pl.BlockSpec((rows, POOL_WIDTH), lambda b: (b, 0)),
        out_shape=jax.ShapeDtypeStruct((n_seq * seq_len, POOL_WIDTH), BF16),
        scratch_shapes=[pltpu.VMEM((seq_len + 2 * POOL_PAD, POOL_WIDTH), F32)],
        name="pool_%d" % seq_len,
    )(xp, bdw, scale)


def _fourier_kernel(c_ref, s_ref, xc_ref, xs_ref, o_ref, *, scale, n_seq=1):
    seq_len = xc_ref.shape[0] // n_seq
    out_len = o_ref.shape[0] // n_seq
    for s in range(n_seq):
        rows = slice(s * seq_len, (s + 1) * seq_len)
        y = _dot(c_ref[...], xc_ref[rows, :]) - _dot(s_ref[...], xs_ref[rows, :])
        o_ref[s * out_len:(s + 1) * out_len, :] = (y * scale).astype(BF16)


def _fourier_prompt_call(cmat, smat, xc, xs):
    blk = lambda b: (b, 0)
    const2 = lambda b: (0, 0)
    rows = SEQ * CTX_SEQ_PER_STEP
    return pl.pallas_call(
        functools.partial(_fourier_kernel, scale=1.0 / math.sqrt(SEQ * FOURIER_GROUP_DIM),
                          n_seq=CTX_SEQ_PER_STEP),
        grid=(BATCH // CTX_SEQ_PER_STEP,),
        in_specs=[
            pl.BlockSpec((SEQ, SEQ), const2),
            pl.BlockSpec((SEQ, SEQ), const2),
            pl.BlockSpec((rows, FOURIER_WIDTH), blk),
            pl.BlockSpec((rows, FOURIER_WIDTH), blk),
        ],
        out_specs=pl.BlockSpec((rows, FOURIER_WIDTH), blk),
        out_shape=jax.ShapeDtypeStruct((N_P, FOURIER_WIDTH), BF16),
        name="fourier_context",
    )(cmat, smat, xc, xs)


def _fourier_sample_call(cmat, smat, xc, xs):
    nt = DEC_SEQ // TF_S
    rows = lambda b, i: (i, 0)
    seq = lambda b, i: (N_P // DEC_SEQ + b, 0)
    out = lambda b, i: (b * nt + i, 0)
    return pl.pallas_call(
        functools.partial(_fourier_kernel, scale=1.0 / math.sqrt(DEC_SEQ * FOURIER_GROUP_DIM)),
        grid=(DEC_BATCH, nt),
        in_specs=[
            pl.BlockSpec((TF_S, DEC_SEQ), rows),
            pl.BlockSpec((TF_S, DEC_SEQ), rows),
            pl.BlockSpec((DEC_SEQ, FOURIER_WIDTH), seq),
            pl.BlockSpec((DEC_SEQ, FOURIER_WIDTH), seq),
        ],
        out_specs=pl.BlockSpec((TF_S, FOURIER_WIDTH), out),
        out_shape=jax.ShapeDtypeStruct((N_S, FOURIER_WIDTH), BF16),
        name="fourier_latent",
    )(cmat, smat, xc, xs)


def _merge_kernel(x_ref, attn_p_ref, attn_s_ref, pool_p_ref, pool_s_ref, four_p_ref, four_s_ref,
                  g_ref, mod_ref, gpost_ref, gffn_ref,
                  wa_ref, wp_ref, wf_ref, wo_ref, wr_ref, br_ref, tri_ref, upper_ref,
                  x1_ref, h2_ref, pos_ref, post_ref, mw_ref, cnt_ref):
    i = pl.program_id(0)
    mod = mod_ref[...]
    g1 = mod[:, 2 * D_MODEL:3 * D_MODEL]
    sh2 = mod[:, 3 * D_MODEL:4 * D_MODEL]
    sc2 = mod[:, 4 * D_MODEL:5 * D_MODEL]

    def gate(c):
        return 0.5 * jnp.tanh(0.5 * g_ref[:, c * D_MODEL:(c + 1) * D_MODEL].astype(F32)) + 0.5

    def branch(p_ref, s_ref):
        return jnp.where(i < P_TILES, p_ref[...], s_ref[...])

    merged = gate(0) * _dot(branch(attn_p_ref, attn_s_ref), wa_ref[...])
    merged = merged + gate(1) * _dot(branch(pool_p_ref, pool_s_ref), wp_ref[...])
    merged = merged + gate(2) * _dot(branch(four_p_ref, four_s_ref), wf_ref[...])
    mix = _dot(merged.astype(BF16), wo_ref[...])
    x1 = x_ref[...] + g1 * (_rms(mix) * gpost_ref[...])
    x1_ref[...] = x1
    h2 = (_rms(x1) * gffn_ref[...]) * (1.0 + sc2) + sh2
    h2b = h2.astype(BF16)
    h2_ref[...] = h2b

    logits = _dot(h2b, wr_ref[...]) + br_ref[...]
    lane = lax.broadcasted_iota(jnp.int32, (TM, LANES), 1)
    lanef = lane.astype(F32)
    neg = jnp.float32(-3e38)
    big = jnp.float32(1e9)
    is_g = (lane >= N_EXPERTS) & (lane < N_EXPERTS + N_EXPERT_GROUPS)
    lg = jnp.where(is_g, logits, neg)
    gmax = jnp.max(lg, axis=1, keepdims=True)
    g_sel = jnp.min(jnp.where(lg == gmax, lanef - N_EXPERTS, big), axis=1, keepdims=True)
    p_g = 1.0 / jnp.sum(jnp.where(is_g, jnp.exp(logits - gmax), 0.0), axis=1, keepdims=True)
    grp = lax.shift_right_logical(lane, int(math.log2(EXPERTS_PER_GROUP))).astype(F32)
    in_grp = (lane < N_EXPERTS) & (grp == g_sel)
    le = jnp.where(in_grp, logits, neg)
    v1 = jnp.max(le, axis=1, keepdims=True)
    i1 = jnp.min(jnp.where(le == v1, lanef, big), axis=1, keepdims=True)
    le2 = jnp.where(lanef == i1, neg, le)
    v2 = jnp.max(le2, axis=1, keepdims=True)
    i2 = jnp.min(jnp.where(le2 == v2, lanef, big), axis=1, keepdims=True)
    e21 = jnp.exp(v2 - v1)
    w1 = p_g / (1.0 + e21)
    w2 = p_g * e21 / (1.0 + e21)

    oh1 = (lanef == i1).astype(F32)
    oh2 = (lanef == i2).astype(F32)
    ohb = (oh1 + oh2).astype(BF16)
    before = _dot(tri_ref[...], ohb)
    cnt = _dot(jnp.ones((MOD_ROWS, TM), BF16), ohb)
    cnt_pad = (lax.shift_right_logical(cnt.astype(jnp.int32) + (RUN_ALIGN - 1), RUN_SHIFT)
               * RUN_ALIGN).astype(F32)
    run_off = _dot(cnt_pad.astype(BF16), upper_ref[...])
    slot = run_off[0:1, :] + before
    p1 = jnp.sum(slot * oh1, axis=1, keepdims=True)
    p2 = jnp.sum(slot * oh2, axis=1, keepdims=True)
    cnt_ref[...] = cnt

    pos = jnp.where(lane == 0, p1, jnp.where(lane == 1, p2, 0.0))
    pos_ref[...] = pos.astype(jnp.int32)
    post_ref[...] = pos.T[0:MOD_ROWS, :].astype(jnp.int32)
    mw_ref[...] = jnp.where(lane == 0, w1, w2)


def _merge_call(l, x, attn, pool, four, gates, mods, gpost, gffn, wa, wp, wf, wo, wr, br, tri,
                upper):
    row = lambda i: (i, 0)
    prow = lambda i: (jnp.minimum(i, P_TILES - 1), 0)
    srow = lambda i: (jnp.maximum(i - P_TILES, 0), 0)
    const2 = lambda i: (0, 0)
    per_layer = lambda i: (l, 0, 0)
    return pl.pallas_call(
        _merge_kernel,
        grid=(N_TILES,),
        in_specs=[
            pl.BlockSpec((TM, D_MODEL), row),
            pl.BlockSpec((TM, ATTN_WIDTH), prow),
            pl.BlockSpec((TM, ATTN_WIDTH), srow),
            pl.BlockSpec((TM, POOL_WIDTH), prow),
            pl.BlockSpec((TM, POOL_WIDTH), srow),
            pl.BlockSpec((TM, FOURIER_WIDTH), prow),
            pl.BlockSpec((TM, FOURIER_WIDTH), srow),
            pl.BlockSpec((TM, N_BRANCHES * D_MODEL), row),
            pl.BlockSpec((None, None, 1, N_MOD * D_MODEL), lambda i: (l, _mod_row(i), 0, 0)),
            pl.BlockSpec((None, 1, D_MODEL), per_layer),
            pl.BlockSpec((None, 1, D_MODEL), per_layer),
            pl.BlockSpec((None, ATTN_WIDTH, D_MODEL), per_layer),
            pl.BlockSpec((None, POOL_WIDTH, D_MODEL), per_layer),
            pl.BlockSpec((None, FOURIER_WIDTH, D_MODEL), per_layer),
            pl.BlockSpec((None, D_MODEL, D_MODEL), per_layer),
            pl.BlockSpec((None, D_MODEL, LANES), per_layer),
            pl.BlockSpec((None, 1, LANES), per_layer),
            pl.BlockSpec((TM, TM), const2),
            pl.BlockSpec((LANES, LANES), const2),
        ],
        out_specs=[
            pl.BlockSpec((TM, D_MODEL), row),
            pl.BlockSpec((TM, D_MODEL), row),
            pl.BlockSpec((TM, LANES), row),
            pl.BlockSpec((MOD_ROWS, TM), lambda i: (0, i)),
            pl.BlockSpec((TM, LANES), row),
            pl.BlockSpec((MOD_ROWS, LANES), row),
        ],
        out_shape=[
            jax.ShapeDtypeStruct((N_TOK, D_MODEL), F32),
            jax.ShapeDtypeStruct((N_TOK, D_MODEL), BF16),
            jax.ShapeDtypeStruct((N_TOK, LANES), jnp.int32),
            jax.ShapeDtypeStruct((MOD_ROWS, N_TOK), jnp.int32),
            jax.ShapeDtypeStruct((N_TOK, LANES), F32),
            jax.ShapeDtypeStruct((N_TILES * MOD_ROWS, LANES), F32),
        ],
        name="merge_router",
    )(x, attn[0], attn[1], pool[0], pool[1], four[0], four[1], gates, mods, gpost, gffn,
      wa, wp, wf, wo, wr, br, tri, upper)


HALF = D_MODEL // 2
U32 = jnp.uint32
HI_MASK = 0xFFFF0000


def _pack_rows(x):
    lo = lax.bitcast_convert_type(x[:, :HALF], U32)
    hi = lax.bitcast_convert_type(x[:, HALF:], U32)
    return lax.shift_right_logical(lo, U32(16)) | (hi & U32(HI_MASK))


def _unpack_rows(u):
    lo = lax.bitcast_convert_type(lax.shift_left(u, U32(16)), F32)
    hi = lax.bitcast_convert_type(u & U32(HI_MASK), F32)
    return jnp.concatenate([lo.astype(BF16), hi.astype(BF16)], axis=1)


def _run_copies(cnt, src_ref, src_off, dst_ref, dst_off, sem, wait, bits=RUN_BITS):
    off = jnp.int32(0)
    for b in reversed(range(bits)):
        size = RUN_ALIGN << b
        take = lax.shift_right_logical(cnt, b) & 1

        @pl.when(take == 1)
        def _(off=off, size=size):
            cp = pltpu.make_async_copy(
                src_ref.at[pl.ds(pl.multiple_of(src_off + off, RUN_ALIGN), size)],
                dst_ref.at[pl.ds(pl.multiple_of(dst_off + off, RUN_ALIGN), size)], sem)
            if wait:
                cp.wait()
            else:
                cp.start()

        off = off + take * size


def _tile_runs(tile, cnt_ref, fn):
    def body(e, carry):
        r = tile * N_EXPERTS + e
        fn(r, cnt_ref[r])
        return carry

    lax.fori_loop(0, N_EXPERTS, body, 0)


def _dispatch_kernel(cnt_ref, toff_ref, xoff_ref, tsum_ref, h_ref, post_ref, xs_in_ref, xs_ref,
                     sorted_ref, sem):
    del xs_in_ref
    i = pl.program_id(0)
    slot = lax.rem(i, DISPATCH_SLOTS)
    rows = lax.broadcasted_iota(jnp.int32, (SORT_ROWS, TM), 0)
    p = post_ref[...]
    perm = jnp.where(rows == p[0:1, :], 1.0, jnp.where(rows == p[1:2, :], 1.0, 0.0)).astype(BF16)
    sorted_ref[slot] = _pack_rows(_dot(perm, h_ref[...]))

    def wait(tile, slot):
        _run_copies(tsum_ref[tile], sorted_ref.at[slot], 0, xs_ref, 0, sem.at[slot], True, TILE_BITS)

    lag = DISPATCH_SLOTS - 1

    @pl.when(i >= lag)
    def _():
        wait(i - lag, lax.rem(i + 1, DISPATCH_SLOTS))

    _tile_runs(i, cnt_ref, lambda r, c: _run_copies(
        c, sorted_ref.at[slot], toff_ref[r], xs_ref, xoff_ref[r], sem.at[slot], False))

    @pl.when(i == pl.num_programs(0) - 1)
    def _():
        for back in reversed(range(lag)):
            wait(i - back, lax.rem(i - back, DISPATCH_SLOTS))


def _dispatch_call(cnt, toff, xoff, tsum, h2, post, xs):
    return pl.pallas_call(
        _dispatch_kernel,
        grid_spec=pltpu.PrefetchScalarGridSpec(
            num_scalar_prefetch=4,
            grid=(N_TILES,),
            in_specs=[
                pl.BlockSpec((TM, D_MODEL), lambda i, *_: (i, 0)),
                pl.BlockSpec((MOD_ROWS, TM), lambda i, *_: (0, i)),
                pl.BlockSpec(memory_space=pl.ANY),
            ],
            out_specs=pl.BlockSpec(memory_space=pl.ANY),
            scratch_shapes=[pltpu.VMEM((DISPATCH_SLOTS, SORT_ROWS, HALF), U32),
                            pltpu.SemaphoreType.DMA((DISPATCH_SLOTS,))],
        ),
        out_shape=jax.ShapeDtypeStruct((XS_ROWS, HALF), U32),
        input_output_aliases={6: 0},
        compiler_params=pltpu.CompilerParams(dimension_semantics=("arbitrary",)),
        name="moe_dispatch",
    )(cnt, toff, xoff, tsum, h2, post, xs)


def _expert_kernel(bstart_ref, bcnt_ref, nu_ref, wg_ref, wu_ref, wd_ref, xs_ref, y_prev_ref, y_ref,
                   wgb, wub, wdb, xbuf, ybuf, xsem, ysem):
    del y_prev_ref
    e = pl.program_id(0)
    n_used = nu_ref[0]

    def x_copy(g, slot):
        rows = pl.ds(pl.multiple_of(g * EBLK, EBLK), EBLK)
        return pltpu.make_async_copy(xs_ref.at[rows], xbuf.at[slot], xsem.at[slot])

    def y_copy(g, slot):
        rows = pl.ds(pl.multiple_of(g * EBLK, EBLK), EBLK)
        return pltpu.make_async_copy(ybuf.at[slot], y_ref.at[rows], ysem.at[slot])

    @pl.when(e == 0)
    def _():
        x_copy(0, 0).start()

    wgb[...] = wg_ref[...].astype(BF16)
    wub[...] = wu_ref[...].astype(BF16)
    wdb[...] = wd_ref[...].astype(BF16)
    first = bstart_ref[e]

    def block(j, carry):
        g = first + j
        slot = lax.rem(g, 2)
        x_copy(g, slot).wait()

        @pl.when(g + 1 < n_used)
        def _():
            x_copy(g + 1, 1 - slot).start()

        xb = _unpack_rows(xbuf[slot])
        gate = _dot(xb, wgb[...])
        up = _dot(xb, wub[...])
        act = (gate * jax.nn.sigmoid(gate)) * up
        y = _dot(act.astype(BF16), wdb[...])

        @pl.when(g >= 2)
        def _():
            y_copy(g - 2, slot).wait()

        ybuf[slot] = _pack_rows(y.astype(BF16).astype(F32))
        y_copy(g, slot).start()
        return carry

    lax.fori_loop(0, bcnt_ref[e], block, 0)

    @pl.when(e == pl.num_programs(0) - 1)
    def _():
        @pl.when(n_used >= 2)
        def _():
            y_copy(n_used - 2, lax.rem(n_used, 2)).wait()

        y_copy(n_used - 1, lax.rem(n_used - 1, 2)).wait()


def _expert_call(l, blk_start, blk_cnt, n_used, xs, y_prev, w_gate, w_up, w_down):
    wsel = lambda e, *_: (l, e, 0, 0)
    return pl.pallas_call(
        _expert_kernel,
        grid_spec=pltpu.PrefetchScalarGridSpec(
            num_scalar_prefetch=3,
            grid=(N_EXPERTS,),
            in_specs=[
                pl.BlockSpec((None, None, D_MODEL, EXPERT_FF), wsel),
                pl.BlockSpec((None, None, D_MODEL, EXPERT_FF), wsel),
                pl.BlockSpec((None, None, EXPERT_FF, D_MODEL), wsel),
                pl.BlockSpec(memory_space=pl.ANY),
                pl.BlockSpec(memory_space=pl.ANY),
            ],
            out_specs=pl.BlockSpec(memory_space=pl.ANY),
            scratch_shapes=[
                pltpu.VMEM((D_MODEL, EXPERT_FF), BF16),
                pltpu.VMEM((D_MODEL, EXPERT_FF), BF16),
                pltpu.VMEM((EXPERT_FF, D_MODEL), BF16),
                pltpu.VMEM((2, EBLK, HALF), U32),
                pltpu.VMEM((2, EBLK, HALF), U32),
                pltpu.SemaphoreType.DMA((2,)),
                pltpu.SemaphoreType.DMA((2,)),
            ],
        ),
        out_shape=jax.ShapeDtypeStruct((XS_ROWS, HALF), U32),
        input_output_aliases={7: 0},
        compiler_params=pltpu.CompilerParams(dimension_semantics=("arbitrary",)),
        name="moe_experts",
    )(blk_start, blk_cnt, n_used, w_gate, w_up, w_down, xs, y_prev)


def _combine_kernel(cnt_ref, toff_ref, xoff_ref, tsum_ref, y_ref, x1_ref, pos_ref, mw_ref, g2_ref,
                    gpost_ref, *rest, has_next):
    if has_next:
        nmod_ref, ngpre_ref, o_ref, hb_ref, ybuf, sem = rest
    else:
        o_ref, ybuf, sem = rest
    i = pl.program_id(0)
    slot = lax.rem(i, 2)

    def start(tile, slot):
        _tile_runs(tile, cnt_ref, lambda r, c: _run_copies(
            c, y_ref, xoff_ref[r], ybuf.at[slot], toff_ref[r], sem.at[slot], False))

    @pl.when(i == 0)
    def _():
        ybuf[...] = jnp.zeros_like(ybuf)
        start(0, 0)

    @pl.when(i + 1 < pl.num_programs(0))
    def _():
        start(i + 1, 1 - slot)

    _run_copies(tsum_ref[i], y_ref, 0, ybuf.at[slot], 0, sem.at[slot], True, TILE_BITS)

    pos = pos_ref[...]
    mw = mw_ref[...]
    cols = lax.broadcasted_iota(jnp.int32, (TM, SORT_ROWS), 1)
    qw = (jnp.where(cols == pos[:, 0:1], mw[:, 0:1], 0.0)
          + jnp.where(cols == pos[:, 1:2], mw[:, 1:2], 0.0)).astype(BF16)
    ffn = _dot(qw, _unpack_rows(ybuf[slot]))
    x2 = x1_ref[...] + g2_ref[...] * (_rms(ffn) * gpost_ref[...])
    o_ref[...] = x2
    if has_next:
        hb_ref[...] = _prenorm(x2, nmod_ref[...], ngpre_ref[...]).astype(BF16)


def _combine_call(l, cnt, toff, xoff, tsum, y, x1, pos, mw, mods, gpost, gpre):
    has_next = l + 1 < DEPTH
    row = lambda i, *_: (i, 0)
    in_specs = [
        pl.BlockSpec(memory_space=pl.ANY),
        pl.BlockSpec((TM, D_MODEL), row),
        pl.BlockSpec((TM, LANES), row),
        pl.BlockSpec((TM, LANES), row),
        pl.BlockSpec((None, None, 1, D_MODEL), lambda i, *_: (l, _mod_row(i), 0, N_MOD - 1)),
        pl.BlockSpec((None, 1, D_MODEL), lambda i, *_: (l, 0, 0)),
    ]
    args = [cnt, toff, xoff, tsum, y, x1, pos, mw, mods, gpost]
    out_specs = [pl.BlockSpec((TM, D_MODEL), row)]
    out_shape = [jax.ShapeDtypeStruct((N_TOK, D_MODEL), F32)]
    if has_next:
        in_specs += [
            pl.BlockSpec((None, None, 1, 2 * D_MODEL), lambda i, *_: (l + 1, _mod_row(i), 0, 0)),
            pl.BlockSpec((None, 1, D_MODEL), lambda i, *_: (l + 1, 0, 0)),
        ]
        args += [mods, gpre]
        out_specs.append(pl.BlockSpec((TM, D_MODEL), row))
        out_shape.append(jax.ShapeDtypeStruct((N_TOK, D_MODEL), BF16))
    return pl.pallas_call(
        functools.partial(_combine_kernel, has_next=has_next),
        grid_spec=pltpu.PrefetchScalarGridSpec(
            num_scalar_prefetch=4,
            grid=(N_TILES,),
            in_specs=in_specs,
            out_specs=out_specs,
            scratch_shapes=[pltpu.VMEM((2, SORT_ROWS, HALF), U32), pltpu.SemaphoreType.DMA((2,))],
        ),
        out_shape=out_shape,
        compiler_params=pltpu.CompilerParams(dimension_semantics=("arbitrary",)),
        name="moe_combine",
    )(*args)


def _dft_mats(n):
    k = np.arange(n, dtype=np.int64)
    ang = 2.0 * np.pi * ((k[:, None] * k[None, :]) % n).astype(np.float64) / n
    return np.cos(ang), np.sin(ang)


def _block_diag(m, reps):
    n = m.shape[0]
    out = np.zeros((n * reps, n * reps), m.dtype)
    for r in range(reps):
        out[r * n:(r + 1) * n, r * n:(r + 1) * n] = m
    return out


def _rope_tables():
    t = np.arange(DEC_SEQ)
    pos = np.stack([t // GRID_W, t % GRID_W], axis=1).astype(np.float64)
    n_freq = ROPE_AXIS_DIM // 2
    inv = ROPE_BASE ** (-np.arange(n_freq, dtype=np.float64) * 2.0 / ROPE_AXIS_DIM)
    ang = pos[:, :, None] * inv[None, None, :]
    cos = np.cos(ang)
    sin = np.sin(ang)
    zero = np.zeros_like(sin[:, 0])
    cos_h = np.concatenate([cos[:, 0], cos[:, 0], cos[:, 1], cos[:, 1]], axis=1)
    s1_h = np.concatenate([-sin[:, 0], zero, -sin[:, 1], zero], axis=1)
    s2_h = np.concatenate([zero, sin[:, 0], zero, sin[:, 1]], axis=1)
    reps = LANES // HEAD_DIM

    def table(a, ident):
        a = np.tile(a, (1, reps))
        pad = np.full((TM, LANES), ident, np.float64)
        return jnp.asarray(np.concatenate([a, pad], axis=0), F32)

    return table(cos_h, 1.0), table(s1_h, 0.0), table(s2_h, 0.0)


def kernel(x_prompt, x_sample, cache_k, cache_v, c, c_ctx, w_ada, b_ada, norm_mix_pre,
           norm_mix_post, norm_ffn_pre, norm_ffn_post, w_in, q_norm, k_norm, w_attn_out,
           w_pool_group, pool_scale, w_pool_out, w_fourier_out, w_out, w_router_group,
           b_router_group, w_router_expert, b_router_expert, w_expert_gate, w_expert_up,
           w_expert_down):
    cos_t, s1_t, s2_t = _rope_tables()
    avg = jnp.asarray(_block_diag(np.full((HEAD_DIM, HEAD_DIM), 1.0 / HEAD_DIM), LANES // HEAD_DIM), BF16)
    c64, s64 = _dft_mats(FOURIER_GROUP_DIM)
    n_fg = FOURIER_WIDTH // FOURIER_GROUP_DIM
    dft_ch = jnp.asarray(np.concatenate([_block_diag(c64, n_fg), _block_diag(s64, n_fg)], axis=1), BF16)
    cp, sp = _dft_mats(SEQ)
    cp, sp = jnp.asarray(cp, BF16), jnp.asarray(sp, BF16)
    cl, sl = _dft_mats(DEC_SEQ)
    cl, sl = jnp.asarray(cl, BF16), jnp.asarray(sl, BF16)
    tri = jnp.asarray(np.tril(np.ones((TM, TM)), -1), BF16)
    upper = jnp.asarray(np.triu(np.ones((LANES, LANES)), 1), BF16)

    w_in_b = w_in.astype(BF16)
    wa_b = w_attn_out.astype(BF16)
    wp_b = w_pool_out.astype(BF16)
    wf_b = w_fourier_out.astype(BF16)
    wo_b = w_out.astype(BF16)
    pad_r = jnp.zeros((DEPTH, D_MODEL, LANES - N_EXPERTS - N_EXPERT_GROUPS), F32)
    wr_b = jnp.concatenate([w_router_expert, w_router_group, pad_r], axis=2).astype(BF16)
    br = jnp.concatenate([b_router_expert, b_router_group,
                          jnp.zeros((DEPTH, LANES - N_EXPERTS - N_EXPERT_GROUPS), F32)],
                         axis=1).reshape(DEPTH, 1, LANES)
    n_pg = POOL_WIDTH // POOL_GROUP_DIM
    bdw = jnp.zeros((DEPTH, POOL_WIDTH, POOL_WIDTH), F32)
    for g in range(n_pg):
        lo = g * POOL_GROUP_DIM
        bdw = bdw.at[:, lo:lo + POOL_GROUP_DIM, lo:lo + POOL_GROUP_DIM].set(w_pool_group[:, g])
    bdw = bdw.astype(BF16)
    pscale = pool_scale.reshape(DEPTH, 1, POOL_WIDTH)
    qg = jnp.tile(q_norm, (1, LANES // HEAD_DIM)).reshape(DEPTH, 1, LANES)
    kg = jnp.tile(k_norm, (1, LANES // HEAD_DIM)).reshape(DEPTH, 1, LANES)
    gpre = norm_mix_pre.reshape(DEPTH, 1, D_MODEL)
    gpost = norm_mix_post.reshape(DEPTH, 1, D_MODEL)
    gffn = norm_ffn_pre.reshape(DEPTH, 1, D_MODEL)
    gfpost = norm_ffn_post.reshape(DEPTH, 1, D_MODEL)
    ck = cache_k.reshape(DEC_BATCH, DEPTH, PAST_LEN, KV_WIDTH)
    cv = cache_v.reshape(DEC_BATCH, DEPTH, PAST_LEN, KV_WIDTH)

    c_all = jnp.concatenate([c_ctx[None, :], c, jnp.zeros((MOD_ROWS - 1 - DEC_BATCH, D_MODEL), F32)], axis=0)
    mods = _mod_call(c_all, w_ada, b_ada).reshape(DEPTH, MOD_ROWS, 1, N_MOD * D_MODEL)

    x = jnp.concatenate([x_prompt.reshape(N_P, D_MODEL), x_sample.reshape(N_S, D_MODEL)], axis=0)
    xs_buf = jnp.zeros((XS_ROWS, HALF), U32)
    y = jnp.zeros((XS_ROWS, HALF), U32)
    new_k, new_v = [], []
    hb = _prenorm_call(0, x, mods, gpre)
    for l in range(DEPTH):
        q, k, v, xp, xc, xsn, gates = _proj_call(l, hb, w_in_b, qg, kg, cos_t, s1_t, s2_t, avg, dft_ch)
        new_k.append(k[:N_P].reshape(BATCH, SEQ, N_KV_HEADS, HEAD_DIM))
        new_v.append(v[:N_P].reshape(BATCH, SEQ, N_KV_HEADS, HEAD_DIM))
        attn = (_attn_prompt_call(q, k, v), _attn_sample_call(l, q, k, v, ck, cv))
        pool = (_pool_call(l, xp, bdw, pscale, SEQ, BATCH, 0, CTX_SEQ_PER_STEP),
                _pool_call(l, xp, bdw, pscale, DEC_SEQ, DEC_BATCH, N_P // DEC_SEQ, 1))
        four = (_fourier_prompt_call(cp, sp, xc, xsn), _fourier_sample_call(cl, sl, xc, xsn))
        x1, h2, pos, post, mw, cnt = _merge_call(l, x, attn, pool, four, gates, mods, gpost, gffn,
                                                 wa_b, wp_b, wf_b, wo_b, wr_b, br, tri, upper)
        runs = cnt.reshape(N_TILES, MOD_ROWS, LANES)[:, 0, :N_EXPERTS].astype(jnp.int32)
        runs = ((runs + RUN_ALIGN - 1) // RUN_ALIGN) * RUN_ALIGN
        tile_off = jnp.cumsum(runs, axis=1) - runs
        rows_e = jnp.sum(runs, axis=0)
        padded = ((rows_e + EBLK - 1) // EBLK) * EBLK
        pad_end = jnp.cumsum(padded)
        xs_off = (pad_end - padded)[None, :] + jnp.cumsum(runs, axis=0) - runs
        blk_cnt = padded // EBLK
        blk_start = (pad_end - padded) // EBLK
        n_used = pad_end[-1:] // EBLK
        run_cnt = (runs // RUN_ALIGN).reshape(N_RUNS)
        tile_off = tile_off.reshape(N_RUNS)
        xs_off = xs_off.reshape(N_RUNS)
        tile_cnt = jnp.sum(runs, axis=1) // RUN_ALIGN
        xs_buf = _dispatch_call(run_cnt, tile_off, xs_off, tile_cnt, h2, post, xs_buf)
        y = _expert_call(l, blk_start, blk_cnt, n_used, xs_buf, y,
                         w_expert_gate, w_expert_up, w_expert_down)
        outs = _combine_call(l, run_cnt, tile_off, xs_off, tile_cnt, y, x1, pos, mw, mods, gfpost,
                             gpre)
        x = outs[0]
        hb = outs[-1]

    y_prompt = x[:N_P].reshape(BATCH, SEQ, D_MODEL)
    y_sample = x[N_P:].reshape(DEC_BATCH, DEC_SEQ, D_MODEL)
    return (y_prompt, y_sample, jnp.stack(new_k, axis=1), jnp.stack(new_v, axis=1))
```

```python
import functools
import math

import numpy as np
import jax
import jax.numpy as jnp
from jax import lax
from jax.experimental import pallas as pl
from jax.experimental.pallas import tpu as pltpu

F32 = jnp.float32
BF16 = jnp.bfloat16

D_MODEL = 1024
BATCH = 32
SEQ = 256
DEPTH = 4
DEC_BATCH = 2
DEC_SEQ = 2048
PAST_LEN = 512
GRID_W = 64
N_HEADS = 8
N_KV_HEADS = 2
HEAD_DIM = 64
KV_GROUP = N_HEADS // N_KV_HEADS
ATTN_WIDTH = N_HEADS * HEAD_DIM
KV_WIDTH = N_KV_HEADS * HEAD_DIM
ROPE_AXIS_DIM = HEAD_DIM // 2
ROPE_BASE = 10000.0
POOL_WINDOWS = (2, 4, 8, 16)
POOL_WIDTH = 256
POOL_GROUP_DIM = 64
FOURIER_WIDTH = 256
FOURIER_GROUP_DIM = 64
N_BRANCHES = 3
OFF_K = ATTN_WIDTH
OFF_P = ATTN_WIDTH + 2 * KV_WIDTH
OFF_G = OFF_P + POOL_WIDTH + FOURIER_WIDTH
IN_WIDTH = OFF_G + N_BRANCHES * D_MODEL
N_EXPERT_GROUPS = 4
EXPERTS_PER_GROUP = 8
N_EXPERTS = N_EXPERT_GROUPS * EXPERTS_PER_GROUP
TOP_K = 2
EXPERT_FF = 256
N_MOD = 6
RMS_EPS = 1e-6
QK_SCALE = HEAD_DIM ** -0.5 * math.log2(math.e)

N_P = BATCH * SEQ
N_S = DEC_BATCH * DEC_SEQ
N_TOK = N_P + N_S
LANES = 128
MOD_ROWS = 8
POOL_PAD = 16

TM = 512
PROJ_TM = 1024
N_TILES = N_TOK // TM
P_TILES = N_P // TM
S_TILES_PER_SEQ = DEC_SEQ // TM
CTX_SEQ_PER_STEP = 4
TQ_S = 128
KEY_CHUNK = 512
TF_S = 512
EBLK = 512
RUN_ALIGN = 8
RUN_SHIFT = 3
RUN_BITS = 7
assert RUN_ALIGN << (RUN_BITS - 1) == TM
SORT_ROWS = TOP_K * TM + N_EXPERTS * RUN_ALIGN
DISPATCH_SLOTS = 3
assert N_TILES >= DISPATCH_SLOTS
TILE_BITS = 8
assert SORT_ROWS < RUN_ALIGN << TILE_BITS
N_RUNS = N_TILES * N_EXPERTS
N_EBLK = (N_TOK * TOP_K + N_RUNS * (RUN_ALIGN - 1) + N_EXPERTS * (EBLK - 1) + EBLK - 1) // EBLK
XS_ROWS = N_EBLK * EBLK
MOD_NT = 1536


def _dot(a, b):
    return jnp.dot(a, b, preferred_element_type=F32)


def _rms(x):
    return x * lax.rsqrt(jnp.mean(x * x, axis=-1, keepdims=True) + RMS_EPS)


def _mod_row(i):
    return jnp.where(i < P_TILES, 0, 1 + (i - P_TILES) // S_TILES_PER_SEQ)


def _rope_block(i):
    p_tiles = N_P // PROJ_TM
    per_seq = DEC_SEQ // PROJ_TM
    return jnp.where(i < p_tiles, per_seq, (i - p_tiles) % per_seq)


def _mod_kernel(c_ref, w_ref, b_ref, o_ref):
    c = c_ref[...]
    s = (c * jax.nn.sigmoid(c)).astype(BF16)
    o_ref[...] = _dot(s, w_ref[...].astype(BF16)) + b_ref[...]


def _mod_call(c_all, w_ada, b_ada):
    nt = (N_MOD * D_MODEL) // MOD_NT
    return pl.pallas_call(
        _mod_kernel,
        grid=(DEPTH, nt),
        in_specs=[
            pl.BlockSpec((MOD_ROWS, D_MODEL), lambda l, j: (0, 0)),
            pl.BlockSpec((None, D_MODEL, MOD_NT), lambda l, j: (l, 0, j)),
            pl.BlockSpec((None, 1, MOD_NT), lambda l, j: (l, 0, j)),
        ],
        out_specs=pl.BlockSpec((None, MOD_ROWS, MOD_NT), lambda l, j: (l, 0, j)),
        out_shape=jax.ShapeDtypeStruct((DEPTH, MOD_ROWS, N_MOD * D_MODEL), F32),
        name="adaln_mod",
    )(c_all, w_ada, b_ada.reshape(DEPTH, 1, N_MOD * D_MODEL))


def _prenorm(x, mod, gain):
    return (_rms(x) * gain) * (1.0 + mod[:, D_MODEL:2 * D_MODEL]) + mod[:, 0:D_MODEL]


def _prenorm_kernel(xp_ref, xs_ref, mod_ref, gpre_ref, x_ref, hb_ref):
    x = jnp.where(pl.program_id(0) < P_TILES, xp_ref[...], xs_ref[...])
    x_ref[...] = x
    hb_ref[...] = _prenorm(x, mod_ref[...], gpre_ref[...]).astype(BF16)


def _prenorm_call(x_prompt, x_sample, mods, gpre):
    row = lambda i: (i, 0)
    return pl.pallas_call(
        _prenorm_kernel,
        grid=(N_TILES,),
        in_specs=[
            pl.BlockSpec((TM, D_MODEL), lambda i: (jnp.minimum(i, P_TILES - 1), 0)),
            pl.BlockSpec((TM, D_MODEL), lambda i: (jnp.maximum(i - P_TILES, 0), 0)),
            pl.BlockSpec((None, None, 1, 2 * D_MODEL), lambda i: (0, _mod_row(i), 0, 0)),
            pl.BlockSpec((None, 1, D_MODEL), lambda i: (0, 0, 0)),
        ],
        out_specs=[pl.BlockSpec((TM, D_MODEL), row), pl.BlockSpec((TM, D_MODEL), row)],
        out_shape=[jax.ShapeDtypeStruct((N_TOK, D_MODEL), F32),
                   jax.ShapeDtypeStruct((N_TOK, D_MODEL), BF16)],
        name="prenorm",
    )(x_prompt, x_sample, mods, gpre)


def _proj_kernel(hb_ref, w_ref, qg_ref, kg_ref, cos_ref, s1_ref, s2_ref,
                 avg_ref, dft_ref, q_ref, k_ref, v_ref, xp_ref, xc_ref, xs_ref, g_ref):
    hb = hb_ref[...]

    cos = cos_ref[...]
    s1 = s1_ref[...]
    s2 = s2_ref[...]

    def rope(t):
        return (t * cos + pltpu.roll(t, LANES - ROPE_AXIS_DIM // 2, 1) * s1
                + pltpu.roll(t, ROPE_AXIS_DIM // 2, 1) * s2)

    avg = avg_ref[...]
    for c in range(ATTN_WIDTH // LANES):
        lo = c * LANES
        q = _dot(hb, w_ref[:, lo:lo + LANES])
        ms = _dot((q * q).astype(BF16), avg)
        q = q * lax.rsqrt(ms + RMS_EPS) * qg_ref[...]
        q_ref[:, lo:lo + LANES] = (rope(q) * QK_SCALE).astype(BF16)

    kv = _dot(hb, w_ref[:, OFF_K:OFF_K + 2 * KV_WIDTH])
    k = kv[:, 0:KV_WIDTH]
    ms = _dot((k * k).astype(BF16), avg)
    k = k * lax.rsqrt(ms + RMS_EPS) * kg_ref[...]
    k_ref[...] = rope(k)
    v_ref[...] = kv[:, KV_WIDTH:]

    pf = _dot(hb, w_ref[:, OFF_P:OFF_G])
    xp_ref[...] = pf[:, 0:POOL_WIDTH]
    cs = _dot(pf[:, POOL_WIDTH:].astype(BF16), dft_ref[...])
    xc_ref[...] = cs[:, 0:FOURIER_WIDTH].astype(BF16)
    xs_ref[...] = cs[:, FOURIER_WIDTH:].astype(BF16)

    for c in range(N_BRANCHES):
        lo = OFF_G + c * D_MODEL
        g_ref[:, c * D_MODEL:(c + 1) * D_MODEL] = _dot(hb, w_ref[:, lo:lo + D_MODEL]).astype(BF16)


def _proj_call(l, hb, w_in, qg, kg, cos_t, s1_t, s2_t, avg, dft):
    row = lambda i: (i, 0)
    const2 = lambda i: (0, 0)
    per_layer = lambda i: (l, 0, 0)
    rope_spec = pl.BlockSpec((PROJ_TM, LANES), lambda i: (_rope_block(i), 0))
    return pl.pallas_call(
        _proj_kernel,
        grid=(N_TOK // PROJ_TM,),
        in_specs=[
            pl.BlockSpec((PROJ_TM, D_MODEL), row),
            pl.BlockSpec((None, D_MODEL, IN_WIDTH), per_layer),
            pl.BlockSpec((None, 1, LANES), per_layer),
            pl.BlockSpec((None, 1, LANES), per_layer),
            rope_spec, rope_spec, rope_spec,
            pl.BlockSpec((LANES, LANES), const2),
            pl.BlockSpec((FOURIER_WIDTH, 2 * FOURIER_WIDTH), const2),
        ],
        out_specs=[
            pl.BlockSpec((PROJ_TM, ATTN_WIDTH), row),
            pl.BlockSpec((PROJ_TM, KV_WIDTH), row),
            pl.BlockSpec((PROJ_TM, KV_WIDTH), row),
            pl.BlockSpec((PROJ_TM, POOL_WIDTH), row),
            pl.BlockSpec((PROJ_TM, FOURIER_WIDTH), row),
            pl.BlockSpec((PROJ_TM, FOURIER_WIDTH), row),
            pl.BlockSpec((PROJ_TM, N_BRANCHES * D_MODEL), row),
        ],
        out_shape=[
            jax.ShapeDtypeStruct((N_TOK, ATTN_WIDTH), BF16),
            jax.ShapeDtypeStruct((N_TOK, KV_WIDTH), F32),
            jax.ShapeDtypeStruct((N_TOK, KV_WIDTH), F32),
            jax.ShapeDtypeStruct((N_TOK, POOL_WIDTH), F32),
            jax.ShapeDtypeStruct((N_TOK, FOURIER_WIDTH), BF16),
            jax.ShapeDtypeStruct((N_TOK, FOURIER_WIDTH), BF16),
            jax.ShapeDtypeStruct((N_TOK, N_BRANCHES * D_MODEL), BF16),
        ],
        name="proj",
    )(hb, w_in, qg, kg, cos_t, s1_t, s2_t, avg, dft)


def _attn_kernel(*refs, n_parts, tq):
    q_ref = refs[0]
    kv_refs = refs[1:1 + 2 * n_parts]
    o_ref = refs[-1]
    single_chunk = n_parts == 1 and kv_refs[0].shape[0] <= KEY_CHUNK
    outs = []
    for j in range(N_KV_HEADS):
        lo = j * HEAD_DIM
        qs = jnp.concatenate(
            [q_ref[:, (KV_GROUP * j + g) * HEAD_DIM:(KV_GROUP * j + g + 1) * HEAD_DIM]
             for g in range(KV_GROUP)], axis=0)
        m = acc = None
        for p in range(n_parts):
            k_ref, v_ref = kv_refs[2 * p], kv_refs[2 * p + 1]
            for c0 in range(0, k_ref.shape[0], KEY_CHUNK):
                c1 = min(c0 + KEY_CHUNK, k_ref.shape[0])
                kc = k_ref[c0:c1, lo:lo + HEAD_DIM].astype(BF16)
                s = lax.dot_general(qs, kc, (((1,), (1,)), ((), ())), preferred_element_type=F32)
                mc = jnp.max(s, axis=1, keepdims=True)
                m_new = mc if m is None else jnp.maximum(m, mc)
                e = jnp.exp2(s - m_new)
                vc = v_ref[c0:c1, lo:lo + HEAD_DIM].astype(BF16)
                if single_chunk:
                    den = jnp.sum(e, axis=1, keepdims=True)
                else:
                    vc = jnp.concatenate([vc, jnp.ones((c1 - c0, HEAD_DIM), BF16)], axis=1)
                pv = _dot(e.astype(BF16), vc)
                acc = pv if m is None else acc * jnp.exp2(m - m_new) + pv
                m = m_new
        o = acc / den if single_chunk else acc[:, 0:HEAD_DIM] / acc[:, HEAD_DIM:2 * HEAD_DIM]
        outs.extend(o[g * tq:(g + 1) * tq] for g in range(KV_GROUP))
    o_ref[...] = jnp.concatenate(outs, axis=1).astype(BF16)


def _attn_prompt_call(q, k, v):
    blk = lambda b: (b, 0)
    return pl.pallas_call(
        functools.partial(_attn_kernel, n_parts=1, tq=SEQ),
        grid=(BATCH,),
        in_specs=[
            pl.BlockSpec((SEQ, ATTN_WIDTH), blk),
            pl.BlockSpec((SEQ, KV_WIDTH), blk),
            pl.BlockSpec((SEQ, KV_WIDTH), blk),
        ],
        out_specs=pl.BlockSpec((SEQ, ATTN_WIDTH), blk),
        out_shape=jax.ShapeDtypeStruct((N_P, ATTN_WIDTH), BF16),
        name="attn_context",
    )(q, k, v)


def _attn_sample_call(l, q, k, v, cache_k, cache_v):
    nq = DEC_SEQ // TQ_S
    qrow = lambda b, i: (N_P // TQ_S + b * nq + i, 0)
    seq = lambda b, i: (N_P // DEC_SEQ + b, 0)
    cache = lambda b, i: (b, l, 0, 0)
    return pl.pallas_call(
        functools.partial(_attn_kernel, n_parts=2, tq=TQ_S),
        grid=(DEC_BATCH, nq),
        in_specs=[
            pl.BlockSpec((TQ_S, ATTN_WIDTH), qrow),
            pl.BlockSpec((None, None, PAST_LEN, KV_WIDTH), cache),
            pl.BlockSpec((None, None, PAST_LEN, KV_WIDTH), cache),
            pl.BlockSpec((DEC_SEQ, KV_WIDTH), seq),
            pl.BlockSpec((DEC_SEQ, KV_WIDTH), seq),
        ],
        out_specs=pl.BlockSpec((TQ_S, ATTN_WIDTH), lambda b, i: (b * nq + i, 0)),
        out_shape=jax.ShapeDtypeStruct((N_S, ATTN_WIDTH), BF16),
        name="attn_latent",
    )(q, cache_k, cache_v, k, v)


def _pool_kernel(xp_ref, bdw_ref, sc_ref, o_ref, pad_ref, *, seq_len, n_seq):
    for s in range(n_seq):
        _pool_sequence(xp_ref, bdw_ref, sc_ref, o_ref, pad_ref, s * seq_len, seq_len)


def _pool_sequence(xp_ref, bdw_ref, sc_ref, o_ref, pad_ref, row0, seq_len):
    half = POOL_WIDTH // 2
    zeros = jnp.zeros((POOL_PAD, POOL_WIDTH), F32)
    pad_ref[0:POOL_PAD, :] = zeros
    pad_ref[POOL_PAD + seq_len:, :] = zeros
    pad_ref[POOL_PAD:POOL_PAD + seq_len, :] = xp_ref[row0:row0 + seq_len, :]
    chunk = min(seq_len, 256)
    lane = lax.broadcasted_iota(jnp.int32, (chunk, half), 1)
    first = lane < POOL_GROUP_DIM
    for c in range(seq_len // chunk):
        base = c * chunk
        t = lax.broadcasted_iota(jnp.int32, (chunk, half), 0) + base

        def sh(j, lo):
            return pad_ref[POOL_PAD + base + j:POOL_PAD + base + j + chunk, lo:lo + half]

        def cnt(w):
            return (jnp.minimum(t + w // 2, seq_len) - jnp.maximum(t - w // 2, 0)).astype(F32)

        xa = sh(0, 0)
        w2 = sh(-1, 0) + xa
        w4 = w2 + sh(-2, 0) + sh(1, 0)
        xb = sh(0, half)
        w8 = xb
        for j in (-4, -3, -2, -1, 1, 2, 3):
            w8 = w8 + sh(j, half)
        w16 = w8
        for j in (-8, -7, -6, -5, 4, 5, 6, 7):
            w16 = w16 + sh(j, half)
        pa = jnp.where(first, w2 / cnt(2), w4 / cnt(4)) - xa
        pb = jnp.where(first, w8 / cnt(8), w16 / cnt(16)) - xb
        pooled = jnp.concatenate([pa, pb], axis=1).astype(BF16)
        o_ref[row0 + base:row0 + base + chunk, :] = (
            _dot(pooled, bdw_ref[...]) * sc_ref[...]).astype(BF16)


def _pool_call(l, xp, bdw, scale, seq_len, n_seq, blk0, seq_per_step):
    per_layer = lambda b: (l, 0, 0)
    rows = seq_len * seq_per_step
    return pl.pallas_call(
        functools.partial(_pool_kernel, seq_len=seq_len, n_seq=seq_per_step),
        grid=(n_seq // seq_per_step,),
        in_specs=[
            pl.BlockSpec((rows, POOL_WIDTH), lambda b: (blk0 + b, 0)),
            pl.BlockSpec((None, POOL_WIDTH, POOL_WIDTH), per_layer),
            pl.BlockSpec((None, 1, POOL_WIDTH), per_layer),
        ],
        out_specs=pl.BlockSpec((rows, POOL_WIDTH), lambda b: (b, 0)),
        out_shape=jax.ShapeDtypeStruct((n_seq * seq_len, POOL_WIDTH), BF16),
        scratch_shapes=[pltpu.VMEM((seq_len + 2 * POOL_PAD, POOL_WIDTH), F32)],
        name="pool_%d" % seq_len,
    )(xp, bdw, scale)


def _fourier_kernel(c_ref, s_ref, xc_ref, xs_ref, o_ref, *, scale, n_seq=1):
    seq_len = xc_ref.shape[0] // n_seq
    out_len = o_ref.shape[0] // n_seq
    for s in range(n_seq):
        rows = slice(s * seq_len, (s + 1) * seq_len)
        y = _dot(c_ref[...], xc_ref[rows, :]) - _dot(s_ref[...], xs_ref[rows, :])
        o_ref[s * out_len:(s + 1) * out_len, :] = (y * scale).astype(BF16)


def _fourier_prompt_call(cmat, smat, xc, xs):
    blk = lambda b: (b, 0)
    const2 = lambda b: (0, 0)
    rows = SEQ * CTX_SEQ_PER_STEP
    return pl.pallas_call(
        functools.partial(_fourier_kernel, scale=1.0 / math.sqrt(SEQ * FOURIER_GROUP_DIM),
                          n_seq=CTX_SEQ_PER_STEP),
        grid=(BATCH // CTX_SEQ_PER_STEP,),
        in_specs=[
            pl.BlockSpec((SEQ, SEQ), const2),
            pl.BlockSpec((SEQ, SEQ), const2),
            pl.BlockSpec((rows, FOURIER_WIDTH), blk),
            pl.BlockSpec((rows, FOURIER_WIDTH), blk),
        ],
        out_specs=pl.BlockSpec((rows, FOURIER_WIDTH), blk),
        out_shape=jax.ShapeDtypeStruct((N_P, FOURIER_WIDTH), BF16),
        name="fourier_context",
    )(cmat, smat, xc, xs)


def _fourier_sample_call(cmat, smat, xc, xs):
    nt = DEC_SEQ // TF_S
    rows = lambda b, i: (i, 0)
    seq = lambda b, i: (N_P // DEC_SEQ + b, 0)
    out = lambda b, i: (b * nt + i, 0)
    return pl.pallas_call(
        functools.partial(_fourier_kernel, scale=1.0 / math.sqrt(DEC_SEQ * FOURIER_GROUP_DIM)),
        grid=(DEC_BATCH, nt),
        in_specs=[
            pl.BlockSpec((TF_S, DEC_SEQ), rows),
            pl.BlockSpec((TF_S, DEC_SEQ), rows),
            pl.BlockSpec((DEC_SEQ, FOURIER_WIDTH), seq),
            pl.BlockSpec((DEC_SEQ, FOURIER_WIDTH), seq),
        ],
        out_specs=pl.BlockSpec((TF_S, FOURIER_WIDTH), out),
        out_shape=jax.ShapeDtypeStruct((N_S, FOURIER_WIDTH), BF16),
        name="fourier_latent",
    )(cmat, smat, xc, xs)


def _merge_kernel(x_ref, attn_p_ref, attn_s_ref, pool_p_ref, pool_s_ref, four_p_ref, four_s_ref,
                  g_ref, mod_ref, gpost_ref, gffn_ref,
                  wa_ref, wp_ref, wf_ref, wo_ref, wr_ref, br_ref, tri_ref, upper_ref,
                  x1_ref, h2_ref, pos_ref, post_ref, mw_ref, cnt_ref):
    i = pl.program_id(0)
    mod = mod_ref[...]
    g1 = mod[:, 2 * D_MODEL:3 * D_MODEL]
    sh2 = mod[:, 3 * D_MODEL:4 * D_MODEL]
    sc2 = mod[:, 4 * D_MODEL:5 * D_MODEL]

    def gate(c):
        return 0.5 * jnp.tanh(0.5 * g_ref[:, c * D_MODEL:(c + 1) * D_MODEL].astype(F32)) + 0.5

    def branch(p_ref, s_ref):
        return jnp.where(i < P_TILES, p_ref[...], s_ref[...])

    merged = gate(0) * _dot(branch(attn_p_ref, attn_s_ref), wa_ref[...])
    merged = merged + gate(1) * _dot(branch(pool_p_ref, pool_s_ref), wp_ref[...])
    merged = merged + gate(2) * _dot(branch(four_p_ref, four_s_ref), wf_ref[...])
    mix = _dot(merged.astype(BF16), wo_ref[...])
    x1 = x_ref[...] + g1 * (_rms(mix) * gpost_ref[...])
    x1_ref[...] = x1
    h2 = (_rms(x1) * gffn_ref[...]) * (1.0 + sc2) + sh2
    h2b = h2.astype(BF16)
    h2_ref[...] = h2b

    logits = _dot(h2b, wr_ref[...]) + br_ref[...]
    lane = lax.broadcasted_iota(jnp.int32, (TM, LANES), 1)
    lanef = lane.astype(F32)
    neg = jnp.float32(-3e38)
    big = jnp.float32(1e9)
    is_g = (lane >= N_EXPERTS) & (lane < N_EXPERTS + N_EXPERT_GROUPS)
    lg = jnp.where(is_g, logits, neg)
    gmax = jnp.max(lg, axis=1, keepdims=True)
    g_sel = jnp.min(jnp.where(lg == gmax, lanef - N_EXPERTS, big), axis=1, keepdims=True)
    p_g = 1.0 / jnp.sum(jnp.where(is_g, jnp.exp(logits - gmax), 0.0), axis=1, keepdims=True)
    grp = lax.shift_right_logical(lane, int(math.log2(EXPERTS_PER_GROUP))).astype(F32)
    in_grp = (lane < N_EXPERTS) & (grp == g_sel)
    le = jnp.where(in_grp, logits, neg)
    v1 = jnp.max(le, axis=1, keepdims=True)
    i1 = jnp.min(jnp.where(le == v1, lanef, big), axis=1, keepdims=True)
    le2 = jnp.where(lanef == i1, neg, le)
    v2 = jnp.max(le2, axis=1, keepdims=True)
    i2 = jnp.min(jnp.where(le2 == v2, lanef, big), axis=1, keepdims=True)
    e21 = jnp.exp(v2 - v1)
    w1 = p_g / (1.0 + e21)
    w2 = p_g * e21 / (1.0 + e21)

    oh1 = (lanef == i1).astype(F32)
    oh2 = (lanef == i2).astype(F32)
    ohb = (oh1 + oh2).astype(BF16)
    before = _dot(tri_ref[...], ohb)
    cnt = _dot(jnp.ones((MOD_ROWS, TM), BF16), ohb)
    cnt_pad = (lax.shift_right_logical(cnt.astype(jnp.int32) + (RUN_ALIGN - 1), RUN_SHIFT)
               * RUN_ALIGN).astype(F32)
    run_off = _dot(cnt_pad.astype(BF16), upper_ref[...])
    slot = run_off[0:1, :] + before
    p1 = jnp.sum(slot * oh1, axis=1, keepdims=True)
    p2 = jnp.sum(slot * oh2, axis=1, keepdims=True)
    cnt_ref[...] = cnt

    pos = jnp.where(lane == 0, p1, jnp.where(lane == 1, p2, 0.0))
    pos_ref[...] = pos.astype(jnp.int32)
    post_ref[...] = pos.T[0:MOD_ROWS, :].astype(jnp.int32)
    mw_ref[...] = jnp.where(lane == 0, w1, w2)


def _merge_call(l, x, attn, pool, four, gates, mods, gpost, gffn, wa, wp, wf, wo, wr, br, tri,
                upper):
    row = lambda i: (i, 0)
    prow = lambda i: (jnp.minimum(i, P_TILES - 1), 0)
    srow = lambda i: (jnp.maximum(i - P_TILES, 0), 0)
    const2 = lambda i: (0, 0)
    per_layer = lambda i: (l, 0, 0)
    return pl.pallas_call(
        _merge_kernel,
        grid=(N_TILES,),
        in_specs=[
            pl.BlockSpec((TM, D_MODEL), row),
            pl.BlockSpec((TM, ATTN_WIDTH), prow),
            pl.BlockSpec((TM, ATTN_WIDTH), srow),
            pl.BlockSpec((TM, POOL_WIDTH), prow),
            pl.BlockSpec((TM, POOL_WIDTH), srow),
            pl.BlockSpec((TM, FOURIER_WIDTH), prow),
            pl.BlockSpec((TM, FOURIER_WIDTH), srow),
            pl.BlockSpec((TM, N_BRANCHES * D_MODEL), row),
            pl.BlockSpec((None, None, 1, N_MOD * D_MODEL), lambda i: (l, _mod_row(i), 0, 0)),
            pl.BlockSpec((None, 1, D_MODEL), per_layer),
            pl.BlockSpec((None, 1, D_MODEL), per_layer),
            pl.BlockSpec((None, ATTN_WIDTH, D_MODEL), per_layer),
            pl.BlockSpec((None, POOL_WIDTH, D_MODEL), per_layer),
            pl.BlockSpec((None, FOURIER_WIDTH, D_MODEL), per_layer),
            pl.BlockSpec((None, D_MODEL, D_MODEL), per_layer),
            pl.BlockSpec((None, D_MODEL, LANES), per_layer),
            pl.BlockSpec((None, 1, LANES), per_layer),
            pl.BlockSpec((TM, TM), const2),
            pl.BlockSpec((LANES, LANES), const2),
        ],
        out_specs=[
            pl.BlockSpec((TM, D_MODEL), row),
            pl.BlockSpec((TM, D_MODEL), row),
            pl.BlockSpec((TM, LANES), row),
            pl.BlockSpec((MOD_ROWS, TM), lambda i: (0, i)),
            pl.BlockSpec((TM, LANES), row),
            pl.BlockSpec((MOD_ROWS, LANES), row),
        ],
        out_shape=[
            jax.ShapeDtypeStruct((N_TOK, D_MODEL), F32),
            jax.ShapeDtypeStruct((N_TOK, D_MODEL), BF16),
            jax.ShapeDtypeStruct((N_TOK, LANES), jnp.int32),
            jax.ShapeDtypeStruct((MOD_ROWS, N_TOK), jnp.int32),
            jax.ShapeDtypeStruct((N_TOK, LANES), F32),
            jax.ShapeDtypeStruct((N_TILES * MOD_ROWS, LANES), F32),
        ],
        name="merge_router",
    )(x, attn[0], attn[1], pool[0], pool[1], four[0], four[1], gates, mods, gpost, gffn,
      wa, wp, wf, wo, wr, br, tri, upper)


HALF = D_MODEL // 2
U32 = jnp.uint32
HI_MASK = 0xFFFF0000


def _pack_rows(x):
    lo = lax.bitcast_convert_type(x[:, :HALF], U32)
    hi = lax.bitcast_convert_type(x[:, HALF:], U32)
    return lax.shift_right_logical(lo, U32(16)) | (hi & U32(HI_MASK))


def _unpack_rows(u):
    lo = lax.bitcast_convert_type(lax.shift_left(u, U32(16)), F32)
    hi = lax.bitcast_convert_type(u & U32(HI_MASK), F32)
    return jnp.concatenate([lo.astype(BF16), hi.astype(BF16)], axis=1)


def _run_copies(cnt, src_ref, src_off, dst_ref, dst_off, sem, wait, bits=RUN_BITS):
    off = jnp.int32(0)
    for b in reversed(range(bits)):
        size = RUN_ALIGN << b
        take = lax.shift_right_logical(cnt, b) & 1

        @pl.when(take == 1)
        def _(off=off, size=size):
            cp = pltpu.make_async_copy(
                src_ref.at[pl.ds(pl.multiple_of(src_off + off, RUN_ALIGN), size)],
                dst_ref.at[pl.ds(pl.multiple_of(dst_off + off, RUN_ALIGN), size)], sem)
            if wait:
                cp.wait()
            else:
                cp.start()

        off = off + take * size


def _tile_runs(tile, cnt_ref, fn):
    def body(e, carry):
        r = tile * N_EXPERTS + e
        fn(r, cnt_ref[r])
        return carry

    lax.fori_loop(0, N_EXPERTS, body, 0)


def _dispatch_kernel(cnt_ref, toff_ref, xoff_ref, tsum_ref, h_ref, post_ref, xs_in_ref, xs_ref,
                     sorted_ref, sem):
    del xs_in_ref
    i = pl.program_id(0)
    slot = lax.rem(i, DISPATCH_SLOTS)
    rows = lax.broadcasted_iota(jnp.int32, (SORT_ROWS, TM), 0)
    p = post_ref[...]
    perm = jnp.where(rows == p[0:1, :], 1.0, jnp.where(rows == p[1:2, :], 1.0, 0.0)).astype(BF16)
    sorted_ref[slot] = _pack_rows(_dot(perm, h_ref[...]))

    def wait(tile, slot):
        _run_copies(tsum_ref[tile], sorted_ref.at[slot], 0, xs_ref, 0, sem.at[slot], True, TILE_BITS)

    lag = DISPATCH_SLOTS - 1

    @pl.when(i >= lag)
    def _():
        wait(i - lag, lax.rem(i + 1, DISPATCH_SLOTS))

    _tile_runs(i, cnt_ref, lambda r, c: _run_copies(
        c, sorted_ref.at[slot], toff_ref[r], xs_ref, xoff_ref[r], sem.at[slot], False))

    @pl.when(i == pl.num_programs(0) - 1)
    def _():
        for back in reversed(range(lag)):
            wait(i - back, lax.rem(i - back, DISPATCH_SLOTS))


def _dispatch_call(cnt, toff, xoff, tsum, h2, post, xs):
    return pl.pallas_call(
        _dispatch_kernel,
        grid_spec=pltpu.PrefetchScalarGridSpec(
            num_scalar_prefetch=4,
            grid=(N_TILES,),
            in_specs=[
                pl.BlockSpec((TM, D_MODEL), lambda i, *_: (i, 0)),
                pl.BlockSpec((MOD_ROWS, TM), lambda i, *_: (0, i)),
                pl.BlockSpec(memory_space=pl.ANY),
            ],
            out_specs=pl.BlockSpec(memory_space=pl.ANY),
            scratch_shapes=[pltpu.VMEM((DISPATCH_SLOTS, SORT_ROWS, HALF), U32),
                            pltpu.SemaphoreType.DMA((DISPATCH_SLOTS,))],
        ),
        out_shape=jax.ShapeDtypeStruct((XS_ROWS, HALF), U32),
        input_output_aliases={6: 0},
        compiler_params=pltpu.CompilerParams(dimension_semantics=("arbitrary",)),
        name="moe_dispatch",
    )(cnt, toff, xoff, tsum, h2, post, xs)


def _expert_kernel(bstart_ref, bcnt_ref, nu_ref, wg_ref, wu_ref, wd_ref, xs_ref, y_prev_ref, y_ref,
                   wgb, wub, wdb, xbuf, ybuf, xsem, ysem):
    del y_prev_ref
    e = pl.program_id(0)
    n_used = nu_ref[0]

    def x_copy(g, slot):
        rows = pl.ds(pl.multiple_of(g * EBLK, EBLK), EBLK)
        return pltpu.make_async_copy(xs_ref.at[rows], xbuf.at[slot], xsem.at[slot])

    def y_copy(g, slot):
        rows = pl.ds(pl.multiple_of(g * EBLK, EBLK), EBLK)
        return pltpu.make_async_copy(ybuf.at[slot], y_ref.at[rows], ysem.at[slot])

    @pl.when(e == 0)
    def _():
        x_copy(0, 0).start()

    wgb[...] = wg_ref[...].astype(BF16)
    wub[...] = wu_ref[...].astype(BF16)
    wdb[...] = wd_ref[...].astype(BF16)
    first = bstart_ref[e]

    def block(j, carry):
        g = first + j
        slot = lax.rem(g, 2)
        x_copy(g, slot).wait()

        @pl.when(g + 1 < n_used)
        def _():
            x_copy(g + 1, 1 - slot).start()

        xb = _unpack_rows(xbuf[slot])
        gate = _dot(xb, wgb[...])
        up = _dot(xb, wub[...])
        act = (gate * jax.nn.sigmoid(gate)) * up
        y = _dot(act.astype(BF16), wdb[...])

        @pl.when(g >= 2)
        def _():
            y_copy(g - 2, slot).wait()

        ybuf[slot] = _pack_rows(y.astype(BF16).astype(F32))
        y_copy(g, slot).start()
        return carry

    lax.fori_loop(0, bcnt_ref[e], block, 0)

    @pl.when(e == pl.num_programs(0) - 1)
    def _():
        @pl.when(n_used >= 2)
        def _():
            y_copy(n_used - 2, lax.rem(n_used, 2)).wait()

        y_copy(n_used - 1, lax.rem(n_used - 1, 2)).wait()


def _expert_call(l, blk_start, blk_cnt, n_used, xs, y_prev, w_gate, w_up, w_down):
    wsel = lambda e, *_: (l, e, 0, 0)
    return pl.pallas_call(
        _expert_kernel,
        grid_spec=pltpu.PrefetchScalarGridSpec(
            num_scalar_prefetch=3,
            grid=(N_EXPERTS,),
            in_specs=[
                pl.BlockSpec((None, None, D_MODEL, EXPERT_FF), wsel),
                pl.BlockSpec((None, None, D_MODEL, EXPERT_FF), wsel),
                pl.BlockSpec((None, None, EXPERT_FF, D_MODEL), wsel),
                pl.BlockSpec(memory_space=pl.ANY),
                pl.BlockSpec(memory_space=pl.ANY),
            ],
            out_specs=pl.BlockSpec(memory_space=pl.ANY),
            scratch_shapes=[
                pltpu.VMEM((D_MODEL, EXPERT_FF), BF16),
                pltpu.VMEM((D_MODEL, EXPERT_FF), BF16),
                pltpu.VMEM((EXPERT_FF, D_MODEL), BF16),
                pltpu.VMEM((2, EBLK, HALF), U32),
                pltpu.VMEM((2, EBLK, HALF), U32),
                pltpu.SemaphoreType.DMA((2,)),
                pltpu.SemaphoreType.DMA((2,)),
            ],
        ),
        out_shape=jax.ShapeDtypeStruct((XS_ROWS, HALF), U32),
        input_output_aliases={7: 0},
        compiler_params=pltpu.CompilerParams(dimension_semantics=("arbitrary",)),
        name="moe_experts",
    )(blk_start, blk_cnt, n_used, w_gate, w_up, w_down, xs, y_prev)


def _combine_kernel(cnt_ref, toff_ref, xoff_ref, tsum_ref, y_ref, x1_ref, pos_ref, mw_ref, g2_ref,
                    gpost_ref, *rest, has_next):
    if has_next:
        nmod_ref, ngpre_ref, o_ref, hb_ref, ybuf, sem = rest
    else:
        o_ref, os_ref, ybuf, sem = rest
    i = pl.program_id(0)
    slot = lax.rem(i, 2)

    def start(tile, slot):
        _tile_runs(tile, cnt_ref, lambda r, c: _run_copies(
            c, y_ref, xoff_ref[r], ybuf.at[slot], toff_ref[r], sem.at[slot], False))

    @pl.when(i == 0)
    def _():
        ybuf[...] = jnp.zeros_like(ybuf)
        start(0, 0)

    @pl.when(i + 1 < pl.num_programs(0))
    def _():
        start(i + 1, 1 - slot)

    _run_copies(tsum_ref[i], y_ref, 0, ybuf.at[slot], 0, sem.at[slot], True, TILE_BITS)

    pos = pos_ref[...]
    mw = mw_ref[...]
    cols = lax.broadcasted_iota(jnp.int32, (TM, SORT_ROWS), 1)
    qw = (jnp.where(cols == pos[:, 0:1], mw[:, 0:1], 0.0)
          + jnp.where(cols == pos[:, 1:2], mw[:, 1:2], 0.0)).astype(BF16)
    ffn = _dot(qw, _unpack_rows(ybuf[slot]))
    x2 = x1_ref[...] + g2_ref[...] * (_rms(ffn) * gpost_ref[...])
    if has_next:
        o_ref[...] = x2
        hb_ref[...] = _prenorm(x2, nmod_ref[...], ngpre_ref[...]).astype(BF16)
    else:
        @pl.when(i < P_TILES)
        def _():
            o_ref[...] = x2

        os_ref[...] = x2


def _combine_call(l, cnt, toff, xoff, tsum, y, x1, pos, mw, mods, gpost, gpre):
    has_next = l + 1 < DEPTH
    row = lambda i, *_: (i, 0)
    in_specs = [
        pl.BlockSpec(memory_space=pl.ANY),
        pl.BlockSpec((TM, D_MODEL), row),
        pl.BlockSpec((TM, LANES), row),
        pl.BlockSpec((TM, LANES), row),
        pl.BlockSpec((None, None, 1, D_MODEL), lambda i, *_: (l, _mod_row(i), 0, N_MOD - 1)),
        pl.BlockSpec((None, 1, D_MODEL), lambda i, *_: (l, 0, 0)),
    ]
    args = [cnt, toff, xoff, tsum, y, x1, pos, mw, mods, gpost]
    if has_next:
        in_specs += [
            pl.BlockSpec((None, None, 1, 2 * D_MODEL), lambda i, *_: (l + 1, _mod_row(i), 0, 0)),
            pl.BlockSpec((None, 1, D_MODEL), lambda i, *_: (l + 1, 0, 0)),
        ]
        args += [mods, gpre]
        out_specs = [pl.BlockSpec((TM, D_MODEL), row), pl.BlockSpec((TM, D_MODEL), row)]
        out_shape = [jax.ShapeDtypeStruct((N_TOK, D_MODEL), F32),
                     jax.ShapeDtypeStruct((N_TOK, D_MODEL), BF16)]
    else:
        out_specs = [
            pl.BlockSpec((TM, D_MODEL), lambda i, *_: (jnp.minimum(i, P_TILES - 1), 0)),
            pl.BlockSpec((TM, D_MODEL), lambda i, *_: (jnp.maximum(i - P_TILES, 0), 0)),
        ]
        out_shape = [jax.ShapeDtypeStruct((N_P, D_MODEL), F32),
                     jax.ShapeDtypeStruct((N_S, D_MODEL), F32)]
    return pl.pallas_call(
        functools.partial(_combine_kernel, has_next=has_next),
        grid_spec=pltpu.PrefetchScalarGridSpec(
            num_scalar_prefetch=4,
            grid=(N_TILES,),
            in_specs=in_specs,
            out_specs=out_specs,
            scratch_shapes=[pltpu.VMEM((2, SORT_ROWS, HALF), U32), pltpu.SemaphoreType.DMA((2,))],
        ),
        out_shape=out_shape,
        compiler_params=pltpu.CompilerParams(dimension_semantics=("arbitrary",)),
        name="moe_combine",
    )(*args)


def _dft_mats(n):
    k = np.arange(n, dtype=np.int64)
    ang = 2.0 * np.pi * ((k[:, None] * k[None, :]) % n).astype(np.float64) / n
    return np.cos(ang), np.sin(ang)


def _block_diag(m, reps):
    n = m.shape[0]
    out = np.zeros((n * reps, n * reps), m.dtype)
    for r in range(reps):
        out[r * n:(r + 1) * n, r * n:(r + 1) * n] = m
    return out


def _rope_tables():
    t = np.arange(DEC_SEQ)
    pos = np.stack([t // GRID_W, t % GRID_W], axis=1).astype(np.float64)
    n_freq = ROPE_AXIS_DIM // 2
    inv = ROPE_BASE ** (-np.arange(n_freq, dtype=np.float64) * 2.0 / ROPE_AXIS_DIM)
    ang = pos[:, :, None] * inv[None, None, :]
    cos = np.cos(ang)
    sin = np.sin(ang)
    zero = np.zeros_like(sin[:, 0])
    cos_h = np.concatenate([cos[:, 0], cos[:, 0], cos[:, 1], cos[:, 1]], axis=1)
    s1_h = np.concatenate([-sin[:, 0], zero, -sin[:, 1], zero], axis=1)
    s2_h = np.concatenate([zero, sin[:, 0], zero, sin[:, 1]], axis=1)
    reps = LANES // HEAD_DIM

    def table(a, ident):
        a = np.tile(a, (1, reps))
        pad = np.full((PROJ_TM, LANES), ident, np.float64)
        return jnp.asarray(np.concatenate([a, pad], axis=0), F32)

    return table(cos_h, 1.0), table(s1_h, 0.0), table(s2_h, 0.0)


def kernel(x_prompt, x_sample, cache_k, cache_v, c, c_ctx, w_ada, b_ada, norm_mix_pre,
           norm_mix_post, norm_ffn_pre, norm_ffn_post, w_in, q_norm, k_norm, w_attn_out,
           w_pool_group, pool_scale, w_pool_out, w_fourier_out, w_out, w_router_group,
           b_router_group, w_router_expert, b_router_expert, w_expert_gate, w_expert_up,
           w_expert_down):
    cos_t, s1_t, s2_t = _rope_tables()
    avg = jnp.asarray(_block_diag(np.full((HEAD_DIM, HEAD_DIM), 1.0 / HEAD_DIM), LANES // HEAD_DIM), BF16)
    c64, s64 = _dft_mats(FOURIER_GROUP_DIM)
    n_fg = FOURIER_WIDTH // FOURIER_GROUP_DIM
    dft_ch = jnp.asarray(np.concatenate([_block_diag(c64, n_fg), _block_diag(s64, n_fg)], axis=1), BF16)
    cp, sp = _dft_mats(SEQ)
    cp, sp = jnp.asarray(cp, BF16), jnp.asarray(sp, BF16)
    cl, sl = _dft_mats(DEC_SEQ)
    cl, sl = jnp.asarray(cl, BF16), jnp.asarray(sl, BF16)
    tri = jnp.asarray(np.tril(np.ones((TM, TM)), -1), BF16)
    upper = jnp.asarray(np.triu(np.ones((LANES, LANES)), 1), BF16)

    w_in_b = w_in.astype(BF16)
    wa_b = w_attn_out.astype(BF16)
    wp_b = w_pool_out.astype(BF16)
    wf_b = w_fourier_out.astype(BF16)
    wo_b = w_out.astype(BF16)
    pad_r = jnp.zeros((DEPTH, D_MODEL, LANES - N_EXPERTS - N_EXPERT_GROUPS), F32)
    wr_b = jnp.concatenate([w_router_expert, w_router_group, pad_r], axis=2).astype(BF16)
    br = jnp.concatenate([b_router_expert, b_router_group,
                          jnp.zeros((DEPTH, LANES - N_EXPERTS - N_EXPERT_GROUPS), F32)],
                         axis=1).reshape(DEPTH, 1, LANES)
    n_pg = POOL_WIDTH // POOL_GROUP_DIM
    bdw = jnp.zeros((DEPTH, POOL_WIDTH, POOL_WIDTH), F32)
    for g in range(n_pg):
        lo = g * POOL_GROUP_DIM
        bdw = bdw.at[:, lo:lo + POOL_GROUP_DIM, lo:lo + POOL_GROUP_DIM].set(w_pool_group[:, g])
    bdw = bdw.astype(BF16)
    pscale = pool_scale.reshape(DEPTH, 1, POOL_WIDTH)
    qg = jnp.tile(q_norm, (1, LANES // HEAD_DIM)).reshape(DEPTH, 1, LANES)
    kg = jnp.tile(k_norm, (1, LANES // HEAD_DIM)).reshape(DEPTH, 1, LANES)
    gpre = norm_mix_pre.reshape(DEPTH, 1, D_MODEL)
    gpost = norm_mix_post.reshape(DEPTH, 1, D_MODEL)
    gffn = norm_ffn_pre.reshape(DEPTH, 1, D_MODEL)
    gfpost = norm_ffn_post.reshape(DEPTH, 1, D_MODEL)
    ck = cache_k.reshape(DEC_BATCH, DEPTH, PAST_LEN, KV_WIDTH)
    cv = cache_v.reshape(DEC_BATCH, DEPTH, PAST_LEN, KV_WIDTH)

    c_all = jnp.concatenate([c_ctx[None, :], c, jnp.zeros((MOD_ROWS - 1 - DEC_BATCH, D_MODEL), F32)], axis=0)
    mods = _mod_call(c_all, w_ada, b_ada).reshape(DEPTH, MOD_ROWS, 1, N_MOD * D_MODEL)

    xs_buf = jnp.zeros((XS_ROWS, HALF), U32)
    y = jnp.zeros((XS_ROWS, HALF), U32)
    new_k, new_v = [], []
    x, hb = _prenorm_call(x_prompt.reshape(N_P, D_MODEL), x_sample.reshape(N_S, D_MODEL), mods, gpre)
    for l in range(DEPTH):
        q, k, v, xp, xc, xsn, gates = _proj_call(l, hb, w_in_b, qg, kg, cos_t, s1_t, s2_t, avg, dft_ch)
        new_k.append(k[:N_P].reshape(BATCH, SEQ, N_KV_HEADS, HEAD_DIM))
        new_v.append(v[:N_P].reshape(BATCH, SEQ, N_KV_HEADS, HEAD_DIM))
        attn = (_attn_prompt_call(q, k, v), _attn_sample_call(l, q, k, v, ck, cv))
        pool = (_pool_call(l, xp, bdw, pscale, SEQ, BATCH, 0, CTX_SEQ_PER_STEP),
                _pool_call(l, xp, bdw, pscale, DEC_SEQ, DEC_BATCH, N_P // DEC_SEQ, 1))
        four = (_fourier_prompt_call(cp, sp, xc, xsn), _fourier_sample_call(cl, sl, xc, xsn))
        x1, h2, pos, post, mw, cnt = _merge_call(l, x, attn, pool, four, gates, mods, gpost, gffn,
                                                 wa_b, wp_b, wf_b, wo_b, wr_b, br, tri, upper)
        runs = cnt.reshape(N_TILES, MOD_ROWS, LANES)[:, 0, :N_EXPERTS].astype(jnp.int32)
        runs = ((runs + RUN_ALIGN - 1) // RUN_ALIGN) * RUN_ALIGN
        tile_off = jnp.cumsum(runs, axis=1) - runs
        rows_e = jnp.sum(runs, axis=0)
        padded = ((rows_e + EBLK - 1) // EBLK) * EBLK
        pad_end = jnp.cumsum(padded)
        xs_off = (pad_end - padded)[None, :] + jnp.cumsum(runs, axis=0) - runs
        blk_cnt = padded // EBLK
        blk_start = (pad_end - padded) // EBLK
        n_used = pad_end[-1:] // EBLK
        run_cnt = (runs // RUN_ALIGN).reshape(N_RUNS)
        tile_off = tile_off.reshape(N_RUNS)
        xs_off = xs_off.reshape(N_RUNS)
        tile_cnt = jnp.sum(runs, axis=1) // RUN_ALIGN
        xs_buf = _dispatch_call(run_cnt, tile_off, xs_off, tile_cnt, h2, post, xs_buf)
        y = _expert_call(l, blk_start, blk_cnt, n_used, xs_buf, y,
                         w_expert_gate, w_expert_up, w_expert_down)
        outs = _combine_call(l, run_cnt, tile_off, xs_off, tile_cnt, y, x1, pos, mw, mods, gfpost,
                             gpre)
        x, hb = outs

    y_prompt = outs[0].reshape(BATCH, SEQ, D_MODEL)
    y_sample = outs[1].reshape(DEC_BATCH, DEC_SEQ, D_MODEL)
    return (y_prompt, y_sample, jnp.stack(new_k, axis=1), jnp.stack(new_v, axis=1))
```

```python
import functools
import math

import numpy as np
import jax
import jax.numpy as jnp
from jax import lax
from jax.experimental import pallas as pl
from jax.experimental.pallas import tpu as pltpu

F32 = jnp.float32
BF16 = jnp.bfloat16

D_MODEL = 1024
BATCH = 32
SEQ = 256
DEPTH = 4
DEC_BATCH = 2
DEC_SEQ = 2048
PAST_LEN = 512
GRID_W = 64
N_HEADS = 8
N_KV_HEADS = 2
HEAD_DIM = 64
KV_GROUP = N_HEADS // N_KV_HEADS
ATTN_WIDTH = N_HEADS * HEAD_DIM
KV_WIDTH = N_KV_HEADS * HEAD_DIM
ROPE_AXIS_DIM = HEAD_DIM // 2
ROPE_BASE = 10000.0
POOL_WINDOWS = (2, 4, 8, 16)
POOL_WIDTH = 256
POOL_GROUP_DIM = 64
FOURIER_WIDTH = 256
FOURIER_GROUP_DIM = 64
N_BRANCHES = 3
OFF_K = ATTN_WIDTH
OFF_P = ATTN_WIDTH + 2 * KV_WIDTH
OFF_G = OFF_P + POOL_WIDTH + FOURIER_WIDTH
IN_WIDTH = OFF_G + N_BRANCHES * D_MODEL
N_EXPERT_GROUPS = 4
EXPERTS_PER_GROUP = 8
N_EXPERTS = N_EXPERT_GROUPS * EXPERTS_PER_GROUP
TOP_K = 2
EXPERT_FF = 256
N_MOD = 6
RMS_EPS = 1e-6
QK_SCALE = HEAD_DIM ** -0.5 * math.log2(math.e)

N_P = BATCH * SEQ
N_S = DEC_BATCH * DEC_SEQ
N_TOK = N_P + N_S
LANES = 128
MOD_ROWS = 8
POOL_PAD = 16

TM = 512
PROJ_TM = 1024
N_TILES = N_TOK // TM
P_TILES = N_P // TM
S_TILES_PER_SEQ = DEC_SEQ // TM
CTX_SEQ_PER_STEP = 4
TQ_S = 128
KEY_CHUNK = 512
TF_S = 512
EBLK = 512
X_SLOTS = 4
RUN_ALIGN = 8
RUN_SHIFT = 3
RUN_BITS = 7
assert RUN_ALIGN << (RUN_BITS - 1) == TM
SORT_ROWS = TOP_K * TM + N_EXPERTS * RUN_ALIGN
DISPATCH_SLOTS = 3
assert N_TILES >= DISPATCH_SLOTS
TILE_BITS = 8
assert SORT_ROWS < RUN_ALIGN << TILE_BITS
N_RUNS = N_TILES * N_EXPERTS
N_EBLK = (N_TOK * TOP_K + N_RUNS * (RUN_ALIGN - 1) + N_EXPERTS * (EBLK - 1) + EBLK - 1) // EBLK
XS_ROWS = N_EBLK * EBLK
MOD_NT = 1536


def _dot(a, b):
    return jnp.dot(a, b, preferred_element_type=F32)


def _rms(x):
    return x * lax.rsqrt(jnp.mean(x * x, axis=-1, keepdims=True) + RMS_EPS)


def _mod_row(i):
    return jnp.where(i < P_TILES, 0, 1 + (i - P_TILES) // S_TILES_PER_SEQ)


def _rope_block(i):
    p_tiles = N_P // PROJ_TM
    per_seq = DEC_SEQ // PROJ_TM
    return jnp.where(i < p_tiles, per_seq, (i - p_tiles) % per_seq)


def _mod_kernel(c_ref, w_ref, b_ref, o_ref):
    c = c_ref[...]
    s = (c * jax.nn.sigmoid(c)).astype(BF16)
    o_ref[...] = _dot(s, w_ref[...].astype(BF16)) + b_ref[...]


def _mod_call(c_all, w_ada, b_ada):
    nt = (N_MOD * D_MODEL) // MOD_NT
    return pl.pallas_call(
        _mod_kernel,
        grid=(DEPTH, nt),
        in_specs=[
            pl.BlockSpec((MOD_ROWS, D_MODEL), lambda l, j: (0, 0)),
            pl.BlockSpec((None, D_MODEL, MOD_NT), lambda l, j: (l, 0, j)),
            pl.BlockSpec((None, 1, MOD_NT), lambda l, j: (l, 0, j)),
        ],
        out_specs=pl.BlockSpec((None, MOD_ROWS, MOD_NT), lambda l, j: (l, 0, j)),
        out_shape=jax.ShapeDtypeStruct((DEPTH, MOD_ROWS, N_MOD * D_MODEL), F32),
        name="adaln_mod",
    )(c_all, w_ada, b_ada.reshape(DEPTH, 1, N_MOD * D_MODEL))


def _prenorm(x, mod, gain):
    return (_rms(x) * gain) * (1.0 + mod[:, D_MODEL:2 * D_MODEL]) + mod[:, 0:D_MODEL]


def _prenorm_kernel(xp_ref, xs_ref, mod_ref, gpre_ref, x_ref, hb_ref):
    x = jnp.where(pl.program_id(0) < P_TILES, xp_ref[...], xs_ref[...])
    x_ref[...] = x
    hb_ref[...] = _prenorm(x, mod_ref[...], gpre_ref[...]).astype(BF16)


def _prenorm_call(x_prompt, x_sample, mods, gpre):
    row = lambda i: (i, 0)
    return pl.pallas_call(
        _prenorm_kernel,
        grid=(N_TILES,),
        in_specs=[
            pl.BlockSpec((TM, D_MODEL), lambda i: (jnp.minimum(i, P_TILES - 1), 0)),
            pl.BlockSpec((TM, D_MODEL), lambda i: (jnp.maximum(i - P_TILES, 0), 0)),
            pl.BlockSpec((None, None, 1, 2 * D_MODEL), lambda i: (0, _mod_row(i), 0, 0)),
            pl.BlockSpec((None, 1, D_MODEL), lambda i: (0, 0, 0)),
        ],
        out_specs=[pl.BlockSpec((TM, D_MODEL), row), pl.BlockSpec((TM, D_MODEL), row)],
        out_shape=[jax.ShapeDtypeStruct((N_TOK, D_MODEL), F32),
                   jax.ShapeDtypeStruct((N_TOK, D_MODEL), BF16)],
        name="prenorm",
    )(x_prompt, x_sample, mods, gpre)


def _proj_kernel(hb_ref, w_ref, qg_ref, kg_ref, cos_ref, s1_ref, s2_ref,
                 avg_ref, dft_ref, q_ref, k_ref, v_ref, xp_ref, xc_ref, xs_ref, g_ref):
    hb = hb_ref[...]

    cos = cos_ref[...]
    s1 = s1_ref[...]
    s2 = s2_ref[...]

    def rope(t):
        return (t * cos + pltpu.roll(t, LANES - ROPE_AXIS_DIM // 2, 1) * s1
                + pltpu.roll(t, ROPE_AXIS_DIM // 2, 1) * s2)

    avg = avg_ref[...]
    for c in range(ATTN_WIDTH // LANES):
        lo = c * LANES
        q = _dot(hb, w_ref[:, lo:lo + LANES])
        ms = _dot((q * q).astype(BF16), avg)
        q = q * lax.rsqrt(ms + RMS_EPS) * qg_ref[...]
        q_ref[:, lo:lo + LANES] = (rope(q) * QK_SCALE).astype(BF16)

    kv = _dot(hb, w_ref[:, OFF_K:OFF_K + 2 * KV_WIDTH])
    k = kv[:, 0:KV_WIDTH]
    ms = _dot((k * k).astype(BF16), avg)
    k = k * lax.rsqrt(ms + RMS_EPS) * kg_ref[...]
    k_ref[...] = rope(k)
    v_ref[...] = kv[:, KV_WIDTH:]

    pf = _dot(hb, w_ref[:, OFF_P:OFF_G])
    xp_ref[...] = pf[:, 0:POOL_WIDTH]
    cs = _dot(pf[:, POOL_WIDTH:].astype(BF16), dft_ref[...])
    xc_ref[...] = cs[:, 0:FOURIER_WIDTH].astype(BF16)
    xs_ref[...] = cs[:, FOURIER_WIDTH:].astype(BF16)

    for c in range(N_BRANCHES):
        lo = OFF_G + c * D_MODEL
        g_ref[:, c * D_MODEL:(c + 1) * D_MODEL] = _dot(hb, w_ref[:, lo:lo + D_MODEL]).astype(BF16)


def _proj_call(l, hb, w_in, qg, kg, cos_t, s1_t, s2_t, avg, dft):
    row = lambda i: (i, 0)
    const2 = lambda i: (0, 0)
    per_layer = lambda i: (l, 0, 0)
    rope_spec = pl.BlockSpec((PROJ_TM, LANES), lambda i: (_rope_block(i), 0))
    return pl.pallas_call(
        _proj_kernel,
        grid=(N_TOK // PROJ_TM,),
        in_specs=[
            pl.BlockSpec((PROJ_TM, D_MODEL), row),
            pl.BlockSpec((None, D_MODEL, IN_WIDTH), per_layer),
            pl.BlockSpec((None, 1, LANES), per_layer),
            pl.BlockSpec((None, 1, LANES), per_layer),
            rope_spec, rope_spec, rope_spec,
            pl.BlockSpec((LANES, LANES), const2),
            pl.BlockSpec((FOURIER_WIDTH, 2 * FOURIER_WIDTH), const2),
        ],
        out_specs=[
            pl.BlockSpec((PROJ_TM, ATTN_WIDTH), row),
            pl.BlockSpec((PROJ_TM, KV_WIDTH), row),
            pl.BlockSpec((PROJ_TM, KV_WIDTH), row),
            pl.BlockSpec((PROJ_TM, POOL_WIDTH), row),
            pl.BlockSpec((PROJ_TM, FOURIER_WIDTH), row),
            pl.BlockSpec((PROJ_TM, FOURIER_WIDTH), row),
            pl.BlockSpec((PROJ_TM, N_BRANCHES * D_MODEL), row),
        ],
        out_shape=[
            jax.ShapeDtypeStruct((N_TOK, ATTN_WIDTH), BF16),
            jax.ShapeDtypeStruct((N_TOK, KV_WIDTH), F32),
            jax.ShapeDtypeStruct((N_TOK, KV_WIDTH), F32),
            jax.ShapeDtypeStruct((N_TOK, POOL_WIDTH), F32),
            jax.ShapeDtypeStruct((N_TOK, FOURIER_WIDTH), BF16),
            jax.ShapeDtypeStruct((N_TOK, FOURIER_WIDTH), BF16),
            jax.ShapeDtypeStruct((N_TOK, N_BRANCHES * D_MODEL), BF16),
        ],
        name="proj",
    )(hb, w_in, qg, kg, cos_t, s1_t, s2_t, avg, dft)


def _attn_kernel(*refs, n_parts, tq):
    q_ref = refs[0]
    kv_refs = refs[1:1 + 2 * n_parts]
    o_ref = refs[-1]
    single_chunk = n_parts == 1 and kv_refs[0].shape[0] <= KEY_CHUNK
    outs = []
    for j in range(N_KV_HEADS):
        lo = j * HEAD_DIM
        qs = jnp.concatenate(
            [q_ref[:, (KV_GROUP * j + g) * HEAD_DIM:(KV_GROUP * j + g + 1) * HEAD_DIM]
             for g in range(KV_GROUP)], axis=0)
        m = acc = None
        for p in range(n_parts):
            k_ref, v_ref = kv_refs[2 * p], kv_refs[2 * p + 1]
            for c0 in range(0, k_ref.shape[0], KEY_CHUNK):
                c1 = min(c0 + KEY_CHUNK, k_ref.shape[0])
                kc = k_ref[c0:c1, lo:lo + HEAD_DIM].astype(BF16)
                s = lax.dot_general(qs, kc, (((1,), (1,)), ((), ())), preferred_element_type=F32)
                mc = jnp.max(s, axis=1, keepdims=True)
                m_new = mc if m is None else jnp.maximum(m, mc)
                e = jnp.exp2(s - m_new)
                vc = v_ref[c0:c1, lo:lo + HEAD_DIM].astype(BF16)
                if single_chunk:
                    den = jnp.sum(e, axis=1, keepdims=True)
                else:
                    vc = jnp.concatenate([vc, jnp.ones((c1 - c0, HEAD_DIM), BF16)], axis=1)
                pv = _dot(e.astype(BF16), vc)
                acc = pv if m is None else acc * jnp.exp2(m - m_new) + pv
                m = m_new
        o = acc / den if single_chunk else acc[:, 0:HEAD_DIM] / acc[:, HEAD_DIM:2 * HEAD_DIM]
        outs.extend(o[g * tq:(g + 1) * tq] for g in range(KV_GROUP))
    o_ref[...] = jnp.concatenate(outs, axis=1).astype(BF16)


def _attn_prompt_call(q, k, v):
    blk = lambda b: (b, 0)
    return pl.pallas_call(
        functools.partial(_attn_kernel, n_parts=1, tq=SEQ),
        grid=(BATCH,),
        in_specs=[
            pl.BlockSpec((SEQ, ATTN_WIDTH), blk),
            pl.BlockSpec((SEQ, KV_WIDTH), blk),
            pl.BlockSpec((SEQ, KV_WIDTH), blk),
        ],
        out_specs=pl.BlockSpec((SEQ, ATTN_WIDTH), blk),
        out_shape=jax.ShapeDtypeStruct((N_P, ATTN_WIDTH), BF16),
        name="attn_context",
    )(q, k, v)


def _attn_sample_call(l, q, k, v, cache_k, cache_v):
    nq = DEC_SEQ // TQ_S
    qrow = lambda b, i: (N_P // TQ_S + b * nq + i, 0)
    seq = lambda b, i: (N_P // DEC_SEQ + b, 0)
    cache = lambda b, i: (b, l, 0, 0)
    return pl.pallas_call(
        functools.partial(_attn_kernel, n_parts=2, tq=TQ_S),
        grid=(DEC_BATCH, nq),
        in_specs=[
            pl.BlockSpec((TQ_S, ATTN_WIDTH), qrow),
            pl.BlockSpec((None, None, PAST_LEN, KV_WIDTH), cache),
            pl.BlockSpec((None, None, PAST_LEN, KV_WIDTH), cache),
            pl.BlockSpec((DEC_SEQ, KV_WIDTH), seq),
            pl.BlockSpec((DEC_SEQ, KV_WIDTH), seq),
        ],
        out_specs=pl.BlockSpec((TQ_S, ATTN_WIDTH), lambda b, i: (b * nq + i, 0)),
        out_shape=jax.ShapeDtypeStruct((N_S, ATTN_WIDTH), BF16),
        name="attn_latent",
    )(q, cache_k, cache_v, k, v)


def _pool_kernel(xp_ref, bdw_ref, sc_ref, o_ref, pad_ref, *, seq_len, n_seq):
    for s in range(n_seq):
        _pool_sequence(xp_ref, bdw_ref, sc_ref, o_ref, pad_ref, s * seq_len, seq_len)


def _pool_sequence(xp_ref, bdw_ref, sc_ref, o_ref, pad_ref, row0, seq_len):
    half = POOL_WIDTH // 2
    zeros = jnp.zeros((POOL_PAD, POOL_WIDTH), F32)
    pad_ref[0:POOL_PAD, :] = zeros
    pad_ref[POOL_PAD + seq_len:, :] = zeros
    pad_ref[POOL_PAD:POOL_PAD + seq_len, :] = xp_ref[row0:row0 + seq_len, :]
    chunk = min(seq_len, 256)
    lane = lax.broadcasted_iota(jnp.int32, (chunk, half), 1)
    first = lane < POOL_GROUP_DIM
    for c in range(seq_len // chunk):
        base = c * chunk
        t = lax.broadcasted_iota(jnp.int32, (chunk, half), 0) + base

        def sh(j, lo):
            return pad_ref[POOL_PAD + base + j:POOL_PAD + base + j + chunk, lo:lo + half]

        def cnt(w):
            return (jnp.minimum(t + w // 2, seq_len) - jnp.maximum(t - w // 2, 0)).astype(F32)

        xa = sh(0, 0)
        w2 = sh(-1, 0) + xa
        w4 = w2 + sh(-2, 0) + sh(1, 0)
        xb = sh(0, half)
        w8 = xb
        for j in (-4, -3, -2, -1, 1, 2, 3):
            w8 = w8 + sh(j, half)
        w16 = w8
        for j in (-8, -7, -6, -5, 4, 5, 6, 7):
            w16 = w16 + sh(j, half)
        pa = jnp.where(first, w2 / cnt(2), w4 / cnt(4)) - xa
        pb = jnp.where(first, w8 / cnt(8), w16 / cnt(16)) - xb
        pooled = jnp.concatenate([pa, pb], axis=1).astype(BF16)
        o_ref[row0 + base:row0 + base + chunk, :] = (
            _dot(pooled, bdw_ref[...]) * sc_ref[...]).astype(BF16)


def _pool_call(l, xp, bdw, scale, seq_len, n_seq, blk0, seq_per_step):
    per_layer = lambda b: (l, 0, 0)
    rows = seq_len * seq_per_step
    return pl.pallas_call(
        functools.partial(_pool_kernel, seq_len=seq_len, n_seq=seq_per_step),
        grid=(n_seq // seq_per_step,),
        in_specs=[
            pl.BlockSpec((rows, POOL_WIDTH), lambda b: (blk0 + b, 0)),
            pl.BlockSpec((None, POOL_WIDTH, POOL_WIDTH), per_layer),
            pl.BlockSpec((None, 1, POOL_WIDTH), per_layer),
        ],
        out_specs=pl.BlockSpec((rows, POOL_WIDTH), lambda b: (b, 0)),
        out_shape=jax.ShapeDtypeStruct((n_seq * seq_len, POOL_WIDTH), BF16),
        scratch_shapes=[pltpu.VMEM((seq_len + 2 * POOL_PAD, POOL_WIDTH), F32)],
        name="pool_%d" % seq_len,
    )(xp, bdw, scale)


def _fourier_kernel(c_ref, s_ref, xc_ref, xs_ref, o_ref, *, scale, n_seq=1):
    seq_len = xc_ref.shape[0] // n_seq
    out_len = o_ref.shape[0] // n_seq
    for s in range(n_seq):
        rows = slice(s * seq_len, (s + 1) * seq_len)
        y = _dot(c_ref[...], xc_ref[rows, :]) - _dot(s_ref[...], xs_ref[rows, :])
        o_ref[s * out_len:(s + 1) * out_len, :] = (y * scale).astype(BF16)


def _fourier_prompt_call(cmat, smat, xc, xs):
    blk = lambda b: (b, 0)
    const2 = lambda b: (0, 0)
    rows = SEQ * CTX_SEQ_PER_STEP
    return pl.pallas_call(
        functools.partial(_fourier_kernel, scale=1.0 / math.sqrt(SEQ * FOURIER_GROUP_DIM),
                          n_seq=CTX_SEQ_PER_STEP),
        grid=(BATCH // CTX_SEQ_PER_STEP,),
        in_specs=[
            pl.BlockSpec((SEQ, SEQ), const2),
            pl.BlockSpec((SEQ, SEQ), const2),
            pl.BlockSpec((rows, FOURIER_WIDTH), blk),
            pl.BlockSpec((rows, FOURIER_WIDTH), blk),
        ],
        out_specs=pl.BlockSpec((rows, FOURIER_WIDTH), blk),
        out_shape=jax.ShapeDtypeStruct((N_P, FOURIER_WIDTH), BF16),
        name="fourier_context",
    )(cmat, smat, xc, xs)


def _fourier_sample_call(cmat, smat, xc, xs):
    nt = DEC_SEQ // TF_S
    rows = lambda b, i: (i, 0)
    seq = lambda b, i: (N_P // DEC_SEQ + b, 0)
    out = lambda b, i: (b * nt + i, 0)
    return pl.pallas_call(
        functools.partial(_fourier_kernel, scale=1.0 / math.sqrt(DEC_SEQ * FOURIER_GROUP_DIM)),
        grid=(DEC_BATCH, nt),
        in_specs=[
            pl.BlockSpec((TF_S, DEC_SEQ), rows),
            pl.BlockSpec((TF_S, DEC_SEQ), rows),
            pl.BlockSpec((DEC_SEQ, FOURIER_WIDTH), seq),
            pl.BlockSpec((DEC_SEQ, FOURIER_WIDTH), seq),
        ],
        out_specs=pl.BlockSpec((TF_S, FOURIER_WIDTH), out),
        out_shape=jax.ShapeDtypeStruct((N_S, FOURIER_WIDTH), BF16),
        name="fourier_latent",
    )(cmat, smat, xc, xs)


def _merge_kernel(x_ref, attn_p_ref, attn_s_ref, pool_p_ref, pool_s_ref, four_p_ref, four_s_ref,
                  g_ref, mod_ref, gpost_ref, gffn_ref,
                  wa_ref, wp_ref, wf_ref, wo_ref, wr_ref, br_ref, tri_ref, upper_ref,
                  x1_ref, h2_ref, pos_ref, post_ref, mw_ref, cnt_ref):
    i = pl.program_id(0)
    mod = mod_ref[...]
    g1 = mod[:, 2 * D_MODEL:3 * D_MODEL]
    sh2 = mod[:, 3 * D_MODEL:4 * D_MODEL]
    sc2 = mod[:, 4 * D_MODEL:5 * D_MODEL]

    def gate(c):
        return 0.5 * jnp.tanh(0.5 * g_ref[:, c * D_MODEL:(c + 1) * D_MODEL].astype(F32)) + 0.5

    def branch(p_ref, s_ref):
        return jnp.where(i < P_TILES, p_ref[...], s_ref[...])

    merged = gate(0) * _dot(branch(attn_p_ref, attn_s_ref), wa_ref[...])
    merged = merged + gate(1) * _dot(branch(pool_p_ref, pool_s_ref), wp_ref[...])
    merged = merged + gate(2) * _dot(branch(four_p_ref, four_s_ref), wf_ref[...])
    mix = _dot(merged.astype(BF16), wo_ref[...])
    x1 = x_ref[...] + g1 * (_rms(mix) * gpost_ref[...])
    x1_ref[...] = x1
    h2 = (_rms(x1) * gffn_ref[...]) * (1.0 + sc2) + sh2
    h2b = h2.astype(BF16)
    h2_ref[...] = h2b

    logits = _dot(h2b, wr_ref[...]) + br_ref[...]
    lane = lax.broadcasted_iota(jnp.int32, (TM, LANES), 1)
    lanef = lane.astype(F32)
    neg = jnp.float32(-3e38)
    big = jnp.float32(1e9)
    is_g = (lane >= N_EXPERTS) & (lane < N_EXPERTS + N_EXPERT_GROUPS)
    lg = jnp.where(is_g, logits, neg)
    gmax = jnp.max(lg, axis=1, keepdims=True)
    g_sel = jnp.min(jnp.where(lg == gmax, lanef - N_EXPERTS, big), axis=1, keepdims=True)
    p_g = 1.0 / jnp.sum(jnp.where(is_g, jnp.exp(logits - gmax), 0.0), axis=1, keepdims=True)
    grp = lax.shift_right_logical(lane, int(math.log2(EXPERTS_PER_GROUP))).astype(F32)
    in_grp = (lane < N_EXPERTS) & (grp == g_sel)
    le = jnp.where(in_grp, logits, neg)
    v1 = jnp.max(le, axis=1, keepdims=True)
    i1 = jnp.min(jnp.where(le == v1, lanef, big), axis=1, keepdims=True)
    le2 = jnp.where(lanef == i1, neg, le)
    v2 = jnp.max(le2, axis=1, keepdims=True)
    i2 = jnp.min(jnp.where(le2 == v2, lanef, big), axis=1, keepdims=True)
    e21 = jnp.exp(v2 - v1)
    w1 = p_g / (1.0 + e21)
    w2 = p_g * e21 / (1.0 + e21)

    oh1 = (lanef == i1).astype(F32)
    oh2 = (lanef == i2).astype(F32)
    ohb = (oh1 + oh2).astype(BF16)
    before = _dot(tri_ref[...], ohb)
    cnt = _dot(jnp.ones((MOD_ROWS, TM), BF16), ohb)
    cnt_pad = (lax.shift_right_logical(cnt.astype(jnp.int32) + (RUN_ALIGN - 1), RUN_SHIFT)
               * RUN_ALIGN).astype(F32)
    run_off = _dot(cnt_pad.astype(BF16), upper_ref[...])
    slot = run_off[0:1, :] + before
    p1 = jnp.sum(slot * oh1, axis=1, keepdims=True)
    p2 = jnp.sum(slot * oh2, axis=1, keepdims=True)
    cnt_ref[...] = cnt

    pos = jnp.where(lane == 0, p1, jnp.where(lane == 1, p2, 0.0))
    pos_ref[...] = pos.astype(jnp.int32)
    post_ref[...] = pos.T[0:MOD_ROWS, :].astype(jnp.int32)
    mw_ref[...] = jnp.where(lane == 0, w1, w2)


def _merge_call(l, x, attn, pool, four, gates, mods, gpost, gffn, wa, wp, wf, wo, wr, br, tri,
                upper):
    row = lambda i: (i, 0)
    prow = lambda i: (jnp.minimum(i, P_TILES - 1), 0)
    srow = lambda i: (jnp.maximum(i - P_TILES, 0), 0)
    const2 = lambda i: (0, 0)
    per_layer = lambda i: (l, 0, 0)
    return pl.pallas_call(
        _merge_kernel,
        grid=(N_TILES,),
        in_specs=[
            pl.BlockSpec((TM, D_MODEL), row),
            pl.BlockSpec((TM, ATTN_WIDTH), prow),
            pl.BlockSpec((TM, ATTN_WIDTH), srow),
            pl.BlockSpec((TM, POOL_WIDTH), prow),
            pl.BlockSpec((TM, POOL_WIDTH), srow),
            pl.BlockSpec((TM, FOURIER_WIDTH), prow),
            pl.BlockSpec((TM, FOURIER_WIDTH), srow),
            pl.BlockSpec((TM, N_BRANCHES * D_MODEL), row),
            pl.BlockSpec((None, None, 1, N_MOD * D_MODEL), lambda i: (l, _mod_row(i), 0, 0)),
            pl.BlockSpec((None, 1, D_MODEL), per_layer),
            pl.BlockSpec((None, 1, D_MODEL), per_layer),
            pl.BlockSpec((None, ATTN_WIDTH, D_MODEL), per_layer),
            pl.BlockSpec((None, POOL_WIDTH, D_MODEL), per_layer),
            pl.BlockSpec((None, FOURIER_WIDTH, D_MODEL), per_layer),
            pl.BlockSpec((None, D_MODEL, D_MODEL), per_layer),
            pl.BlockSpec((None, D_MODEL, LANES), per_layer),
            pl.BlockSpec((None, 1, LANES), per_layer),
            pl.BlockSpec((TM, TM), const2),
            pl.BlockSpec((LANES, LANES), const2),
        ],
        out_specs=[
            pl.BlockSpec((TM, D_MODEL), row),
            pl.BlockSpec((TM, D_MODEL), row),
            pl.BlockSpec((TM, LANES), row),
            pl.BlockSpec((MOD_ROWS, TM), lambda i: (0, i)),
            pl.BlockSpec((TM, LANES), row),
            pl.BlockSpec((MOD_ROWS, LANES), row),
        ],
        out_shape=[
            jax.ShapeDtypeStruct((N_TOK, D_MODEL), F32),
            jax.ShapeDtypeStruct((N_TOK, D_MODEL), BF16),
            jax.ShapeDtypeStruct((N_TOK, LANES), jnp.int32),
            jax.ShapeDtypeStruct((MOD_ROWS, N_TOK), jnp.int32),
            jax.ShapeDtypeStruct((N_TOK, LANES), F32),
            jax.ShapeDtypeStruct((N_TILES * MOD_ROWS, LANES), F32),
        ],
        name="merge_router",
    )(x, attn[0], attn[1], pool[0], pool[1], four[0], four[1], gates, mods, gpost, gffn,
      wa, wp, wf, wo, wr, br, tri, upper)


HALF = D_MODEL // 2
U32 = jnp.uint32
HI_MASK = 0xFFFF0000


def _pack_rows(x):
    lo = lax.bitcast_convert_type(x[:, :HALF], U32)
    hi = lax.bitcast_convert_type(x[:, HALF:], U32)
    return lax.shift_right_logical(lo, U32(16)) | (hi & U32(HI_MASK))


def _unpack_rows(u):
    lo = lax.bitcast_convert_type(lax.shift_left(u, U32(16)), F32)
    hi = lax.bitcast_convert_type(u & U32(HI_MASK), F32)
    return jnp.concatenate([lo.astype(BF16), hi.astype(BF16)], axis=1)


def _run_copies(cnt, src_ref, src_off, dst_ref, dst_off, sem, wait, bits=RUN_BITS):
    off = jnp.int32(0)
    for b in reversed(range(bits)):
        size = RUN_ALIGN << b
        take = lax.shift_right_logical(cnt, b) & 1

        @pl.when(take == 1)
        def _(off=off, size=size):
            cp = pltpu.make_async_copy(
                src_ref.at[pl.ds(pl.multiple_of(src_off + off, RUN_ALIGN), size)],
                dst_ref.at[pl.ds(pl.multiple_of(dst_off + off, RUN_ALIGN), size)], sem)
            if wait:
                cp.wait()
            else:
                cp.start()

        off = off + take * size


def _tile_runs(tile, cnt_ref, fn):
    def body(e, carry):
        r = tile * N_EXPERTS + e
        fn(r, cnt_ref[r])
        return carry

    lax.fori_loop(0, N_EXPERTS, body, 0)


def _dispatch_kernel(cnt_ref, toff_ref, xoff_ref, tsum_ref, h_ref, post_ref, xs_in_ref, xs_ref,
                     sorted_ref, sem):
    del xs_in_ref
    i = pl.program_id(0)
    slot = lax.rem(i, DISPATCH_SLOTS)
    rows = lax.broadcasted_iota(jnp.int32, (SORT_ROWS, TM), 0)
    p = post_ref[...]
    perm = jnp.where(rows == p[0:1, :], 1.0, jnp.where(rows == p[1:2, :], 1.0, 0.0)).astype(BF16)
    sorted_ref[slot] = _pack_rows(_dot(perm, h_ref[...]))

    def wait(tile, slot):
        _run_copies(tsum_ref[tile], sorted_ref.at[slot], 0, xs_ref, 0, sem.at[slot], True, TILE_BITS)

    lag = DISPATCH_SLOTS - 1

    @pl.when(i >= lag)
    def _():
        wait(i - lag, lax.rem(i + 1, DISPATCH_SLOTS))

    _tile_runs(i, cnt_ref, lambda r, c: _run_copies(
        c, sorted_ref.at[slot], toff_ref[r], xs_ref, xoff_ref[r], sem.at[slot], False))

    @pl.when(i == pl.num_programs(0) - 1)
    def _():
        for back in reversed(range(lag)):
            wait(i - back, lax.rem(i - back, DISPATCH_SLOTS))


def _dispatch_call(cnt, toff, xoff, tsum, h2, post, xs):
    return pl.pallas_call(
        _dispatch_kernel,
        grid_spec=pltpu.PrefetchScalarGridSpec(
            num_scalar_prefetch=4,
            grid=(N_TILES,),
            in_specs=[
                pl.BlockSpec((TM, D_MODEL), lambda i, *_: (i, 0)),
                pl.BlockSpec((MOD_ROWS, TM), lambda i, *_: (0, i)),
                pl.BlockSpec(memory_space=pl.ANY),
            ],
            out_specs=pl.BlockSpec(memory_space=pl.ANY),
            scratch_shapes=[pltpu.VMEM((DISPATCH_SLOTS, SORT_ROWS, HALF), U32),
                            pltpu.SemaphoreType.DMA((DISPATCH_SLOTS,))],
        ),
        out_shape=jax.ShapeDtypeStruct((XS_ROWS, HALF), U32),
        input_output_aliases={6: 0},
        compiler_params=pltpu.CompilerParams(dimension_semantics=("arbitrary",)),
        name="moe_dispatch",
    )(cnt, toff, xoff, tsum, h2, post, xs)


def _expert_kernel(bstart_ref, bcnt_ref, nu_ref, wg_ref, wu_ref, wd_ref, xs_ref, y_prev_ref, y_ref,
                   wgb, wub, wdb, xbuf, ybuf, xsem, ysem):
    del y_prev_ref
    e = pl.program_id(0)
    n_used = nu_ref[0]

    def x_copy(g, slot):
        rows = pl.ds(pl.multiple_of(g * EBLK, EBLK), EBLK)
        return pltpu.make_async_copy(xs_ref.at[rows], xbuf.at[slot], xsem.at[slot])

    def y_copy(g, slot):
        rows = pl.ds(pl.multiple_of(g * EBLK, EBLK), EBLK)
        return pltpu.make_async_copy(ybuf.at[slot], y_ref.at[rows], ysem.at[slot])

    @pl.when(e == 0)
    def _():
        for g in range(X_SLOTS - 1):
            @pl.when(g < n_used)
            def _(g=g):
                x_copy(g, g).start()

    wgb[...] = wg_ref[...].astype(BF16)
    wub[...] = wu_ref[...].astype(BF16)
    wdb[...] = wd_ref[...].astype(BF16)
    first = bstart_ref[e]

    def block(j, carry):
        g = first + j
        slot = lax.rem(g, 2)
        xslot = lax.rem(g, X_SLOTS)
        x_copy(g, xslot).wait()

        @pl.when(g + (X_SLOTS - 1) < n_used)
        def _():
            x_copy(g + (X_SLOTS - 1), lax.rem(g + (X_SLOTS - 1), X_SLOTS)).start()

        xb = _unpack_rows(xbuf[xslot])
        gate = _dot(xb, wgb[...])
        up = _dot(xb, wub[...])
        act = (gate * jax.nn.sigmoid(gate)) * up
        y = _dot(act.astype(BF16), wdb[...])

        @pl.when(g >= 2)
        def _():
            y_copy(g - 2, slot).wait()

        ybuf[slot] = _pack_rows(y.astype(BF16).astype(F32))
        y_copy(g, slot).start()
        return carry

    lax.fori_loop(0, bcnt_ref[e], block, 0)

    @pl.when(e == pl.num_programs(0) - 1)
    def _():
        @pl.when(n_used >= 2)
        def _():
            y_copy(n_used - 2, lax.rem(n_used, 2)).wait()

        y_copy(n_used - 1, lax.rem(n_used - 1, 2)).wait()


def _expert_call(l, blk_start, blk_cnt, n_used, xs, y_prev, w_gate, w_up, w_down):
    wsel = lambda e, *_: (l, e, 0, 0)
    return pl.pallas_call(
        _expert_kernel,
        grid_spec=pltpu.PrefetchScalarGridSpec(
            num_scalar_prefetch=3,
            grid=(N_EXPERTS,),
            in_specs=[
                pl.BlockSpec((None, None, D_MODEL, EXPERT_FF), wsel),
                pl.BlockSpec((None, None, D_MODEL, EXPERT_FF), wsel),
                pl.BlockSpec((None, None, EXPERT_FF, D_MODEL), wsel),
                pl.BlockSpec(memory_space=pl.ANY),
                pl.BlockSpec(memory_space=pl.ANY),
            ],
            out_specs=pl.BlockSpec(memory_space=pl.ANY),
            scratch_shapes=[
                pltpu.VMEM((D_MODEL, EXPERT_FF), BF16),
                pltpu.VMEM((D_MODEL, EXPERT_FF), BF16),
                pltpu.VMEM((EXPERT_FF, D_MODEL), BF16),
                pltpu.VMEM((X_SLOTS, EBLK, HALF), U32),
                pltpu.VMEM((2, EBLK, HALF), U32),
                pltpu.SemaphoreType.DMA((X_SLOTS,)),
                pltpu.SemaphoreType.DMA((2,)),
            ],
        ),
        out_shape=jax.ShapeDtypeStruct((XS_ROWS, HALF), U32),
        input_output_aliases={7: 0},
        compiler_params=pltpu.CompilerParams(dimension_semantics=("arbitrary",)),
        name="moe_experts",
    )(blk_start, blk_cnt, n_used, w_gate, w_up, w_down, xs, y_prev)


def _combine_kernel(cnt_ref, toff_ref, xoff_ref, tsum_ref, y_ref, x1_ref, pos_ref, mw_ref, g2_ref,
                    gpost_ref, *rest, has_next):
    if has_next:
        nmod_ref, ngpre_ref, o_ref, hb_ref, ybuf, sem = rest
    else:
        o_ref, os_ref, ybuf, sem = rest
    i = pl.program_id(0)
    slot = lax.rem(i, 2)

    def start(tile, slot):
        _tile_runs(tile, cnt_ref, lambda r, c: _run_copies(
            c, y_ref, xoff_ref[r], ybuf.at[slot], toff_ref[r], sem.at[slot], False))

    @pl.when(i == 0)
    def _():
        ybuf[...] = jnp.zeros_like(ybuf)
        start(0, 0)

    @pl.when(i + 1 < pl.num_programs(0))
    def _():
        start(i + 1, 1 - slot)

    _run_copies(tsum_ref[i], y_ref, 0, ybuf.at[slot], 0, sem.at[slot], True, TILE_BITS)

    pos = pos_ref[...]
    mw = mw_ref[...]
    cols = lax.broadcasted_iota(jnp.int32, (TM, SORT_ROWS), 1)
    qw = (jnp.where(cols == pos[:, 0:1], mw[:, 0:1], 0.0)
          + jnp.where(cols == pos[:, 1:2], mw[:, 1:2], 0.0)).astype(BF16)
    ffn = _dot(qw, _unpack_rows(ybuf[slot]))
    x2 = x1_ref[...] + g2_ref[...] * (_rms(ffn) * gpost_ref[...])
    if has_next:
        o_ref[...] = x2
        hb_ref[...] = _prenorm(x2, nmod_ref[...], ngpre_ref[...]).astype(BF16)
    else:
        @pl.when(i < P_TILES)
        def _():
            o_ref[...] = x2

        os_ref[...] = x2


def _combine_call(l, cnt, toff, xoff, tsum, y, x1, pos, mw, mods, gpost, gpre):
    has_next = l + 1 < DEPTH
    row = lambda i, *_: (i, 0)
    in_specs = [
        pl.BlockSpec(memory_space=pl.ANY),
        pl.BlockSpec((TM, D_MODEL), row),
        pl.BlockSpec((TM, LANES), row),
        pl.BlockSpec((TM, LANES), row),
        pl.BlockSpec((None, None, 1, D_MODEL), lambda i, *_: (l, _mod_row(i), 0, N_MOD - 1)),
        pl.BlockSpec((None, 1, D_MODEL), lambda i, *_: (l, 0, 0)),
    ]
    args = [cnt, toff, xoff, tsum, y, x1, pos, mw, mods, gpost]
    if has_next:
        in_specs += [
            pl.BlockSpec((None, None, 1, 2 * D_MODEL), lambda i, *_: (l + 1, _mod_row(i), 0, 0)),
            pl.BlockSpec((None, 1, D_MODEL), lambda i, *_: (l + 1, 0, 0)),
        ]
        args += [mods, gpre]
        out_specs = [pl.BlockSpec((TM, D_MODEL), row), pl.BlockSpec((TM, D_MODEL), row)]
        out_shape = [jax.ShapeDtypeStruct((N_TOK, D_MODEL), F32),
                     jax.ShapeDtypeStruct((N_TOK, D_MODEL), BF16)]
    else:
        out_specs = [
            pl.BlockSpec((TM, D_MODEL), lambda i, *_: (jnp.minimum(i, P_TILES - 1), 0)),
            pl.BlockSpec((TM, D_MODEL), lambda i, *_: (jnp.maximum(i - P_TILES, 0), 0)),
        ]
        out_shape = [jax.ShapeDtypeStruct((N_P, D_MODEL), F32),
                     jax.ShapeDtypeStruct((N_S, D_MODEL), F32)]
    return pl.pallas_call(
        functools.partial(_combine_kernel, has_next=has_next),
        grid_spec=pltpu.PrefetchScalarGridSpec(
            num_scalar_prefetch=4,
            grid=(N_TILES,),
            in_specs=in_specs,
            out_specs=out_specs,
            scratch_shapes=[pltpu.VMEM((2, SORT_ROWS, HALF), U32), pltpu.SemaphoreType.DMA((2,))],
        ),
        out_shape=out_shape,
        compiler_params=pltpu.CompilerParams(dimension_semantics=("arbitrary",)),
        name="moe_combine",
    )(*args)


def _dft_mats(n):
    k = np.arange(n, dtype=np.int64)
    ang = 2.0 * np.pi * ((k[:, None] * k[None, :]) % n).astype(np.float64) / n
    return np.cos(ang), np.sin(ang)


def _block_diag(m, reps):
    n = m.shape[0]
    out = np.zeros((n * reps, n * reps), m.dtype)
    for r in range(reps):
        out[r * n:(r + 1) * n, r * n:(r + 1) * n] = m
    return out


def _rope_tables():
    t = np.arange(DEC_SEQ)
    pos = np.stack([t // GRID_W, t % GRID_W], axis=1).astype(np.float64)
    n_freq = ROPE_AXIS_DIM // 2
    inv = ROPE_BASE ** (-np.arange(n_freq, dtype=np.float64) * 2.0 / ROPE_AXIS_DIM)
    ang = pos[:, :, None] * inv[None, None, :]
    cos = np.cos(ang)
    sin = np.sin(ang)
    zero = np.zeros_like(sin[:, 0])
    cos_h = np.concatenate([cos[:, 0], cos[:, 0], cos[:, 1], cos[:, 1]], axis=1)
    s1_h = np.concatenate([-sin[:, 0], zero, -sin[:, 1], zero], axis=1)
    s2_h = np.concatenate([zero, sin[:, 0], zero, sin[:, 1]], axis=1)
    reps = LANES // HEAD_DIM

    def table(a, ident):
        a = np.tile(a, (1, reps))
        pad = np.full((PROJ_TM, LANES), ident, np.float64)
        return jnp.asarray(np.concatenate([a, pad], axis=0), F32)

    return table(cos_h, 1.0), table(s1_h, 0.0), table(s2_h, 0.0)


def kernel(x_prompt, x_sample, cache_k, cache_v, c, c_ctx, w_ada, b_ada, norm_mix_pre,
           norm_mix_post, norm_ffn_pre, norm_ffn_post, w_in, q_norm, k_norm, w_attn_out,
           w_pool_group, pool_scale, w_pool_out, w_fourier_out, w_out, w_router_group,
           b_router_group, w_router_expert, b_router_expert, w_expert_gate, w_expert_up,
           w_expert_down):
    cos_t, s1_t, s2_t = _rope_tables()
    avg = jnp.asarray(_block_diag(np.full((HEAD_DIM, HEAD_DIM), 1.0 / HEAD_DIM), LANES // HEAD_DIM), BF16)
    c64, s64 = _dft_mats(FOURIER_GROUP_DIM)
    n_fg = FOURIER_WIDTH // FOURIER_GROUP_DIM
    dft_ch = jnp.asarray(np.concatenate([_block_diag(c64, n_fg), _block_diag(s64, n_fg)], axis=1), BF16)
    cp, sp = _dft_mats(SEQ)
    cp, sp = jnp.asarray(cp, BF16), jnp.asarray(sp, BF16)
    cl, sl = _dft_mats(DEC_SEQ)
    cl, sl = jnp.asarray(cl, BF16), jnp.asarray(sl, BF16)
    tri = jnp.asarray(np.tril(np.ones((TM, TM)), -1), BF16)
    upper = jnp.asarray(np.triu(np.ones((LANES, LANES)), 1), BF16)

    w_in_b = w_in.astype(BF16)
    wa_b = w_attn_out.astype(BF16)
    wp_b = w_pool_out.astype(BF16)
    wf_b = w_fourier_out.astype(BF16)
    wo_b = w_out.astype(BF16)
    pad_r = jnp.zeros((DEPTH, D_MODEL, LANES - N_EXPERTS - N_EXPERT_GROUPS), F32)
    wr_b = jnp.concatenate([w_router_expert, w_router_group, pad_r], axis=2).astype(BF16)
    br = jnp.concatenate([b_router_expert, b_router_group,
                          jnp.zeros((DEPTH, LANES - N_EXPERTS - N_EXPERT_GROUPS), F32)],
                         axis=1).reshape(DEPTH, 1, LANES)
    n_pg = POOL_WIDTH // POOL_GROUP_DIM
    bdw = jnp.zeros((DEPTH, POOL_WIDTH, POOL_WIDTH), F32)
    for g in range(n_pg):
        lo = g * POOL_GROUP_DIM
        bdw = bdw.at[:, lo:lo + POOL_GROUP_DIM, lo:lo + POOL_GROUP_DIM].set(w_pool_group[:, g])
    bdw = bdw.astype(BF16)
    pscale = pool_scale.reshape(DEPTH, 1, POOL_WIDTH)
    qg = jnp.tile(q_norm, (1, LANES // HEAD_DIM)).reshape(DEPTH, 1, LANES)
    kg = jnp.tile(k_norm, (1, LANES // HEAD_DIM)).reshape(DEPTH, 1, LANES)
    gpre = norm_mix_pre.reshape(DEPTH, 1, D_MODEL)
    gpost = norm_mix_post.reshape(DEPTH, 1, D_MODEL)
    gffn = norm_ffn_pre.reshape(DEPTH, 1, D_MODEL)
    gfpost = norm_ffn_post.reshape(DEPTH, 1, D_MODEL)
    ck = cache_k.reshape(DEC_BATCH, DEPTH, PAST_LEN, KV_WIDTH)
    cv = cache_v.reshape(DEC_BATCH, DEPTH, PAST_LEN, KV_WIDTH)

    c_all = jnp.concatenate([c_ctx[None, :], c, jnp.zeros((MOD_ROWS - 1 - DEC_BATCH, D_MODEL), F32)], axis=0)
    mods = _mod_call(c_all, w_ada, b_ada).reshape(DEPTH, MOD_ROWS, 1, N_MOD * D_MODEL)

    xs_buf = jnp.zeros((XS_ROWS, HALF), U32)
    y = jnp.zeros((XS_ROWS, HALF), U32)
    new_k, new_v = [], []
    x, hb = _prenorm_call(x_prompt.reshape(N_P, D_MODEL), x_sample.reshape(N_S, D_MODEL), mods, gpre)
    for l in range(DEPTH):
        q, k, v, xp, xc, xsn, gates = _proj_call(l, hb, w_in_b, qg, kg, cos_t, s1_t, s2_t, avg, dft_ch)
        new_k.append(k[:N_P].reshape(BATCH, SEQ, N_KV_HEADS, HEAD_DIM))
        new_v.append(v[:N_P].reshape(BATCH, SEQ, N_KV_HEADS, HEAD_DIM))
        attn = (_attn_prompt_call(q, k, v), _attn_sample_call(l, q, k, v, ck, cv))
        pool = (_pool_call(l, xp, bdw, pscale, SEQ, BATCH, 0, CTX_SEQ_PER_STEP),
                _pool_call(l, xp, bdw, pscale, DEC_SEQ, DEC_BATCH, N_P // DEC_SEQ, 1))
        four = (_fourier_prompt_call(cp, sp, xc, xsn), _fourier_sample_call(cl, sl, xc, xsn))
        x1, h2, pos, post, mw, cnt = _merge_call(l, x, attn, pool, four, gates, mods, gpost, gffn,
                                                 wa_b, wp_b, wf_b, wo_b, wr_b, br, tri, upper)
        runs = cnt.reshape(N_TILES, MOD_ROWS, LANES)[:, 0, :N_EXPERTS].astype(jnp.int32)
        runs = ((runs + RUN_ALIGN - 1) // RUN_ALIGN) * RUN_ALIGN
        tile_off = jnp.cumsum(runs, axis=1) - runs
        rows_e = jnp.sum(runs, axis=0)
        padded = ((rows_e + EBLK - 1) // EBLK) * EBLK
        pad_end = jnp.cumsum(padded)
        xs_off = (pad_end - padded)[None, :] + jnp.cumsum(runs, axis=0) - runs
        blk_cnt = padded // EBLK
        blk_start = (pad_end - padded) // EBLK
        n_used = pad_end[-1:] // EBLK
        run_cnt = (runs // RUN_ALIGN).reshape(N_RUNS)
        tile_off = tile_off.reshape(N_RUNS)
        xs_off = xs_off.reshape(N_RUNS)
        tile_cnt = jnp.sum(runs, axis=1) // RUN_ALIGN
        xs_buf = _dispatch_call(run_cnt, tile_off, xs_off, tile_cnt, h2, post, xs_buf)
        y = _expert_call(l, blk_start, blk_cnt, n_used, xs_buf, y,
                         w_expert_gate, w_expert_up, w_expert_down)
        outs = _combine_call(l, run_cnt, tile_off, xs_off, tile_cnt, y, x1, pos, mw, mods, gfpost,
                             gpre)
        x, hb = outs

    y_prompt = outs[0].reshape(BATCH, SEQ, D_MODEL)
    y_sample = outs[1].reshape(DEC_BATCH, DEC_SEQ, D_MODEL)
    return (y_prompt, y_sample, jnp.stack(new_k, axis=1), jnp.stack(new_v, axis=1))
```

```python
import functools
import math

import numpy as np
import jax
import jax.numpy as jnp
from jax import lax
from jax.experimental import pallas as pl
from jax.experimental.pallas import tpu as pltpu

F32 = jnp.float32
BF16 = jnp.bfloat16

D_MODEL = 1024
BATCH = 32
SEQ = 256
DEPTH = 4
DEC_BATCH = 2
DEC_SEQ = 2048
PAST_LEN = 512
GRID_W = 64
N_HEADS = 8
N_KV_HEADS = 2
HEAD_DIM = 64
KV_GROUP = N_HEADS // N_KV_HEADS
ATTN_WIDTH = N_HEADS * HEAD_DIM
KV_WIDTH = N_KV_HEADS * HEAD_DIM
ROPE_AXIS_DIM = HEAD_DIM // 2
ROPE_BASE = 10000.0
POOL_WINDOWS = (2, 4, 8, 16)
POOL_WIDTH = 256
POOL_GROUP_DIM = 64
FOURIER_WIDTH = 256
FOURIER_GROUP_DIM = 64
N_BRANCHES = 3
OFF_K = ATTN_WIDTH
OFF_P = ATTN_WIDTH + 2 * KV_WIDTH
OFF_G = OFF_P + POOL_WIDTH + FOURIER_WIDTH
IN_WIDTH = OFF_G + N_BRANCHES * D_MODEL
N_EXPERT_GROUPS = 4
EXPERTS_PER_GROUP = 8
N_EXPERTS = N_EXPERT_GROUPS * EXPERTS_PER_GROUP
TOP_K = 2
EXPERT_FF = 256
N_MOD = 6
RMS_EPS = 1e-6
QK_SCALE = HEAD_DIM ** -0.5 * math.log2(math.e)

N_P = BATCH * SEQ
N_S = DEC_BATCH * DEC_SEQ
N_TOK = N_P + N_S
LANES = 128
MOD_ROWS = 8
POOL_PAD = 16

TM = 512
PROJ_TM = 1024
N_TILES = N_TOK // TM
P_TILES = N_P // TM
S_TILES_PER_SEQ = DEC_SEQ // TM
CTX_SEQ_PER_STEP = 4
TQ_S = 128
KEY_CHUNK = 512
TF_S = 512
EBLK = 512
X_SLOTS = 4
RUN_ALIGN = 8
RUN_SHIFT = 3
RUN_BITS = 7
assert RUN_ALIGN << (RUN_BITS - 1) == TM
SORT_ROWS = TOP_K * TM + N_EXPERTS * RUN_ALIGN
DISPATCH_SLOTS = 3
assert N_TILES >= DISPATCH_SLOTS
TILE_BITS = 8
assert SORT_ROWS < RUN_ALIGN << TILE_BITS
N_RUNS = N_TILES * N_EXPERTS
N_EBLK = (N_TOK * TOP_K + N_RUNS * (RUN_ALIGN - 1) + N_EXPERTS * (EBLK - 1) + EBLK - 1) // EBLK
XS_ROWS = N_EBLK * EBLK
MOD_NT = 1536


def _dot(a, b):
    return jnp.dot(a, b, preferred_element_type=F32)


def _rms(x):
    return x * lax.rsqrt(jnp.mean(x * x, axis=-1, keepdims=True) + RMS_EPS)


def _mod_row(i):
    return jnp.where(i < P_TILES, 0, 1 + (i - P_TILES) // S_TILES_PER_SEQ)


def _rope_block(i):
    p_tiles = N_P // PROJ_TM
    per_seq = DEC_SEQ // PROJ_TM
    return jnp.where(i < p_tiles, per_seq, (i - p_tiles) % per_seq)


def _mod_kernel(c_ref, w_ref, b_ref, o_ref):
    c = c_ref[...]
    s = (c * jax.nn.sigmoid(c)).astype(BF16)
    o_ref[...] = _dot(s, w_ref[...].astype(BF16)) + b_ref[...]


def _mod_call(c_all, w_ada, b_ada):
    nt = (N_MOD * D_MODEL) // MOD_NT
    return pl.pallas_call(
        _mod_kernel,
        grid=(DEPTH, nt),
        in_specs=[
            pl.BlockSpec((MOD_ROWS, D_MODEL), lambda l, j: (0, 0)),
            pl.BlockSpec((None, D_MODEL, MOD_NT), lambda l, j: (l, 0, j)),
            pl.BlockSpec((None, 1, MOD_NT), lambda l, j: (l, 0, j)),
        ],
        out_specs=pl.BlockSpec((None, MOD_ROWS, MOD_NT), lambda l, j: (l, 0, j)),
        out_shape=jax.ShapeDtypeStruct((DEPTH, MOD_ROWS, N_MOD * D_MODEL), F32),
        name="adaln_mod",
    )(c_all, w_ada, b_ada.reshape(DEPTH, 1, N_MOD * D_MODEL))


def _prenorm(x, mod, gain):
    return (_rms(x) * gain) * (1.0 + mod[:, D_MODEL:2 * D_MODEL]) + mod[:, 0:D_MODEL]


def _prenorm_kernel(xp_ref, xs_ref, mod_ref, gpre_ref, x_ref, hb_ref):
    x = jnp.where(pl.program_id(0) < P_TILES, xp_ref[...], xs_ref[...])
    x_ref[...] = x
    hb_ref[...] = _prenorm(x, mod_ref[...], gpre_ref[...]).astype(BF16)


def _prenorm_call(x_prompt, x_sample, mods, gpre):
    row = lambda i: (i, 0)
    return pl.pallas_call(
        _prenorm_kernel,
        grid=(N_TILES,),
        in_specs=[
            pl.BlockSpec((TM, D_MODEL), lambda i: (jnp.minimum(i, P_TILES - 1), 0)),
            pl.BlockSpec((TM, D_MODEL), lambda i: (jnp.maximum(i - P_TILES, 0), 0)),
            pl.BlockSpec((None, None, 1, 2 * D_MODEL), lambda i: (0, _mod_row(i), 0, 0)),
            pl.BlockSpec((None, 1, D_MODEL), lambda i: (0, 0, 0)),
        ],
        out_specs=[pl.BlockSpec((TM, D_MODEL), row), pl.BlockSpec((TM, D_MODEL), row)],
        out_shape=[jax.ShapeDtypeStruct((N_TOK, D_MODEL), F32),
                   jax.ShapeDtypeStruct((N_TOK, D_MODEL), BF16)],
        name="prenorm",
    )(x_prompt, x_sample, mods, gpre)


def _proj_kernel(hb_ref, w_ref, qg_ref, kg_ref, cos_ref, s1_ref, s2_ref, avg_ref, dft_ref,
                 ck_in_ref, cv_in_ref,
                 q_ref, k_ref, v_ref, xp_ref, xc_ref, xs_ref, g_ref, ck_ref, cv_ref):
    del ck_in_ref, cv_in_ref
    hb = hb_ref[...]

    cos = cos_ref[...]
    s1 = s1_ref[...]
    s2 = s2_ref[...]

    def rope(t):
        return (t * cos + pltpu.roll(t, LANES - ROPE_AXIS_DIM // 2, 1) * s1
                + pltpu.roll(t, ROPE_AXIS_DIM // 2, 1) * s2)

    avg = avg_ref[...]
    for c in range(ATTN_WIDTH // LANES):
        lo = c * LANES
        q = _dot(hb, w_ref[:, lo:lo + LANES])
        ms = _dot((q * q).astype(BF16), avg)
        q = q * lax.rsqrt(ms + RMS_EPS) * qg_ref[...]
        q_ref[:, lo:lo + LANES] = (rope(q) * QK_SCALE).astype(BF16)

    kv = _dot(hb, w_ref[:, OFF_K:OFF_K + 2 * KV_WIDTH])
    k = kv[:, 0:KV_WIDTH]
    ms = _dot((k * k).astype(BF16), avg)
    k = rope(k * lax.rsqrt(ms + RMS_EPS) * kg_ref[...])
    v = kv[:, KV_WIDTH:]
    k_ref[...] = k
    v_ref[...] = v

    @pl.when(pl.program_id(0) < N_P // PROJ_TM)
    def _():
        for s in range(PROJ_TM // SEQ):
            ck_ref[s] = k[s * SEQ:(s + 1) * SEQ]
            cv_ref[s] = v[s * SEQ:(s + 1) * SEQ]

    pf = _dot(hb, w_ref[:, OFF_P:OFF_G])
    xp_ref[...] = pf[:, 0:POOL_WIDTH]
    cs = _dot(pf[:, POOL_WIDTH:].astype(BF16), dft_ref[...])
    xc_ref[...] = cs[:, 0:FOURIER_WIDTH].astype(BF16)
    xs_ref[...] = cs[:, FOURIER_WIDTH:].astype(BF16)

    for c in range(N_BRANCHES):
        lo = OFF_G + c * D_MODEL
        g_ref[:, c * D_MODEL:(c + 1) * D_MODEL] = _dot(hb, w_ref[:, lo:lo + D_MODEL]).astype(BF16)


def _proj_call(l, hb, w_in, qg, kg, cos_t, s1_t, s2_t, avg, dft, cache_k, cache_v):
    row = lambda i: (i, 0)
    const2 = lambda i: (0, 0)
    per_layer = lambda i: (l, 0, 0)
    rope_spec = pl.BlockSpec((PROJ_TM, LANES), lambda i: (_rope_block(i), 0))
    cache_spec = pl.BlockSpec((PROJ_TM // SEQ, None, SEQ, KV_WIDTH),
                              lambda i: (jnp.minimum(i, N_P // PROJ_TM - 1), l, 0, 0))
    cache_shape = jax.ShapeDtypeStruct((BATCH, DEPTH, SEQ, KV_WIDTH), F32)
    return pl.pallas_call(
        _proj_kernel,
        grid=(N_TOK // PROJ_TM,),
        in_specs=[
            pl.BlockSpec((PROJ_TM, D_MODEL), row),
            pl.BlockSpec((None, D_MODEL, IN_WIDTH), per_layer),
            pl.BlockSpec((None, 1, LANES), per_layer),
            pl.BlockSpec((None, 1, LANES), per_layer),
            rope_spec, rope_spec, rope_spec,
            pl.BlockSpec((LANES, LANES), const2),
            pl.BlockSpec((FOURIER_WIDTH, 2 * FOURIER_WIDTH), const2),
            pl.BlockSpec(memory_space=pl.ANY),
            pl.BlockSpec(memory_space=pl.ANY),
        ],
        out_specs=[
            pl.BlockSpec((PROJ_TM, ATTN_WIDTH), row),
            pl.BlockSpec((PROJ_TM, KV_WIDTH), row),
            pl.BlockSpec((PROJ_TM, KV_WIDTH), row),
            pl.BlockSpec((PROJ_TM, POOL_WIDTH), row),
            pl.BlockSpec((PROJ_TM, FOURIER_WIDTH), row),
            pl.BlockSpec((PROJ_TM, FOURIER_WIDTH), row),
            pl.BlockSpec((PROJ_TM, N_BRANCHES * D_MODEL), row),
            cache_spec, cache_spec,
        ],
        out_shape=[
            jax.ShapeDtypeStruct((N_TOK, ATTN_WIDTH), BF16),
            jax.ShapeDtypeStruct((N_TOK, KV_WIDTH), F32),
            jax.ShapeDtypeStruct((N_TOK, KV_WIDTH), F32),
            jax.ShapeDtypeStruct((N_TOK, POOL_WIDTH), F32),
            jax.ShapeDtypeStruct((N_TOK, FOURIER_WIDTH), BF16),
            jax.ShapeDtypeStruct((N_TOK, FOURIER_WIDTH), BF16),
            jax.ShapeDtypeStruct((N_TOK, N_BRANCHES * D_MODEL), BF16),
            cache_shape, cache_shape,
        ],
        input_output_aliases={9: 7, 10: 8},
        name="proj",
    )(hb, w_in, qg, kg, cos_t, s1_t, s2_t, avg, dft, cache_k, cache_v)


def _attn_kernel(*refs, n_parts, tq):
    q_ref = refs[0]
    kv_refs = refs[1:1 + 2 * n_parts]
    o_ref = refs[-1]
    single_chunk = n_parts == 1 and kv_refs[0].shape[0] <= KEY_CHUNK
    outs = []
    for j in range(N_KV_HEADS):
        lo = j * HEAD_DIM
        qs = jnp.concatenate(
            [q_ref[:, (KV_GROUP * j + g) * HEAD_DIM:(KV_GROUP * j + g + 1) * HEAD_DIM]
             for g in range(KV_GROUP)], axis=0)
        m = acc = None
        for p in range(n_parts):
            k_ref, v_ref = kv_refs[2 * p], kv_refs[2 * p + 1]
            for c0 in range(0, k_ref.shape[0], KEY_CHUNK):
                c1 = min(c0 + KEY_CHUNK, k_ref.shape[0])
                kc = k_ref[c0:c1, lo:lo + HEAD_DIM].astype(BF16)
                s = lax.dot_general(qs, kc, (((1,), (1,)), ((), ())), preferred_element_type=F32)
                mc = jnp.max(s, axis=1, keepdims=True)
                m_new = mc if m is None else jnp.maximum(m, mc)
                e = jnp.exp2(s - m_new)
                vc = v_ref[c0:c1, lo:lo + HEAD_DIM].astype(BF16)
                if single_chunk:
                    den = jnp.sum(e, axis=1, keepdims=True)
                else:
                    vc = jnp.concatenate([vc, jnp.ones((c1 - c0, HEAD_DIM), BF16)], axis=1)
                pv = _dot(e.astype(BF16), vc)
                acc = pv if m is None else acc * jnp.exp2(m - m_new) + pv
                m = m_new
        o = acc / den if single_chunk else acc[:, 0:HEAD_DIM] / acc[:, HEAD_DIM:2 * HEAD_DIM]
        outs.extend(o[g * tq:(g + 1) * tq] for g in range(KV_GROUP))
    o_ref[...] = jnp.concatenate(outs, axis=1).astype(BF16)


def _attn_prompt_call(q, k, v):
    blk = lambda b: (b, 0)
    return pl.pallas_call(
        functools.partial(_attn_kernel, n_parts=1, tq=SEQ),
        grid=(BATCH,),
        in_specs=[
            pl.BlockSpec((SEQ, ATTN_WIDTH), blk),
            pl.BlockSpec((SEQ, KV_WIDTH), blk),
            pl.BlockSpec((SEQ, KV_WIDTH), blk),
        ],
        out_specs=pl.BlockSpec((SEQ, ATTN_WIDTH), blk),
        out_shape=jax.ShapeDtypeStruct((N_P, ATTN_WIDTH), BF16),
        name="attn_context",
    )(q, k, v)


def _attn_sample_call(l, q, k, v, cache_k, cache_v):
    nq = DEC_SEQ // TQ_S
    qrow = lambda b, i: (N_P // TQ_S + b * nq + i, 0)
    seq = lambda b, i: (N_P // DEC_SEQ + b, 0)
    cache = lambda b, i: (b, l, 0, 0)
    return pl.pallas_call(
        functools.partial(_attn_kernel, n_parts=2, tq=TQ_S),
        grid=(DEC_BATCH, nq),
        in_specs=[
            pl.BlockSpec((TQ_S, ATTN_WIDTH), qrow),
            pl.BlockSpec((None, None, PAST_LEN, KV_WIDTH), cache),
            pl.BlockSpec((None, None, PAST_LEN, KV_WIDTH), cache),
            pl.BlockSpec((DEC_SEQ, KV_WIDTH), seq),
            pl.BlockSpec((DEC_SEQ, KV_WIDTH), seq),
        ],
        out_specs=pl.BlockSpec((TQ_S, ATTN_WIDTH), lambda b, i: (b * nq + i, 0)),
        out_shape=jax.ShapeDtypeStruct((N_S, ATTN_WIDTH), BF16),
        name="attn_latent",
    )(q, cache_k, cache_v, k, v)


def _pool_kernel(xp_ref, bdw_ref, sc_ref, o_ref, pad_ref, *, seq_len, n_seq):
    for s in range(n_seq):
        _pool_sequence(xp_ref, bdw_ref, sc_ref, o_ref, pad_ref, s * seq_len, seq_len)


def _pool_sequence(xp_ref, bdw_ref, sc_ref, o_ref, pad_ref, row0, seq_len):
    half = POOL_WIDTH // 2
    zeros = jnp.zeros((POOL_PAD, POOL_WIDTH), F32)
    pad_ref[0:POOL_PAD, :] = zeros
    pad_ref[POOL_PAD + seq_len:, :] = zeros
    pad_ref[POOL_PAD:POOL_PAD + seq_len, :] = xp_ref[row0:row0 + seq_len, :]
    chunk = min(seq_len, 256)
    lane = lax.broadcasted_iota(jnp.int32, (chunk, half), 1)
    first = lane < POOL_GROUP_DIM
    for c in range(seq_len // chunk):
        base = c * chunk
        t = lax.broadcasted_iota(jnp.int32, (chunk, half), 0) + base

        def sh(j, lo):
            return pad_ref[POOL_PAD + base + j:POOL_PAD + base + j + chunk, lo:lo + half]

        def cnt(w):
            return (jnp.minimum(t + w // 2, seq_len) - jnp.maximum(t - w // 2, 0)).astype(F32)

        xa = sh(0, 0)
        w2 = sh(-1, 0) + xa
        w4 = w2 + sh(-2, 0) + sh(1, 0)
        xb = sh(0, half)
        w8 = xb
        for j in (-4, -3, -2, -1, 1, 2, 3):
            w8 = w8 + sh(j, half)
        w16 = w8
        for j in (-8, -7, -6, -5, 4, 5, 6, 7):
            w16 = w16 + sh(j, half)
        pa = jnp.where(first, w2 / cnt(2), w4 / cnt(4)) - xa
        pb = jnp.where(first, w8 / cnt(8), w16 / cnt(16)) - xb
        pooled = jnp.concatenate([pa, pb], axis=1).astype(BF16)
        o_ref[row0 + base:row0 + base + chunk, :] = (
            _dot(pooled, bdw_ref[...]) * sc_ref[...]).astype(BF16)


def _pool_call(l, xp, bdw, scale, seq_len, n_seq, blk0, seq_per_step):
    per_layer = lambda b: (l, 0, 0)
    rows = seq_len * seq_per_step
    return pl.pallas_call(
        functools.partial(_pool_kernel, seq_len=seq_len, n_seq=seq_per_step),
        grid=(n_seq // seq_per_step,),
        in_specs=[
            pl.BlockSpec((rows, POOL_WIDTH), lambda b: (blk0 + b, 0)),
            pl.BlockSpec((None, POOL_WIDTH, POOL_WIDTH), per_layer),
            pl.BlockSpec((None, 1, POOL_WIDTH), per_layer),
        ],
        out_specs=pl.BlockSpec((rows, POOL_WIDTH), lambda b: (b, 0)),
        out_shape=jax.ShapeDtypeStruct((n_seq * seq_len, POOL_WIDTH), BF16),
        scratch_shapes=[pltpu.VMEM((seq_len + 2 * POOL_PAD, POOL_WIDTH), F32)],
        name="pool_%d" % seq_len,
    )(xp, bdw, scale)


def _fourier_kernel(c_ref, s_ref, xc_ref, xs_ref, o_ref, *, scale, n_seq=1):
    seq_len = xc_ref.shape[0] // n_seq
    out_len = o_ref.shape[0] // n_seq
    for s in range(n_seq):
        rows = slice(s * seq_len, (s + 1) * seq_len)
        y = _dot(c_ref[...], xc_ref[rows, :]) - _dot(s_ref[...], xs_ref[rows, :])
        o_ref[s * out_len:(s + 1) * out_len, :] = (y * scale).astype(BF16)


def _fourier_prompt_call(cmat, smat, xc, xs):
    blk = lambda b: (b, 0)
    const2 = lambda b: (0, 0)
    rows = SEQ * CTX_SEQ_PER_STEP
    return pl.pallas_call(
        functools.partial(_fourier_kernel, scale=1.0 / math.sqrt(SEQ * FOURIER_GROUP_DIM),
                          n_seq=CTX_SEQ_PER_STEP),
        grid=(BATCH // CTX_SEQ_PER_STEP,),
        in_specs=[
            pl.BlockSpec((SEQ, SEQ), const2),
            pl.BlockSpec((SEQ, SEQ), const2),
            pl.BlockSpec((rows, FOURIER_WIDTH), blk),
            pl.BlockSpec((rows, FOURIER_WIDTH), blk),
        ],
        out_specs=pl.BlockSpec((rows, FOURIER_WIDTH), blk),
        out_shape=jax.ShapeDtypeStruct((N_P, FOURIER_WIDTH), BF16),
        name="fourier_context",
    )(cmat, smat, xc, xs)


def _fourier_sample_call(cmat, smat, xc, xs):
    nt = DEC_SEQ // TF_S
    rows = lambda b, i: (i, 0)
    seq = lambda b, i: (N_P // DEC_SEQ + b, 0)
    out = lambda b, i: (b * nt + i, 0)
    return pl.pallas_call(
        functools.partial(_fourier_kernel, scale=1.0 / math.sqrt(DEC_SEQ * FOURIER_GROUP_DIM)),
        grid=(DEC_BATCH, nt),
        in_specs=[
            pl.BlockSpec((TF_S, DEC_SEQ), rows),
            pl.BlockSpec((TF_S, DEC_SEQ), rows),
            pl.BlockSpec((DEC_SEQ, FOURIER_WIDTH), seq),
            pl.BlockSpec((DEC_SEQ, FOURIER_WIDTH), seq),
        ],
        out_specs=pl.BlockSpec((TF_S, FOURIER_WIDTH), out),
        out_shape=jax.ShapeDtypeStruct((N_S, FOURIER_WIDTH), BF16),
        name="fourier_latent",
    )(cmat, smat, xc, xs)


def _merge_kernel(x_ref, attn_p_ref, attn_s_ref, pool_p_ref, pool_s_ref, four_p_ref, four_s_ref,
                  g_ref, mod_ref, gpost_ref, gffn_ref,
                  wa_ref, wp_ref, wf_ref, wo_ref, wr_ref, br_ref, tri_ref, upper_ref,
                  x1_ref, h2_ref, pos_ref, post_ref, mw_ref, cnt_ref):
    i = pl.program_id(0)
    mod = mod_ref[...]
    g1 = mod[:, 2 * D_MODEL:3 * D_MODEL]
    sh2 = mod[:, 3 * D_MODEL:4 * D_MODEL]
    sc2 = mod[:, 4 * D_MODEL:5 * D_MODEL]

    def gate2(c):
        return jnp.tanh(g_ref[:, c * D_MODEL:(c + 1) * D_MODEL].astype(F32)) + 1.0

    def branch(p_ref, s_ref):
        return jnp.where(i < P_TILES, p_ref[...], s_ref[...])

    merged = gate2(0) * _dot(branch(attn_p_ref, attn_s_ref), wa_ref[...])
    merged = merged + gate2(1) * _dot(branch(pool_p_ref, pool_s_ref), wp_ref[...])
    merged = merged + gate2(2) * _dot(branch(four_p_ref, four_s_ref), wf_ref[...])
    mix = _dot((0.5 * merged).astype(BF16), wo_ref[...])
    x1 = x_ref[...] + g1 * (_rms(mix) * gpost_ref[...])
    x1_ref[...] = x1
    h2 = (_rms(x1) * gffn_ref[...]) * (1.0 + sc2) + sh2
    h2b = h2.astype(BF16)
    h2_ref[...] = h2b

    logits = _dot(h2b, wr_ref[...]) + br_ref[...]
    lane = lax.broadcasted_iota(jnp.int32, (TM, LANES), 1)
    lanef = lane.astype(F32)
    neg = jnp.float32(-3e38)
    big = jnp.float32(1e9)
    is_g = (lane >= N_EXPERTS) & (lane < N_EXPERTS + N_EXPERT_GROUPS)
    lg = jnp.where(is_g, logits, neg)
    gmax = jnp.max(lg, axis=1, keepdims=True)
    g_sel = jnp.min(jnp.where(lg == gmax, lanef - N_EXPERTS, big), axis=1, keepdims=True)
    p_g = 1.0 / jnp.sum(jnp.where(is_g, jnp.exp(logits - gmax), 0.0), axis=1, keepdims=True)
    grp = lax.shift_right_logical(lane, int(math.log2(EXPERTS_PER_GROUP))).astype(F32)
    in_grp = (lane < N_EXPERTS) & (grp == g_sel)
    le = jnp.where(in_grp, logits, neg)
    v1 = jnp.max(le, axis=1, keepdims=True)
    i1 = jnp.min(jnp.where(le == v1, lanef, big), axis=1, keepdims=True)
    le2 = jnp.where(lanef == i1, neg, le)
    v2 = jnp.max(le2, axis=1, keepdims=True)
    i2 = jnp.min(jnp.where(le2 == v2, lanef, big), axis=1, keepdims=True)
    e21 = jnp.exp(v2 - v1)
    w1 = p_g / (1.0 + e21)
    w2 = p_g * e21 / (1.0 + e21)

    oh1 = (lanef == i1).astype(F32)
    oh2 = (lanef == i2).astype(F32)
    ohb = (oh1 + oh2).astype(BF16)
    before = _dot(tri_ref[...], ohb)
    cnt = _dot(jnp.ones((MOD_ROWS, TM), BF16), ohb)
    cnt_pad = (lax.shift_right_logical(cnt.astype(jnp.int32) + (RUN_ALIGN - 1), RUN_SHIFT)
               * RUN_ALIGN).astype(F32)
    run_off = _dot(cnt_pad.astype(BF16), upper_ref[...])
    slot = run_off[0:1, :] + before
    p1 = jnp.sum(slot * oh1, axis=1, keepdims=True)
    p2 = jnp.sum(slot * oh2, axis=1, keepdims=True)
    cnt_ref[...] = cnt

    pos = jnp.where(lane == 0, p1, jnp.where(lane == 1, p2, 0.0))
    pos_ref[...] = pos.astype(jnp.int32)
    post_ref[...] = pos.T[0:MOD_ROWS, :].astype(jnp.int32)
    mw_ref[...] = jnp.where(lane == 0, w1, w2)


def _merge_call(l, x, attn, pool, four, gates, mods, gpost, gffn, wa, wp, wf, wo, wr, br, tri,
                upper):
    row = lambda i: (i, 0)
    prow = lambda i: (jnp.minimum(i, P_TILES - 1), 0)
    srow = lambda i: (jnp.maximum(i - P_TILES, 0), 0)
    const2 = lambda i: (0, 0)
    per_layer = lambda i: (l, 0, 0)
    return pl.pallas_call(
        _merge_kernel,
        grid=(N_TILES,),
        in_specs=[
            pl.BlockSpec((TM, D_MODEL), row),
            pl.BlockSpec((TM, ATTN_WIDTH), prow),
            pl.BlockSpec((TM, ATTN_WIDTH), srow),
            pl.BlockSpec((TM, POOL_WIDTH), prow),
            pl.BlockSpec((TM, POOL_WIDTH), srow),
            pl.BlockSpec((TM, FOURIER_WIDTH), prow),
            pl.BlockSpec((TM, FOURIER_WIDTH), srow),
            pl.BlockSpec((TM, N_BRANCHES * D_MODEL), row),
            pl.BlockSpec((None, None, 1, N_MOD * D_MODEL), lambda i: (l, _mod_row(i), 0, 0)),
            pl.BlockSpec((None, 1, D_MODEL), per_layer),
            pl.BlockSpec((None, 1, D_MODEL), per_layer),
            pl.BlockSpec((None, ATTN_WIDTH, D_MODEL), per_layer),
            pl.BlockSpec((None, POOL_WIDTH, D_MODEL), per_layer),
            pl.BlockSpec((None, FOURIER_WIDTH, D_MODEL), per_layer),
            pl.BlockSpec((None, D_MODEL, D_MODEL), per_layer),
            pl.BlockSpec((None, D_MODEL, LANES), per_layer),
            pl.BlockSpec((None, 1, LANES), per_layer),
            pl.BlockSpec((TM, TM), const2),
            pl.BlockSpec((LANES, LANES), const2),
        ],
        out_specs=[
            pl.BlockSpec((TM, D_MODEL), row),
            pl.BlockSpec((TM, D_MODEL), row),
            pl.BlockSpec((TM, LANES), row),
            pl.BlockSpec((MOD_ROWS, TM), lambda i: (0, i)),
            pl.BlockSpec((TM, LANES), row),
            pl.BlockSpec((MOD_ROWS, LANES), row),
        ],
        out_shape=[
            jax.ShapeDtypeStruct((N_TOK, D_MODEL), F32),
            jax.ShapeDtypeStruct((N_TOK, D_MODEL), BF16),
            jax.ShapeDtypeStruct((N_TOK, LANES), jnp.int32),
            jax.ShapeDtypeStruct((MOD_ROWS, N_TOK), jnp.int32),
            jax.ShapeDtypeStruct((N_TOK, LANES), F32),
            jax.ShapeDtypeStruct((N_TILES * MOD_ROWS, LANES), F32),
        ],
        name="merge_router",
    )(x, attn[0], attn[1], pool[0], pool[1], four[0], four[1], gates, mods, gpost, gffn,
      wa, wp, wf, wo, wr, br, tri, upper)


HALF = D_MODEL // 2
U32 = jnp.uint32
HI_MASK = 0xFFFF0000


def _pack_rows(x):
    lo = lax.bitcast_convert_type(x[:, :HALF], U32)
    hi = lax.bitcast_convert_type(x[:, HALF:], U32)
    return lax.shift_right_logical(lo, U32(16)) | (hi & U32(HI_MASK))


def _unpack_rows(u):
    lo = lax.bitcast_convert_type(lax.shift_left(u, U32(16)), F32)
    hi = lax.bitcast_convert_type(u & U32(HI_MASK), F32)
    return jnp.concatenate([lo.astype(BF16), hi.astype(BF16)], axis=1)


def _run_copies(cnt, src_ref, src_off, dst_ref, dst_off, sem, wait, bits=RUN_BITS):
    off = jnp.int32(0)
    for b in reversed(range(bits)):
        size = RUN_ALIGN << b
        take = lax.shift_right_logical(cnt, b) & 1

        @pl.when(take == 1)
        def _(off=off, size=size):
            cp = pltpu.make_async_copy(
                src_ref.at[pl.ds(pl.multiple_of(src_off + off, RUN_ALIGN), size)],
                dst_ref.at[pl.ds(pl.multiple_of(dst_off + off, RUN_ALIGN), size)], sem)
            if wait:
                cp.wait()
            else:
                cp.start()

        off = off + take * size


def _tile_runs(tile, cnt_ref, fn):
    def body(e, carry):
        r = tile * N_EXPERTS + e
        fn(r, cnt_ref[r])
        return carry

    lax.fori_loop(0, N_EXPERTS, body, 0)


def _dispatch_kernel(cnt_ref, toff_ref, xoff_ref, tsum_ref, h_ref, post_ref, xs_in_ref, xs_ref,
                     sorted_ref, sem):
    del xs_in_ref
    i = pl.program_id(0)
    slot = lax.rem(i, DISPATCH_SLOTS)
    rows = lax.broadcasted_iota(jnp.int32, (SORT_ROWS, TM), 0)
    p = post_ref[...]
    perm = jnp.where(rows == p[0:1, :], 1.0, jnp.where(rows == p[1:2, :], 1.0, 0.0)).astype(BF16)
    sorted_ref[slot] = _pack_rows(_dot(perm, h_ref[...]))

    def wait(tile, slot):
        _run_copies(tsum_ref[tile], sorted_ref.at[slot], 0, xs_ref, 0, sem.at[slot], True, TILE_BITS)

    lag = DISPATCH_SLOTS - 1

    @pl.when(i >= lag)
    def _():
        wait(i - lag, lax.rem(i + 1, DISPATCH_SLOTS))

    _tile_runs(i, cnt_ref, lambda r, c: _run_copies(
        c, sorted_ref.at[slot], toff_ref[r], xs_ref, xoff_ref[r], sem.at[slot], False))

    @pl.when(i == pl.num_programs(0) - 1)
    def _():
        for back in reversed(range(lag)):
            wait(i - back, lax.rem(i - back, DISPATCH_SLOTS))


def _dispatch_call(cnt, toff, xoff, tsum, h2, post, xs):
    return pl.pallas_call(
        _dispatch_kernel,
        grid_spec=pltpu.PrefetchScalarGridSpec(
            num_scalar_prefetch=4,
            grid=(N_TILES,),
            in_specs=[
                pl.BlockSpec((TM, D_MODEL), lambda i, *_: (i, 0)),
                pl.BlockSpec((MOD_ROWS, TM), lambda i, *_: (0, i)),
                pl.BlockSpec(memory_space=pl.ANY),
            ],
            out_specs=pl.BlockSpec(memory_space=pl.ANY),
            scratch_shapes=[pltpu.VMEM((DISPATCH_SLOTS, SORT_ROWS, HALF), U32),
                            pltpu.SemaphoreType.DMA((DISPATCH_SLOTS,))],
        ),
        out_shape=jax.ShapeDtypeStruct((XS_ROWS, HALF), U32),
        input_output_aliases={6: 0},
        compiler_params=pltpu.CompilerParams(dimension_semantics=("arbitrary",)),
        name="moe_dispatch",
    )(cnt, toff, xoff, tsum, h2, post, xs)


def _expert_kernel(bstart_ref, bcnt_ref, nu_ref, wg_ref, wu_ref, wd_ref, xs_ref, y_prev_ref, y_ref,
                   wgb, wub, wdb, xbuf, ybuf, xsem, ysem):
    del y_prev_ref
    e = pl.program_id(0)
    n_used = nu_ref[0]

    def x_copy(g, slot):
        rows = pl.ds(pl.multiple_of(g * EBLK, EBLK), EBLK)
        return pltpu.make_async_copy(xs_ref.at[rows], xbuf.at[slot], xsem.at[slot])

    def y_copy(g, slot):
        rows = pl.ds(pl.multiple_of(g * EBLK, EBLK), EBLK)
        return pltpu.make_async_copy(ybuf.at[slot], y_ref.at[rows], ysem.at[slot])

    @pl.when(e == 0)
    def _():
        for g in range(X_SLOTS - 1):
            @pl.when(g < n_used)
            def _(g=g):
                x_copy(g, g).start()

    wgb[...] = wg_ref[...].astype(BF16)
    wub[...] = wu_ref[...].astype(BF16)
    wdb[...] = wd_ref[...].astype(BF16)
    first = bstart_ref[e]

    def block(j, carry):
        g = first + j
        slot = lax.rem(g, 2)
        xslot = lax.rem(g, X_SLOTS)
        x_copy(g, xslot).wait()

        @pl.when(g + (X_SLOTS - 1) < n_used)
        def _():
            x_copy(g + (X_SLOTS - 1), lax.rem(g + (X_SLOTS - 1), X_SLOTS)).start()

        xb = _unpack_rows(xbuf[xslot])
        gate = _dot(xb, wgb[...])
        up = _dot(xb, wub[...])
        act = (gate * jax.nn.sigmoid(gate)) * up
        y = _dot(act.astype(BF16), wdb[...])

        @pl.when(g >= 2)
        def _():
            y_copy(g - 2, slot).wait()

        ybuf[slot] = _pack_rows(y.astype(BF16).astype(F32))
        y_copy(g, slot).start()
        return carry

    lax.fori_loop(0, bcnt_ref[e], block, 0)

    @pl.when(e == pl.num_programs(0) - 1)
    def _():
        @pl.when(n_used >= 2)
        def _():
            y_copy(n_used - 2, lax.rem(n_used, 2)).wait()

        y_copy(n_used - 1, lax.rem(n_used - 1, 2)).wait()


def _expert_call(l, blk_start, blk_cnt, n_used, xs, y_prev, w_gate, w_up, w_down):
    wsel = lambda e, *_: (l, e, 0, 0)
    return pl.pallas_call(
        _expert_kernel,
        grid_spec=pltpu.PrefetchScalarGridSpec(
            num_scalar_prefetch=3,
            grid=(N_EXPERTS,),
            in_specs=[
                pl.BlockSpec((None, None, D_MODEL, EXPERT_FF), wsel),
                pl.BlockSpec((None, None, D_MODEL, EXPERT_FF), wsel),
                pl.BlockSpec((None, None, EXPERT_FF, D_MODEL), wsel),
                pl.BlockSpec(memory_space=pl.ANY),
                pl.BlockSpec(memory_space=pl.ANY),
            ],
            out_specs=pl.BlockSpec(memory_space=pl.ANY),
            scratch_shapes=[
                pltpu.VMEM((D_MODEL, EXPERT_FF), BF16),
                pltpu.VMEM((D_MODEL, EXPERT_FF), BF16),
                pltpu.VMEM((EXPERT_FF, D_MODEL), BF16),
                pltpu.VMEM((X_SLOTS, EBLK, HALF), U32),
                pltpu.VMEM((2, EBLK, HALF), U32),
                pltpu.SemaphoreType.DMA((X_SLOTS,)),
                pltpu.SemaphoreType.DMA((2,)),
            ],
        ),
        out_shape=jax.ShapeDtypeStruct((XS_ROWS, HALF), U32),
        input_output_aliases={7: 0},
        compiler_params=pltpu.CompilerParams(dimension_semantics=("arbitrary",)),
        name="moe_experts",
    )(blk_start, blk_cnt, n_used, w_gate, w_up, w_down, xs, y_prev)


def _combine_kernel(cnt_ref, toff_ref, xoff_ref, tsum_ref, y_ref, x1_ref, pos_ref, mw_ref, g2_ref,
                    gpost_ref, *rest, has_next):
    if has_next:
        nmod_ref, ngpre_ref, o_ref, hb_ref, ybuf, sem = rest
    else:
        o_ref, os_ref, ybuf, sem = rest
    i = pl.program_id(0)
    slot = lax.rem(i, 2)

    def start(tile, slot):
        _tile_runs(tile, cnt_ref, lambda r, c: _run_copies(
            c, y_ref, xoff_ref[r], ybuf.at[slot], toff_ref[r], sem.at[slot], False))

    @pl.when(i == 0)
    def _():
        ybuf[...] = jnp.zeros_like(ybuf)
        start(0, 0)

    @pl.when(i + 1 < pl.num_programs(0))
    def _():
        start(i + 1, 1 - slot)

    _run_copies(tsum_ref[i], y_ref, 0, ybuf.at[slot], 0, sem.at[slot], True, TILE_BITS)

    pos = pos_ref[...]
    mw = mw_ref[...]
    cols = lax.broadcasted_iota(jnp.int32, (TM, SORT_ROWS), 1)
    qw = (jnp.where(cols == pos[:, 0:1], mw[:, 0:1], 0.0)
          + jnp.where(cols == pos[:, 1:2], mw[:, 1:2], 0.0)).astype(BF16)
    ffn = _dot(qw, _unpack_rows(ybuf[slot]))
    x2 = x1_ref[...] + g2_ref[...] * (_rms(ffn) * gpost_ref[...])
    if has_next:
        o_ref[...] = x2
        hb_ref[...] = _prenorm(x2, nmod_ref[...], ngpre_ref[...]).astype(BF16)
    else:
        @pl.when(i < P_TILES)
        def _():
            o_ref[...] = x2

        os_ref[...] = x2


def _combine_call(l, cnt, toff, xoff, tsum, y, x1, pos, mw, mods, gpost, gpre):
    has_next = l + 1 < DEPTH
    row = lambda i, *_: (i, 0)
    in_specs = [
        pl.BlockSpec(memory_space=pl.ANY),
        pl.BlockSpec((TM, D_MODEL), row),
        pl.BlockSpec((TM, LANES), row),
        pl.BlockSpec((TM, LANES), row),
        pl.BlockSpec((None, None, 1, D_MODEL), lambda i, *_: (l, _mod_row(i), 0, N_MOD - 1)),
        pl.BlockSpec((None, 1, D_MODEL), lambda i, *_: (l, 0, 0)),
    ]
    args = [cnt, toff, xoff, tsum, y, x1, pos, mw, mods, gpost]
    if has_next:
        in_specs += [
            pl.BlockSpec((None, None, 1, 2 * D_MODEL), lambda i, *_: (l + 1, _mod_row(i), 0, 0)),
            pl.BlockSpec((None, 1, D_MODEL), lambda i, *_: (l + 1, 0, 0)),
        ]
        args += [mods, gpre]
        out_specs = [pl.BlockSpec((TM, D_MODEL), row), pl.BlockSpec((TM, D_MODEL), row)]
        out_shape = [jax.ShapeDtypeStruct((N_TOK, D_MODEL), F32),
                     jax.ShapeDtypeStruct((N_TOK, D_MODEL), BF16)]
    else:
        out_specs = [
            pl.BlockSpec((TM, D_MODEL), lambda i, *_: (jnp.minimum(i, P_TILES - 1), 0)),
            pl.BlockSpec((TM, D_MODEL), lambda i, *_: (jnp.maximum(i - P_TILES, 0), 0)),
        ]
        out_shape = [jax.ShapeDtypeStruct((N_P, D_MODEL), F32),
                     jax.ShapeDtypeStruct((N_S, D_MODEL), F32)]
    return pl.pallas_call(
        functools.partial(_combine_kernel, has_next=has_next),
        grid_spec=pltpu.PrefetchScalarGridSpec(
            num_scalar_prefetch=4,
            grid=(N_TILES,),
            in_specs=in_specs,
            out_specs=out_specs,
            scratch_shapes=[pltpu.VMEM((2, SORT_ROWS, HALF), U32), pltpu.SemaphoreType.DMA((2,))],
        ),
        out_shape=out_shape,
        compiler_params=pltpu.CompilerParams(dimension_semantics=("arbitrary",)),
        name="moe_combine",
    )(*args)


def _dft_mats(n):
    k = np.arange(n, dtype=np.int64)
    ang = 2.0 * np.pi * ((k[:, None] * k[None, :]) % n).astype(np.float64) / n
    return np.cos(ang), np.sin(ang)


def _block_diag(m, reps):
    n = m.shape[0]
    out = np.zeros((n * reps, n * reps), m.dtype)
    for r in range(reps):
        out[r * n:(r + 1) * n, r * n:(r + 1) * n] = m
    return out


def _rope_tables():
    t = np.arange(DEC_SEQ)
    pos = np.stack([t // GRID_W, t % GRID_W], axis=1).astype(np.float64)
    n_freq = ROPE_AXIS_DIM // 2
    inv = ROPE_BASE ** (-np.arange(n_freq, dtype=np.float64) * 2.0 / ROPE_AXIS_DIM)
    ang = pos[:, :, None] * inv[None, None, :]
    cos = np.cos(ang)
    sin = np.sin(ang)
    zero = np.zeros_like(sin[:, 0])
    cos_h = np.concatenate([cos[:, 0], cos[:, 0], cos[:, 1], cos[:, 1]], axis=1)
    s1_h = np.concatenate([-sin[:, 0], zero, -sin[:, 1], zero], axis=1)
    s2_h = np.concatenate([zero, sin[:, 0], zero, sin[:, 1]], axis=1)
    reps = LANES // HEAD_DIM

    def table(a, ident):
        a = np.tile(a, (1, reps))
        pad = np.full((PROJ_TM, LANES), ident, np.float64)
        return jnp.asarray(np.concatenate([a, pad], axis=0), F32)

    return table(cos_h, 1.0), table(s1_h, 0.0), table(s2_h, 0.0)


def kernel(x_prompt, x_sample, cache_k, cache_v, c, c_ctx, w_ada, b_ada, norm_mix_pre,
           norm_mix_post, norm_ffn_pre, norm_ffn_post, w_in, q_norm, k_norm, w_attn_out,
           w_pool_group, pool_scale, w_pool_out, w_fourier_out, w_out, w_router_group,
           b_router_group, w_router_expert, b_router_expert, w_expert_gate, w_expert_up,
           w_expert_down):
    cos_t, s1_t, s2_t = _rope_tables()
    avg = jnp.asarray(_block_diag(np.full((HEAD_DIM, HEAD_DIM), 1.0 / HEAD_DIM), LANES // HEAD_DIM), BF16)
    c64, s64 = _dft_mats(FOURIER_GROUP_DIM)
    n_fg = FOURIER_WIDTH // FOURIER_GROUP_DIM
    dft_ch = jnp.asarray(np.concatenate([_block_diag(c64, n_fg), _block_diag(s64, n_fg)], axis=1), BF16)
    cp, sp = _dft_mats(SEQ)
    cp, sp = jnp.asarray(cp, BF16), jnp.asarray(sp, BF16)
    cl, sl = _dft_mats(DEC_SEQ)
    cl, sl = jnp.asarray(cl, BF16), jnp.asarray(sl, BF16)
    tri = jnp.asarray(np.tril(np.ones((TM, TM)), -1), BF16)
    upper = jnp.asarray(np.triu(np.ones((LANES, LANES)), 1), BF16)

    in_scale = np.ones((IN_WIDTH,), np.float32)
    in_scale[OFF_G:] = 0.5
    w_in_b = (w_in * in_scale).astype(BF16)
    wa_b = w_attn_out.astype(BF16)
    wp_b = w_pool_out.astype(BF16)
    wf_b = w_fourier_out.astype(BF16)
    wo_b = w_out.astype(BF16)
    pad_r = jnp.zeros((DEPTH, D_MODEL, LANES - N_EXPERTS - N_EXPERT_GROUPS), F32)
    wr_b = jnp.concatenate([w_router_expert, w_router_group, pad_r], axis=2).astype(BF16)
    br = jnp.concatenate([b_router_expert, b_router_group,
                          jnp.zeros((DEPTH, LANES - N_EXPERTS - N_EXPERT_GROUPS), F32)],
                         axis=1).reshape(DEPTH, 1, LANES)
    n_pg = POOL_WIDTH // POOL_GROUP_DIM
    bdw = jnp.zeros((DEPTH, POOL_WIDTH, POOL_WIDTH), F32)
    for g in range(n_pg):
        lo = g * POOL_GROUP_DIM
        bdw = bdw.at[:, lo:lo + POOL_GROUP_DIM, lo:lo + POOL_GROUP_DIM].set(w_pool_group[:, g])
    bdw = bdw.astype(BF16)
    pscale = pool_scale.reshape(DEPTH, 1, POOL_WIDTH)
    qg = jnp.tile(q_norm, (1, LANES // HEAD_DIM)).reshape(DEPTH, 1, LANES)
    kg = jnp.tile(k_norm, (1, LANES // HEAD_DIM)).reshape(DEPTH, 1, LANES)
    gpre = norm_mix_pre.reshape(DEPTH, 1, D_MODEL)
    gpost = norm_mix_post.reshape(DEPTH, 1, D_MODEL)
    gffn = norm_ffn_pre.reshape(DEPTH, 1, D_MODEL)
    gfpost = norm_ffn_post.reshape(DEPTH, 1, D_MODEL)
    ck = cache_k.reshape(DEC_BATCH, DEPTH, PAST_LEN, KV_WIDTH)
    cv = cache_v.reshape(DEC_BATCH, DEPTH, PAST_LEN, KV_WIDTH)

    c_all = jnp.concatenate([c_ctx[None, :], c, jnp.zeros((MOD_ROWS - 1 - DEC_BATCH, D_MODEL), F32)], axis=0)
    mods = _mod_call(c_all, w_ada, b_ada).reshape(DEPTH, MOD_ROWS, 1, N_MOD * D_MODEL)

    xs_buf = jnp.zeros((XS_ROWS, HALF), U32)
    y = jnp.zeros((XS_ROWS, HALF), U32)
    new_k = jnp.zeros((BATCH, DEPTH, SEQ, KV_WIDTH), F32)
    new_v = jnp.zeros((BATCH, DEPTH, SEQ, KV_WIDTH), F32)
    x, hb = _prenorm_call(x_prompt.reshape(N_P, D_MODEL), x_sample.reshape(N_S, D_MODEL), mods, gpre)
    for l in range(DEPTH):
        q, k, v, xp, xc, xsn, gates, new_k, new_v = _proj_call(
            l, hb, w_in_b, qg, kg, cos_t, s1_t, s2_t, avg, dft_ch, new_k, new_v)
        attn = (_attn_prompt_call(q, k, v), _attn_sample_call(l, q, k, v, ck, cv))
        pool = (_pool_call(l, xp, bdw, pscale, SEQ, BATCH, 0, CTX_SEQ_PER_STEP),
                _pool_call(l, xp, bdw, pscale, DEC_SEQ, DEC_BATCH, N_P // DEC_SEQ, 1))
        four = (_fourier_prompt_call(cp, sp, xc, xsn), _fourier_sample_call(cl, sl, xc, xsn))
        x1, h2, pos, post, mw, cnt = _merge_call(l, x, attn, pool, four, gates, mods, gpost, gffn,
                                                 wa_b, wp_b, wf_b, wo_b, wr_b, br, tri, upper)
        runs = cnt.reshape(N_TILES, MOD_ROWS, LANES)[:, 0, :N_EXPERTS].astype(jnp.int32)
        runs = ((runs + RUN_ALIGN - 1) // RUN_ALIGN) * RUN_ALIGN
        tile_off = jnp.cumsum(runs, axis=1) - runs
        rows_e = jnp.sum(runs, axis=0)
        padded = ((rows_e + EBLK - 1) // EBLK) * EBLK
        pad_end = jnp.cumsum(padded)
        xs_off = (pad_end - padded)[None, :] + jnp.cumsum(runs, axis=0) - runs
        blk_cnt = padded // EBLK
        blk_start = (pad_end - padded) // EBLK
        n_used = pad_end[-1:] // EBLK
        run_cnt = (runs // RUN_ALIGN).reshape(N_RUNS)
        tile_off = tile_off.reshape(N_RUNS)
        xs_off = xs_off.reshape(N_RUNS)
        tile_cnt = jnp.sum(runs, axis=1) // RUN_ALIGN
        xs_buf = _dispatch_call(run_cnt, tile_off, xs_off, tile_cnt, h2, post, xs_buf)
        y = _expert_call(l, blk_start, blk_cnt, n_used, xs_buf, y,
                         w_expert_gate, w_expert_up, w_expert_down)
        outs = _combine_call(l, run_cnt, tile_off, xs_off, tile_cnt, y, x1, pos, mw, mods, gfpost,
                             gpre)
        x, hb = outs

    y_prompt = outs[0].reshape(BATCH, SEQ, D_MODEL)
    y_sample = outs[1].reshape(DEC_BATCH, DEC_SEQ, D_MODEL)
    cache_shape = (BATCH, DEPTH, SEQ, N_KV_HEADS, HEAD_DIM)
    return (y_prompt, y_sample, new_k.reshape(cache_shape), new_v.reshape(cache_shape))
```

```python
import functools
import math

import numpy as np
import jax
import jax.numpy as jnp
from jax import lax
from jax.experimental import pallas as pl
from jax.experimental.pallas import tpu as pltpu

F32 = jnp.float32
BF16 = jnp.bfloat16

D_MODEL = 1024
BATCH = 32
SEQ = 256
DEPTH = 4
DEC_BATCH = 2
DEC_SEQ = 2048
PAST_LEN = 512
GRID_W = 64
N_HEADS = 8
N_KV_HEADS = 2
HEAD_DIM = 64
KV_GROUP = N_HEADS // N_KV_HEADS
ATTN_WIDTH = N_HEADS * HEAD_DIM
KV_WIDTH = N_KV_HEADS * HEAD_DIM
ROPE_AXIS_DIM = HEAD_DIM // 2
ROPE_BASE = 10000.0
POOL_WINDOWS = (2, 4, 8, 16)
POOL_WIDTH = 256
POOL_GROUP_DIM = 64
FOURIER_WIDTH = 256
FOURIER_GROUP_DIM = 64
N_BRANCHES = 3
OFF_K = ATTN_WIDTH
OFF_P = ATTN_WIDTH + 2 * KV_WIDTH
OFF_G = OFF_P + POOL_WIDTH + FOURIER_WIDTH
IN_WIDTH = OFF_G + N_BRANCHES * D_MODEL
N_EXPERT_GROUPS = 4
EXPERTS_PER_GROUP = 8
N_EXPERTS = N_EXPERT_GROUPS * EXPERTS_PER_GROUP
TOP_K = 2
EXPERT_FF = 256
N_MOD = 6
RMS_EPS = 1e-6
QK_SCALE = HEAD_DIM ** -0.5 * math.log2(math.e)

N_P = BATCH * SEQ
N_S = DEC_BATCH * DEC_SEQ
N_TOK = N_P + N_S
LANES = 128
MOD_ROWS = 8
POOL_PAD = 16

TM = 512
PROJ_TM = 1024
N_TILES = N_TOK // TM
P_TILES = N_P // TM
S_TILES_PER_SEQ = DEC_SEQ // TM
CTX_SEQ_PER_STEP = 4
TQ_S = 128
KEY_CHUNK = 512
TF_S = 512
EBLK = 512
X_SLOTS = 4
RUN_ALIGN = 8
RUN_SHIFT = 3
RUN_BITS = 7
assert RUN_ALIGN << (RUN_BITS - 1) == TM
SORT_ROWS = TOP_K * TM + N_EXPERTS * RUN_ALIGN
DISPATCH_SLOTS = 3
assert N_TILES >= DISPATCH_SLOTS
TILE_BITS = 8
assert SORT_ROWS < RUN_ALIGN << TILE_BITS
N_RUNS = N_TILES * N_EXPERTS
N_EBLK = (N_TOK * TOP_K + N_RUNS * (RUN_ALIGN - 1) + N_EXPERTS * (EBLK - 1) + EBLK - 1) // EBLK
XS_ROWS = N_EBLK * EBLK
MOD_NT = 1536


def _dot(a, b):
    return jnp.dot(a, b, preferred_element_type=F32)


def _rms(x):
    return x * lax.rsqrt(jnp.mean(x * x, axis=-1, keepdims=True) + RMS_EPS)


def _mod_row(i):
    return jnp.where(i < P_TILES, 0, 1 + (i - P_TILES) // S_TILES_PER_SEQ)


def _rope_block(i):
    p_tiles = N_P // PROJ_TM
    per_seq = DEC_SEQ // PROJ_TM
    return jnp.where(i < p_tiles, per_seq, (i - p_tiles) % per_seq)


def _mod_kernel(c_ref, w_ref, b_ref, o_ref):
    c = c_ref[...]
    s = (c * jax.nn.sigmoid(c)).astype(BF16)
    o_ref[...] = _dot(s, w_ref[...].astype(BF16)) + b_ref[...]


def _mod_call(c_all, w_ada, b_ada):
    nt = (N_MOD * D_MODEL) // MOD_NT
    return pl.pallas_call(
        _mod_kernel,
        grid=(DEPTH, nt),
        in_specs=[
            pl.BlockSpec((MOD_ROWS, D_MODEL), lambda l, j: (0, 0)),
            pl.BlockSpec((None, D_MODEL, MOD_NT), lambda l, j: (l, 0, j)),
            pl.BlockSpec((None, 1, MOD_NT), lambda l, j: (l, 0, j)),
        ],
        out_specs=pl.BlockSpec((None, MOD_ROWS, MOD_NT), lambda l, j: (l, 0, j)),
        out_shape=jax.ShapeDtypeStruct((DEPTH, MOD_ROWS, N_MOD * D_MODEL), F32),
        name="adaln_mod",
    )(c_all, w_ada, b_ada.reshape(DEPTH, 1, N_MOD * D_MODEL))


def _prenorm(x, mod, gain):
    return (_rms(x) * gain) * (1.0 + mod[:, D_MODEL:2 * D_MODEL]) + mod[:, 0:D_MODEL]


def _prenorm_kernel(xp_ref, xs_ref, mod_ref, gpre_ref, x_ref, hb_ref):
    x = jnp.where(pl.program_id(0) < P_TILES, xp_ref[...], xs_ref[...])
    x_ref[...] = x
    hb_ref[...] = _prenorm(x, mod_ref[...], gpre_ref[...]).astype(BF16)


def _prenorm_call(x_prompt, x_sample, mods, gpre):
    row = lambda i: (i, 0)
    return pl.pallas_call(
        _prenorm_kernel,
        grid=(N_TILES,),
        in_specs=[
            pl.BlockSpec((TM, D_MODEL), lambda i: (jnp.minimum(i, P_TILES - 1), 0)),
            pl.BlockSpec((TM, D_MODEL), lambda i: (jnp.maximum(i - P_TILES, 0), 0)),
            pl.BlockSpec((None, None, 1, 2 * D_MODEL), lambda i: (0, _mod_row(i), 0, 0)),
            pl.BlockSpec((None, 1, D_MODEL), lambda i: (0, 0, 0)),
        ],
        out_specs=[pl.BlockSpec((TM, D_MODEL), row), pl.BlockSpec((TM, D_MODEL), row)],
        out_shape=[jax.ShapeDtypeStruct((N_TOK, D_MODEL), F32),
                   jax.ShapeDtypeStruct((N_TOK, D_MODEL), BF16)],
        name="prenorm",
    )(x_prompt, x_sample, mods, gpre)


def _proj_kernel(hb_ref, w_ref, qg_ref, kg_ref, cos_ref, s1_ref, s2_ref, avg_ref, dft_ref,
                 ck_in_ref, cv_in_ref,
                 q_ref, k_ref, v_ref, xp_ref, xc_ref, xs_ref, g_ref, ck_ref, cv_ref):
    del ck_in_ref, cv_in_ref
    hb = hb_ref[...]

    cos = cos_ref[...]
    s1 = s1_ref[...]
    s2 = s2_ref[...]

    def rope(t):
        return (t * cos + pltpu.roll(t, LANES - ROPE_AXIS_DIM // 2, 1) * s1
                + pltpu.roll(t, ROPE_AXIS_DIM // 2, 1) * s2)

    avg = avg_ref[...]
    for c in range(ATTN_WIDTH // LANES):
        lo = c * LANES
        q = _dot(hb, w_ref[:, lo:lo + LANES])
        ms = _dot((q * q).astype(BF16), avg)
        q = q * lax.rsqrt(ms + RMS_EPS) * qg_ref[...]
        q_ref[:, lo:lo + LANES] = (rope(q) * QK_SCALE).astype(BF16)

    kv = _dot(hb, w_ref[:, OFF_K:OFF_K + 2 * KV_WIDTH])
    k = kv[:, 0:KV_WIDTH]
    ms = _dot((k * k).astype(BF16), avg)
    k = rope(k * lax.rsqrt(ms + RMS_EPS) * kg_ref[...])
    v = kv[:, KV_WIDTH:]
    k_ref[...] = k
    v_ref[...] = v

    @pl.when(pl.program_id(0) < N_P // PROJ_TM)
    def _():
        for s in range(PROJ_TM // SEQ):
            ck_ref[s] = k[s * SEQ:(s + 1) * SEQ]
            cv_ref[s] = v[s * SEQ:(s + 1) * SEQ]

    pf = _dot(hb, w_ref[:, OFF_P:OFF_G])
    xp_ref[...] = pf[:, 0:POOL_WIDTH]
    cs = _dot(pf[:, POOL_WIDTH:].astype(BF16), dft_ref[...])
    xc_ref[...] = cs[:, 0:FOURIER_WIDTH].astype(BF16)
    xs_ref[...] = cs[:, FOURIER_WIDTH:].astype(BF16)

    for c in range(N_BRANCHES):
        lo = OFF_G + c * D_MODEL
        g_ref[:, c * D_MODEL:(c + 1) * D_MODEL] = _dot(hb, w_ref[:, lo:lo + D_MODEL]).astype(BF16)


def _proj_call(l, hb, w_in, qg, kg, cos_t, s1_t, s2_t, avg, dft, cache_k, cache_v):
    row = lambda i: (i, 0)
    const2 = lambda i: (0, 0)
    per_layer = lambda i: (l, 0, 0)
    rope_spec = pl.BlockSpec((PROJ_TM, LANES), lambda i: (_rope_block(i), 0))
    cache_spec = pl.BlockSpec((PROJ_TM // SEQ, None, SEQ, KV_WIDTH),
                              lambda i: (jnp.minimum(i, N_P // PROJ_TM - 1), l, 0, 0))
    cache_shape = jax.ShapeDtypeStruct((BATCH, DEPTH, SEQ, KV_WIDTH), F32)
    return pl.pallas_call(
        _proj_kernel,
        grid=(N_TOK // PROJ_TM,),
        in_specs=[
            pl.BlockSpec((PROJ_TM, D_MODEL), row),
            pl.BlockSpec((None, D_MODEL, IN_WIDTH), per_layer),
            pl.BlockSpec((None, 1, LANES), per_layer),
            pl.BlockSpec((None, 1, LANES), per_layer),
            rope_spec, rope_spec, rope_spec,
            pl.BlockSpec((LANES, LANES), const2),
            pl.BlockSpec((FOURIER_WIDTH, 2 * FOURIER_WIDTH), const2),
            pl.BlockSpec(memory_space=pl.ANY),
            pl.BlockSpec(memory_space=pl.ANY),
        ],
        out_specs=[
            pl.BlockSpec((PROJ_TM, ATTN_WIDTH), row),
            pl.BlockSpec((PROJ_TM, KV_WIDTH), row),
            pl.BlockSpec((PROJ_TM, KV_WIDTH), row),
            pl.BlockSpec((PROJ_TM, POOL_WIDTH), row),
            pl.BlockSpec((PROJ_TM, FOURIER_WIDTH), row),
            pl.BlockSpec((PROJ_TM, FOURIER_WIDTH), row),
            pl.BlockSpec((PROJ_TM, N_BRANCHES * D_MODEL), row),
            cache_spec, cache_spec,
        ],
        out_shape=[
            jax.ShapeDtypeStruct((N_TOK, ATTN_WIDTH), BF16),
            jax.ShapeDtypeStruct((N_TOK, KV_WIDTH), F32),
            jax.ShapeDtypeStruct((N_TOK, KV_WIDTH), F32),
            jax.ShapeDtypeStruct((N_TOK, POOL_WIDTH), F32),
            jax.ShapeDtypeStruct((N_TOK, FOURIER_WIDTH), BF16),
            jax.ShapeDtypeStruct((N_TOK, FOURIER_WIDTH), BF16),
            jax.ShapeDtypeStruct((N_TOK, N_BRANCHES * D_MODEL), BF16),
            cache_shape, cache_shape,
        ],
        input_output_aliases={9: 7, 10: 8},
        name="proj",
    )(hb, w_in, qg, kg, cos_t, s1_t, s2_t, avg, dft, cache_k, cache_v)


def _attn_kernel(*refs, n_parts, tq):
    q_ref = refs[0]
    kv_refs = refs[1:1 + 2 * n_parts]
    o_ref = refs[-1]
    single_chunk = n_parts == 1 and kv_refs[0].shape[0] <= KEY_CHUNK
    outs = []
    for j in range(N_KV_HEADS):
        lo = j * HEAD_DIM
        qs = jnp.concatenate(
            [q_ref[:, (KV_GROUP * j + g) * HEAD_DIM:(KV_GROUP * j + g + 1) * HEAD_DIM]
             for g in range(KV_GROUP)], axis=0)
        m = acc = None
        for p in range(n_parts):
            k_ref, v_ref = kv_refs[2 * p], kv_refs[2 * p + 1]
            for c0 in range(0, k_ref.shape[0], KEY_CHUNK):
                c1 = min(c0 + KEY_CHUNK, k_ref.shape[0])
                kc = k_ref[c0:c1, lo:lo + HEAD_DIM].astype(BF16)
                s = lax.dot_general(qs, kc, (((1,), (1,)), ((), ())), preferred_element_type=F32)
                mc = jnp.max(s, axis=1, keepdims=True)
                m_new = mc if m is None else jnp.maximum(m, mc)
                e = jnp.exp2(s - m_new)
                vc = v_ref[c0:c1, lo:lo + HEAD_DIM].astype(BF16)
                if single_chunk:
                    den = jnp.sum(e, axis=1, keepdims=True)
                else:
                    vc = jnp.concatenate([vc, jnp.ones((c1 - c0, HEAD_DIM), BF16)], axis=1)
                pv = _dot(e.astype(BF16), vc)
                acc = pv if m is None else acc * jnp.exp2(m - m_new) + pv
                m = m_new
        o = acc / den if single_chunk else acc[:, 0:HEAD_DIM] / acc[:, HEAD_DIM:2 * HEAD_DIM]
        outs.extend(o[g * tq:(g + 1) * tq] for g in range(KV_GROUP))
    o_ref[...] = jnp.concatenate(outs, axis=1).astype(BF16)


def _attn_prompt_call(q, k, v):
    blk = lambda b: (b, 0)
    return pl.pallas_call(
        functools.partial(_attn_kernel, n_parts=1, tq=SEQ),
        grid=(BATCH,),
        in_specs=[
            pl.BlockSpec((SEQ, ATTN_WIDTH), blk),
            pl.BlockSpec((SEQ, KV_WIDTH), blk),
            pl.BlockSpec((SEQ, KV_WIDTH), blk),
        ],
        out_specs=pl.BlockSpec((SEQ, ATTN_WIDTH), blk),
        out_shape=jax.ShapeDtypeStruct((N_P, ATTN_WIDTH), BF16),
        name="attn_context",
    )(q, k, v)


def _attn_sample_call(l, q, k, v, cache_k, cache_v):
    nq = DEC_SEQ // TQ_S
    qrow = lambda b, i: (N_P // TQ_S + b * nq + i, 0)
    seq = lambda b, i: (N_P // DEC_SEQ + b, 0)
    cache = lambda b, i: (b, l, 0, 0)
    return pl.pallas_call(
        functools.partial(_attn_kernel, n_parts=2, tq=TQ_S),
        grid=(DEC_BATCH, nq),
        in_specs=[
            pl.BlockSpec((TQ_S, ATTN_WIDTH), qrow),
            pl.BlockSpec((None, None, PAST_LEN, KV_WIDTH), cache),
            pl.BlockSpec((None, None, PAST_LEN, KV_WIDTH), cache),
            pl.BlockSpec((DEC_SEQ, KV_WIDTH), seq),
            pl.BlockSpec((DEC_SEQ, KV_WIDTH), seq),
        ],
        out_specs=pl.BlockSpec((TQ_S, ATTN_WIDTH), lambda b, i: (b * nq + i, 0)),
        out_shape=jax.ShapeDtypeStruct((N_S, ATTN_WIDTH), BF16),
        name="attn_latent",
    )(q, cache_k, cache_v, k, v)


def _pool_kernel(xp_ref, bdw_ref, sc_ref, o_ref, pad_ref, *, seq_len, n_seq):
    for s in range(n_seq):
        _pool_sequence(xp_ref, bdw_ref, sc_ref, o_ref, pad_ref, s * seq_len, seq_len)


def _pool_sequence(xp_ref, bdw_ref, sc_ref, o_ref, pad_ref, row0, seq_len):
    half = POOL_WIDTH // 2
    zeros = jnp.zeros((POOL_PAD, POOL_WIDTH), F32)
    pad_ref[0:POOL_PAD, :] = zeros
    pad_ref[POOL_PAD + seq_len:, :] = zeros
    pad_ref[POOL_PAD:POOL_PAD + seq_len, :] = xp_ref[row0:row0 + seq_len, :]
    chunk = min(seq_len, 256)
    lane = lax.broadcasted_iota(jnp.int32, (chunk, half), 1)
    first = lane < POOL_GROUP_DIM
    for c in range(seq_len // chunk):
        base = c * chunk
        t = lax.broadcasted_iota(jnp.int32, (chunk, half), 0) + base

        def sh(j, lo):
            return pad_ref[POOL_PAD + base + j:POOL_PAD + base + j + chunk, lo:lo + half]

        def cnt(w):
            return (jnp.minimum(t + w // 2, seq_len) - jnp.maximum(t - w // 2, 0)).astype(F32)

        xa = sh(0, 0)
        w2 = sh(-1, 0) + xa
        w4 = w2 + sh(-2, 0) + sh(1, 0)
        xb = sh(0, half)
        w8 = xb
        for j in (-4, -3, -2, -1, 1, 2, 3):
            w8 = w8 + sh(j, half)
        w16 = w8
        for j in (-8, -7, -6, -5, 4, 5, 6, 7):
            w16 = w16 + sh(j, half)
        pa = jnp.where(first, w2 / cnt(2), w4 / cnt(4)) - xa
        pb = jnp.where(first, w8 / cnt(8), w16 / cnt(16)) - xb
        pooled = jnp.concatenate([pa, pb], axis=1).astype(BF16)
        o_ref[row0 + base:row0 + base + chunk, :] = (
            _dot(pooled, bdw_ref[...]) * sc_ref[...]).astype(BF16)


def _pool_call(l, xp, bdw, scale, seq_len, n_seq, blk0, seq_per_step):
    per_layer = lambda b: (l, 0, 0)
    rows = seq_len * seq_per_step
    return pl.pallas_call(
        functools.partial(_pool_kernel, seq_len=seq_len, n_seq=seq_per_step),
        grid=(n_seq // seq_per_step,),
        in_specs=[
            pl.BlockSpec((rows, POOL_WIDTH), lambda b: (blk0 + b, 0)),
            pl.BlockSpec((None, POOL_WIDTH, POOL_WIDTH), per_layer),
            pl.BlockSpec((None, 1, POOL_WIDTH), per_layer),
        ],
        out_specs=pl.BlockSpec((rows, POOL_WIDTH), lambda b: (b, 0)),
        out_shape=jax.ShapeDtypeStruct((n_seq * seq_len, POOL_WIDTH), BF16),
        scratch_shapes=[pltpu.VMEM((seq_len + 2 * POOL_PAD, POOL_WIDTH), F32)],
        name="pool_%d" % seq_len,
    )(xp, bdw, scale)


def _fourier_kernel(c_ref, s_ref, xc_ref, xs_ref, o_ref, *, scale, n_seq=1):
    seq_len = xc_ref.shape[0] // n_seq
    out_len = o_ref.shape[0] // n_seq
    for s in range(n_seq):
        rows = slice(s * seq_len, (s + 1) * seq_len)
        y = _dot(c_ref[...], xc_ref[rows, :]) - _dot(s_ref[...], xs_ref[rows, :])
        o_ref[s * out_len:(s + 1) * out_len, :] = (y * scale).astype(BF16)


def _fourier_prompt_call(cmat, smat, xc, xs):
    blk = lambda b: (b, 0)
    const2 = lambda b: (0, 0)
    rows = SEQ * CTX_SEQ_PER_STEP
    return pl.pallas_call(
        functools.partial(_fourier_kernel, scale=1.0 / math.sqrt(SEQ * FOURIER_GROUP_DIM),
                          n_seq=CTX_SEQ_PER_STEP),
        grid=(BATCH // CTX_SEQ_PER_STEP,),
        in_specs=[
            pl.BlockSpec((SEQ, SEQ), const2),
            pl.BlockSpec((SEQ, SEQ), const2),
            pl.BlockSpec((rows, FOURIER_WIDTH), blk),
            pl.BlockSpec((rows, FOURIER_WIDTH), blk),
        ],
        out_specs=pl.BlockSpec((rows, FOURIER_WIDTH), blk),
        out_shape=jax.ShapeDtypeStruct((N_P, FOURIER_WIDTH), BF16),
        name="fourier_context",
    )(cmat, smat, xc, xs)


def _fourier_sample_call(cmat, smat, xc, xs):
    nt = DEC_SEQ // TF_S
    rows = lambda b, i: (i, 0)
    seq = lambda b, i: (N_P // DEC_SEQ + b, 0)
    out = lambda b, i: (b * nt + i, 0)
    return pl.pallas_call(
        functools.partial(_fourier_kernel, scale=1.0 / math.sqrt(DEC_SEQ * FOURIER_GROUP_DIM)),
        grid=(DEC_BATCH, nt),
        in_specs=[
            pl.BlockSpec((TF_S, DEC_SEQ), rows),
            pl.BlockSpec((TF_S, DEC_SEQ), rows),
            pl.BlockSpec((DEC_SEQ, FOURIER_WIDTH), seq),
            pl.BlockSpec((DEC_SEQ, FOURIER_WIDTH), seq),
        ],
        out_specs=pl.BlockSpec((TF_S, FOURIER_WIDTH), out),
        out_shape=jax.ShapeDtypeStruct((N_S, FOURIER_WIDTH), BF16),
        name="fourier_latent",
    )(cmat, smat, xc, xs)


def _merge_kernel(x_ref, attn_p_ref, attn_s_ref, pool_p_ref, pool_s_ref, four_p_ref, four_s_ref,
                  g_ref, mod_ref, gpost_ref, gffn_ref,
                  wa_ref, wp_ref, wf_ref, wo_ref, wr_ref, br_ref, tri_ref, upper_ref,
                  x1_ref, h2_ref, pos_ref, post_ref, mw_ref, cnt_ref):
    i = pl.program_id(0)
    mod = mod_ref[...]
    g1 = mod[:, 2 * D_MODEL:3 * D_MODEL]
    sh2 = mod[:, 3 * D_MODEL:4 * D_MODEL]
    sc2 = mod[:, 4 * D_MODEL:5 * D_MODEL]

    def gate2(c):
        return jnp.tanh(g_ref[:, c * D_MODEL:(c + 1) * D_MODEL].astype(F32)) + 1.0

    def branch(p_ref, s_ref):
        return jnp.where(i < P_TILES, p_ref[...], s_ref[...])

    merged = gate2(0) * _dot(branch(attn_p_ref, attn_s_ref), wa_ref[...])
    merged = merged + gate2(1) * _dot(branch(pool_p_ref, pool_s_ref), wp_ref[...])
    merged = merged + gate2(2) * _dot(branch(four_p_ref, four_s_ref), wf_ref[...])
    mix = _dot((0.5 * merged).astype(BF16), wo_ref[...])
    x1 = x_ref[...] + g1 * (_rms(mix) * gpost_ref[...])
    x1_ref[...] = x1
    h2 = (_rms(x1) * gffn_ref[...]) * (1.0 + sc2) + sh2
    h2b = h2.astype(BF16)
    h2_ref[...] = h2b

    logits = _dot(h2b, wr_ref[...]) + br_ref[...]
    lane = lax.broadcasted_iota(jnp.int32, (TM, LANES), 1)
    lanef = lane.astype(F32)
    neg = jnp.float32(-3e38)
    big = jnp.float32(1e9)
    is_g = (lane >= N_EXPERTS) & (lane < N_EXPERTS + N_EXPERT_GROUPS)
    lg = jnp.where(is_g, logits, neg)
    gmax = jnp.max(lg, axis=1, keepdims=True)
    g_sel = jnp.min(jnp.where(lg == gmax, lanef - N_EXPERTS, big), axis=1, keepdims=True)
    p_g = 1.0 / jnp.sum(jnp.where(is_g, jnp.exp(logits - gmax), 0.0), axis=1, keepdims=True)
    grp = lax.shift_right_logical(lane, int(math.log2(EXPERTS_PER_GROUP))).astype(F32)
    in_grp = (lane < N_EXPERTS) & (grp == g_sel)
    le = jnp.where(in_grp, logits, neg)
    v1 = jnp.max(le, axis=1, keepdims=True)
    i1 = jnp.min(jnp.where(le == v1, lanef, big), axis=1, keepdims=True)
    le2 = jnp.where(lanef == i1, neg, le)
    v2 = jnp.max(le2, axis=1, keepdims=True)
    i2 = jnp.min(jnp.where(le2 == v2, lanef, big), axis=1, keepdims=True)
    e21 = jnp.exp(v2 - v1)
    w1 = p_g / (1.0 + e21)
    w2 = p_g * e21 / (1.0 + e21)

    oh1 = (lanef == i1).astype(F32)
    oh2 = (lanef == i2).astype(F32)
    ohb = (oh1 + oh2).astype(BF16)
    before = _dot(tri_ref[...], ohb)
    cnt = _dot(jnp.ones((MOD_ROWS, TM), BF16), ohb)
    cnt_pad = (lax.shift_right_logical(cnt.astype(jnp.int32) + (RUN_ALIGN - 1), RUN_SHIFT)
               * RUN_ALIGN).astype(F32)
    run_off = _dot(cnt_pad.astype(BF16), upper_ref[...])
    slot = run_off[0:1, :] + before
    p1 = jnp.sum(slot * oh1, axis=1, keepdims=True)
    p2 = jnp.sum(slot * oh2, axis=1, keepdims=True)
    cnt_ref[...] = cnt

    pos = jnp.where(lane == 0, p1, jnp.where(lane == 1, p2, 0.0))
    pos_ref[...] = pos.astype(jnp.int32)
    post_ref[...] = pos.T[0:MOD_ROWS, :].astype(jnp.int32)
    mw_ref[...] = jnp.where(lane == 0, w1, w2)


def _merge_call(l, x, attn, pool, four, gates, mods, gpost, gffn, wa, wp, wf, wo, wr, br, tri,
                upper):
    row = lambda i: (i, 0)
    prow = lambda i: (jnp.minimum(i, P_TILES - 1), 0)
    srow = lambda i: (jnp.maximum(i - P_TILES, 0), 0)
    const2 = lambda i: (0, 0)
    per_layer = lambda i: (l, 0, 0)
    return pl.pallas_call(
        _merge_kernel,
        grid=(N_TILES,),
        in_specs=[
            pl.BlockSpec((TM, D_MODEL), row),
            pl.BlockSpec((TM, ATTN_WIDTH), prow),
            pl.BlockSpec((TM, ATTN_WIDTH), srow),
            pl.BlockSpec((TM, POOL_WIDTH), prow),
            pl.BlockSpec((TM, POOL_WIDTH), srow),
            pl.BlockSpec((TM, FOURIER_WIDTH), prow),
            pl.BlockSpec((TM, FOURIER_WIDTH), srow),
            pl.BlockSpec((TM, N_BRANCHES * D_MODEL), row),
            pl.BlockSpec((None, None, 1, N_MOD * D_MODEL), lambda i: (l, _mod_row(i), 0, 0)),
            pl.BlockSpec((None, 1, D_MODEL), per_layer),
            pl.BlockSpec((None, 1, D_MODEL), per_layer),
            pl.BlockSpec((None, ATTN_WIDTH, D_MODEL), per_layer),
            pl.BlockSpec((None, POOL_WIDTH, D_MODEL), per_layer),
            pl.BlockSpec((None, FOURIER_WIDTH, D_MODEL), per_layer),
            pl.BlockSpec((None, D_MODEL, D_MODEL), per_layer),
            pl.BlockSpec((None, D_MODEL, LANES), per_layer),
            pl.BlockSpec((None, 1, LANES), per_layer),
            pl.BlockSpec((TM, TM), const2),
            pl.BlockSpec((LANES, LANES), const2),
        ],
        out_specs=[
            pl.BlockSpec((TM, D_MODEL), row),
            pl.BlockSpec((TM, D_MODEL), row),
            pl.BlockSpec((TM, LANES), row),
            pl.BlockSpec((MOD_ROWS, TM), lambda i: (0, i)),
            pl.BlockSpec((TM, LANES), row),
            pl.BlockSpec((MOD_ROWS, LANES), row),
        ],
        out_shape=[
            jax.ShapeDtypeStruct((N_TOK, D_MODEL), F32),
            jax.ShapeDtypeStruct((N_TOK, D_MODEL), BF16),
            jax.ShapeDtypeStruct((N_TOK, LANES), jnp.int32),
            jax.ShapeDtypeStruct((MOD_ROWS, N_TOK), jnp.int32),
            jax.ShapeDtypeStruct((N_TOK, LANES), F32),
            jax.ShapeDtypeStruct((N_TILES * MOD_ROWS, LANES), F32),
        ],
        name="merge_router",
    )(x, attn[0], attn[1], pool[0], pool[1], four[0], four[1], gates, mods, gpost, gffn,
      wa, wp, wf, wo, wr, br, tri, upper)


HALF = D_MODEL // 2
U32 = jnp.uint32
HI_MASK = 0xFFFF0000


def _pack_rows(x):
    lo = lax.bitcast_convert_type(x[:, :HALF], U32)
    hi = lax.bitcast_convert_type(x[:, HALF:], U32)
    return lax.shift_right_logical(lo, U32(16)) | (hi & U32(HI_MASK))


def _unpack_rows(u):
    lo = lax.bitcast_convert_type(lax.shift_left(u, U32(16)), F32)
    hi = lax.bitcast_convert_type(u & U32(HI_MASK), F32)
    return jnp.concatenate([lo.astype(BF16), hi.astype(BF16)], axis=1)


def _rows(ref, start, size):
    return ref.at[pl.ds(pl.multiple_of(start, RUN_ALIGN), size)]


def _start_tile_copies(tile, n_ref, trow_ref, xrow_ref, make_copy):
    for bit in range(RUN_BITS):
        lst = bit * N_TILES + tile

        def body(k, carry, bit=bit, lst=lst):
            j = lst * N_EXPERTS + k
            make_copy(trow_ref[j], xrow_ref[j], RUN_ALIGN << bit).start()
            return carry

        lax.fori_loop(0, n_ref[lst], body, 0)


def _wait_tile_copies(total, src_ref, dst_ref, sem):
    for b in range(TILE_BITS):
        size = RUN_ALIGN << b

        @pl.when((lax.shift_right_logical(total, b) & 1) == 1)
        def _(size=size):
            pltpu.make_async_copy(_rows(src_ref, 0, size), _rows(dst_ref, 0, size), sem).wait()


def _dispatch_kernel(n_ref, trow_ref, xrow_ref, tsum_ref, h_ref, post_ref, xs_in_ref, xs_ref,
                     sorted_ref, sem):
    del xs_in_ref
    i = pl.program_id(0)
    slot = lax.rem(i, DISPATCH_SLOTS)
    rows = lax.broadcasted_iota(jnp.int32, (SORT_ROWS, TM), 0)
    p = post_ref[...]
    perm = jnp.where(rows == p[0:1, :], 1.0, jnp.where(rows == p[1:2, :], 1.0, 0.0)).astype(BF16)
    sorted_ref[slot] = _pack_rows(_dot(perm, h_ref[...]))

    def wait(tile, slot):
        _wait_tile_copies(tsum_ref[tile], sorted_ref.at[slot], xs_ref, sem.at[slot])

    lag = DISPATCH_SLOTS - 1

    @pl.when(i >= lag)
    def _():
        wait(i - lag, lax.rem(i + 1, DISPATCH_SLOTS))

    _start_tile_copies(i, n_ref, trow_ref, xrow_ref, lambda t, x, size: pltpu.make_async_copy(
        _rows(sorted_ref.at[slot], t, size), _rows(xs_ref, x, size), sem.at[slot]))

    @pl.when(i == pl.num_programs(0) - 1)
    def _():
        for back in reversed(range(lag)):
            wait(i - back, lax.rem(i - back, DISPATCH_SLOTS))


def _dispatch_call(n_list, trow, xrow, tsum, h2, post, xs):
    return pl.pallas_call(
        _dispatch_kernel,
        grid_spec=pltpu.PrefetchScalarGridSpec(
            num_scalar_prefetch=4,
            grid=(N_TILES,),
            in_specs=[
                pl.BlockSpec((TM, D_MODEL), lambda i, *_: (i, 0)),
                pl.BlockSpec((MOD_ROWS, TM), lambda i, *_: (0, i)),
                pl.BlockSpec(memory_space=pl.ANY),
            ],
            out_specs=pl.BlockSpec(memory_space=pl.ANY),
            scratch_shapes=[pltpu.VMEM((DISPATCH_SLOTS, SORT_ROWS, HALF), U32),
                            pltpu.SemaphoreType.DMA((DISPATCH_SLOTS,))],
        ),
        out_shape=jax.ShapeDtypeStruct((XS_ROWS, HALF), U32),
        input_output_aliases={6: 0},
        compiler_params=pltpu.CompilerParams(dimension_semantics=("arbitrary",)),
        name="moe_dispatch",
    )(n_list, trow, xrow, tsum, h2, post, xs)


def _expert_kernel(bstart_ref, bcnt_ref, nu_ref, wg_ref, wu_ref, wd_ref, xs_ref, y_prev_ref, y_ref,
                   wgb, wub, wdb, xbuf, ybuf, xsem, ysem):
    del y_prev_ref
    e = pl.program_id(0)
    n_used = nu_ref[0]

    def x_copy(g, slot):
        rows = pl.ds(pl.multiple_of(g * EBLK, EBLK), EBLK)
        return pltpu.make_async_copy(xs_ref.at[rows], xbuf.at[slot], xsem.at[slot])

    def y_copy(g, slot):
        rows = pl.ds(pl.multiple_of(g * EBLK, EBLK), EBLK)
        return pltpu.make_async_copy(ybuf.at[slot], y_ref.at[rows], ysem.at[slot])

    @pl.when(e == 0)
    def _():
        for g in range(X_SLOTS - 1):
            @pl.when(g < n_used)
            def _(g=g):
                x_copy(g, g).start()

    wgb[...] = wg_ref[...].astype(BF16)
    wub[...] = wu_ref[...].astype(BF16)
    wdb[...] = wd_ref[...].astype(BF16)
    first = bstart_ref[e]

    def block(j, carry):
        g = first + j
        slot = lax.rem(g, 2)
        xslot = lax.rem(g, X_SLOTS)
        x_copy(g, xslot).wait()

        @pl.when(g + (X_SLOTS - 1) < n_used)
        def _():
            x_copy(g + (X_SLOTS - 1), lax.rem(g + (X_SLOTS - 1), X_SLOTS)).start()

        xb = _unpack_rows(xbuf[xslot])
        gate = _dot(xb, wgb[...])
        up = _dot(xb, wub[...])
        act = (gate * jax.nn.sigmoid(gate)) * up
        y = _dot(act.astype(BF16), wdb[...])

        @pl.when(g >= 2)
        def _():
            y_copy(g - 2, slot).wait()

        ybuf[slot] = _pack_rows(y.astype(BF16).astype(F32))
        y_copy(g, slot).start()
        return carry

    lax.fori_loop(0, bcnt_ref[e], block, 0)

    @pl.when(e == pl.num_programs(0) - 1)
    def _():
        @pl.when(n_used >= 2)
        def _():
            y_copy(n_used - 2, lax.rem(n_used, 2)).wait()

        y_copy(n_used - 1, lax.rem(n_used - 1, 2)).wait()


def _expert_call(l, blk_start, blk_cnt, n_used, xs, y_prev, w_gate, w_up, w_down):
    wsel = lambda e, *_: (l, e, 0, 0)
    return pl.pallas_call(
        _expert_kernel,
        grid_spec=pltpu.PrefetchScalarGridSpec(
            num_scalar_prefetch=3,
            grid=(N_EXPERTS,),
            in_specs=[
                pl.BlockSpec((None, None, D_MODEL, EXPERT_FF), wsel),
                pl.BlockSpec((None, None, D_MODEL, EXPERT_FF), wsel),
                pl.BlockSpec((None, None, EXPERT_FF, D_MODEL), wsel),
                pl.BlockSpec(memory_space=pl.ANY),
                pl.BlockSpec(memory_space=pl.ANY),
            ],
            out_specs=pl.BlockSpec(memory_space=pl.ANY),
            scratch_shapes=[
                pltpu.VMEM((D_MODEL, EXPERT_FF), BF16),
                pltpu.VMEM((D_MODEL, EXPERT_FF), BF16),
                pltpu.VMEM((EXPERT_FF, D_MODEL), BF16),
                pltpu.VMEM((X_SLOTS, EBLK, HALF), U32),
                pltpu.VMEM((2, EBLK, HALF), U32),
                pltpu.SemaphoreType.DMA((X_SLOTS,)),
                pltpu.SemaphoreType.DMA((2,)),
            ],
        ),
        out_shape=jax.ShapeDtypeStruct((XS_ROWS, HALF), U32),
        input_output_aliases={7: 0},
        compiler_params=pltpu.CompilerParams(dimension_semantics=("arbitrary",)),
        name="moe_experts",
    )(blk_start, blk_cnt, n_used, w_gate, w_up, w_down, xs, y_prev)


def _combine_kernel(n_ref, trow_ref, xrow_ref, tsum_ref, y_ref, x1_ref, pos_ref, mw_ref, g2_ref,
                    gpost_ref, *rest, has_next):
    if has_next:
        nmod_ref, ngpre_ref, o_ref, hb_ref, ybuf, sem = rest
    else:
        o_ref, os_ref, ybuf, sem = rest
    i = pl.program_id(0)
    slot = lax.rem(i, 2)

    def start(tile, slot):
        _start_tile_copies(tile, n_ref, trow_ref, xrow_ref, lambda t, x, size: pltpu.make_async_copy(
            _rows(y_ref, x, size), _rows(ybuf.at[slot], t, size), sem.at[slot]))

    @pl.when(i == 0)
    def _():
        ybuf[...] = jnp.zeros_like(ybuf)
        start(0, 0)

    @pl.when(i + 1 < pl.num_programs(0))
    def _():
        start(i + 1, 1 - slot)

    _wait_tile_copies(tsum_ref[i], y_ref, ybuf.at[slot], sem.at[slot])

    pos = pos_ref[...]
    mw = mw_ref[...]
    cols = lax.broadcasted_iota(jnp.int32, (TM, SORT_ROWS), 1)
    qw = (jnp.where(cols == pos[:, 0:1], mw[:, 0:1], 0.0)
          + jnp.where(cols == pos[:, 1:2], mw[:, 1:2], 0.0)).astype(BF16)
    ffn = _dot(qw, _unpack_rows(ybuf[slot]))
    x2 = x1_ref[...] + g2_ref[...] * (_rms(ffn) * gpost_ref[...])
    if has_next:
        o_ref[...] = x2
        hb_ref[...] = _prenorm(x2, nmod_ref[...], ngpre_ref[...]).astype(BF16)
    else:
        @pl.when(i < P_TILES)
        def _():
            o_ref[...] = x2

        os_ref[...] = x2


def _combine_call(l, n_list, trow, xrow, tsum, y, x1, pos, mw, mods, gpost, gpre):
    has_next = l + 1 < DEPTH
    row = lambda i, *_: (i, 0)
    in_specs = [
        pl.BlockSpec(memory_space=pl.ANY),
        pl.BlockSpec((TM, D_MODEL), row),
        pl.BlockSpec((TM, LANES), row),
        pl.BlockSpec((TM, LANES), row),
        pl.BlockSpec((None, None, 1, D_MODEL), lambda i, *_: (l, _mod_row(i), 0, N_MOD - 1)),
        pl.BlockSpec((None, 1, D_MODEL), lambda i, *_: (l, 0, 0)),
    ]
    args = [n_list, trow, xrow, tsum, y, x1, pos, mw, mods, gpost]
    if has_next:
        in_specs += [
            pl.BlockSpec((None, None, 1, 2 * D_MODEL), lambda i, *_: (l + 1, _mod_row(i), 0, 0)),
            pl.BlockSpec((None, 1, D_MODEL), lambda i, *_: (l + 1, 0, 0)),
        ]
        args += [mods, gpre]
        out_specs = [pl.BlockSpec((TM, D_MODEL), row), pl.BlockSpec((TM, D_MODEL), row)]
        out_shape = [jax.ShapeDtypeStruct((N_TOK, D_MODEL), F32),
                     jax.ShapeDtypeStruct((N_TOK, D_MODEL), BF16)]
    else:
        out_specs = [
            pl.BlockSpec((TM, D_MODEL), lambda i, *_: (jnp.minimum(i, P_TILES - 1), 0)),
            pl.BlockSpec((TM, D_MODEL), lambda i, *_: (jnp.maximum(i - P_TILES, 0), 0)),
        ]
        out_shape = [jax.ShapeDtypeStruct((N_P, D_MODEL), F32),
                     jax.ShapeDtypeStruct((N_S, D_MODEL), F32)]
    return pl.pallas_call(
        functools.partial(_combine_kernel, has_next=has_next),
        grid_spec=pltpu.PrefetchScalarGridSpec(
            num_scalar_prefetch=4,
            grid=(N_TILES,),
            in_specs=in_specs,
            out_specs=out_specs,
            scratch_shapes=[pltpu.VMEM((2, SORT_ROWS, HALF), U32), pltpu.SemaphoreType.DMA((2,))],
        ),
        out_shape=out_shape,
        compiler_params=pltpu.CompilerParams(dimension_semantics=("arbitrary",)),
        name="moe_combine",
    )(*args)


def _dft_mats(n):
    k = np.arange(n, dtype=np.int64)
    ang = 2.0 * np.pi * ((k[:, None] * k[None, :]) % n).astype(np.float64) / n
    return np.cos(ang), np.sin(ang)


def _block_diag(m, reps):
    n = m.shape[0]
    out = np.zeros((n * reps, n * reps), m.dtype)
    for r in range(reps):
        out[r * n:(r + 1) * n, r * n:(r + 1) * n] = m
    return out


def _rope_tables():
    t = np.arange(DEC_SEQ)
    pos = np.stack([t // GRID_W, t % GRID_W], axis=1).astype(np.float64)
    n_freq = ROPE_AXIS_DIM // 2
    inv = ROPE_BASE ** (-np.arange(n_freq, dtype=np.float64) * 2.0 / ROPE_AXIS_DIM)
    ang = pos[:, :, None] * inv[None, None, :]
    cos = np.cos(ang)
    sin = np.sin(ang)
    zero = np.zeros_like(sin[:, 0])
    cos_h = np.concatenate([cos[:, 0], cos[:, 0], cos[:, 1], cos[:, 1]], axis=1)
    s1_h = np.concatenate([-sin[:, 0], zero, -sin[:, 1], zero], axis=1)
    s2_h = np.concatenate([zero, sin[:, 0], zero, sin[:, 1]], axis=1)
    reps = LANES // HEAD_DIM

    def table(a, ident):
        a = np.tile(a, (1, reps))
        pad = np.full((PROJ_TM, LANES), ident, np.float64)
        return jnp.asarray(np.concatenate([a, pad], axis=0), F32)

    return table(cos_h, 1.0), table(s1_h, 0.0), table(s2_h, 0.0)


def _copy_lists(run_cnt, tile_off, xs_off):
    bit = jnp.arange(RUN_BITS, dtype=jnp.int32)[:, None, None]
    has = (run_cnt[None] >> bit) & 1
    before = ((run_cnt[None] >> (bit + 1)) << (bit + 1)) * RUN_ALIGN
    slot = jnp.cumsum(has, axis=-1) - has
    hit = (has[..., None] == 1) & (slot[..., None] == jnp.arange(N_EXPERTS, dtype=jnp.int32))

    def compact(rows):
        return jnp.sum(jnp.where(hit, rows[..., None], 0), axis=-2).reshape(-1)

    return (jnp.sum(has, axis=-1).reshape(-1), compact(tile_off[None] + before),
            compact(xs_off[None] + before))


def kernel(x_prompt, x_sample, cache_k, cache_v, c, c_ctx, w_ada, b_ada, norm_mix_pre,
           norm_mix_post, norm_ffn_pre, norm_ffn_post, w_in, q_norm, k_norm, w_attn_out,
           w_pool_group, pool_scale, w_pool_out, w_fourier_out, w_out, w_router_group,
           b_router_group, w_router_expert, b_router_expert, w_expert_gate, w_expert_up,
           w_expert_down):
    cos_t, s1_t, s2_t = _rope_tables()
    avg = jnp.asarray(_block_diag(np.full((HEAD_DIM, HEAD_DIM), 1.0 / HEAD_DIM), LANES // HEAD_DIM), BF16)
    c64, s64 = _dft_mats(FOURIER_GROUP_DIM)
    n_fg = FOURIER_WIDTH // FOURIER_GROUP_DIM
    dft_ch = jnp.asarray(np.concatenate([_block_diag(c64, n_fg), _block_diag(s64, n_fg)], axis=1), BF16)
    cp, sp = _dft_mats(SEQ)
    cp, sp = jnp.asarray(cp, BF16), jnp.asarray(sp, BF16)
    cl, sl = _dft_mats(DEC_SEQ)
    cl, sl = jnp.asarray(cl, BF16), jnp.asarray(sl, BF16)
    tri = jnp.asarray(np.tril(np.ones((TM, TM)), -1), BF16)
    upper = jnp.asarray(np.triu(np.ones((LANES, LANES)), 1), BF16)

    in_scale = np.ones((IN_WIDTH,), np.float32)
    in_scale[OFF_G:] = 0.5
    w_in_b = (w_in * in_scale).astype(BF16)
    wa_b = w_attn_out.astype(BF16)
    wp_b = w_pool_out.astype(BF16)
    wf_b = w_fourier_out.astype(BF16)
    wo_b = w_out.astype(BF16)
    pad_r = jnp.zeros((DEPTH, D_MODEL, LANES - N_EXPERTS - N_EXPERT_GROUPS), F32)
    wr_b = jnp.concatenate([w_router_expert, w_router_group, pad_r], axis=2).astype(BF16)
    br = jnp.concatenate([b_router_expert, b_router_group,
                          jnp.zeros((DEPTH, LANES - N_EXPERTS - N_EXPERT_GROUPS), F32)],
                         axis=1).reshape(DEPTH, 1, LANES)
    n_pg = POOL_WIDTH // POOL_GROUP_DIM
    bdw = jnp.zeros((DEPTH, POOL_WIDTH, POOL_WIDTH), F32)
    for g in range(n_pg):
        lo = g * POOL_GROUP_DIM
        bdw = bdw.at[:, lo:lo + POOL_GROUP_DIM, lo:lo + POOL_GROUP_DIM].set(w_pool_group[:, g])
    bdw = bdw.astype(BF16)
    pscale = pool_scale.reshape(DEPTH, 1, POOL_WIDTH)
    qg = jnp.tile(q_norm, (1, LANES // HEAD_DIM)).reshape(DEPTH, 1, LANES)
    kg = jnp.tile(k_norm, (1, LANES // HEAD_DIM)).reshape(DEPTH, 1, LANES)
    gpre = norm_mix_pre.reshape(DEPTH, 1, D_MODEL)
    gpost = norm_mix_post.reshape(DEPTH, 1, D_MODEL)
    gffn = norm_ffn_pre.reshape(DEPTH, 1, D_MODEL)
    gfpost = norm_ffn_post.reshape(DEPTH, 1, D_MODEL)
    ck = cache_k.reshape(DEC_BATCH, DEPTH, PAST_LEN, KV_WIDTH)
    cv = cache_v.reshape(DEC_BATCH, DEPTH, PAST_LEN, KV_WIDTH)

    c_all = jnp.concatenate([c_ctx[None, :], c, jnp.zeros((MOD_ROWS - 1 - DEC_BATCH, D_MODEL), F32)], axis=0)
    mods = _mod_call(c_all, w_ada, b_ada).reshape(DEPTH, MOD_ROWS, 1, N_MOD * D_MODEL)

    xs_buf = jnp.zeros((XS_ROWS, HALF), U32)
    y = jnp.zeros((XS_ROWS, HALF), U32)
    new_k = jnp.zeros((BATCH, DEPTH, SEQ, KV_WIDTH), F32)
    new_v = jnp.zeros((BATCH, DEPTH, SEQ, KV_WIDTH), F32)
    x, hb = _prenorm_call(x_prompt.reshape(N_P, D_MODEL), x_sample.reshape(N_S, D_MODEL), mods, gpre)
    for l in range(DEPTH):
        q, k, v, xp, xc, xsn, gates, new_k, new_v = _proj_call(
            l, hb, w_in_b, qg, kg, cos_t, s1_t, s2_t, avg, dft_ch, new_k, new_v)
        attn = (_attn_prompt_call(q, k, v), _attn_sample_call(l, q, k, v, ck, cv))
        pool = (_pool_call(l, xp, bdw, pscale, SEQ, BATCH, 0, CTX_SEQ_PER_STEP),
                _pool_call(l, xp, bdw, pscale, DEC_SEQ, DEC_BATCH, N_P // DEC_SEQ, 1))
        four = (_fourier_prompt_call(cp, sp, xc, xsn), _fourier_sample_call(cl, sl, xc, xsn))
        x1, h2, pos, post, mw, cnt = _merge_call(l, x, attn, pool, four, gates, mods, gpost, gffn,
                                                 wa_b, wp_b, wf_b, wo_b, wr_b, br, tri, upper)
        runs = cnt.reshape(N_TILES, MOD_ROWS, LANES)[:, 0, :N_EXPERTS].astype(jnp.int32)
        runs = ((runs + RUN_ALIGN - 1) // RUN_ALIGN) * RUN_ALIGN
        tile_off = jnp.cumsum(runs, axis=1) - runs
        rows_e = jnp.sum(runs, axis=0)
        padded = ((rows_e + EBLK - 1) // EBLK) * EBLK
        pad_end = jnp.cumsum(padded)
        xs_off = (pad_end - padded)[None, :] + jnp.cumsum(runs, axis=0) - runs
        blk_cnt = padded // EBLK
        blk_start = (pad_end - padded) // EBLK
        n_used = pad_end[-1:] // EBLK
        n_list, trow, xrow = _copy_lists(runs // RUN_ALIGN, tile_off, xs_off)
        tile_cnt = jnp.sum(runs, axis=1) // RUN_ALIGN
        xs_buf = _dispatch_call(n_list, trow, xrow, tile_cnt, h2, post, xs_buf)
        y = _expert_call(l, blk_start, blk_cnt, n_used, xs_buf, y,
                         w_expert_gate, w_expert_up, w_expert_down)
        outs = _combine_call(l, n_list, trow, xrow, tile_cnt, y, x1, pos, mw, mods, gfpost, gpre)
        x, hb = outs

    y_prompt = outs[0].reshape(BATCH, SEQ, D_MODEL)
    y_sample = outs[1].reshape(DEC_BATCH, DEC_SEQ, D_MODEL)
    cache_shape = (BATCH, DEPTH, SEQ, N_KV_HEADS, HEAD_DIM)
    return (y_prompt, y_sample, new_k.reshape(cache_shape), new_v.reshape(cache_shape))
```

```python
import functools
import math

import numpy as np
import jax
import jax.numpy as jnp
from jax import lax
from jax.experimental import pallas as pl
from jax.experimental.pallas import tpu as pltpu

F32 = jnp.float32
BF16 = jnp.bfloat16

D_MODEL = 1024
BATCH = 32
SEQ = 256
DEPTH = 4
DEC_BATCH = 2
DEC_SEQ = 2048
PAST_LEN = 512
GRID_W = 64
N_HEADS = 8
N_KV_HEADS = 2
HEAD_DIM = 64
KV_GROUP = N_HEADS // N_KV_HEADS
ATTN_WIDTH = N_HEADS * HEAD_DIM
KV_WIDTH = N_KV_HEADS * HEAD_DIM
ROPE_AXIS_DIM = HEAD_DIM // 2
ROPE_BASE = 10000.0
POOL_WINDOWS = (2, 4, 8, 16)
POOL_WIDTH = 256
POOL_GROUP_DIM = 64
FOURIER_WIDTH = 256
FOURIER_GROUP_DIM = 64
N_BRANCHES = 3
OFF_K = ATTN_WIDTH
OFF_P = ATTN_WIDTH + 2 * KV_WIDTH
OFF_G = OFF_P + POOL_WIDTH + FOURIER_WIDTH
IN_WIDTH = OFF_G + N_BRANCHES * D_MODEL
N_EXPERT_GROUPS = 4
EXPERTS_PER_GROUP = 8
N_EXPERTS = N_EXPERT_GROUPS * EXPERTS_PER_GROUP
TOP_K = 2
EXPERT_FF = 256
N_MOD = 6
RMS_EPS = 1e-6
QK_SCALE = HEAD_DIM ** -0.5 * math.log2(math.e)

N_P = BATCH * SEQ
N_S = DEC_BATCH * DEC_SEQ
N_TOK = N_P + N_S
LANES = 128
MOD_ROWS = 8
ROUTER_ROWS = 40
POOL_PAD = 16

TM = 512
PROJ_TM = 1024
N_TILES = N_TOK // TM
P_TILES = N_P // TM
S_TILES_PER_SEQ = DEC_SEQ // TM
CTX_SEQ_PER_STEP = 4
TQ_S = 128
KEY_CHUNK = 512
TF_S = 512
EBLK = 512
X_SLOTS = 4
RUN_ALIGN = 8
RUN_SHIFT = 3
RUN_BITS = 7
assert RUN_ALIGN << (RUN_BITS - 1) == TM
SORT_ROWS = TOP_K * TM + N_EXPERTS * RUN_ALIGN
DISPATCH_SLOTS = 3
assert N_TILES >= DISPATCH_SLOTS
TILE_BITS = 8
assert SORT_ROWS < RUN_ALIGN << TILE_BITS
N_RUNS = N_TILES * N_EXPERTS
N_EBLK = (N_TOK * TOP_K + N_RUNS * (RUN_ALIGN - 1) + N_EXPERTS * (EBLK - 1) + EBLK - 1) // EBLK
XS_ROWS = N_EBLK * EBLK
MOD_NT = 1536


def _dot(a, b):
    return jnp.dot(a, b, preferred_element_type=F32)


def _rms(x):
    return x * lax.rsqrt(jnp.mean(x * x, axis=-1, keepdims=True) + RMS_EPS)


def _mod_row(i):
    return jnp.where(i < P_TILES, 0, 1 + (i - P_TILES) // S_TILES_PER_SEQ)


def _rope_block(i):
    p_tiles = N_P // PROJ_TM
    per_seq = DEC_SEQ // PROJ_TM
    return jnp.where(i < p_tiles, per_seq, (i - p_tiles) % per_seq)


def _mod_kernel(c_ref, w_ref, b_ref, o_ref):
    c = c_ref[...]
    s = (c * jax.nn.sigmoid(c)).astype(BF16)
    o_ref[...] = _dot(s, w_ref[...].astype(BF16)) + b_ref[...]


def _mod_call(c_all, w_ada, b_ada):
    nt = (N_MOD * D_MODEL) // MOD_NT
    return pl.pallas_call(
        _mod_kernel,
        grid=(DEPTH, nt),
        in_specs=[
            pl.BlockSpec((MOD_ROWS, D_MODEL), lambda l, j: (0, 0)),
            pl.BlockSpec((None, D_MODEL, MOD_NT), lambda l, j: (l, 0, j)),
            pl.BlockSpec((None, 1, MOD_NT), lambda l, j: (l, 0, j)),
        ],
        out_specs=pl.BlockSpec((None, MOD_ROWS, MOD_NT), lambda l, j: (l, 0, j)),
        out_shape=jax.ShapeDtypeStruct((DEPTH, MOD_ROWS, N_MOD * D_MODEL), F32),
        name="adaln_mod",
    )(c_all, w_ada, b_ada.reshape(DEPTH, 1, N_MOD * D_MODEL))


def _prenorm(x, mod, gain):
    return (_rms(x) * gain) * (1.0 + mod[:, D_MODEL:2 * D_MODEL]) + mod[:, 0:D_MODEL]


def _prenorm_kernel(xp_ref, xs_ref, mod_ref, gpre_ref, x_ref, hb_ref):
    x = jnp.where(pl.program_id(0) < P_TILES, xp_ref[...], xs_ref[...])
    x_ref[...] = x
    hb_ref[...] = _prenorm(x, mod_ref[...], gpre_ref[...]).astype(BF16)


def _prenorm_call(x_prompt, x_sample, mods, gpre):
    row = lambda i: (i, 0)
    return pl.pallas_call(
        _prenorm_kernel,
        grid=(N_TILES,),
        in_specs=[
            pl.BlockSpec((TM, D_MODEL), lambda i: (jnp.minimum(i, P_TILES - 1), 0)),
            pl.BlockSpec((TM, D_MODEL), lambda i: (jnp.maximum(i - P_TILES, 0), 0)),
            pl.BlockSpec((None, None, 1, 2 * D_MODEL), lambda i: (0, _mod_row(i), 0, 0)),
            pl.BlockSpec((None, 1, D_MODEL), lambda i: (0, 0, 0)),
        ],
        out_specs=[pl.BlockSpec((TM, D_MODEL), row), pl.BlockSpec((TM, D_MODEL), row)],
        out_shape=[jax.ShapeDtypeStruct((N_TOK, D_MODEL), F32),
                   jax.ShapeDtypeStruct((N_TOK, D_MODEL), BF16)],
        name="prenorm",
    )(x_prompt, x_sample, mods, gpre)


def _proj_kernel(hb_ref, w_ref, qg_ref, kg_ref, cos_ref, s1_ref, s2_ref, avg_ref, dft_ref,
                 ck_in_ref, cv_in_ref,
                 q_ref, k_ref, v_ref, xp_ref, xc_ref, xs_ref, g_ref, ck_ref, cv_ref):
    del ck_in_ref, cv_in_ref
    hb = hb_ref[...]

    cos = cos_ref[...]
    s1 = s1_ref[...]
    s2 = s2_ref[...]

    def rope(t):
        return (t * cos + pltpu.roll(t, LANES - ROPE_AXIS_DIM // 2, 1) * s1
                + pltpu.roll(t, ROPE_AXIS_DIM // 2, 1) * s2)

    head = _dot(hb, w_ref[:, 0:OFF_G])
    avg = avg_ref[...]
    for c in range(ATTN_WIDTH // LANES):
        lo = c * LANES
        q = head[:, lo:lo + LANES]
        ms = _dot((q * q).astype(BF16), avg)
        q = q * lax.rsqrt(ms + RMS_EPS) * qg_ref[...]
        q_ref[:, lo:lo + LANES] = (rope(q) * QK_SCALE).astype(BF16)

    k = head[:, OFF_K:OFF_K + KV_WIDTH]
    ms = _dot((k * k).astype(BF16), avg)
    k = rope(k * lax.rsqrt(ms + RMS_EPS) * kg_ref[...])
    v = head[:, OFF_K + KV_WIDTH:OFF_P]
    k_ref[...] = k
    v_ref[...] = v

    @pl.when(pl.program_id(0) < N_P // PROJ_TM)
    def _():
        for s in range(PROJ_TM // SEQ):
            ck_ref[s] = k[s * SEQ:(s + 1) * SEQ]
            cv_ref[s] = v[s * SEQ:(s + 1) * SEQ]

    xp_ref[...] = head[:, OFF_P:OFF_P + POOL_WIDTH]
    cs = _dot(head[:, OFF_P + POOL_WIDTH:OFF_G].astype(BF16), dft_ref[...])
    xc_ref[...] = cs[:, 0:FOURIER_WIDTH].astype(BF16)
    xs_ref[...] = cs[:, FOURIER_WIDTH:].astype(BF16)

    for c in range(N_BRANCHES):
        lo = OFF_G + c * D_MODEL
        g_ref[:, c * D_MODEL:(c + 1) * D_MODEL] = _dot(hb, w_ref[:, lo:lo + D_MODEL]).astype(BF16)


def _proj_call(l, hb, w_in, qg, kg, cos_t, s1_t, s2_t, avg, dft, cache_k, cache_v):
    row = lambda i: (i, 0)
    const2 = lambda i: (0, 0)
    per_layer = lambda i: (l, 0, 0)
    rope_spec = pl.BlockSpec((PROJ_TM, LANES), lambda i: (_rope_block(i), 0))
    cache_spec = pl.BlockSpec((PROJ_TM // SEQ, None, SEQ, KV_WIDTH),
                              lambda i: (jnp.minimum(i, N_P // PROJ_TM - 1), l, 0, 0))
    cache_shape = jax.ShapeDtypeStruct((BATCH, DEPTH, SEQ, KV_WIDTH), F32)
    return pl.pallas_call(
        _proj_kernel,
        grid=(N_TOK // PROJ_TM,),
        in_specs=[
            pl.BlockSpec((PROJ_TM, D_MODEL), row),
            pl.BlockSpec((None, D_MODEL, IN_WIDTH), per_layer),
            pl.BlockSpec((None, 1, LANES), per_layer),
            pl.BlockSpec((None, 1, LANES), per_layer),
            rope_spec, rope_spec, rope_spec,
            pl.BlockSpec((LANES, LANES), const2),
            pl.BlockSpec((FOURIER_WIDTH, 2 * FOURIER_WIDTH), const2),
            pl.BlockSpec(memory_space=pl.ANY),
            pl.BlockSpec(memory_space=pl.ANY),
        ],
        out_specs=[
            pl.BlockSpec((PROJ_TM, ATTN_WIDTH), row),
            pl.BlockSpec((PROJ_TM, KV_WIDTH), row),
            pl.BlockSpec((PROJ_TM, KV_WIDTH), row),
            pl.BlockSpec((PROJ_TM, POOL_WIDTH), row),
            pl.BlockSpec((PROJ_TM, FOURIER_WIDTH), row),
            pl.BlockSpec((PROJ_TM, FOURIER_WIDTH), row),
            pl.BlockSpec((PROJ_TM, N_BRANCHES * D_MODEL), row),
            cache_spec, cache_spec,
        ],
        out_shape=[
            jax.ShapeDtypeStruct((N_TOK, ATTN_WIDTH), BF16),
            jax.ShapeDtypeStruct((N_TOK, KV_WIDTH), F32),
            jax.ShapeDtypeStruct((N_TOK, KV_WIDTH), F32),
            jax.ShapeDtypeStruct((N_TOK, POOL_WIDTH), F32),
            jax.ShapeDtypeStruct((N_TOK, FOURIER_WIDTH), BF16),
            jax.ShapeDtypeStruct((N_TOK, FOURIER_WIDTH), BF16),
            jax.ShapeDtypeStruct((N_TOK, N_BRANCHES * D_MODEL), BF16),
            cache_shape, cache_shape,
        ],
        input_output_aliases={9: 7, 10: 8},
        name="proj",
    )(hb, w_in, qg, kg, cos_t, s1_t, s2_t, avg, dft, cache_k, cache_v)


def _attn_kernel(*refs, n_parts, tq):
    q_ref = refs[0]
    kv_refs = refs[1:1 + 2 * n_parts]
    o_ref = refs[-1]
    single_chunk = n_parts == 1 and kv_refs[0].shape[0] <= KEY_CHUNK
    outs = []
    for j in range(N_KV_HEADS):
        lo = j * HEAD_DIM
        qs = jnp.concatenate(
            [q_ref[:, (KV_GROUP * j + g) * HEAD_DIM:(KV_GROUP * j + g + 1) * HEAD_DIM]
             for g in range(KV_GROUP)], axis=0)
        m = acc = None
        for p in range(n_parts):
            k_ref, v_ref = kv_refs[2 * p], kv_refs[2 * p + 1]
            for c0 in range(0, k_ref.shape[0], KEY_CHUNK):
                c1 = min(c0 + KEY_CHUNK, k_ref.shape[0])
                kc = k_ref[c0:c1, lo:lo + HEAD_DIM].astype(BF16)
                s = lax.dot_general(qs, kc, (((1,), (1,)), ((), ())), preferred_element_type=F32)
                mc = jnp.max(s, axis=1, keepdims=True)
                m_new = mc if m is None else jnp.maximum(m, mc)
                e = jnp.exp2(s - m_new)
                vc = v_ref[c0:c1, lo:lo + HEAD_DIM].astype(BF16)
                if single_chunk:
                    den = jnp.sum(e, axis=1, keepdims=True)
                else:
                    vc = jnp.concatenate([vc, jnp.ones((c1 - c0, HEAD_DIM), BF16)], axis=1)
                pv = _dot(e.astype(BF16), vc)
                acc = pv if m is None else acc * jnp.exp2(m - m_new) + pv
                m = m_new
        o = acc / den if single_chunk else acc[:, 0:HEAD_DIM] / acc[:, HEAD_DIM:2 * HEAD_DIM]
        outs.extend(o[g * tq:(g + 1) * tq] for g in range(KV_GROUP))
    o_ref[...] = jnp.concatenate(outs, axis=1).astype(BF16)


def _attn_prompt_call(q, k, v):
    blk = lambda b: (b, 0)
    return pl.pallas_call(
        functools.partial(_attn_kernel, n_parts=1, tq=SEQ),
        grid=(BATCH,),
        in_specs=[
            pl.BlockSpec((SEQ, ATTN_WIDTH), blk),
            pl.BlockSpec((SEQ, KV_WIDTH), blk),
            pl.BlockSpec((SEQ, KV_WIDTH), blk),
        ],
        out_specs=pl.BlockSpec((SEQ, ATTN_WIDTH), blk),
        out_shape=jax.ShapeDtypeStruct((N_P, ATTN_WIDTH), BF16),
        name="attn_context",
    )(q, k, v)


def _attn_sample_call(l, q, k, v, cache_k, cache_v):
    nq = DEC_SEQ // TQ_S
    qrow = lambda b, i: (N_P // TQ_S + b * nq + i, 0)
    seq = lambda b, i: (N_P // DEC_SEQ + b, 0)
    cache = lambda b, i: (b, l, 0, 0)
    return pl.pallas_call(
        functools.partial(_attn_kernel, n_parts=2, tq=TQ_S),
        grid=(DEC_BATCH, nq),
        in_specs=[
            pl.BlockSpec((TQ_S, ATTN_WIDTH), qrow),
            pl.BlockSpec((None, None, PAST_LEN, KV_WIDTH), cache),
            pl.BlockSpec((None, None, PAST_LEN, KV_WIDTH), cache),
            pl.BlockSpec((DEC_SEQ, KV_WIDTH), seq),
            pl.BlockSpec((DEC_SEQ, KV_WIDTH), seq),
        ],
        out_specs=pl.BlockSpec((TQ_S, ATTN_WIDTH), lambda b, i: (b * nq + i, 0)),
        out_shape=jax.ShapeDtypeStruct((N_S, ATTN_WIDTH), BF16),
        name="attn_latent",
    )(q, cache_k, cache_v, k, v)


def _pool_kernel(xp_ref, bdw_ref, sc_ref, o_ref, pad_ref, *, seq_len, n_seq):
    for s in range(n_seq):
        _pool_sequence(xp_ref, bdw_ref, sc_ref, o_ref, pad_ref, s * seq_len, seq_len)


def _pool_sequence(xp_ref, bdw_ref, sc_ref, o_ref, pad_ref, row0, seq_len):
    half = POOL_WIDTH // 2
    zeros = jnp.zeros((POOL_PAD, POOL_WIDTH), F32)
    pad_ref[0:POOL_PAD, :] = zeros
    pad_ref[POOL_PAD + seq_len:, :] = zeros
    pad_ref[POOL_PAD:POOL_PAD + seq_len, :] = xp_ref[row0:row0 + seq_len, :]
    chunk = min(seq_len, 256)
    lane = lax.broadcasted_iota(jnp.int32, (chunk, half), 1)
    first = lane < POOL_GROUP_DIM
    for c in range(seq_len // chunk):
        base = c * chunk
        t = lax.broadcasted_iota(jnp.int32, (chunk, half), 0) + base

        def sh(j, lo):
            return pad_ref[POOL_PAD + base + j:POOL_PAD + base + j + chunk, lo:lo + half]

        def cnt(w):
            return (jnp.minimum(t + w // 2, seq_len) - jnp.maximum(t - w // 2, 0)).astype(F32)

        xa = sh(0, 0)
        w2 = sh(-1, 0) + xa
        w4 = w2 + sh(-2, 0) + sh(1, 0)
        xb = sh(0, half)
        w8 = xb
        for j in (-4, -3, -2, -1, 1, 2, 3):
            w8 = w8 + sh(j, half)
        w16 = w8
        for j in (-8, -7, -6, -5, 4, 5, 6, 7):
            w16 = w16 + sh(j, half)
        pa = jnp.where(first, w2 / cnt(2), w4 / cnt(4)) - xa
        pb = jnp.where(first, w8 / cnt(8), w16 / cnt(16)) - xb
        pooled = jnp.concatenate([pa, pb], axis=1).astype(BF16)
        o_ref[row0 + base:row0 + base + chunk, :] = (
            _dot(pooled, bdw_ref[...]) * sc_ref[...]).astype(BF16)


def _pool_call(l, xp, bdw, scale, seq_len, n_seq, blk0, seq_per_step):
    per_layer = lambda b: (l, 0, 0)
    rows = seq_len * seq_per_step
    return pl.pallas_call(
        functools.partial(_pool_kernel, seq_len=seq_len, n_seq=seq_per_step),
        grid=(n_seq // seq_per_step,),
        in_specs=[
            pl.BlockSpec((rows, POOL_WIDTH), lambda b: (blk0 + b, 0)),
            pl.BlockSpec((None, POOL_WIDTH, POOL_WIDTH), per_layer),
            pl.BlockSpec((None, 1, POOL_WIDTH), per_layer),
        ],
        out_specs=pl.BlockSpec((rows, POOL_WIDTH), lambda b: (b, 0)),
        out_shape=jax.ShapeDtypeStruct((n_seq * seq_len, POOL_WIDTH), BF16),
        scratch_shapes=[pltpu.VMEM((seq_len + 2 * POOL_PAD, POOL_WIDTH), F32)],
        name="pool_%d" % seq_len,
    )(xp, bdw, scale)


def _fourier_kernel(c_ref, s_ref, xc_ref, xs_ref, o_ref, *, scale, n_seq=1):
    seq_len = xc_ref.shape[0] // n_seq
    out_len = o_ref.shape[0] // n_seq
    for s in range(n_seq):
        rows = slice(s * seq_len, (s + 1) * seq_len)
        y = _dot(c_ref[...], xc_ref[rows, :]) - _dot(s_ref[...], xs_ref[rows, :])
        o_ref[s * out_len:(s + 1) * out_len, :] = (y * scale).astype(BF16)


def _fourier_prompt_call(cmat, smat, xc, xs):
    blk = lambda b: (b, 0)
    const2 = lambda b: (0, 0)
    rows = SEQ * CTX_SEQ_PER_STEP
    return pl.pallas_call(
        functools.partial(_fourier_kernel, scale=1.0 / math.sqrt(SEQ * FOURIER_GROUP_DIM),
                          n_seq=CTX_SEQ_PER_STEP),
        grid=(BATCH // CTX_SEQ_PER_STEP,),
        in_specs=[
            pl.BlockSpec((SEQ, SEQ), const2),
            pl.BlockSpec((SEQ, SEQ), const2),
            pl.BlockSpec((rows, FOURIER_WIDTH), blk),
            pl.BlockSpec((rows, FOURIER_WIDTH), blk),
        ],
        out_specs=pl.BlockSpec((rows, FOURIER_WIDTH), blk),
        out_shape=jax.ShapeDtypeStruct((N_P, FOURIER_WIDTH), BF16),
        name="fourier_context",
    )(cmat, smat, xc, xs)


def _fourier_sample_call(cmat, smat, xc, xs):
    nt = DEC_SEQ // TF_S
    rows = lambda b, i: (i, 0)
    seq = lambda b, i: (N_P // DEC_SEQ + b, 0)
    out = lambda b, i: (b * nt + i, 0)
    return pl.pallas_call(
        functools.partial(_fourier_kernel, scale=1.0 / math.sqrt(DEC_SEQ * FOURIER_GROUP_DIM)),
        grid=(DEC_BATCH, nt),
        in_specs=[
            pl.BlockSpec((TF_S, DEC_SEQ), rows),
            pl.BlockSpec((TF_S, DEC_SEQ), rows),
            pl.BlockSpec((DEC_SEQ, FOURIER_WIDTH), seq),
            pl.BlockSpec((DEC_SEQ, FOURIER_WIDTH), seq),
        ],
        out_specs=pl.BlockSpec((TF_S, FOURIER_WIDTH), out),
        out_shape=jax.ShapeDtypeStruct((N_S, FOURIER_WIDTH), BF16),
        name="fourier_latent",
    )(cmat, smat, xc, xs)


def _merge_kernel(x_ref, attn_p_ref, attn_s_ref, pool_p_ref, pool_s_ref, four_p_ref, four_s_ref,
                  g_ref, mod_ref, gpost_ref, gffn_ref,
                  wa_ref, wp_ref, wf_ref, wo_ref, wrt_ref, brt_ref, triu_ref, lower_ref,
                  x1_ref, h2_ref, pos_ref, post_ref, mw_ref, cnt_ref):
    i = pl.program_id(0)
    mod = mod_ref[...]
    g1 = mod[:, 2 * D_MODEL:3 * D_MODEL]
    sh2 = mod[:, 3 * D_MODEL:4 * D_MODEL]
    sc2 = mod[:, 4 * D_MODEL:5 * D_MODEL]

    def gate2(c):
        return jnp.tanh(g_ref[:, c * D_MODEL:(c + 1) * D_MODEL].astype(F32)) + 1.0

    def branch(p_ref, s_ref):
        return jnp.where(i < P_TILES, p_ref[...], s_ref[...])

    merged = gate2(0) * _dot(branch(attn_p_ref, attn_s_ref), wa_ref[...])
    merged = merged + gate2(1) * _dot(branch(pool_p_ref, pool_s_ref), wp_ref[...])
    merged = merged + gate2(2) * _dot(branch(four_p_ref, four_s_ref), wf_ref[...])
    mix = _dot((0.5 * merged).astype(BF16), wo_ref[...])
    x1 = x_ref[...] + g1 * (_rms(mix) * gpost_ref[...])
    x1_ref[...] = x1
    h2 = (_rms(x1) * gffn_ref[...]) * (1.0 + sc2) + sh2
    h2b = h2.astype(BF16)
    h2_ref[...] = h2b

    logits = lax.dot_general(wrt_ref[...], h2b, (((1,), (1,)), ((), ())),
                             preferred_element_type=F32) + brt_ref[...]
    top = logits[0:ROUTER_ROWS, :]
    row = lax.broadcasted_iota(jnp.int32, (ROUTER_ROWS, TM), 0)
    rowf = row.astype(F32)
    neg = jnp.float32(-3e38)
    big = jnp.float32(1e9)
    is_g = (row >= N_EXPERTS) & (row < N_EXPERTS + N_EXPERT_GROUPS)
    lg = jnp.where(is_g, top, neg)
    gmax = jnp.max(lg, axis=0, keepdims=True)
    g_sel = jnp.min(jnp.where(lg == gmax, rowf - N_EXPERTS, big), axis=0, keepdims=True)
    p_g = 1.0 / jnp.sum(jnp.where(is_g, jnp.exp(top - gmax), 0.0), axis=0, keepdims=True)
    grp = lax.shift_right_logical(row, int(math.log2(EXPERTS_PER_GROUP))).astype(F32)
    in_grp = (row < N_EXPERTS) & (grp == g_sel)
    le = jnp.where(in_grp, top, neg)
    v1 = jnp.max(le, axis=0, keepdims=True)
    i1 = jnp.min(jnp.where(le == v1, rowf, big), axis=0, keepdims=True)
    le2 = jnp.where(rowf == i1, neg, le)
    v2 = jnp.max(le2, axis=0, keepdims=True)
    i2 = jnp.min(jnp.where(le2 == v2, rowf, big), axis=0, keepdims=True)
    e21 = jnp.exp(v2 - v1)
    w1 = p_g / (1.0 + e21)
    w2 = p_g * e21 / (1.0 + e21)

    rows_all = lax.broadcasted_iota(jnp.int32, (LANES, TM), 0)
    rows_allf = rows_all.astype(F32)
    oh1 = (rows_allf == i1).astype(F32)
    oh2 = (rows_allf == i2).astype(F32)
    ohb = (oh1 + oh2).astype(BF16)
    before = _dot(ohb, triu_ref[...])
    cnt = _dot(ohb, jnp.ones((TM, LANES), BF16))
    cnt_pad = (lax.shift_right_logical(cnt.astype(jnp.int32) + (RUN_ALIGN - 1), RUN_SHIFT)
               * RUN_ALIGN).astype(F32)
    run_off = _dot(lower_ref[...], cnt_pad.astype(BF16))
    slot = jnp.concatenate([run_off] * (TM // LANES), axis=1) + before
    p1 = jnp.sum(slot * oh1, axis=0, keepdims=True)
    p2 = jnp.sum(slot * oh2, axis=0, keepdims=True)
    cnt_ref[...] = cnt

    meta = jnp.where(rows_all == 0, p1, jnp.where(rows_all == 1, p2,
                     jnp.where(rows_all == 2, w1, jnp.where(rows_all == 3, w2, 0.0))))
    post_ref[...] = meta[0:MOD_ROWS, :].astype(jnp.int32)
    meta_t = meta.T
    pos_ref[...] = meta_t.astype(jnp.int32)
    mw_ref[...] = meta_t


def _merge_call(l, x, attn, pool, four, gates, mods, gpost, gffn, wa, wp, wf, wo, wrt, brt, triu,
                lower):
    row = lambda i: (i, 0)
    prow = lambda i: (jnp.minimum(i, P_TILES - 1), 0)
    srow = lambda i: (jnp.maximum(i - P_TILES, 0), 0)
    const2 = lambda i: (0, 0)
    per_layer = lambda i: (l, 0, 0)
    return pl.pallas_call(
        _merge_kernel,
        grid=(N_TILES,),
        in_specs=[
            pl.BlockSpec((TM, D_MODEL), row),
            pl.BlockSpec((TM, ATTN_WIDTH), prow),
            pl.BlockSpec((TM, ATTN_WIDTH), srow),
            pl.BlockSpec((TM, POOL_WIDTH), prow),
            pl.BlockSpec((TM, POOL_WIDTH), srow),
            pl.BlockSpec((TM, FOURIER_WIDTH), prow),
            pl.BlockSpec((TM, FOURIER_WIDTH), srow),
            pl.BlockSpec((TM, N_BRANCHES * D_MODEL), row),
            pl.BlockSpec((None, None, 1, N_MOD * D_MODEL), lambda i: (l, _mod_row(i), 0, 0)),
            pl.BlockSpec((None, 1, D_MODEL), per_layer),
            pl.BlockSpec((None, 1, D_MODEL), per_layer),
            pl.BlockSpec((None, ATTN_WIDTH, D_MODEL), per_layer),
            pl.BlockSpec((None, POOL_WIDTH, D_MODEL), per_layer),
            pl.BlockSpec((None, FOURIER_WIDTH, D_MODEL), per_layer),
            pl.BlockSpec((None, D_MODEL, D_MODEL), per_layer),
            pl.BlockSpec((None, LANES, D_MODEL), per_layer),
            pl.BlockSpec((None, LANES, TM), per_layer),
            pl.BlockSpec((TM, TM), const2),
            pl.BlockSpec((LANES, LANES), const2),
        ],
        out_specs=[
            pl.BlockSpec((TM, D_MODEL), row),
            pl.BlockSpec((TM, D_MODEL), row),
            pl.BlockSpec((TM, LANES), row),
            pl.BlockSpec((MOD_ROWS, TM), lambda i: (0, i)),
            pl.BlockSpec((TM, LANES), row),
            pl.BlockSpec((LANES, LANES), row),
        ],
        out_shape=[
            jax.ShapeDtypeStruct((N_TOK, D_MODEL), F32),
            jax.ShapeDtypeStruct((N_TOK, D_MODEL), BF16),
            jax.ShapeDtypeStruct((N_TOK, LANES), jnp.int32),
            jax.ShapeDtypeStruct((MOD_ROWS, N_TOK), jnp.int32),
            jax.ShapeDtypeStruct((N_TOK, LANES), F32),
            jax.ShapeDtypeStruct((N_TILES * LANES, LANES), F32),
        ],
        name="merge_router",
    )(x, attn[0], attn[1], pool[0], pool[1], four[0], four[1], gates, mods, gpost, gffn,
      wa, wp, wf, wo, wrt, brt, triu, lower)


HALF = D_MODEL // 2
U32 = jnp.uint32
HI_MASK = 0xFFFF0000


def _pack_rows(x):
    lo = lax.bitcast_convert_type(x[:, :HALF], U32)
    hi = lax.bitcast_convert_type(x[:, HALF:], U32)
    return lax.shift_right_logical(lo, U32(16)) | (hi & U32(HI_MASK))


def _unpack_rows(u):
    lo = lax.bitcast_convert_type(lax.shift_left(u, U32(16)), F32)
    hi = lax.bitcast_convert_type(u & U32(HI_MASK), F32)
    return jnp.concatenate([lo.astype(BF16), hi.astype(BF16)], axis=1)


def _rows(ref, start, size):
    return ref.at[pl.ds(pl.multiple_of(start, RUN_ALIGN), size)]


def _start_tile_copies(tile, n_ref, trow_ref, xrow_ref, make_copy):
    for bit in range(RUN_BITS):
        lst = bit * N_TILES + tile

        def body(k, carry, bit=bit, lst=lst):
            j = lst * N_EXPERTS + k
            make_copy(trow_ref[j], xrow_ref[j], RUN_ALIGN << bit).start()
            return carry

        lax.fori_loop(0, n_ref[lst], body, 0)


def _wait_tile_copies(total, src_ref, dst_ref, sem):
    for b in range(TILE_BITS):
        size = RUN_ALIGN << b

        @pl.when((lax.shift_right_logical(total, b) & 1) == 1)
        def _(size=size):
            pltpu.make_async_copy(_rows(src_ref, 0, size), _rows(dst_ref, 0, size), sem).wait()


def _dispatch_kernel(n_ref, trow_ref, xrow_ref, tsum_ref, h_ref, post_ref, xs_in_ref, xs_ref,
                     sorted_ref, sem):
    del xs_in_ref
    i = pl.program_id(0)
    slot = lax.rem(i, DISPATCH_SLOTS)
    rows = lax.broadcasted_iota(jnp.int32, (SORT_ROWS, TM), 0)
    p = post_ref[...]
    perm = jnp.where(rows == p[0:1, :], 1.0, jnp.where(rows == p[1:2, :], 1.0, 0.0)).astype(BF16)
    sorted_ref[slot] = _pack_rows(_dot(perm, h_ref[...]))

    def wait(tile, slot):
        _wait_tile_copies(tsum_ref[tile], sorted_ref.at[slot], xs_ref, sem.at[slot])

    lag = DISPATCH_SLOTS - 1

    @pl.when(i >= lag)
    def _():
        wait(i - lag, lax.rem(i + 1, DISPATCH_SLOTS))

    _start_tile_copies(i, n_ref, trow_ref, xrow_ref, lambda t, x, size: pltpu.make_async_copy(
        _rows(sorted_ref.at[slot], t, size), _rows(xs_ref, x, size), sem.at[slot]))

    @pl.when(i == pl.num_programs(0) - 1)
    def _():
        for back in reversed(range(lag)):
            wait(i - back, lax.rem(i - back, DISPATCH_SLOTS))


def _dispatch_call(n_list, trow, xrow, tsum, h2, post, xs):
    return pl.pallas_call(
        _dispatch_kernel,
        grid_spec=pltpu.PrefetchScalarGridSpec(
            num_scalar_prefetch=4,
            grid=(N_TILES,),
            in_specs=[
                pl.BlockSpec((TM, D_MODEL), lambda i, *_: (i, 0)),
                pl.BlockSpec((MOD_ROWS, TM), lambda i, *_: (0, i)),
                pl.BlockSpec(memory_space=pl.ANY),
            ],
            out_specs=pl.BlockSpec(memory_space=pl.ANY),
            scratch_shapes=[pltpu.VMEM((DISPATCH_SLOTS, SORT_ROWS, HALF), U32),
                            pltpu.SemaphoreType.DMA((DISPATCH_SLOTS,))],
        ),
        out_shape=jax.ShapeDtypeStruct((XS_ROWS, HALF), U32),
        input_output_aliases={6: 0},
        compiler_params=pltpu.CompilerParams(dimension_semantics=("arbitrary",)),
        name="moe_dispatch",
    )(n_list, trow, xrow, tsum, h2, post, xs)


def _expert_kernel(bstart_ref, bcnt_ref, nu_ref, wg_ref, wu_ref, wd_ref, xs_ref, y_prev_ref, y_ref,
                   wgub, wdb, xbuf, ybuf, xsem, ysem):
    del y_prev_ref
    e = pl.program_id(0)
    n_used = nu_ref[0]

    def x_copy(g, slot):
        rows = pl.ds(pl.multiple_of(g * EBLK, EBLK), EBLK)
        return pltpu.make_async_copy(xs_ref.at[rows], xbuf.at[slot], xsem.at[slot])

    def y_copy(g, slot):
        rows = pl.ds(pl.multiple_of(g * EBLK, EBLK), EBLK)
        return pltpu.make_async_copy(ybuf.at[slot], y_ref.at[rows], ysem.at[slot])

    @pl.when(e == 0)
    def _():
        for g in range(X_SLOTS - 1):
            @pl.when(g < n_used)
            def _(g=g):
                x_copy(g, g).start()

    wgub[:, 0:EXPERT_FF] = wg_ref[...].astype(BF16)
    wgub[:, EXPERT_FF:] = wu_ref[...].astype(BF16)
    wdb[...] = wd_ref[...].astype(BF16)
    first = bstart_ref[e]

    def block(j, carry):
        g = first + j
        slot = lax.rem(g, 2)
        xslot = lax.rem(g, X_SLOTS)
        x_copy(g, xslot).wait()

        @pl.when(g + (X_SLOTS - 1) < n_used)
        def _():
            x_copy(g + (X_SLOTS - 1), lax.rem(g + (X_SLOTS - 1), X_SLOTS)).start()

        xb = _unpack_rows(xbuf[xslot])
        gu = _dot(xb, wgub[...])
        gate = gu[:, 0:EXPERT_FF]
        act = (gate * jax.nn.sigmoid(gate)) * gu[:, EXPERT_FF:]
        y = _dot(act.astype(BF16), wdb[...])

        @pl.when(g >= 2)
        def _():
            y_copy(g - 2, slot).wait()

        ybuf[slot] = _pack_rows(y.astype(BF16).astype(F32))
        y_copy(g, slot).start()
        return carry

    lax.fori_loop(0, bcnt_ref[e], block, 0)

    @pl.when(e == pl.num_programs(0) - 1)
    def _():
        @pl.when(n_used >= 2)
        def _():
            y_copy(n_used - 2, lax.rem(n_used, 2)).wait()

        y_copy(n_used - 1, lax.rem(n_used - 1, 2)).wait()


def _expert_call(l, blk_start, blk_cnt, n_used, xs, y_prev, w_gate, w_up, w_down):
    wsel = lambda e, *_: (l, e, 0, 0)
    return pl.pallas_call(
        _expert_kernel,
        grid_spec=pltpu.PrefetchScalarGridSpec(
            num_scalar_prefetch=3,
            grid=(N_EXPERTS,),
            in_specs=[
                pl.BlockSpec((None, None, D_MODEL, EXPERT_FF), wsel),
                pl.BlockSpec((None, None, D_MODEL, EXPERT_FF), wsel),
                pl.BlockSpec((None, None, EXPERT_FF, D_MODEL), wsel),
                pl.BlockSpec(memory_space=pl.ANY),
                pl.BlockSpec(memory_space=pl.ANY),
            ],
            out_specs=pl.BlockSpec(memory_space=pl.ANY),
            scratch_shapes=[
                pltpu.VMEM((D_MODEL, 2 * EXPERT_FF), BF16),
                pltpu.VMEM((EXPERT_FF, D_MODEL), BF16),
                pltpu.VMEM((X_SLOTS, EBLK, HALF), U32),
                pltpu.VMEM((2, EBLK, HALF), U32),
                pltpu.SemaphoreType.DMA((X_SLOTS,)),
                pltpu.SemaphoreType.DMA((2,)),
            ],
        ),
        out_shape=jax.ShapeDtypeStruct((XS_ROWS, HALF), U32),
        input_output_aliases={7: 0},
        compiler_params=pltpu.CompilerParams(dimension_semantics=("arbitrary",)),
        name="moe_experts",
    )(blk_start, blk_cnt, n_used, w_gate, w_up, w_down, xs, y_prev)


def _combine_kernel(n_ref, trow_ref, xrow_ref, tsum_ref, y_ref, x1_ref, pos_ref, mw_ref, g2_ref,
                    gpost_ref, *rest, has_next):
    if has_next:
        nmod_ref, ngpre_ref, o_ref, hb_ref, ybuf, sem = rest
    else:
        o_ref, os_ref, ybuf, sem = rest
    i = pl.program_id(0)
    slot = lax.rem(i, 2)

    def start(tile, slot):
        _start_tile_copies(tile, n_ref, trow_ref, xrow_ref, lambda t, x, size: pltpu.make_async_copy(
            _rows(y_ref, x, size), _rows(ybuf.at[slot], t, size), sem.at[slot]))

    @pl.when(i == 0)
    def _():
        ybuf[...] = jnp.zeros_like(ybuf)
        start(0, 0)

    @pl.when(i + 1 < pl.num_programs(0))
    def _():
        start(i + 1, 1 - slot)

    _wait_tile_copies(tsum_ref[i], y_ref, ybuf.at[slot], sem.at[slot])

    pos = pos_ref[...]
    mw = mw_ref[...]
    cols = lax.broadcasted_iota(jnp.int32, (TM, SORT_ROWS), 1)
    qw = (jnp.where(cols == pos[:, 0:1], mw[:, 2:3], 0.0)
          + jnp.where(cols == pos[:, 1:2], mw[:, 3:4], 0.0)).astype(BF16)
    ffn = _dot(qw, _unpack_rows(ybuf[slot]))
    x2 = x1_ref[...] + g2_ref[...] * (_rms(ffn) * gpost_ref[...])
    if has_next:
        o_ref[...] = x2
        hb_ref[...] = _prenorm(x2, nmod_ref[...], ngpre_ref[...]).astype(BF16)
    else:
        @pl.when(i < P_TILES)
        def _():
            o_ref[...] = x2

        os_ref[...] = x2


def _combine_call(l, n_list, trow, xrow, tsum, y, x1, pos, mw, mods, gpost, gpre):
    has_next = l + 1 < DEPTH
    row = lambda i, *_: (i, 0)
    in_specs = [
        pl.BlockSpec(memory_space=pl.ANY),
        pl.BlockSpec((TM, D_MODEL), row),
        pl.BlockSpec((TM, LANES), row),
        pl.BlockSpec((TM, LANES), row),
        pl.BlockSpec((None, None, 1, D_MODEL), lambda i, *_: (l, _mod_row(i), 0, N_MOD - 1)),
        pl.BlockSpec((None, 1, D_MODEL), lambda i, *_: (l, 0, 0)),
    ]
    args = [n_list, trow, xrow, tsum, y, x1, pos, mw, mods, gpost]
    if has_next:
        in_specs += [
            pl.BlockSpec((None, None, 1, 2 * D_MODEL), lambda i, *_: (l + 1, _mod_row(i), 0, 0)),
            pl.BlockSpec((None, 1, D_MODEL), lambda i, *_: (l + 1, 0, 0)),
        ]
        args += [mods, gpre]
        out_specs = [pl.BlockSpec((TM, D_MODEL), row), pl.BlockSpec((TM, D_MODEL), row)]
        out_shape = [jax.ShapeDtypeStruct((N_TOK, D_MODEL), F32),
                     jax.ShapeDtypeStruct((N_TOK, D_MODEL), BF16)]
    else:
        out_specs = [
            pl.BlockSpec((TM, D_MODEL), lambda i, *_: (jnp.minimum(i, P_TILES - 1), 0)),
            pl.BlockSpec((TM, D_MODEL), lambda i, *_: (jnp.maximum(i - P_TILES, 0), 0)),
        ]
        out_shape = [jax.ShapeDtypeStruct((N_P, D_MODEL), F32),
                     jax.ShapeDtypeStruct((N_S, D_MODEL), F32)]
    return pl.pallas_call(
        functools.partial(_combine_kernel, has_next=has_next),
        grid_spec=pltpu.PrefetchScalarGridSpec(
            num_scalar_prefetch=4,
            grid=(N_TILES,),
            in_specs=in_specs,
            out_specs=out_specs,
            scratch_shapes=[pltpu.VMEM((2, SORT_ROWS, HALF), U32), pltpu.SemaphoreType.DMA((2,))],
        ),
        out_shape=out_shape,
        compiler_params=pltpu.CompilerParams(dimension_semantics=("arbitrary",)),
        name="moe_combine",
    )(*args)


def _dft_mats(n):
    k = np.arange(n, dtype=np.int64)
    ang = 2.0 * np.pi * ((k[:, None] * k[None, :]) % n).astype(np.float64) / n
    return np.cos(ang), np.sin(ang)


def _block_diag(m, reps):
    n = m.shape[0]
    out = np.zeros((n * reps, n * reps), m.dtype)
    for r in range(reps):
        out[r * n:(r + 1) * n, r * n:(r + 1) * n] = m
    return out


def _rope_tables():
    t = np.arange(DEC_SEQ)
    pos = np.stack([t // GRID_W, t % GRID_W], axis=1).astype(np.float64)
    n_freq = ROPE_AXIS_DIM // 2
    inv = ROPE_BASE ** (-np.arange(n_freq, dtype=np.float64) * 2.0 / ROPE_AXIS_DIM)
    ang = pos[:, :, None] * inv[None, None, :]
    cos = np.cos(ang)
    sin = np.sin(ang)
    zero = np.zeros_like(sin[:, 0])
    cos_h = np.concatenate([cos[:, 0], cos[:, 0], cos[:, 1], cos[:, 1]], axis=1)
    s1_h = np.concatenate([-sin[:, 0], zero, -sin[:, 1], zero], axis=1)
    s2_h = np.concatenate([zero, sin[:, 0], zero, sin[:, 1]], axis=1)
    reps = LANES // HEAD_DIM

    def table(a, ident):
        a = np.tile(a, (1, reps))
        pad = np.full((PROJ_TM, LANES), ident, np.float64)
        return jnp.asarray(np.concatenate([a, pad], axis=0), F32)

    return table(cos_h, 1.0), table(s1_h, 0.0), table(s2_h, 0.0)


def _copy_lists(run_cnt, tile_off, xs_off):
    bit = jnp.arange(RUN_BITS, dtype=jnp.int32)[:, None, None]
    has = (run_cnt[None] >> bit) & 1
    before = ((run_cnt[None] >> (bit + 1)) << (bit + 1)) * RUN_ALIGN
    slot = jnp.cumsum(has, axis=-1) - has
    hit = (has[..., None] == 1) & (slot[..., None] == jnp.arange(N_EXPERTS, dtype=jnp.int32))

    def compact(rows):
        return jnp.sum(jnp.where(hit, rows[..., None], 0), axis=-2).reshape(-1)

    return (jnp.sum(has, axis=-1).reshape(-1), compact(tile_off[None] + before),
            compact(xs_off[None] + before))


def kernel(x_prompt, x_sample, cache_k, cache_v, c, c_ctx, w_ada, b_ada, norm_mix_pre,
           norm_mix_post, norm_ffn_pre, norm_ffn_post, w_in, q_norm, k_norm, w_attn_out,
           w_pool_group, pool_scale, w_pool_out, w_fourier_out, w_out, w_router_group,
           b_router_group, w_router_expert, b_router_expert, w_expert_gate, w_expert_up,
           w_expert_down):
    cos_t, s1_t, s2_t = _rope_tables()
    avg = jnp.asarray(_block_diag(np.full((HEAD_DIM, HEAD_DIM), 1.0 / HEAD_DIM), LANES // HEAD_DIM), BF16)
    c64, s64 = _dft_mats(FOURIER_GROUP_DIM)
    n_fg = FOURIER_WIDTH // FOURIER_GROUP_DIM
    dft_ch = jnp.asarray(np.concatenate([_block_diag(c64, n_fg), _block_diag(s64, n_fg)], axis=1), BF16)
    cp, sp = _dft_mats(SEQ)
    cp, sp = jnp.asarray(cp, BF16), jnp.asarray(sp, BF16)
    cl, sl = _dft_mats(DEC_SEQ)
    cl, sl = jnp.asarray(cl, BF16), jnp.asarray(sl, BF16)
    triu = jnp.asarray(np.triu(np.ones((TM, TM)), 1), BF16)
    lower = jnp.asarray(np.tril(np.ones((LANES, LANES)), -1), BF16)

    in_scale = np.ones((IN_WIDTH,), np.float32)
    in_scale[OFF_G:] = 0.5
    w_in_b = (w_in * in_scale).astype(BF16)
    wa_b = w_attn_out.astype(BF16)
    wp_b = w_pool_out.astype(BF16)
    wf_b = w_fourier_out.astype(BF16)
    wo_b = w_out.astype(BF16)
    pad_r = jnp.zeros((DEPTH, D_MODEL, LANES - N_EXPERTS - N_EXPERT_GROUPS), F32)
    wrt = jnp.concatenate([w_router_expert, w_router_group, pad_r], axis=2).astype(BF16)
    wrt = jnp.transpose(wrt, (0, 2, 1))
    br = jnp.concatenate([b_router_expert, b_router_group,
                          jnp.zeros((DEPTH, LANES - N_EXPERTS - N_EXPERT_GROUPS), F32)], axis=1)
    brt = jnp.broadcast_to(br[:, :, None], (DEPTH, LANES, TM))
    n_pg = POOL_WIDTH // POOL_GROUP_DIM
    bdw = jnp.zeros((DEPTH, POOL_WIDTH, POOL_WIDTH), F32)
    for g in range(n_pg):
        lo = g * POOL_GROUP_DIM
        bdw = bdw.at[:, lo:lo + POOL_GROUP_DIM, lo:lo + POOL_GROUP_DIM].set(w_pool_group[:, g])
    bdw = bdw.astype(BF16)
    pscale = pool_scale.reshape(DEPTH, 1, POOL_WIDTH)
    qg = jnp.tile(q_norm, (1, LANES // HEAD_DIM)).reshape(DEPTH, 1, LANES)
    kg = jnp.tile(k_norm, (1, LANES // HEAD_DIM)).reshape(DEPTH, 1, LANES)
    gpre = norm_mix_pre.reshape(DEPTH, 1, D_MODEL)
    gpost = norm_mix_post.reshape(DEPTH, 1, D_MODEL)
    gffn = norm_ffn_pre.reshape(DEPTH, 1, D_MODEL)
    gfpost = norm_ffn_post.reshape(DEPTH, 1, D_MODEL)
    ck = cache_k.reshape(DEC_BATCH, DEPTH, PAST_LEN, KV_WIDTH)
    cv = cache_v.reshape(DEC_BATCH, DEPTH, PAST_LEN, KV_WIDTH)

    c_all = jnp.concatenate([c_ctx[None, :], c, jnp.zeros((MOD_ROWS - 1 - DEC_BATCH, D_MODEL), F32)], axis=0)
    mods = _mod_call(c_all, w_ada, b_ada).reshape(DEPTH, MOD_ROWS, 1, N_MOD * D_MODEL)

    xs_buf = jnp.zeros((XS_ROWS, HALF), U32)
    y = jnp.zeros((XS_ROWS, HALF), U32)
    new_k = jnp.zeros((BATCH, DEPTH, SEQ, KV_WIDTH), F32)
    new_v = jnp.zeros((BATCH, DEPTH, SEQ, KV_WIDTH), F32)
    x, hb = _prenorm_call(x_prompt.reshape(N_P, D_MODEL), x_sample.reshape(N_S, D_MODEL), mods, gpre)
    for l in range(DEPTH):
        q, k, v, xp, xc, xsn, gates, new_k, new_v = _proj_call(
            l, hb, w_in_b, qg, kg, cos_t, s1_t, s2_t, avg, dft_ch, new_k, new_v)
        attn = (_attn_prompt_call(q, k, v), _attn_sample_call(l, q, k, v, ck, cv))
        pool = (_pool_call(l, xp, bdw, pscale, SEQ, BATCH, 0, CTX_SEQ_PER_STEP),
                _pool_call(l, xp, bdw, pscale, DEC_SEQ, DEC_BATCH, N_P // DEC_SEQ, 1))
        four = (_fourier_prompt_call(cp, sp, xc, xsn), _fourier_sample_call(cl, sl, xc, xsn))
        x1, h2, pos, post, mw, cnt = _merge_call(l, x, attn, pool, four, gates, mods, gpost, gffn,
                                                 wa_b, wp_b, wf_b, wo_b, wrt, brt, triu, lower)
        runs = cnt.reshape(N_TILES, LANES, LANES)[:, :N_EXPERTS, 0].astype(jnp.int32)
        runs = ((runs + RUN_ALIGN - 1) // RUN_ALIGN) * RUN_ALIGN
        tile_off = jnp.cumsum(runs, axis=1) - runs
        rows_e = jnp.sum(runs, axis=0)
        padded = ((rows_e + EBLK - 1) // EBLK) * EBLK
        pad_end = jnp.cumsum(padded)
        xs_off = (pad_end - padded)[None, :] + jnp.cumsum(runs, axis=0) - runs
        blk_cnt = padded // EBLK
        blk_start = (pad_end - padded) // EBLK
        n_used = pad_end[-1:] // EBLK
        n_list, trow, xrow = _copy_lists(runs // RUN_ALIGN, tile_off, xs_off)
        tile_cnt = jnp.sum(runs, axis=1) // RUN_ALIGN
        xs_buf = _dispatch_call(n_list, trow, xrow, tile_cnt, h2, post, xs_buf)
        y = _expert_call(l, blk_start, blk_cnt, n_used, xs_buf, y,
                         w_expert_gate, w_expert_up, w_expert_down)
        outs = _combine_call(l, n_list, trow, xrow, tile_cnt, y, x1, pos, mw, mods, gfpost, gpre)
        x, hb = outs

    y_prompt = outs[0].reshape(BATCH, SEQ, D_MODEL)
    y_sample = outs[1].reshape(DEC_BATCH, DEC_SEQ, D_MODEL)
    cache_shape = (BATCH, DEPTH, SEQ, N_KV_HEADS, HEAD_DIM)
    return (y_prompt, y_sample, new_k.reshape(cache_shape), new_v.reshape(cache_shape))
```

```python
import functools
import math

import numpy as np
import jax
import jax.numpy as jnp
from jax import lax
from jax.experimental import pallas as pl
from jax.experimental.pallas import tpu as pltpu

F32 = jnp.float32
BF16 = jnp.bfloat16

D_MODEL = 1024
BATCH = 32
SEQ = 256
DEPTH = 4
DEC_BATCH = 2
DEC_SEQ = 2048
PAST_LEN = 512
GRID_W = 64
N_HEADS = 8
N_KV_HEADS = 2
HEAD_DIM = 64
KV_GROUP = N_HEADS // N_KV_HEADS
ATTN_WIDTH = N_HEADS * HEAD_DIM
KV_WIDTH = N_KV_HEADS * HEAD_DIM
ROPE_AXIS_DIM = HEAD_DIM // 2
ROPE_BASE = 10000.0
POOL_WINDOWS = (2, 4, 8, 16)
POOL_WIDTH = 256
POOL_GROUP_DIM = 64
FOURIER_WIDTH = 256
FOURIER_GROUP_DIM = 64
N_BRANCHES = 3
OFF_K = ATTN_WIDTH
OFF_P = ATTN_WIDTH + 2 * KV_WIDTH
OFF_G = OFF_P + POOL_WIDTH + FOURIER_WIDTH
IN_WIDTH = OFF_G + N_BRANCHES * D_MODEL
N_EXPERT_GROUPS = 4
EXPERTS_PER_GROUP = 8
N_EXPERTS = N_EXPERT_GROUPS * EXPERTS_PER_GROUP
TOP_K = 2
EXPERT_FF = 256
N_MOD = 6
RMS_EPS = 1e-6
QK_SCALE = HEAD_DIM ** -0.5 * math.log2(math.e)

N_P = BATCH * SEQ
N_S = DEC_BATCH * DEC_SEQ
N_TOK = N_P + N_S
LANES = 128
MOD_ROWS = 8
ROUTER_ROWS = 40
POOL_PAD = 16

TM = 512
PROJ_TM = 1024
N_TILES = N_TOK // TM
P_TILES = N_P // TM
S_TILES_PER_SEQ = DEC_SEQ // TM
CTX_SEQ_PER_STEP = 4
TQ_S = 256
KEY_CHUNK = 512
TF_S = 512
EBLK = 512
X_SLOTS = 4
RUN_ALIGN = 8
RUN_SHIFT = 3
RUN_BITS = 7
assert RUN_ALIGN << (RUN_BITS - 1) == TM
SORT_ROWS = TOP_K * TM + N_EXPERTS * RUN_ALIGN
DISPATCH_SLOTS = 3
assert N_TILES >= DISPATCH_SLOTS
TILE_BITS = 8
assert SORT_ROWS < RUN_ALIGN << TILE_BITS
N_RUNS = N_TILES * N_EXPERTS
N_EBLK = (N_TOK * TOP_K + N_RUNS * (RUN_ALIGN - 1) + N_EXPERTS * (EBLK - 1) + EBLK - 1) // EBLK
XS_ROWS = N_EBLK * EBLK
MOD_NT = 1536


def _dot(a, b):
    return jnp.dot(a, b, preferred_element_type=F32)


def _rms(x):
    return x * lax.rsqrt(jnp.mean(x * x, axis=-1, keepdims=True) + RMS_EPS)


def _mod_row(i):
    return jnp.where(i < P_TILES, 0, 1 + (i - P_TILES) // S_TILES_PER_SEQ)


def _rope_block(i):
    p_tiles = N_P // PROJ_TM
    per_seq = DEC_SEQ // PROJ_TM
    return jnp.where(i < p_tiles, per_seq, (i - p_tiles) % per_seq)


def _mod_kernel(c_ref, w_ref, b_ref, o_ref):
    c = c_ref[...]
    s = (c * jax.nn.sigmoid(c)).astype(BF16)
    o_ref[...] = _dot(s, w_ref[...].astype(BF16)) + b_ref[...]


def _mod_call(c_all, w_ada, b_ada):
    nt = (N_MOD * D_MODEL) // MOD_NT
    return pl.pallas_call(
        _mod_kernel,
        grid=(DEPTH, nt),
        in_specs=[
            pl.BlockSpec((MOD_ROWS, D_MODEL), lambda l, j: (0, 0)),
            pl.BlockSpec((None, D_MODEL, MOD_NT), lambda l, j: (l, 0, j)),
            pl.BlockSpec((None, 1, MOD_NT), lambda l, j: (l, 0, j)),
        ],
        out_specs=pl.BlockSpec((None, MOD_ROWS, MOD_NT), lambda l, j: (l, 0, j)),
        out_shape=jax.ShapeDtypeStruct((DEPTH, MOD_ROWS, N_MOD * D_MODEL), F32),
        name="adaln_mod",
    )(c_all, w_ada, b_ada.reshape(DEPTH, 1, N_MOD * D_MODEL))


def _prenorm(x, mod, gain):
    return (_rms(x) * gain) * (1.0 + mod[:, D_MODEL:2 * D_MODEL]) + mod[:, 0:D_MODEL]


def _prenorm_kernel(xp_ref, xs_ref, mod_ref, gpre_ref, x_ref, hb_ref):
    x = jnp.where(pl.program_id(0) < P_TILES, xp_ref[...], xs_ref[...])
    x_ref[...] = x
    hb_ref[...] = _prenorm(x, mod_ref[...], gpre_ref[...]).astype(BF16)


def _prenorm_call(x_prompt, x_sample, mods, gpre):
    row = lambda i: (i, 0)
    return pl.pallas_call(
        _prenorm_kernel,
        grid=(N_TILES,),
        in_specs=[
            pl.BlockSpec((TM, D_MODEL), lambda i: (jnp.minimum(i, P_TILES - 1), 0)),
            pl.BlockSpec((TM, D_MODEL), lambda i: (jnp.maximum(i - P_TILES, 0), 0)),
            pl.BlockSpec((None, None, 1, 2 * D_MODEL), lambda i: (0, _mod_row(i), 0, 0)),
            pl.BlockSpec((None, 1, D_MODEL), lambda i: (0, 0, 0)),
        ],
        out_specs=[pl.BlockSpec((TM, D_MODEL), row), pl.BlockSpec((TM, D_MODEL), row)],
        out_shape=[jax.ShapeDtypeStruct((N_TOK, D_MODEL), F32),
                   jax.ShapeDtypeStruct((N_TOK, D_MODEL), BF16)],
        name="prenorm",
    )(x_prompt, x_sample, mods, gpre)


def _proj_kernel(hb_ref, w_ref, qg_ref, kg_ref, cos_ref, s1_ref, s2_ref, avg_ref, dft_ref,
                 ck_in_ref, cv_in_ref,
                 q_ref, k_ref, v_ref, xp_ref, xc_ref, xs_ref, g_ref, ck_ref, cv_ref):
    del ck_in_ref, cv_in_ref
    hb = hb_ref[...]

    cos = cos_ref[...]
    s1 = s1_ref[...]
    s2 = s2_ref[...]

    def rope(t):
        return (t * cos + pltpu.roll(t, LANES - ROPE_AXIS_DIM // 2, 1) * s1
                + pltpu.roll(t, ROPE_AXIS_DIM // 2, 1) * s2)

    head = _dot(hb, w_ref[:, 0:OFF_G])
    avg = avg_ref[...]
    for c in range(ATTN_WIDTH // LANES):
        lo = c * LANES
        q = head[:, lo:lo + LANES]
        ms = _dot((q * q).astype(BF16), avg)
        q = q * lax.rsqrt(ms + RMS_EPS) * qg_ref[...]
        q_ref[:, lo:lo + LANES] = (rope(q) * QK_SCALE).astype(BF16)

    k = head[:, OFF_K:OFF_K + KV_WIDTH]
    ms = _dot((k * k).astype(BF16), avg)
    k = rope(k * lax.rsqrt(ms + RMS_EPS) * kg_ref[...])
    v = head[:, OFF_K + KV_WIDTH:OFF_P]
    k_ref[...] = k
    v_ref[...] = v

    @pl.when(pl.program_id(0) < N_P // PROJ_TM)
    def _():
        for s in range(PROJ_TM // SEQ):
            ck_ref[s] = k[s * SEQ:(s + 1) * SEQ]
            cv_ref[s] = v[s * SEQ:(s + 1) * SEQ]

    xp_ref[...] = head[:, OFF_P:OFF_P + POOL_WIDTH]
    cs = _dot(head[:, OFF_P + POOL_WIDTH:OFF_G].astype(BF16), dft_ref[...])
    xc_ref[...] = cs[:, 0:FOURIER_WIDTH].astype(BF16)
    xs_ref[...] = cs[:, FOURIER_WIDTH:].astype(BF16)

    for c in range(N_BRANCHES):
        lo = OFF_G + c * D_MODEL
        g_ref[:, c * D_MODEL:(c + 1) * D_MODEL] = _dot(hb, w_ref[:, lo:lo + D_MODEL]).astype(BF16)


def _proj_call(l, hb, w_in, qg, kg, cos_t, s1_t, s2_t, avg, dft, cache_k, cache_v):
    row = lambda i: (i, 0)
    const2 = lambda i: (0, 0)
    per_layer = lambda i: (l, 0, 0)
    rope_spec = pl.BlockSpec((PROJ_TM, LANES), lambda i: (_rope_block(i), 0))
    cache_spec = pl.BlockSpec((PROJ_TM // SEQ, None, SEQ, KV_WIDTH),
                              lambda i: (jnp.minimum(i, N_P // PROJ_TM - 1), l, 0, 0))
    cache_shape = jax.ShapeDtypeStruct((BATCH, DEPTH, SEQ, KV_WIDTH), F32)
    return pl.pallas_call(
        _proj_kernel,
        grid=(N_TOK // PROJ_TM,),
        in_specs=[
            pl.BlockSpec((PROJ_TM, D_MODEL), row),
            pl.BlockSpec((None, D_MODEL, IN_WIDTH), per_layer),
            pl.BlockSpec((None, 1, LANES), per_layer),
            pl.BlockSpec((None, 1, LANES), per_layer),
            rope_spec, rope_spec, rope_spec,
            pl.BlockSpec((LANES, LANES), const2),
            pl.BlockSpec((FOURIER_WIDTH, 2 * FOURIER_WIDTH), const2),
            pl.BlockSpec(memory_space=pl.ANY),
            pl.BlockSpec(memory_space=pl.ANY),
        ],
        out_specs=[
            pl.BlockSpec((PROJ_TM, ATTN_WIDTH), row),
            pl.BlockSpec((PROJ_TM, KV_WIDTH), row),
            pl.BlockSpec((PROJ_TM, KV_WIDTH), row),
            pl.BlockSpec((PROJ_TM, POOL_WIDTH), row),
            pl.BlockSpec((PROJ_TM, FOURIER_WIDTH), row),
            pl.BlockSpec((PROJ_TM, FOURIER_WIDTH), row),
            pl.BlockSpec((PROJ_TM, N_BRANCHES * D_MODEL), row),
            cache_spec, cache_spec,
        ],
        out_shape=[
            jax.ShapeDtypeStruct((N_TOK, ATTN_WIDTH), BF16),
            jax.ShapeDtypeStruct((N_TOK, KV_WIDTH), F32),
            jax.ShapeDtypeStruct((N_TOK, KV_WIDTH), F32),
            jax.ShapeDtypeStruct((N_TOK, POOL_WIDTH), F32),
            jax.ShapeDtypeStruct((N_TOK, FOURIER_WIDTH), BF16),
            jax.ShapeDtypeStruct((N_TOK, FOURIER_WIDTH), BF16),
            jax.ShapeDtypeStruct((N_TOK, N_BRANCHES * D_MODEL), BF16),
            cache_shape, cache_shape,
        ],
        input_output_aliases={9: 7, 10: 8},
        name="proj",
    )(hb, w_in, qg, kg, cos_t, s1_t, s2_t, avg, dft, cache_k, cache_v)


def _attn_kernel(*refs, n_parts, tq):
    q_ref = refs[0]
    kv_refs = refs[1:1 + 2 * n_parts]
    o_ref = refs[-1]
    single_chunk = n_parts == 1 and kv_refs[0].shape[0] <= KEY_CHUNK
    outs = []
    for j in range(N_KV_HEADS):
        lo = j * HEAD_DIM
        qs = jnp.concatenate(
            [q_ref[:, (KV_GROUP * j + g) * HEAD_DIM:(KV_GROUP * j + g + 1) * HEAD_DIM]
             for g in range(KV_GROUP)], axis=0)
        m = acc = None
        for p in range(n_parts):
            k_ref, v_ref = kv_refs[2 * p], kv_refs[2 * p + 1]
            for c0 in range(0, k_ref.shape[0], KEY_CHUNK):
                c1 = min(c0 + KEY_CHUNK, k_ref.shape[0])
                kc = k_ref[c0:c1, lo:lo + HEAD_DIM].astype(BF16)
                s = lax.dot_general(qs, kc, (((1,), (1,)), ((), ())), preferred_element_type=F32)
                mc = jnp.max(s, axis=1, keepdims=True)
                m_new = mc if m is None else jnp.maximum(m, mc)
                e = jnp.exp2(s - m_new)
                vc = v_ref[c0:c1, lo:lo + HEAD_DIM].astype(BF16)
                if single_chunk:
                    den = jnp.sum(e, axis=1, keepdims=True)
                else:
                    vc = jnp.concatenate([vc, jnp.ones((c1 - c0, HEAD_DIM), BF16)], axis=1)
                pv = _dot(e.astype(BF16), vc)
                acc = pv if m is None else acc * jnp.exp2(m - m_new) + pv
                m = m_new
        o = acc / den if single_chunk else acc[:, 0:HEAD_DIM] / acc[:, HEAD_DIM:2 * HEAD_DIM]
        outs.extend(o[g * tq:(g + 1) * tq] for g in range(KV_GROUP))
    o_ref[...] = jnp.concatenate(outs, axis=1).astype(BF16)


def _attn_prompt_call(q, k, v):
    blk = lambda b: (b, 0)
    return pl.pallas_call(
        functools.partial(_attn_kernel, n_parts=1, tq=SEQ),
        grid=(BATCH,),
        in_specs=[
            pl.BlockSpec((SEQ, ATTN_WIDTH), blk),
            pl.BlockSpec((SEQ, KV_WIDTH), blk),
            pl.BlockSpec((SEQ, KV_WIDTH), blk),
        ],
        out_specs=pl.BlockSpec((SEQ, ATTN_WIDTH), blk),
        out_shape=jax.ShapeDtypeStruct((N_P, ATTN_WIDTH), BF16),
        name="attn_context",
    )(q, k, v)


def _attn_sample_call(l, q, k, v, cache_k, cache_v):
    nq = DEC_SEQ // TQ_S
    qrow = lambda b, i: (N_P // TQ_S + b * nq + i, 0)
    seq = lambda b, i: (N_P // DEC_SEQ + b, 0)
    cache = lambda b, i: (b, l, 0, 0)
    return pl.pallas_call(
        functools.partial(_attn_kernel, n_parts=2, tq=TQ_S),
        grid=(DEC_BATCH, nq),
        in_specs=[
            pl.BlockSpec((TQ_S, ATTN_WIDTH), qrow),
            pl.BlockSpec((None, None, PAST_LEN, KV_WIDTH), cache),
            pl.BlockSpec((None, None, PAST_LEN, KV_WIDTH), cache),
            pl.BlockSpec((DEC_SEQ, KV_WIDTH), seq),
            pl.BlockSpec((DEC_SEQ, KV_WIDTH), seq),
        ],
        out_specs=pl.BlockSpec((TQ_S, ATTN_WIDTH), lambda b, i: (b * nq + i, 0)),
        out_shape=jax.ShapeDtypeStruct((N_S, ATTN_WIDTH), BF16),
        name="attn_latent",
    )(q, cache_k, cache_v, k, v)


def _pool_kernel(xp_ref, bdw_ref, sc_ref, o_ref, pad_ref, *, seq_len, n_seq):
    for s in range(n_seq):
        _pool_sequence(xp_ref, bdw_ref, sc_ref, o_ref, pad_ref, s * seq_len, seq_len)


def _pool_sequence(xp_ref, bdw_ref, sc_ref, o_ref, pad_ref, row0, seq_len):
    half = POOL_WIDTH // 2
    zeros = jnp.zeros((POOL_PAD, POOL_WIDTH), F32)
    pad_ref[0:POOL_PAD, :] = zeros
    pad_ref[POOL_PAD + seq_len:, :] = zeros
    pad_ref[POOL_PAD:POOL_PAD + seq_len, :] = xp_ref[row0:row0 + seq_len, :]
    chunk = min(seq_len, 256)
    lane = lax.broadcasted_iota(jnp.int32, (chunk, half), 1)
    first = lane < POOL_GROUP_DIM
    for c in range(seq_len // chunk):
        base = c * chunk
        t = lax.broadcasted_iota(jnp.int32, (chunk, half), 0) + base

        def sh(j, lo):
            return pad_ref[POOL_PAD + base + j:POOL_PAD + base + j + chunk, lo:lo + half]

        def cnt(w):
            return (jnp.minimum(t + w // 2, seq_len) - jnp.maximum(t - w // 2, 0)).astype(F32)

        xa = sh(0, 0)
        w2 = sh(-1, 0) + xa
        w4 = w2 + sh(-2, 0) + sh(1, 0)
        xb = sh(0, half)
        w8 = xb
        for j in (-4, -3, -2, -1, 1, 2, 3):
            w8 = w8 + sh(j, half)
        w16 = w8
        for j in (-8, -7, -6, -5, 4, 5, 6, 7):
            w16 = w16 + sh(j, half)
        pa = jnp.where(first, w2 / cnt(2), w4 / cnt(4)) - xa
        pb = jnp.where(first, w8 / cnt(8), w16 / cnt(16)) - xb
        pooled = jnp.concatenate([pa, pb], axis=1).astype(BF16)
        o_ref[row0 + base:row0 + base + chunk, :] = (
            _dot(pooled, bdw_ref[...]) * sc_ref[...]).astype(BF16)


def _pool_call(l, xp, bdw, scale, seq_len, n_seq, blk0, seq_per_step):
    per_layer = lambda b: (l, 0, 0)
    rows = seq_len * seq_per_step
    return pl.pallas_call(
        functools.partial(_pool_kernel, seq_len=seq_len, n_seq=seq_per_step),
        grid=(n_seq // seq_per_step,),
        in_specs=[
            pl.BlockSpec((rows, POOL_WIDTH), lambda b: (blk0 + b, 0)),
            pl.BlockSpec((None, POOL_WIDTH, POOL_WIDTH), per_layer),
            pl.BlockSpec((None, 1, POOL_WIDTH), per_layer),
        ],
        out_specs=pl.BlockSpec((rows, POOL_WIDTH), lambda b: (b, 0)),
        out_shape=jax.ShapeDtypeStruct((n_seq * seq_len, POOL_WIDTH), BF16),
        scratch_shapes=[pltpu.VMEM((seq_len + 2 * POOL_PAD, POOL_WIDTH), F32)],
        name="pool_%d" % seq_len,
    )(xp, bdw, scale)


def _fourier_kernel(c_ref, s_ref, xc_ref, xs_ref, o_ref, *, scale, n_seq=1):
    seq_len = xc_ref.shape[0] // n_seq
    out_len = o_ref.shape[0] // n_seq
    for s in range(n_seq):
        rows = slice(s * seq_len, (s + 1) * seq_len)
        y = _dot(c_ref[...], xc_ref[rows, :]) - _dot(s_ref[...], xs_ref[rows, :])
        o_ref[s * out_len:(s + 1) * out_len, :] = (y * scale).astype(BF16)


def _fourier_prompt_call(cmat, smat, xc, xs):
    blk = lambda b: (b, 0)
    const2 = lambda b: (0, 0)
    rows = SEQ * CTX_SEQ_PER_STEP
    return pl.pallas_call(
        functools.partial(_fourier_kernel, scale=1.0 / math.sqrt(SEQ * FOURIER_GROUP_DIM),
                          n_seq=CTX_SEQ_PER_STEP),
        grid=(BATCH // CTX_SEQ_PER_STEP,),
        in_specs=[
            pl.BlockSpec((SEQ, SEQ), const2),
            pl.BlockSpec((SEQ, SEQ), const2),
            pl.BlockSpec((rows, FOURIER_WIDTH), blk),
            pl.BlockSpec((rows, FOURIER_WIDTH), blk),
        ],
        out_specs=pl.BlockSpec((rows, FOURIER_WIDTH), blk),
        out_shape=jax.ShapeDtypeStruct((N_P, FOURIER_WIDTH), BF16),
        name="fourier_context",
    )(cmat, smat, xc, xs)


def _fourier_sample_call(cmat, smat, xc, xs):
    nt = DEC_SEQ // TF_S
    rows = lambda b, i: (i, 0)
    seq = lambda b, i: (N_P // DEC_SEQ + b, 0)
    out = lambda b, i: (b * nt + i, 0)
    return pl.pallas_call(
        functools.partial(_fourier_kernel, scale=1.0 / math.sqrt(DEC_SEQ * FOURIER_GROUP_DIM)),
        grid=(DEC_BATCH, nt),
        in_specs=[
            pl.BlockSpec((TF_S, DEC_SEQ), rows),
            pl.BlockSpec((TF_S, DEC_SEQ), rows),
            pl.BlockSpec((DEC_SEQ, FOURIER_WIDTH), seq),
            pl.BlockSpec((DEC_SEQ, FOURIER_WIDTH), seq),
        ],
        out_specs=pl.BlockSpec((TF_S, FOURIER_WIDTH), out),
        out_shape=jax.ShapeDtypeStruct((N_S, FOURIER_WIDTH), BF16),
        name="fourier_latent",
    )(cmat, smat, xc, xs)


def _merge_kernel(x_ref, attn_p_ref, attn_s_ref, pool_p_ref, pool_s_ref, four_p_ref, four_s_ref,
                  g_ref, mod_ref, gpost_ref, gffn_ref,
                  wa_ref, wp_ref, wf_ref, wo_ref, wrt_ref, brt_ref, triu_ref, lower_ref,
                  x1_ref, h2_ref, pos_ref, post_ref, mw_ref, cnt_ref):
    i = pl.program_id(0)
    mod = mod_ref[...]
    g1 = mod[:, 2 * D_MODEL:3 * D_MODEL]
    sh2 = mod[:, 3 * D_MODEL:4 * D_MODEL]
    sc2 = mod[:, 4 * D_MODEL:5 * D_MODEL]

    def gate2(c):
        return jnp.tanh(g_ref[:, c * D_MODEL:(c + 1) * D_MODEL].astype(F32)) + 1.0

    def branch(p_ref, s_ref):
        return jnp.where(i < P_TILES, p_ref[...], s_ref[...])

    merged = gate2(0) * _dot(branch(attn_p_ref, attn_s_ref), wa_ref[...])
    merged = merged + gate2(1) * _dot(branch(pool_p_ref, pool_s_ref), wp_ref[...])
    merged = merged + gate2(2) * _dot(branch(four_p_ref, four_s_ref), wf_ref[...])
    mix = _dot((0.5 * merged).astype(BF16), wo_ref[...])
    x1 = x_ref[...] + g1 * (_rms(mix) * gpost_ref[...])
    x1_ref[...] = x1
    h2 = (_rms(x1) * gffn_ref[...]) * (1.0 + sc2) + sh2
    h2b = h2.astype(BF16)
    h2_ref[...] = h2b

    logits = lax.dot_general(wrt_ref[...], h2b, (((1,), (1,)), ((), ())),
                             preferred_element_type=F32) + brt_ref[...]
    top = logits[0:ROUTER_ROWS, :]
    row = lax.broadcasted_iota(jnp.int32, (ROUTER_ROWS, TM), 0)
    rowf = row.astype(F32)
    neg = jnp.float32(-3e38)
    big = jnp.float32(1e9)
    is_g = (row >= N_EXPERTS) & (row < N_EXPERTS + N_EXPERT_GROUPS)
    lg = jnp.where(is_g, top, neg)
    gmax = jnp.max(lg, axis=0, keepdims=True)
    g_sel = jnp.min(jnp.where(lg == gmax, rowf - N_EXPERTS, big), axis=0, keepdims=True)
    p_g = 1.0 / jnp.sum(jnp.where(is_g, jnp.exp(top - gmax), 0.0), axis=0, keepdims=True)
    grp = lax.shift_right_logical(row, int(math.log2(EXPERTS_PER_GROUP))).astype(F32)
    in_grp = (row < N_EXPERTS) & (grp == g_sel)
    le = jnp.where(in_grp, top, neg)
    v1 = jnp.max(le, axis=0, keepdims=True)
    i1 = jnp.min(jnp.where(le == v1, rowf, big), axis=0, keepdims=True)
    le2 = jnp.where(rowf == i1, neg, le)
    v2 = jnp.max(le2, axis=0, keepdims=True)
    i2 = jnp.min(jnp.where(le2 == v2, rowf, big), axis=0, keepdims=True)
    e21 = jnp.exp(v2 - v1)
    w1 = p_g / (1.0 + e21)
    w2 = p_g * e21 / (1.0 + e21)

    rows_all = lax.broadcasted_iota(jnp.int32, (LANES, TM), 0)
    rows_allf = rows_all.astype(F32)
    oh1 = (rows_allf == i1).astype(F32)
    oh2 = (rows_allf == i2).astype(F32)
    ohb = (oh1 + oh2).astype(BF16)
    before = _dot(ohb, triu_ref[...])
    cnt = _dot(ohb, jnp.ones((TM, LANES), BF16))
    cnt_pad = (lax.shift_right_logical(cnt.astype(jnp.int32) + (RUN_ALIGN - 1), RUN_SHIFT)
               * RUN_ALIGN).astype(F32)
    run_off = _dot(lower_ref[...], cnt_pad.astype(BF16))
    slot = jnp.concatenate([run_off] * (TM // LANES), axis=1) + before
    p1 = jnp.sum(slot * oh1, axis=0, keepdims=True)
    p2 = jnp.sum(slot * oh2, axis=0, keepdims=True)
    cnt_ref[...] = cnt

    meta = jnp.where(rows_all == 0, p1, jnp.where(rows_all == 1, p2,
                     jnp.where(rows_all == 2, w1, jnp.where(rows_all == 3, w2, 0.0))))
    post_ref[...] = meta[0:MOD_ROWS, :].astype(jnp.int32)
    meta_t = meta.T
    pos_ref[...] = meta_t.astype(jnp.int32)
    mw_ref[...] = meta_t


def _merge_call(l, x, attn, pool, four, gates, mods, gpost, gffn, wa, wp, wf, wo, wrt, brt, triu,
                lower):
    row = lambda i: (i, 0)
    prow = lambda i: (jnp.minimum(i, P_TILES - 1), 0)
    srow = lambda i: (jnp.maximum(i - P_TILES, 0), 0)
    const2 = lambda i: (0, 0)
    per_layer = lambda i: (l, 0, 0)
    return pl.pallas_call(
        _merge_kernel,
        grid=(N_TILES,),
        in_specs=[
            pl.BlockSpec((TM, D_MODEL), row),
            pl.BlockSpec((TM, ATTN_WIDTH), prow),
            pl.BlockSpec((TM, ATTN_WIDTH), srow),
            pl.BlockSpec((TM, POOL_WIDTH), prow),
            pl.BlockSpec((TM, POOL_WIDTH), srow),
            pl.BlockSpec((TM, FOURIER_WIDTH), prow),
            pl.BlockSpec((TM, FOURIER_WIDTH), srow),
            pl.BlockSpec((TM, N_BRANCHES * D_MODEL), row),
            pl.BlockSpec((None, None, 1, N_MOD * D_MODEL), lambda i: (l, _mod_row(i), 0, 0)),
            pl.BlockSpec((None, 1, D_MODEL), per_layer),
            pl.BlockSpec((None, 1, D_MODEL), per_layer),
            pl.BlockSpec((None, ATTN_WIDTH, D_MODEL), per_layer),
            pl.BlockSpec((None, POOL_WIDTH, D_MODEL), per_layer),
            pl.BlockSpec((None, FOURIER_WIDTH, D_MODEL), per_layer),
            pl.BlockSpec((None, D_MODEL, D_MODEL), per_layer),
            pl.BlockSpec((None, LANES, D_MODEL), per_layer),
            pl.BlockSpec((None, LANES, TM), per_layer),
            pl.BlockSpec((TM, TM), const2),
            pl.BlockSpec((LANES, LANES), const2),
        ],
        out_specs=[
            pl.BlockSpec((TM, D_MODEL), row),
            pl.BlockSpec((TM, D_MODEL), row),
            pl.BlockSpec((TM, LANES), row),
            pl.BlockSpec((MOD_ROWS, TM), lambda i: (0, i)),
            pl.BlockSpec((TM, LANES), row),
            pl.BlockSpec((LANES, LANES), row),
        ],
        out_shape=[
            jax.ShapeDtypeStruct((N_TOK, D_MODEL), F32),
            jax.ShapeDtypeStruct((N_TOK, D_MODEL), BF16),
            jax.ShapeDtypeStruct((N_TOK, LANES), jnp.int32),
            jax.ShapeDtypeStruct((MOD_ROWS, N_TOK), jnp.int32),
            jax.ShapeDtypeStruct((N_TOK, LANES), F32),
            jax.ShapeDtypeStruct((N_TILES * LANES, LANES), F32),
        ],
        name="merge_router",
    )(x, attn[0], attn[1], pool[0], pool[1], four[0], four[1], gates, mods, gpost, gffn,
      wa, wp, wf, wo, wrt, brt, triu, lower)


HALF = D_MODEL // 2
U32 = jnp.uint32
HI_MASK = 0xFFFF0000


def _pack_rows(x):
    lo = lax.bitcast_convert_type(x[:, :HALF], U32)
    hi = lax.bitcast_convert_type(x[:, HALF:], U32)
    return lax.shift_right_logical(lo, U32(16)) | (hi & U32(HI_MASK))


def _unpack_rows(u):
    lo = lax.bitcast_convert_type(lax.shift_left(u, U32(16)), F32)
    hi = lax.bitcast_convert_type(u & U32(HI_MASK), F32)
    return jnp.concatenate([lo.astype(BF16), hi.astype(BF16)], axis=1)


def _rows(ref, start, size):
    return ref.at[pl.ds(pl.multiple_of(start, RUN_ALIGN), size)]


def _start_tile_copies(tile, n_ref, trow_ref, xrow_ref, make_copy):
    for bit in range(RUN_BITS):
        lst = bit * N_TILES + tile

        def body(k, carry, bit=bit, lst=lst):
            j = lst * N_EXPERTS + k
            make_copy(trow_ref[j], xrow_ref[j], RUN_ALIGN << bit).start()
            return carry

        lax.fori_loop(0, n_ref[lst], body, 0)


def _wait_tile_copies(total, src_ref, dst_ref, sem):
    for b in range(TILE_BITS):
        size = RUN_ALIGN << b

        @pl.when((lax.shift_right_logical(total, b) & 1) == 1)
        def _(size=size):
            pltpu.make_async_copy(_rows(src_ref, 0, size), _rows(dst_ref, 0, size), sem).wait()


def _dispatch_kernel(n_ref, trow_ref, xrow_ref, tsum_ref, h_ref, post_ref, xs_in_ref, xs_ref,
                     sorted_ref, sem):
    del xs_in_ref
    i = pl.program_id(0)
    slot = lax.rem(i, DISPATCH_SLOTS)
    rows = lax.broadcasted_iota(jnp.int32, (SORT_ROWS, TM), 0)
    p = post_ref[...]
    perm = jnp.where(rows == p[0:1, :], 1.0, jnp.where(rows == p[1:2, :], 1.0, 0.0)).astype(BF16)
    sorted_ref[slot] = _pack_rows(_dot(perm, h_ref[...]))

    def wait(tile, slot):
        _wait_tile_copies(tsum_ref[tile], sorted_ref.at[slot], xs_ref, sem.at[slot])

    lag = DISPATCH_SLOTS - 1

    @pl.when(i >= lag)
    def _():
        wait(i - lag, lax.rem(i + 1, DISPATCH_SLOTS))

    _start_tile_copies(i, n_ref, trow_ref, xrow_ref, lambda t, x, size: pltpu.make_async_copy(
        _rows(sorted_ref.at[slot], t, size), _rows(xs_ref, x, size), sem.at[slot]))

    @pl.when(i == pl.num_programs(0) - 1)
    def _():
        for back in reversed(range(lag)):
            wait(i - back, lax.rem(i - back, DISPATCH_SLOTS))


def _dispatch_call(n_list, trow, xrow, tsum, h2, post, xs):
    return pl.pallas_call(
        _dispatch_kernel,
        grid_spec=pltpu.PrefetchScalarGridSpec(
            num_scalar_prefetch=4,
            grid=(N_TILES,),
            in_specs=[
                pl.BlockSpec((TM, D_MODEL), lambda i, *_: (i, 0)),
                pl.BlockSpec((MOD_ROWS, TM), lambda i, *_: (0, i)),
                pl.BlockSpec(memory_space=pl.ANY),
            ],
            out_specs=pl.BlockSpec(memory_space=pl.ANY),
            scratch_shapes=[pltpu.VMEM((DISPATCH_SLOTS, SORT_ROWS, HALF), U32),
                            pltpu.SemaphoreType.DMA((DISPATCH_SLOTS,))],
        ),
        out_shape=jax.ShapeDtypeStruct((XS_ROWS, HALF), U32),
        input_output_aliases={6: 0},
        compiler_params=pltpu.CompilerParams(dimension_semantics=("arbitrary",)),
        name="moe_dispatch",
    )(n_list, trow, xrow, tsum, h2, post, xs)


def _expert_kernel(bstart_ref, bcnt_ref, nu_ref, wg_ref, wu_ref, wd_ref, xs_ref, y_prev_ref, y_ref,
                   wgub, wdb, xbuf, ybuf, xsem, ysem):
    del y_prev_ref
    e = pl.program_id(0)
    n_used = nu_ref[0]

    def x_copy(g, slot):
        rows = pl.ds(pl.multiple_of(g * EBLK, EBLK), EBLK)
        return pltpu.make_async_copy(xs_ref.at[rows], xbuf.at[slot], xsem.at[slot])

    def y_copy(g, slot):
        rows = pl.ds(pl.multiple_of(g * EBLK, EBLK), EBLK)
        return pltpu.make_async_copy(ybuf.at[slot], y_ref.at[rows], ysem.at[slot])

    @pl.when(e == 0)
    def _():
        for g in range(X_SLOTS - 1):
            @pl.when(g < n_used)
            def _(g=g):
                x_copy(g, g).start()

    wgub[:, 0:EXPERT_FF] = wg_ref[...].astype(BF16)
    wgub[:, EXPERT_FF:] = wu_ref[...].astype(BF16)
    wdb[...] = wd_ref[...].astype(BF16)
    first = bstart_ref[e]

    def block(j, carry):
        g = first + j
        slot = lax.rem(g, 2)
        xslot = lax.rem(g, X_SLOTS)
        x_copy(g, xslot).wait()

        @pl.when(g + (X_SLOTS - 1) < n_used)
        def _():
            x_copy(g + (X_SLOTS - 1), lax.rem(g + (X_SLOTS - 1), X_SLOTS)).start()

        xb = _unpack_rows(xbuf[xslot])
        gu = _dot(xb, wgub[...])
        gate = gu[:, 0:EXPERT_FF]
        act = (gate * jax.nn.sigmoid(gate)) * gu[:, EXPERT_FF:]
        y = _dot(act.astype(BF16), wdb[...])

        @pl.when(g >= 2)
        def _():
            y_copy(g - 2, slot).wait()

        ybuf[slot] = _pack_rows(y.astype(BF16).astype(F32))
        y_copy(g, slot).start()
        return carry

    lax.fori_loop(0, bcnt_ref[e], block, 0)

    @pl.when(e == pl.num_programs(0) - 1)
    def _():
        @pl.when(n_used >= 2)
        def _():
            y_copy(n_used - 2, lax.rem(n_used, 2)).wait()

        y_copy(n_used - 1, lax.rem(n_used - 1, 2)).wait()


def _expert_call(l, blk_start, blk_cnt, n_used, xs, y_prev, w_gate, w_up, w_down):
    wsel = lambda e, *_: (l, e, 0, 0)
    return pl.pallas_call(
        _expert_kernel,
        grid_spec=pltpu.PrefetchScalarGridSpec(
            num_scalar_prefetch=3,
            grid=(N_EXPERTS,),
            in_specs=[
                pl.BlockSpec((None, None, D_MODEL, EXPERT_FF), wsel),
                pl.BlockSpec((None, None, D_MODEL, EXPERT_FF), wsel),
                pl.BlockSpec((None, None, EXPERT_FF, D_MODEL), wsel),
                pl.BlockSpec(memory_space=pl.ANY),
                pl.BlockSpec(memory_space=pl.ANY),
            ],
            out_specs=pl.BlockSpec(memory_space=pl.ANY),
            scratch_shapes=[
                pltpu.VMEM((D_MODEL, 2 * EXPERT_FF), BF16),
                pltpu.VMEM((EXPERT_FF, D_MODEL), BF16),
                pltpu.VMEM((X_SLOTS, EBLK, HALF), U32),
                pltpu.VMEM((2, EBLK, HALF), U32),
                pltpu.SemaphoreType.DMA((X_SLOTS,)),
                pltpu.SemaphoreType.DMA((2,)),
            ],
        ),
        out_shape=jax.ShapeDtypeStruct((XS_ROWS, HALF), U32),
        input_output_aliases={7: 0},
        compiler_params=pltpu.CompilerParams(dimension_semantics=("arbitrary",)),
        name="moe_experts",
    )(blk_start, blk_cnt, n_used, w_gate, w_up, w_down, xs, y_prev)


def _combine_kernel(n_ref, trow_ref, xrow_ref, tsum_ref, y_ref, x1_ref, pos_ref, mw_ref, g2_ref,
                    gpost_ref, *rest, has_next):
    if has_next:
        nmod_ref, ngpre_ref, o_ref, hb_ref, ybuf, sem = rest
    else:
        o_ref, os_ref, ybuf, sem = rest
    i = pl.program_id(0)
    slot = lax.rem(i, 2)

    def start(tile, slot):
        _start_tile_copies(tile, n_ref, trow_ref, xrow_ref, lambda t, x, size: pltpu.make_async_copy(
            _rows(y_ref, x, size), _rows(ybuf.at[slot], t, size), sem.at[slot]))

    @pl.when(i == 0)
    def _():
        ybuf[...] = jnp.zeros_like(ybuf)
        start(0, 0)

    @pl.when(i + 1 < pl.num_programs(0))
    def _():
        start(i + 1, 1 - slot)

    _wait_tile_copies(tsum_ref[i], y_ref, ybuf.at[slot], sem.at[slot])

    pos = pos_ref[...]
    mw = mw_ref[...]
    cols = lax.broadcasted_iota(jnp.int32, (TM, SORT_ROWS), 1)
    qw = (jnp.where(cols == pos[:, 0:1], mw[:, 2:3], 0.0)
          + jnp.where(cols == pos[:, 1:2], mw[:, 3:4], 0.0)).astype(BF16)
    ffn = _dot(qw, _unpack_rows(ybuf[slot]))
    x2 = x1_ref[...] + g2_ref[...] * (_rms(ffn) * gpost_ref[...])
    if has_next:
        o_ref[...] = x2
        hb_ref[...] = _prenorm(x2, nmod_ref[...], ngpre_ref[...]).astype(BF16)
    else:
        @pl.when(i < P_TILES)
        def _():
            o_ref[...] = x2

        os_ref[...] = x2


def _combine_call(l, n_list, trow, xrow, tsum, y, x1, pos, mw, mods, gpost, gpre):
    has_next = l + 1 < DEPTH
    row = lambda i, *_: (i, 0)
    in_specs = [
        pl.BlockSpec(memory_space=pl.ANY),
        pl.BlockSpec((TM, D_MODEL), row),
        pl.BlockSpec((TM, LANES), row),
        pl.BlockSpec((TM, LANES), row),
        pl.BlockSpec((None, None, 1, D_MODEL), lambda i, *_: (l, _mod_row(i), 0, N_MOD - 1)),
        pl.BlockSpec((None, 1, D_MODEL), lambda i, *_: (l, 0, 0)),
    ]
    args = [n_list, trow, xrow, tsum, y, x1, pos, mw, mods, gpost]
    if has_next:
        in_specs += [
            pl.BlockSpec((None, None, 1, 2 * D_MODEL), lambda i, *_: (l + 1, _mod_row(i), 0, 0)),
            pl.BlockSpec((None, 1, D_MODEL), lambda i, *_: (l + 1, 0, 0)),
        ]
        args += [mods, gpre]
        out_specs = [pl.BlockSpec((TM, D_MODEL), row), pl.BlockSpec((TM, D_MODEL), row)]
        out_shape = [jax.ShapeDtypeStruct((N_TOK, D_MODEL), F32),
                     jax.ShapeDtypeStruct((N_TOK, D_MODEL), BF16)]
    else:
        out_specs = [
            pl.BlockSpec((TM, D_MODEL), lambda i, *_: (jnp.minimum(i, P_TILES - 1), 0)),
            pl.BlockSpec((TM, D_MODEL), lambda i, *_: (jnp.maximum(i - P_TILES, 0), 0)),
        ]
        out_shape = [jax.ShapeDtypeStruct((N_P, D_MODEL), F32),
                     jax.ShapeDtypeStruct((N_S, D_MODEL), F32)]
    return pl.pallas_call(
        functools.partial(_combine_kernel, has_next=has_next),
        grid_spec=pltpu.PrefetchScalarGridSpec(
            num_scalar_prefetch=4,
            grid=(N_TILES,),
            in_specs=in_specs,
            out_specs=out_specs,
            scratch_shapes=[pltpu.VMEM((2, SORT_ROWS, HALF), U32), pltpu.SemaphoreType.DMA((2,))],
        ),
        out_shape=out_shape,
        compiler_params=pltpu.CompilerParams(dimension_semantics=("arbitrary",)),
        name="moe_combine",
    )(*args)


def _dft_mats(n):
    k = np.arange(n, dtype=np.int64)
    ang = 2.0 * np.pi * ((k[:, None] * k[None, :]) % n).astype(np.float64) / n
    return np.cos(ang), np.sin(ang)


def _block_diag(m, reps):
    n = m.shape[0]
    out = np.zeros((n * reps, n * reps), m.dtype)
    for r in range(reps):
        out[r * n:(r + 1) * n, r * n:(r + 1) * n] = m
    return out


def _rope_tables():
    t = np.arange(DEC_SEQ)
    pos = np.stack([t // GRID_W, t % GRID_W], axis=1).astype(np.float64)
    n_freq = ROPE_AXIS_DIM // 2
    inv = ROPE_BASE ** (-np.arange(n_freq, dtype=np.float64) * 2.0 / ROPE_AXIS_DIM)
    ang = pos[:, :, None] * inv[None, None, :]
    cos = np.cos(ang)
    sin = np.sin(ang)
    zero = np.zeros_like(sin[:, 0])
    cos_h = np.concatenate([cos[:, 0], cos[:, 0], cos[:, 1], cos[:, 1]], axis=1)
    s1_h = np.concatenate([-sin[:, 0], zero, -sin[:, 1], zero], axis=1)
    s2_h = np.concatenate([zero, sin[:, 0], zero, sin[:, 1]], axis=1)
    reps = LANES // HEAD_DIM

    def table(a, ident):
        a = np.tile(a, (1, reps))
        pad = np.full((PROJ_TM, LANES), ident, np.float64)
        return jnp.asarray(np.concatenate([a, pad], axis=0), F32)

    return table(cos_h, 1.0), table(s1_h, 0.0), table(s2_h, 0.0)


def _copy_lists(run_cnt, tile_off, xs_off):
    bit = jnp.arange(RUN_BITS, dtype=jnp.int32)[:, None, None]
    has = (run_cnt[None] >> bit) & 1
    before = ((run_cnt[None] >> (bit + 1)) << (bit + 1)) * RUN_ALIGN
    slot = jnp.cumsum(has, axis=-1) - has
    hit = (has[..., None] == 1) & (slot[..., None] == jnp.arange(N_EXPERTS, dtype=jnp.int32))

    def compact(rows):
        return jnp.sum(jnp.where(hit, rows[..., None], 0), axis=-2).reshape(-1)

    return (jnp.sum(has, axis=-1).reshape(-1), compact(tile_off[None] + before),
            compact(xs_off[None] + before))


def kernel(x_prompt, x_sample, cache_k, cache_v, c, c_ctx, w_ada, b_ada, norm_mix_pre,
           norm_mix_post, norm_ffn_pre, norm_ffn_post, w_in, q_norm, k_norm, w_attn_out,
           w_pool_group, pool_scale, w_pool_out, w_fourier_out, w_out, w_router_group,
           b_router_group, w_router_expert, b_router_expert, w_expert_gate, w_expert_up,
           w_expert_down):
    cos_t, s1_t, s2_t = _rope_tables()
    avg = jnp.asarray(_block_diag(np.full((HEAD_DIM, HEAD_DIM), 1.0 / HEAD_DIM), LANES // HEAD_DIM), BF16)
    c64, s64 = _dft_mats(FOURIER_GROUP_DIM)
    n_fg = FOURIER_WIDTH // FOURIER_GROUP_DIM
    dft_ch = jnp.asarray(np.concatenate([_block_diag(c64, n_fg), _block_diag(s64, n_fg)], axis=1), BF16)
    cp, sp = _dft_mats(SEQ)
    cp, sp = jnp.asarray(cp, BF16), jnp.asarray(sp, BF16)
    cl, sl = _dft_mats(DEC_SEQ)
    cl, sl = jnp.asarray(cl, BF16), jnp.asarray(sl, BF16)
    triu = jnp.asarray(np.triu(np.ones((TM, TM)), 1), BF16)
    lower = jnp.asarray(np.tril(np.ones((LANES, LANES)), -1), BF16)

    in_scale = np.ones((IN_WIDTH,), np.float32)
    in_scale[OFF_G:] = 0.5
    w_in_b = (w_in * in_scale).astype(BF16)
    wa_b = w_attn_out.astype(BF16)
    wp_b = w_pool_out.astype(BF16)
    wf_b = w_fourier_out.astype(BF16)
    wo_b = w_out.astype(BF16)
    pad_r = jnp.zeros((DEPTH, D_MODEL, LANES - N_EXPERTS - N_EXPERT_GROUPS), F32)
    wrt = jnp.concatenate([w_router_expert, w_router_group, pad_r], axis=2).astype(BF16)
    wrt = jnp.transpose(wrt, (0, 2, 1))
    br = jnp.concatenate([b_router_expert, b_router_group,
                          jnp.zeros((DEPTH, LANES - N_EXPERTS - N_EXPERT_GROUPS), F32)], axis=1)
    brt = jnp.broadcast_to(br[:, :, None], (DEPTH, LANES, TM))
    n_pg = POOL_WIDTH // POOL_GROUP_DIM
    bdw = jnp.zeros((DEPTH, POOL_WIDTH, POOL_WIDTH), F32)
    for g in range(n_pg):
        lo = g * POOL_GROUP_DIM
        bdw = bdw.at[:, lo:lo + POOL_GROUP_DIM, lo:lo + POOL_GROUP_DIM].set(w_pool_group[:, g])
    bdw = bdw.astype(BF16)
    pscale = pool_scale.reshape(DEPTH, 1, POOL_WIDTH)
    qg = jnp.tile(q_norm, (1, LANES // HEAD_DIM)).reshape(DEPTH, 1, LANES)
    kg = jnp.tile(k_norm, (1, LANES // HEAD_DIM)).reshape(DEPTH, 1, LANES)
    gpre = norm_mix_pre.reshape(DEPTH, 1, D_MODEL)
    gpost = norm_mix_post.reshape(DEPTH, 1, D_MODEL)
    gffn = norm_ffn_pre.reshape(DEPTH, 1, D_MODEL)
    gfpost = norm_ffn_post.reshape(DEPTH, 1, D_MODEL)
    ck = cache_k.reshape(DEC_BATCH, DEPTH, PAST_LEN, KV_WIDTH)
    cv = cache_v.reshape(DEC_BATCH, DEPTH, PAST_LEN, KV_WIDTH)

    c_all = jnp.concatenate([c_ctx[None, :], c, jnp.zeros((MOD_ROWS - 1 - DEC_BATCH, D_MODEL), F32)], axis=0)
    mods = _mod_call(c_all, w_ada, b_ada).reshape(DEPTH, MOD_ROWS, 1, N_MOD * D_MODEL)

    xs_buf = jnp.zeros((XS_ROWS, HALF), U32)
    y = jnp.zeros((XS_ROWS, HALF), U32)
    new_k = jnp.zeros((BATCH, DEPTH, SEQ, KV_WIDTH), F32)
    new_v = jnp.zeros((BATCH, DEPTH, SEQ, KV_WIDTH), F32)
    x, hb = _prenorm_call(x_prompt.reshape(N_P, D_MODEL), x_sample.reshape(N_S, D_MODEL), mods, gpre)
    for l in range(DEPTH):
        q, k, v, xp, xc, xsn, gates, new_k, new_v = _proj_call(
            l, hb, w_in_b, qg, kg, cos_t, s1_t, s2_t, avg, dft_ch, new_k, new_v)
        attn = (_attn_prompt_call(q, k, v), _attn_sample_call(l, q, k, v, ck, cv))
        pool = (_pool_call(l, xp, bdw, pscale, SEQ, BATCH, 0, CTX_SEQ_PER_STEP),
                _pool_call(l, xp, bdw, pscale, DEC_SEQ, DEC_BATCH, N_P // DEC_SEQ, 1))
        four = (_fourier_prompt_call(cp, sp, xc, xsn), _fourier_sample_call(cl, sl, xc, xsn))
        x1, h2, pos, post, mw, cnt = _merge_call(l, x, attn, pool, four, gates, mods, gpost, gffn,
                                                 wa_b, wp_b, wf_b, wo_b, wrt, brt, triu, lower)
        runs = cnt.reshape(N_TILES, LANES, LANES)[:, :N_EXPERTS, 0].astype(jnp.int32)
        runs = ((runs + RUN_ALIGN - 1) // RUN_ALIGN) * RUN_ALIGN
        tile_off = jnp.cumsum(runs, axis=1) - runs
        rows_e = jnp.sum(runs, axis=0)
        padded = ((rows_e + EBLK - 1) // EBLK) * EBLK
        pad_end = jnp.cumsum(padded)
        xs_off = (pad_end - padded)[None, :] + jnp.cumsum(runs, axis=0) - runs
        blk_cnt = padded // EBLK
        blk_start = (pad_end - padded) // EBLK
        n_used = pad_end[-1:] // EBLK
        n_list, trow, xrow = _copy_lists(runs // RUN_ALIGN, tile_off, xs_off)
        tile_cnt = jnp.sum(runs, axis=1) // RUN_ALIGN
        xs_buf = _dispatch_call(n_list, trow, xrow, tile_cnt, h2, post, xs_buf)
        y = _expert_call(l, blk_start, blk_cnt, n_used, xs_buf, y,
                         w_expert_gate, w_expert_up, w_expert_down)
        outs = _combine_call(l, n_list, trow, xrow, tile_cnt, y, x1, pos, mw, mods, gfpost, gpre)
        x, hb = outs

    y_prompt = outs[0].reshape(BATCH, SEQ, D_MODEL)
    y_sample = outs[1].reshape(DEC_BATCH, DEC_SEQ, D_MODEL)
    cache_shape = (BATCH, DEPTH, SEQ, N_KV_HEADS, HEAD_DIM)
    return (y_prompt, y_sample, new_k.reshape(cache_shape), new_v.reshape(cache_shape))
```

```python
import functools
import math

import numpy as np
import jax
import jax.numpy as jnp
from jax import lax
from jax.experimental import pallas as pl
from jax.experimental.pallas import tpu as pltpu

F32 = jnp.float32
BF16 = jnp.bfloat16

D_MODEL = 1024
BATCH = 32
SEQ = 256
DEPTH = 4
DEC_BATCH = 2
DEC_SEQ = 2048
PAST_LEN = 512
GRID_W = 64
N_HEADS = 8
N_KV_HEADS = 2
HEAD_DIM = 64
KV_GROUP = N_HEADS // N_KV_HEADS
ATTN_WIDTH = N_HEADS * HEAD_DIM
KV_WIDTH = N_KV_HEADS * HEAD_DIM
ROPE_AXIS_DIM = HEAD_DIM // 2
ROPE_BASE = 10000.0
POOL_WINDOWS = (2, 4, 8, 16)
POOL_WIDTH = 256
POOL_GROUP_DIM = 64
FOURIER_WIDTH = 256
FOURIER_GROUP_DIM = 64
N_BRANCHES = 3
OFF_K = ATTN_WIDTH
OFF_P = ATTN_WIDTH + 2 * KV_WIDTH
OFF_G = OFF_P + POOL_WIDTH + FOURIER_WIDTH
IN_WIDTH = OFF_G + N_BRANCHES * D_MODEL
N_EXPERT_GROUPS = 4
EXPERTS_PER_GROUP = 8
N_EXPERTS = N_EXPERT_GROUPS * EXPERTS_PER_GROUP
TOP_K = 2
EXPERT_FF = 256
N_MOD = 6
RMS_EPS = 1e-6
QK_SCALE = HEAD_DIM ** -0.5 * math.log2(math.e)

N_P = BATCH * SEQ
N_S = DEC_BATCH * DEC_SEQ
N_TOK = N_P + N_S
LANES = 128
MOD_ROWS = 8
ROUTER_ROWS = 40
POOL_PAD = 16

TM = 512
PROJ_TM = 1024
N_TILES = N_TOK // TM
P_TILES = N_P // TM
S_TILES_PER_SEQ = DEC_SEQ // TM
CTX_SEQ_PER_STEP = 4
TQ_S = 256
KEY_CHUNK = 512
TF_S = 512
EBLK = 512
X_SLOTS = 4
RUN_ALIGN = 8
RUN_SHIFT = 3
RUN_BITS = 7
assert RUN_ALIGN << (RUN_BITS - 1) == TM
SORT_ROWS = TOP_K * TM + N_EXPERTS * RUN_ALIGN
DISPATCH_SLOTS = 3
assert N_TILES >= DISPATCH_SLOTS
TILE_BITS = 8
assert SORT_ROWS < RUN_ALIGN << TILE_BITS
N_RUNS = N_TILES * N_EXPERTS
N_EBLK = (N_TOK * TOP_K + N_RUNS * (RUN_ALIGN - 1) + N_EXPERTS * (EBLK - 1) + EBLK - 1) // EBLK
XS_ROWS = N_EBLK * EBLK
MOD_NT = 1536


def _dot(a, b):
    return jnp.dot(a, b, preferred_element_type=F32)


def _rms(x):
    return x * lax.rsqrt(jnp.mean(x * x, axis=-1, keepdims=True) + RMS_EPS)


def _mod_row(i):
    return jnp.where(i < P_TILES, 0, 1 + (i - P_TILES) // S_TILES_PER_SEQ)


def _rope_block(i):
    p_tiles = N_P // PROJ_TM
    per_seq = DEC_SEQ // PROJ_TM
    return jnp.where(i < p_tiles, per_seq, (i - p_tiles) % per_seq)


def _mod_kernel(c_ref, w_ref, b_ref, o_ref):
    c = c_ref[...]
    s = (c * jax.nn.sigmoid(c)).astype(BF16)
    o_ref[...] = _dot(s, w_ref[...].astype(BF16)) + b_ref[...]


def _mod_call(c_all, w_ada, b_ada):
    nt = (N_MOD * D_MODEL) // MOD_NT
    return pl.pallas_call(
        _mod_kernel,
        grid=(DEPTH, nt),
        in_specs=[
            pl.BlockSpec((MOD_ROWS, D_MODEL), lambda l, j: (0, 0)),
            pl.BlockSpec((None, D_MODEL, MOD_NT), lambda l, j: (l, 0, j)),
            pl.BlockSpec((None, 1, MOD_NT), lambda l, j: (l, 0, j)),
        ],
        out_specs=pl.BlockSpec((None, MOD_ROWS, MOD_NT), lambda l, j: (l, 0, j)),
        out_shape=jax.ShapeDtypeStruct((DEPTH, MOD_ROWS, N_MOD * D_MODEL), F32),
        name="adaln_mod",
    )(c_all, w_ada, b_ada.reshape(DEPTH, 1, N_MOD * D_MODEL))


def _prenorm(x, mod, gain):
    return _rms(x) * (gain * (1.0 + mod[:, D_MODEL:2 * D_MODEL])) + mod[:, 0:D_MODEL]


def _prenorm_kernel(xp_ref, xs_ref, mod_ref, gpre_ref, x_ref, hb_ref):
    x = jnp.where(pl.program_id(0) < P_TILES, xp_ref[...], xs_ref[...])
    x_ref[...] = x
    hb_ref[...] = _prenorm(x, mod_ref[...], gpre_ref[...]).astype(BF16)


def _prenorm_call(x_prompt, x_sample, mods, gpre):
    row = lambda i: (i, 0)
    return pl.pallas_call(
        _prenorm_kernel,
        grid=(N_TILES,),
        in_specs=[
            pl.BlockSpec((TM, D_MODEL), lambda i: (jnp.minimum(i, P_TILES - 1), 0)),
            pl.BlockSpec((TM, D_MODEL), lambda i: (jnp.maximum(i - P_TILES, 0), 0)),
            pl.BlockSpec((None, None, 1, 2 * D_MODEL), lambda i: (0, _mod_row(i), 0, 0)),
            pl.BlockSpec((None, 1, D_MODEL), lambda i: (0, 0, 0)),
        ],
        out_specs=[pl.BlockSpec((TM, D_MODEL), row), pl.BlockSpec((TM, D_MODEL), row)],
        out_shape=[jax.ShapeDtypeStruct((N_TOK, D_MODEL), F32),
                   jax.ShapeDtypeStruct((N_TOK, D_MODEL), BF16)],
        name="prenorm",
    )(x_prompt, x_sample, mods, gpre)


def _proj_kernel(hb_ref, w_ref, qg_ref, kg_ref, cos_ref, s1_ref, s2_ref, avg_ref, dft_ref,
                 ck_in_ref, cv_in_ref,
                 q_ref, k_ref, v_ref, xp_ref, xc_ref, xs_ref, g_ref, ck_ref, cv_ref):
    del ck_in_ref, cv_in_ref
    hb = hb_ref[...]

    cos = cos_ref[...]
    s1 = s1_ref[...]
    s2 = s2_ref[...]

    def rope(t):
        return (t * cos + pltpu.roll(t, LANES - ROPE_AXIS_DIM // 2, 1) * s1
                + pltpu.roll(t, ROPE_AXIS_DIM // 2, 1) * s2)

    head = _dot(hb, w_ref[:, 0:OFF_G])
    avg = avg_ref[...]
    for c in range(ATTN_WIDTH // LANES):
        lo = c * LANES
        q = head[:, lo:lo + LANES]
        ms = _dot((q * q).astype(BF16), avg)
        q = q * lax.rsqrt(ms + RMS_EPS) * qg_ref[...]
        q_ref[:, lo:lo + LANES] = (rope(q) * QK_SCALE).astype(BF16)

    k = head[:, OFF_K:OFF_K + KV_WIDTH]
    ms = _dot((k * k).astype(BF16), avg)
    k = rope(k * lax.rsqrt(ms + RMS_EPS) * kg_ref[...])
    v = head[:, OFF_K + KV_WIDTH:OFF_P]
    k_ref[...] = k
    v_ref[...] = v

    @pl.when(pl.program_id(0) < N_P // PROJ_TM)
    def _():
        for s in range(PROJ_TM // SEQ):
            ck_ref[s] = k[s * SEQ:(s + 1) * SEQ]
            cv_ref[s] = v[s * SEQ:(s + 1) * SEQ]

    xp_ref[...] = head[:, OFF_P:OFF_P + POOL_WIDTH]
    cs = _dot(head[:, OFF_P + POOL_WIDTH:OFF_G].astype(BF16), dft_ref[...])
    xc_ref[...] = cs[:, 0:FOURIER_WIDTH].astype(BF16)
    xs_ref[...] = cs[:, FOURIER_WIDTH:].astype(BF16)

    for c in range(N_BRANCHES):
        lo = OFF_G + c * D_MODEL
        g_ref[:, c * D_MODEL:(c + 1) * D_MODEL] = _dot(hb, w_ref[:, lo:lo + D_MODEL]).astype(BF16)


def _proj_call(l, hb, w_in, qg, kg, cos_t, s1_t, s2_t, avg, dft, cache_k, cache_v):
    row = lambda i: (i, 0)
    const2 = lambda i: (0, 0)
    per_layer = lambda i: (l, 0, 0)
    rope_spec = pl.BlockSpec((PROJ_TM, LANES), lambda i: (_rope_block(i), 0))
    cache_spec = pl.BlockSpec((PROJ_TM // SEQ, None, SEQ, KV_WIDTH),
                              lambda i: (jnp.minimum(i, N_P // PROJ_TM - 1), l, 0, 0))
    cache_shape = jax.ShapeDtypeStruct((BATCH, DEPTH, SEQ, KV_WIDTH), F32)
    return pl.pallas_call(
        _proj_kernel,
        grid=(N_TOK // PROJ_TM,),
        in_specs=[
            pl.BlockSpec((PROJ_TM, D_MODEL), row),
            pl.BlockSpec((None, D_MODEL, IN_WIDTH), per_layer),
            pl.BlockSpec((None, 1, LANES), per_layer),
            pl.BlockSpec((None, 1, LANES), per_layer),
            rope_spec, rope_spec, rope_spec,
            pl.BlockSpec((LANES, LANES), const2),
            pl.BlockSpec((FOURIER_WIDTH, 2 * FOURIER_WIDTH), const2),
            pl.BlockSpec(memory_space=pl.ANY),
            pl.BlockSpec(memory_space=pl.ANY),
        ],
        out_specs=[
            pl.BlockSpec((PROJ_TM, ATTN_WIDTH), row),
            pl.BlockSpec((PROJ_TM, KV_WIDTH), row),
            pl.BlockSpec((PROJ_TM, KV_WIDTH), row),
            pl.BlockSpec((PROJ_TM, POOL_WIDTH), row),
            pl.BlockSpec((PROJ_TM, FOURIER_WIDTH), row),
            pl.BlockSpec((PROJ_TM, FOURIER_WIDTH), row),
            pl.BlockSpec((PROJ_TM, N_BRANCHES * D_MODEL), row),
            cache_spec, cache_spec,
        ],
        out_shape=[
            jax.ShapeDtypeStruct((N_TOK, ATTN_WIDTH), BF16),
            jax.ShapeDtypeStruct((N_TOK, KV_WIDTH), F32),
            jax.ShapeDtypeStruct((N_TOK, KV_WIDTH), F32),
            jax.ShapeDtypeStruct((N_TOK, POOL_WIDTH), F32),
            jax.ShapeDtypeStruct((N_TOK, FOURIER_WIDTH), BF16),
            jax.ShapeDtypeStruct((N_TOK, FOURIER_WIDTH), BF16),
            jax.ShapeDtypeStruct((N_TOK, N_BRANCHES * D_MODEL), BF16),
            cache_shape, cache_shape,
        ],
        input_output_aliases={9: 7, 10: 8},
        name="proj",
    )(hb, w_in, qg, kg, cos_t, s1_t, s2_t, avg, dft, cache_k, cache_v)


def _attn_kernel(*refs, n_parts, tq):
    q_ref = refs[0]
    kv_refs = refs[1:1 + 2 * n_parts]
    o_ref = refs[-1]
    single_chunk = n_parts == 1 and kv_refs[0].shape[0] <= KEY_CHUNK
    outs = []
    for j in range(N_KV_HEADS):
        lo = j * HEAD_DIM
        qs = jnp.concatenate(
            [q_ref[:, (KV_GROUP * j + g) * HEAD_DIM:(KV_GROUP * j + g + 1) * HEAD_DIM]
             for g in range(KV_GROUP)], axis=0)
        m = acc = None
        for p in range(n_parts):
            k_ref, v_ref = kv_refs[2 * p], kv_refs[2 * p + 1]
            for c0 in range(0, k_ref.shape[0], KEY_CHUNK):
                c1 = min(c0 + KEY_CHUNK, k_ref.shape[0])
                kc = k_ref[c0:c1, lo:lo + HEAD_DIM].astype(BF16)
                s = lax.dot_general(qs, kc, (((1,), (1,)), ((), ())), preferred_element_type=F32)
                mc = jnp.max(s, axis=1, keepdims=True)
                m_new = mc if m is None else jnp.maximum(m, mc)
                e = jnp.exp2(s - m_new)
                vc = v_ref[c0:c1, lo:lo + HEAD_DIM].astype(BF16)
                if single_chunk:
                    den = jnp.sum(e, axis=1, keepdims=True)
                else:
                    vc = jnp.concatenate([vc, jnp.ones((c1 - c0, HEAD_DIM), BF16)], axis=1)
                pv = _dot(e.astype(BF16), vc)
                acc = pv if m is None else acc * jnp.exp2(m - m_new) + pv
                m = m_new
        o = acc / den if single_chunk else acc[:, 0:HEAD_DIM] / acc[:, HEAD_DIM:2 * HEAD_DIM]
        outs.extend(o[g * tq:(g + 1) * tq] for g in range(KV_GROUP))
    o_ref[...] = jnp.concatenate(outs, axis=1).astype(BF16)


def _attn_prompt_call(q, k, v):
    blk = lambda b: (b, 0)
    return pl.pallas_call(
        functools.partial(_attn_kernel, n_parts=1, tq=SEQ),
        grid=(BATCH,),
        in_specs=[
            pl.BlockSpec((SEQ, ATTN_WIDTH), blk),
            pl.BlockSpec((SEQ, KV_WIDTH), blk),
            pl.BlockSpec((SEQ, KV_WIDTH), blk),
        ],
        out_specs=pl.BlockSpec((SEQ, ATTN_WIDTH), blk),
        out_shape=jax.ShapeDtypeStruct((N_P, ATTN_WIDTH), BF16),
        name="attn_context",
    )(q, k, v)


def _attn_sample_call(l, q, k, v, cache_k, cache_v):
    nq = DEC_SEQ // TQ_S
    qrow = lambda b, i: (N_P // TQ_S + b * nq + i, 0)
    seq = lambda b, i: (N_P // DEC_SEQ + b, 0)
    cache = lambda b, i: (b, l, 0, 0)
    return pl.pallas_call(
        functools.partial(_attn_kernel, n_parts=2, tq=TQ_S),
        grid=(DEC_BATCH, nq),
        in_specs=[
            pl.BlockSpec((TQ_S, ATTN_WIDTH), qrow),
            pl.BlockSpec((None, None, PAST_LEN, KV_WIDTH), cache),
            pl.BlockSpec((None, None, PAST_LEN, KV_WIDTH), cache),
            pl.BlockSpec((DEC_SEQ, KV_WIDTH), seq),
            pl.BlockSpec((DEC_SEQ, KV_WIDTH), seq),
        ],
        out_specs=pl.BlockSpec((TQ_S, ATTN_WIDTH), lambda b, i: (b * nq + i, 0)),
        out_shape=jax.ShapeDtypeStruct((N_S, ATTN_WIDTH), BF16),
        name="attn_latent",
    )(q, cache_k, cache_v, k, v)


def _pool_kernel(xp_ref, bdw_ref, sc_ref, o_ref, pad_ref, *, seq_len, n_seq):
    for s in range(n_seq):
        _pool_sequence(xp_ref, bdw_ref, sc_ref, o_ref, pad_ref, s * seq_len, seq_len)


def _pool_sequence(xp_ref, bdw_ref, sc_ref, o_ref, pad_ref, row0, seq_len):
    half = POOL_WIDTH // 2
    zeros = jnp.zeros((POOL_PAD, POOL_WIDTH), F32)
    pad_ref[0:POOL_PAD, :] = zeros
    pad_ref[POOL_PAD + seq_len:, :] = zeros
    pad_ref[POOL_PAD:POOL_PAD + seq_len, :] = xp_ref[row0:row0 + seq_len, :]
    chunk = min(seq_len, 256)
    lane = lax.broadcasted_iota(jnp.int32, (chunk, half), 1)
    first = lane < POOL_GROUP_DIM
    for c in range(seq_len // chunk):
        base = c * chunk
        t = lax.broadcasted_iota(jnp.int32, (chunk, half), 0) + base

        def sh(j, lo):
            return pad_ref[POOL_PAD + base + j:POOL_PAD + base + j + chunk, lo:lo + half]

        def cnt(w):
            return (jnp.minimum(t + w // 2, seq_len) - jnp.maximum(t - w // 2, 0)).astype(F32)

        xa = sh(0, 0)
        w2 = sh(-1, 0) + xa
        w4 = w2 + sh(-2, 0) + sh(1, 0)
        xb = sh(0, half)
        w8 = xb
        for j in (-4, -3, -2, -1, 1, 2, 3):
            w8 = w8 + sh(j, half)
        w16 = w8
        for j in (-8, -7, -6, -5, 4, 5, 6, 7):
            w16 = w16 + sh(j, half)
        pa = jnp.where(first, w2 / cnt(2), w4 / cnt(4)) - xa
        pb = jnp.where(first, w8 / cnt(8), w16 / cnt(16)) - xb
        pooled = jnp.concatenate([pa, pb], axis=1).astype(BF16)
        o_ref[row0 + base:row0 + base + chunk, :] = (
            _dot(pooled, bdw_ref[...]) * sc_ref[...]).astype(BF16)


def _pool_call(l, xp, bdw, scale, seq_len, n_seq, blk0, seq_per_step):
    per_layer = lambda b: (l, 0, 0)
    rows = seq_len * seq_per_step
    return pl.pallas_call(
        functools.partial(_pool_kernel, seq_len=seq_len, n_seq=seq_per_step),
        grid=(n_seq // seq_per_step,),
        in_specs=[
            pl.BlockSpec((rows, POOL_WIDTH), lambda b: (blk0 + b, 0)),
            pl.BlockSpec((None, POOL_WIDTH, POOL_WIDTH), per_layer),
            pl.BlockSpec((None, 1, POOL_WIDTH), per_layer),
        ],
        out_specs=pl.BlockSpec((rows, POOL_WIDTH), lambda b: (b, 0)),
        out_shape=jax.ShapeDtypeStruct((n_seq * seq_len, POOL_WIDTH), BF16),
        scratch_shapes=[pltpu.VMEM((seq_len + 2 * POOL_PAD, POOL_WIDTH), F32)],
        name="pool_%d" % seq_len,
    )(xp, bdw, scale)


def _fourier_kernel(c_ref, s_ref, xc_ref, xs_ref, o_ref, *, scale, n_seq=1):
    seq_len = xc_ref.shape[0] // n_seq
    out_len = o_ref.shape[0] // n_seq
    for s in range(n_seq):
        rows = slice(s * seq_len, (s + 1) * seq_len)
        y = _dot(c_ref[...], xc_ref[rows, :]) - _dot(s_ref[...], xs_ref[rows, :])
        o_ref[s * out_len:(s + 1) * out_len, :] = (y * scale).astype(BF16)


def _fourier_prompt_call(cmat, smat, xc, xs):
    blk = lambda b: (b, 0)
    const2 = lambda b: (0, 0)
    rows = SEQ * CTX_SEQ_PER_STEP
    return pl.pallas_call(
        functools.partial(_fourier_kernel, scale=1.0 / math.sqrt(SEQ * FOURIER_GROUP_DIM),
                          n_seq=CTX_SEQ_PER_STEP),
        grid=(BATCH // CTX_SEQ_PER_STEP,),
        in_specs=[
            pl.BlockSpec((SEQ, SEQ), const2),
            pl.BlockSpec((SEQ, SEQ), const2),
            pl.BlockSpec((rows, FOURIER_WIDTH), blk),
            pl.BlockSpec((rows, FOURIER_WIDTH), blk),
        ],
        out_specs=pl.BlockSpec((rows, FOURIER_WIDTH), blk),
        out_shape=jax.ShapeDtypeStruct((N_P, FOURIER_WIDTH), BF16),
        name="fourier_context",
    )(cmat, smat, xc, xs)


def _fourier_sample_call(cmat, smat, xc, xs):
    nt = DEC_SEQ // TF_S
    rows = lambda b, i: (i, 0)
    seq = lambda b, i: (N_P // DEC_SEQ + b, 0)
    out = lambda b, i: (b * nt + i, 0)
    return pl.pallas_call(
        functools.partial(_fourier_kernel, scale=1.0 / math.sqrt(DEC_SEQ * FOURIER_GROUP_DIM)),
        grid=(DEC_BATCH, nt),
        in_specs=[
            pl.BlockSpec((TF_S, DEC_SEQ), rows),
            pl.BlockSpec((TF_S, DEC_SEQ), rows),
            pl.BlockSpec((DEC_SEQ, FOURIER_WIDTH), seq),
            pl.BlockSpec((DEC_SEQ, FOURIER_WIDTH), seq),
        ],
        out_specs=pl.BlockSpec((TF_S, FOURIER_WIDTH), out),
        out_shape=jax.ShapeDtypeStruct((N_S, FOURIER_WIDTH), BF16),
        name="fourier_latent",
    )(cmat, smat, xc, xs)


def _merge_kernel(x_ref, attn_p_ref, attn_s_ref, pool_p_ref, pool_s_ref, four_p_ref, four_s_ref,
                  g_ref, mod_ref, gpost_ref, gffn_ref,
                  wa_ref, wp_ref, wf_ref, wo_ref, wrt_ref, brt_ref, triu_ref, lower_ref,
                  x1_ref, h2_ref, pos_ref, post_ref, mw_ref, cnt_ref):
    i = pl.program_id(0)
    mod = mod_ref[...]
    g1 = mod[:, 2 * D_MODEL:3 * D_MODEL]
    sh2 = mod[:, 3 * D_MODEL:4 * D_MODEL]
    sc2 = mod[:, 4 * D_MODEL:5 * D_MODEL]

    def gate2(c):
        return jnp.tanh(g_ref[:, c * D_MODEL:(c + 1) * D_MODEL].astype(F32)) + 1.0

    def branch(p_ref, s_ref):
        return jnp.where(i < P_TILES, p_ref[...], s_ref[...])

    merged = gate2(0) * _dot(branch(attn_p_ref, attn_s_ref), wa_ref[...])
    merged = merged + gate2(1) * _dot(branch(pool_p_ref, pool_s_ref), wp_ref[...])
    merged = merged + gate2(2) * _dot(branch(four_p_ref, four_s_ref), wf_ref[...])
    mix = _dot((0.5 * merged).astype(BF16), wo_ref[...])
    x1 = x_ref[...] + (g1 * gpost_ref[...]) * _rms(mix)
    x1_ref[...] = x1
    h2 = _rms(x1) * (gffn_ref[...] * (1.0 + sc2)) + sh2
    h2b = h2.astype(BF16)
    h2_ref[...] = h2b

    logits = lax.dot_general(wrt_ref[...], h2b, (((1,), (1,)), ((), ())),
                             preferred_element_type=F32) + brt_ref[...]
    top = logits[0:ROUTER_ROWS, :]
    row = lax.broadcasted_iota(jnp.int32, (ROUTER_ROWS, TM), 0)
    rowf = row.astype(F32)
    neg = jnp.float32(-3e38)
    big = jnp.float32(1e9)
    is_g = (row >= N_EXPERTS) & (row < N_EXPERTS + N_EXPERT_GROUPS)
    lg = jnp.where(is_g, top, neg)
    gmax = jnp.max(lg, axis=0, keepdims=True)
    g_sel = jnp.min(jnp.where(lg == gmax, rowf - N_EXPERTS, big), axis=0, keepdims=True)
    p_g = 1.0 / jnp.sum(jnp.where(is_g, jnp.exp(top - gmax), 0.0), axis=0, keepdims=True)
    grp = lax.shift_right_logical(row, int(math.log2(EXPERTS_PER_GROUP))).astype(F32)
    in_grp = (row < N_EXPERTS) & (grp == g_sel)
    le = jnp.where(in_grp, top, neg)
    v1 = jnp.max(le, axis=0, keepdims=True)
    i1 = jnp.min(jnp.where(le == v1, rowf, big), axis=0, keepdims=True)
    le2 = jnp.where(rowf == i1, neg, le)
    v2 = jnp.max(le2, axis=0, keepdims=True)
    i2 = jnp.min(jnp.where(le2 == v2, rowf, big), axis=0, keepdims=True)
    e21 = jnp.exp(v2 - v1)
    w1 = p_g / (1.0 + e21)
    w2 = p_g * e21 / (1.0 + e21)

    rows_all = lax.broadcasted_iota(jnp.int32, (LANES, TM), 0)
    rows_allf = rows_all.astype(F32)
    oh1 = (rows_allf == i1).astype(F32)
    oh2 = (rows_allf == i2).astype(F32)
    ohb = (oh1 + oh2).astype(BF16)
    before = _dot(ohb, triu_ref[...])
    cnt = _dot(ohb, jnp.ones((TM, LANES), BF16))
    cnt_pad = (lax.shift_right_logical(cnt.astype(jnp.int32) + (RUN_ALIGN - 1), RUN_SHIFT)
               * RUN_ALIGN).astype(F32)
    run_off = _dot(lower_ref[...], cnt_pad.astype(BF16))
    slot = jnp.concatenate([run_off] * (TM // LANES), axis=1) + before
    p1 = jnp.sum(slot * oh1, axis=0, keepdims=True)
    p2 = jnp.sum(slot * oh2, axis=0, keepdims=True)
    cnt_ref[...] = cnt

    meta = jnp.where(rows_all == 0, p1, jnp.where(rows_all == 1, p2,
                     jnp.where(rows_all == 2, w1, jnp.where(rows_all == 3, w2, 0.0))))
    post_ref[...] = meta[0:MOD_ROWS, :].astype(jnp.int32)
    meta_t = meta.T
    pos_ref[...] = meta_t.astype(jnp.int32)
    mw_ref[...] = meta_t


def _merge_call(l, x, attn, pool, four, gates, mods, gpost, gffn, wa, wp, wf, wo, wrt, brt, triu,
                lower):
    row = lambda i: (i, 0)
    prow = lambda i: (jnp.minimum(i, P_TILES - 1), 0)
    srow = lambda i: (jnp.maximum(i - P_TILES, 0), 0)
    const2 = lambda i: (0, 0)
    per_layer = lambda i: (l, 0, 0)
    return pl.pallas_call(
        _merge_kernel,
        grid=(N_TILES,),
        in_specs=[
            pl.BlockSpec((TM, D_MODEL), row),
            pl.BlockSpec((TM, ATTN_WIDTH), prow),
            pl.BlockSpec((TM, ATTN_WIDTH), srow),
            pl.BlockSpec((TM, POOL_WIDTH), prow),
            pl.BlockSpec((TM, POOL_WIDTH), srow),
            pl.BlockSpec((TM, FOURIER_WIDTH), prow),
            pl.BlockSpec((TM, FOURIER_WIDTH), srow),
            pl.BlockSpec((TM, N_BRANCHES * D_MODEL), row),
            pl.BlockSpec((None, None, 1, N_MOD * D_MODEL), lambda i: (l, _mod_row(i), 0, 0)),
            pl.BlockSpec((None, 1, D_MODEL), per_layer),
            pl.BlockSpec((None, 1, D_MODEL), per_layer),
            pl.BlockSpec((None, ATTN_WIDTH, D_MODEL), per_layer),
            pl.BlockSpec((None, POOL_WIDTH, D_MODEL), per_layer),
            pl.BlockSpec((None, FOURIER_WIDTH, D_MODEL), per_layer),
            pl.BlockSpec((None, D_MODEL, D_MODEL), per_layer),
            pl.BlockSpec((None, LANES, D_MODEL), per_layer),
            pl.BlockSpec((None, LANES, TM), per_layer),
            pl.BlockSpec((TM, TM), const2),
            pl.BlockSpec((LANES, LANES), const2),
        ],
        out_specs=[
            pl.BlockSpec((TM, D_MODEL), row),
            pl.BlockSpec((TM, D_MODEL), row),
            pl.BlockSpec((TM, LANES), row),
            pl.BlockSpec((MOD_ROWS, TM), lambda i: (0, i)),
            pl.BlockSpec((TM, LANES), row),
            pl.BlockSpec((LANES, LANES), row),
        ],
        out_shape=[
            jax.ShapeDtypeStruct((N_TOK, D_MODEL), F32),
            jax.ShapeDtypeStruct((N_TOK, D_MODEL), BF16),
            jax.ShapeDtypeStruct((N_TOK, LANES), jnp.int32),
            jax.ShapeDtypeStruct((MOD_ROWS, N_TOK), jnp.int32),
            jax.ShapeDtypeStruct((N_TOK, LANES), F32),
            jax.ShapeDtypeStruct((N_TILES * LANES, LANES), F32),
        ],
        name="merge_router",
    )(x, attn[0], attn[1], pool[0], pool[1], four[0], four[1], gates, mods, gpost, gffn,
      wa, wp, wf, wo, wrt, brt, triu, lower)


HALF = D_MODEL // 2
U32 = jnp.uint32
HI_MASK = 0xFFFF0000


def _pack_rows(x):
    lo = lax.bitcast_convert_type(x[:, :HALF], U32)
    hi = lax.bitcast_convert_type(x[:, HALF:], U32)
    return lax.shift_right_logical(lo, U32(16)) | (hi & U32(HI_MASK))


def _unpack_rows(u):
    lo = lax.bitcast_convert_type(lax.shift_left(u, U32(16)), F32)
    hi = lax.bitcast_convert_type(u & U32(HI_MASK), F32)
    return jnp.concatenate([lo.astype(BF16), hi.astype(BF16)], axis=1)


def _rows(ref, start, size):
    return ref.at[pl.ds(pl.multiple_of(start, RUN_ALIGN), size)]


def _start_tile_copies(tile, n_ref, trow_ref, xrow_ref, make_copy):
    for bit in range(RUN_BITS):
        lst = bit * N_TILES + tile

        def body(k, carry, bit=bit, lst=lst):
            j = lst * N_EXPERTS + k
            make_copy(trow_ref[j], xrow_ref[j], RUN_ALIGN << bit).start()
            return carry

        lax.fori_loop(0, n_ref[lst], body, 0)


def _wait_tile_copies(total, src_ref, dst_ref, sem):
    for b in range(TILE_BITS):
        size = RUN_ALIGN << b

        @pl.when((lax.shift_right_logical(total, b) & 1) == 1)
        def _(size=size):
            pltpu.make_async_copy(_rows(src_ref, 0, size), _rows(dst_ref, 0, size), sem).wait()


def _dispatch_kernel(n_ref, trow_ref, xrow_ref, tsum_ref, h_ref, post_ref, xs_in_ref, xs_ref,
                     sorted_ref, sem):
    del xs_in_ref
    i = pl.program_id(0)
    slot = lax.rem(i, DISPATCH_SLOTS)
    rows = lax.broadcasted_iota(jnp.int32, (SORT_ROWS, TM), 0)
    p = post_ref[...]
    perm = jnp.where(rows == p[0:1, :], 1.0, jnp.where(rows == p[1:2, :], 1.0, 0.0)).astype(BF16)
    sorted_ref[slot] = _pack_rows(_dot(perm, h_ref[...]))

    def wait(tile, slot):
        _wait_tile_copies(tsum_ref[tile], sorted_ref.at[slot], xs_ref, sem.at[slot])

    lag = DISPATCH_SLOTS - 1

    @pl.when(i >= lag)
    def _():
        wait(i - lag, lax.rem(i + 1, DISPATCH_SLOTS))

    _start_tile_copies(i, n_ref, trow_ref, xrow_ref, lambda t, x, size: pltpu.make_async_copy(
        _rows(sorted_ref.at[slot], t, size), _rows(xs_ref, x, size), sem.at[slot]))

    @pl.when(i == pl.num_programs(0) - 1)
    def _():
        for back in reversed(range(lag)):
            wait(i - back, lax.rem(i - back, DISPATCH_SLOTS))


def _dispatch_call(n_list, trow, xrow, tsum, h2, post, xs):
    return pl.pallas_call(
        _dispatch_kernel,
        grid_spec=pltpu.PrefetchScalarGridSpec(
            num_scalar_prefetch=4,
            grid=(N_TILES,),
            in_specs=[
                pl.BlockSpec((TM, D_MODEL), lambda i, *_: (i, 0)),
                pl.BlockSpec((MOD_ROWS, TM), lambda i, *_: (0, i)),
                pl.BlockSpec(memory_space=pl.ANY),
            ],
            out_specs=pl.BlockSpec(memory_space=pl.ANY),
            scratch_shapes=[pltpu.VMEM((DISPATCH_SLOTS, SORT_ROWS, HALF), U32),
                            pltpu.SemaphoreType.DMA((DISPATCH_SLOTS,))],
        ),
        out_shape=jax.ShapeDtypeStruct((XS_ROWS, HALF), U32),
        input_output_aliases={6: 0},
        compiler_params=pltpu.CompilerParams(dimension_semantics=("arbitrary",)),
        name="moe_dispatch",
    )(n_list, trow, xrow, tsum, h2, post, xs)


def _expert_kernel(bstart_ref, bcnt_ref, nu_ref, wg_ref, wu_ref, wd_ref, xs_ref, y_prev_ref, y_ref,
                   wgub, wdb, xbuf, ybuf, xsem, ysem):
    del y_prev_ref
    e = pl.program_id(0)
    n_used = nu_ref[0]

    def x_copy(g, slot):
        rows = pl.ds(pl.multiple_of(g * EBLK, EBLK), EBLK)
        return pltpu.make_async_copy(xs_ref.at[rows], xbuf.at[slot], xsem.at[slot])

    def y_copy(g, slot):
        rows = pl.ds(pl.multiple_of(g * EBLK, EBLK), EBLK)
        return pltpu.make_async_copy(ybuf.at[slot], y_ref.at[rows], ysem.at[slot])

    @pl.when(e == 0)
    def _():
        for g in range(X_SLOTS - 1):
            @pl.when(g < n_used)
            def _(g=g):
                x_copy(g, g).start()

    wgub[:, 0:EXPERT_FF] = wg_ref[...].astype(BF16)
    wgub[:, EXPERT_FF:] = wu_ref[...].astype(BF16)
    wdb[...] = wd_ref[...].astype(BF16)
    first = bstart_ref[e]

    def block(j, carry):
        g = first + j
        slot = lax.rem(g, 2)
        xslot = lax.rem(g, X_SLOTS)
        x_copy(g, xslot).wait()

        @pl.when(g + (X_SLOTS - 1) < n_used)
        def _():
            x_copy(g + (X_SLOTS - 1), lax.rem(g + (X_SLOTS - 1), X_SLOTS)).start()

        xb = _unpack_rows(xbuf[xslot])
        gu = _dot(xb, wgub[...])
        gate = gu[:, 0:EXPERT_FF]
        act = (gate * jax.nn.sigmoid(gate)) * gu[:, EXPERT_FF:]
        y = _dot(act.astype(BF16), wdb[...])

        @pl.when(g >= 2)
        def _():
            y_copy(g - 2, slot).wait()

        ybuf[slot] = _pack_rows(y.astype(BF16).astype(F32))
        y_copy(g, slot).start()
        return carry

    lax.fori_loop(0, bcnt_ref[e], block, 0)

    @pl.when(e == pl.num_programs(0) - 1)
    def _():
        @pl.when(n_used >= 2)
        def _():
            y_copy(n_used - 2, lax.rem(n_used, 2)).wait()

        y_copy(n_used - 1, lax.rem(n_used - 1, 2)).wait()


def _expert_call(l, blk_start, blk_cnt, n_used, xs, y_prev, w_gate, w_up, w_down):
    wsel = lambda e, *_: (l, e, 0, 0)
    return pl.pallas_call(
        _expert_kernel,
        grid_spec=pltpu.PrefetchScalarGridSpec(
            num_scalar_prefetch=3,
            grid=(N_EXPERTS,),
            in_specs=[
                pl.BlockSpec((None, None, D_MODEL, EXPERT_FF), wsel),
                pl.BlockSpec((None, None, D_MODEL, EXPERT_FF), wsel),
                pl.BlockSpec((None, None, EXPERT_FF, D_MODEL), wsel),
                pl.BlockSpec(memory_space=pl.ANY),
                pl.BlockSpec(memory_space=pl.ANY),
            ],
            out_specs=pl.BlockSpec(memory_space=pl.ANY),
            scratch_shapes=[
                pltpu.VMEM((D_MODEL, 2 * EXPERT_FF), BF16),
                pltpu.VMEM((EXPERT_FF, D_MODEL), BF16),
                pltpu.VMEM((X_SLOTS, EBLK, HALF), U32),
                pltpu.VMEM((2, EBLK, HALF), U32),
                pltpu.SemaphoreType.DMA((X_SLOTS,)),
                pltpu.SemaphoreType.DMA((2,)),
            ],
        ),
        out_shape=jax.ShapeDtypeStruct((XS_ROWS, HALF), U32),
        input_output_aliases={7: 0},
        compiler_params=pltpu.CompilerParams(dimension_semantics=("arbitrary",)),
        name="moe_experts",
    )(blk_start, blk_cnt, n_used, w_gate, w_up, w_down, xs, y_prev)


def _combine_kernel(n_ref, trow_ref, xrow_ref, tsum_ref, y_ref, x1_ref, pos_ref, mw_ref, g2_ref,
                    gpost_ref, *rest, has_next):
    if has_next:
        nmod_ref, ngpre_ref, o_ref, hb_ref, ybuf, sem = rest
    else:
        o_ref, os_ref, ybuf, sem = rest
    i = pl.program_id(0)
    slot = lax.rem(i, 2)

    def start(tile, slot):
        _start_tile_copies(tile, n_ref, trow_ref, xrow_ref, lambda t, x, size: pltpu.make_async_copy(
            _rows(y_ref, x, size), _rows(ybuf.at[slot], t, size), sem.at[slot]))

    @pl.when(i == 0)
    def _():
        ybuf[...] = jnp.zeros_like(ybuf)
        start(0, 0)

    @pl.when(i + 1 < pl.num_programs(0))
    def _():
        start(i + 1, 1 - slot)

    _wait_tile_copies(tsum_ref[i], y_ref, ybuf.at[slot], sem.at[slot])

    pos = pos_ref[...]
    mw = mw_ref[...]
    cols = lax.broadcasted_iota(jnp.int32, (TM, SORT_ROWS), 1)
    qw = jnp.where(cols == pos[:, 0:1], mw[:, 2:3],
                   jnp.where(cols == pos[:, 1:2], mw[:, 3:4], 0.0)).astype(BF16)
    ffn = _dot(qw, _unpack_rows(ybuf[slot]))
    x2 = x1_ref[...] + (g2_ref[...] * gpost_ref[...]) * _rms(ffn)
    if has_next:
        o_ref[...] = x2
        hb_ref[...] = _prenorm(x2, nmod_ref[...], ngpre_ref[...]).astype(BF16)
    else:
        @pl.when(i < P_TILES)
        def _():
            o_ref[...] = x2

        os_ref[...] = x2


def _combine_call(l, n_list, trow, xrow, tsum, y, x1, pos, mw, mods, gpost, gpre):
    has_next = l + 1 < DEPTH
    row = lambda i, *_: (i, 0)
    in_specs = [
        pl.BlockSpec(memory_space=pl.ANY),
        pl.BlockSpec((TM, D_MODEL), row),
        pl.BlockSpec((TM, LANES), row),
        pl.BlockSpec((TM, LANES), row),
        pl.BlockSpec((None, None, 1, D_MODEL), lambda i, *_: (l, _mod_row(i), 0, N_MOD - 1)),
        pl.BlockSpec((None, 1, D_MODEL), lambda i, *_: (l, 0, 0)),
    ]
    args = [n_list, trow, xrow, tsum, y, x1, pos, mw, mods, gpost]
    if has_next:
        in_specs += [
            pl.BlockSpec((None, None, 1, 2 * D_MODEL), lambda i, *_: (l + 1, _mod_row(i), 0, 0)),
            pl.BlockSpec((None, 1, D_MODEL), lambda i, *_: (l + 1, 0, 0)),
        ]
        args += [mods, gpre]
        out_specs = [pl.BlockSpec((TM, D_MODEL), row), pl.BlockSpec((TM, D_MODEL), row)]
        out_shape = [jax.ShapeDtypeStruct((N_TOK, D_MODEL), F32),
                     jax.ShapeDtypeStruct((N_TOK, D_MODEL), BF16)]
    else:
        out_specs = [
            pl.BlockSpec((TM, D_MODEL), lambda i, *_: (jnp.minimum(i, P_TILES - 1), 0)),
            pl.BlockSpec((TM, D_MODEL), lambda i, *_: (jnp.maximum(i - P_TILES, 0), 0)),
        ]
        out_shape = [jax.ShapeDtypeStruct((N_P, D_MODEL), F32),
                     jax.ShapeDtypeStruct((N_S, D_MODEL), F32)]
    return pl.pallas_call(
        functools.partial(_combine_kernel, has_next=has_next),
        grid_spec=pltpu.PrefetchScalarGridSpec(
            num_scalar_prefetch=4,
            grid=(N_TILES,),
            in_specs=in_specs,
            out_specs=out_specs,
            scratch_shapes=[pltpu.VMEM((2, SORT_ROWS, HALF), U32), pltpu.SemaphoreType.DMA((2,))],
        ),
        out_shape=out_shape,
        compiler_params=pltpu.CompilerParams(dimension_semantics=("arbitrary",)),
        name="moe_combine",
    )(*args)


def _dft_mats(n):
    k = np.arange(n, dtype=np.int64)
    ang = 2.0 * np.pi * ((k[:, None] * k[None, :]) % n).astype(np.float64) / n
    return np.cos(ang), np.sin(ang)


def _block_diag(m, reps):
    n = m.shape[0]
    out = np.zeros((n * reps, n * reps), m.dtype)
    for r in range(reps):
        out[r * n:(r + 1) * n, r * n:(r + 1) * n] = m
    return out


def _rope_tables():
    t = np.arange(DEC_SEQ)
    pos = np.stack([t // GRID_W, t % GRID_W], axis=1).astype(np.float64)
    n_freq = ROPE_AXIS_DIM // 2
    inv = ROPE_BASE ** (-np.arange(n_freq, dtype=np.float64) * 2.0 / ROPE_AXIS_DIM)
    ang = pos[:, :, None] * inv[None, None, :]
    cos = np.cos(ang)
    sin = np.sin(ang)
    zero = np.zeros_like(sin[:, 0])
    cos_h = np.concatenate([cos[:, 0], cos[:, 0], cos[:, 1], cos[:, 1]], axis=1)
    s1_h = np.concatenate([-sin[:, 0], zero, -sin[:, 1], zero], axis=1)
    s2_h = np.concatenate([zero, sin[:, 0], zero, sin[:, 1]], axis=1)
    reps = LANES // HEAD_DIM

    def table(a, ident):
        a = np.tile(a, (1, reps))
        pad = np.full((PROJ_TM, LANES), ident, np.float64)
        return jnp.asarray(np.concatenate([a, pad], axis=0), F32)

    return table(cos_h, 1.0), table(s1_h, 0.0), table(s2_h, 0.0)


def _copy_lists(run_cnt, tile_off, xs_off):
    bit = jnp.arange(RUN_BITS, dtype=jnp.int32)[:, None, None]
    has = (run_cnt[None] >> bit) & 1
    before = ((run_cnt[None] >> (bit + 1)) << (bit + 1)) * RUN_ALIGN
    slot = jnp.cumsum(has, axis=-1) - has
    hit = (has[..., None] == 1) & (slot[..., None] == jnp.arange(N_EXPERTS, dtype=jnp.int32))

    def compact(rows):
        return jnp.sum(jnp.where(hit, rows[..., None], 0), axis=-2).reshape(-1)

    return (jnp.sum(has, axis=-1).reshape(-1), compact(tile_off[None] + before),
            compact(xs_off[None] + before))


def kernel(x_prompt, x_sample, cache_k, cache_v, c, c_ctx, w_ada, b_ada, norm_mix_pre,
           norm_mix_post, norm_ffn_pre, norm_ffn_post, w_in, q_norm, k_norm, w_attn_out,
           w_pool_group, pool_scale, w_pool_out, w_fourier_out, w_out, w_router_group,
           b_router_group, w_router_expert, b_router_expert, w_expert_gate, w_expert_up,
           w_expert_down):
    cos_t, s1_t, s2_t = _rope_tables()
    avg = jnp.asarray(_block_diag(np.full((HEAD_DIM, HEAD_DIM), 1.0 / HEAD_DIM), LANES // HEAD_DIM), BF16)
    c64, s64 = _dft_mats(FOURIER_GROUP_DIM)
    n_fg = FOURIER_WIDTH // FOURIER_GROUP_DIM
    dft_ch = jnp.asarray(np.concatenate([_block_diag(c64, n_fg), _block_diag(s64, n_fg)], axis=1), BF16)
    cp, sp = _dft_mats(SEQ)
    cp, sp = jnp.asarray(cp, BF16), jnp.asarray(sp, BF16)
    cl, sl = _dft_mats(DEC_SEQ)
    cl, sl = jnp.asarray(cl, BF16), jnp.asarray(sl, BF16)
    triu = jnp.asarray(np.triu(np.ones((TM, TM)), 1), BF16)
    lower = jnp.asarray(np.tril(np.ones((LANES, LANES)), -1), BF16)

    in_scale = np.ones((IN_WIDTH,), np.float32)
    in_scale[OFF_G:] = 0.5
    w_in_b = (w_in * in_scale).astype(BF16)
    wa_b = w_attn_out.astype(BF16)
    wp_b = w_pool_out.astype(BF16)
    wf_b = w_fourier_out.astype(BF16)
    wo_b = w_out.astype(BF16)
    pad_r = jnp.zeros((DEPTH, D_MODEL, LANES - N_EXPERTS - N_EXPERT_GROUPS), F32)
    wrt = jnp.concatenate([w_router_expert, w_router_group, pad_r], axis=2).astype(BF16)
    wrt = jnp.transpose(wrt, (0, 2, 1))
    br = jnp.concatenate([b_router_expert, b_router_group,
                          jnp.zeros((DEPTH, LANES - N_EXPERTS - N_EXPERT_GROUPS), F32)], axis=1)
    brt = jnp.broadcast_to(br[:, :, None], (DEPTH, LANES, TM))
    n_pg = POOL_WIDTH // POOL_GROUP_DIM
    bdw = jnp.zeros((DEPTH, POOL_WIDTH, POOL_WIDTH), F32)
    for g in range(n_pg):
        lo = g * POOL_GROUP_DIM
        bdw = bdw.at[:, lo:lo + POOL_GROUP_DIM, lo:lo + POOL_GROUP_DIM].set(w_pool_group[:, g])
    bdw = bdw.astype(BF16)
    pscale = pool_scale.reshape(DEPTH, 1, POOL_WIDTH)
    qg = jnp.tile(q_norm, (1, LANES // HEAD_DIM)).reshape(DEPTH, 1, LANES)
    kg = jnp.tile(k_norm, (1, LANES // HEAD_DIM)).reshape(DEPTH, 1, LANES)
    gpre = norm_mix_pre.reshape(DEPTH, 1, D_MODEL)
    gpost = norm_mix_post.reshape(DEPTH, 1, D_MODEL)
    gffn = norm_ffn_pre.reshape(DEPTH, 1, D_MODEL)
    gfpost = norm_ffn_post.reshape(DEPTH, 1, D_MODEL)
    ck = cache_k.reshape(DEC_BATCH, DEPTH, PAST_LEN, KV_WIDTH)
    cv = cache_v.reshape(DEC_BATCH, DEPTH, PAST_LEN, KV_WIDTH)

    c_all = jnp.concatenate([c_ctx[None, :], c, jnp.zeros((MOD_ROWS - 1 - DEC_BATCH, D_MODEL), F32)], axis=0)
    mods = _mod_call(c_all, w_ada, b_ada).reshape(DEPTH, MOD_ROWS, 1, N_MOD * D_MODEL)

    xs_buf = jnp.zeros((XS_ROWS, HALF), U32)
    y = jnp.zeros((XS_ROWS, HALF), U32)
    new_k = jnp.zeros((BATCH, DEPTH, SEQ, KV_WIDTH), F32)
    new_v = jnp.zeros((BATCH, DEPTH, SEQ, KV_WIDTH), F32)
    x, hb = _prenorm_call(x_prompt.reshape(N_P, D_MODEL), x_sample.reshape(N_S, D_MODEL), mods, gpre)
    for l in range(DEPTH):
        q, k, v, xp, xc, xsn, gates, new_k, new_v = _proj_call(
            l, hb, w_in_b, qg, kg, cos_t, s1_t, s2_t, avg, dft_ch, new_k, new_v)
        attn = (_attn_prompt_call(q, k, v), _attn_sample_call(l, q, k, v, ck, cv))
        pool = (_pool_call(l, xp, bdw, pscale, SEQ, BATCH, 0, CTX_SEQ_PER_STEP),
                _pool_call(l, xp, bdw, pscale, DEC_SEQ, DEC_BATCH, N_P // DEC_SEQ, 1))
        four = (_fourier_prompt_call(cp, sp, xc, xsn), _fourier_sample_call(cl, sl, xc, xsn))
        x1, h2, pos, post, mw, cnt = _merge_call(l, x, attn, pool, four, gates, mods, gpost, gffn,
                                                 wa_b, wp_b, wf_b, wo_b, wrt, brt, triu, lower)
        runs = cnt.reshape(N_TILES, LANES, LANES)[:, :N_EXPERTS, 0].astype(jnp.int32)
        runs = ((runs + RUN_ALIGN - 1) // RUN_ALIGN) * RUN_ALIGN
        tile_off = jnp.cumsum(runs, axis=1) - runs
        rows_e = jnp.sum(runs, axis=0)
        padded = ((rows_e + EBLK - 1) // EBLK) * EBLK
        pad_end = jnp.cumsum(padded)
        xs_off = (pad_end - padded)[None, :] + jnp.cumsum(runs, axis=0) - runs
        blk_cnt = padded // EBLK
        blk_start = (pad_end - padded) // EBLK
        n_used = pad_end[-1:] // EBLK
        n_list, trow, xrow = _copy_lists(runs // RUN_ALIGN, tile_off, xs_off)
        tile_cnt = jnp.sum(runs, axis=1) // RUN_ALIGN
        xs_buf = _dispatch_call(n_list, trow, xrow, tile_cnt, h2, post, xs_buf)
        y = _expert_call(l, blk_start, blk_cnt, n_used, xs_buf, y,
                         w_expert_gate, w_expert_up, w_expert_down)
        outs = _combine_call(l, n_list, trow, xrow, tile_cnt, y, x1, pos, mw, mods, gfpost, gpre)
        x, hb = outs

    y_prompt = outs[0].reshape(BATCH, SEQ, D_MODEL)
    y_sample = outs[1].reshape(DEC_BATCH, DEC_SEQ, D_MODEL)
    cache_shape = (BATCH, DEPTH, SEQ, N_KV_HEADS, HEAD_DIM)
    return (y_prompt, y_sample, new_k.reshape(cache_shape), new_v.reshape(cache_shape))
```

```python
import functools
import math

import numpy as np
import jax
import jax.numpy as jnp
from jax import lax
from jax.experimental import pallas as pl
from jax.experimental.pallas import tpu as pltpu

F32 = jnp.float32
BF16 = jnp.bfloat16

D_MODEL = 1024
BATCH = 32
SEQ = 256
DEPTH = 4
DEC_BATCH = 2
DEC_SEQ = 2048
PAST_LEN = 512
GRID_W = 64
N_HEADS = 8
N_KV_HEADS = 2
HEAD_DIM = 64
KV_GROUP = N_HEADS // N_KV_HEADS
ATTN_WIDTH = N_HEADS * HEAD_DIM
KV_WIDTH = N_KV_HEADS * HEAD_DIM
ROPE_AXIS_DIM = HEAD_DIM // 2
ROPE_BASE = 10000.0
POOL_WINDOWS = (2, 4, 8, 16)
POOL_WIDTH = 256
POOL_GROUP_DIM = 64
FOURIER_WIDTH = 256
FOURIER_GROUP_DIM = 64
N_BRANCHES = 3
OFF_K = ATTN_WIDTH
OFF_P = ATTN_WIDTH + 2 * KV_WIDTH
OFF_G = OFF_P + POOL_WIDTH + FOURIER_WIDTH
IN_WIDTH = OFF_G + N_BRANCHES * D_MODEL
N_EXPERT_GROUPS = 4
EXPERTS_PER_GROUP = 8
N_EXPERTS = N_EXPERT_GROUPS * EXPERTS_PER_GROUP
TOP_K = 2
EXPERT_FF = 256
N_MOD = 6
RMS_EPS = 1e-6
QK_SCALE = HEAD_DIM ** -0.5 * math.log2(math.e)

N_P = BATCH * SEQ
N_S = DEC_BATCH * DEC_SEQ
N_TOK = N_P + N_S
LANES = 128
MOD_ROWS = 8
ROUTER_ROWS = 40
POOL_PAD = 16

TM = 512
PROJ_TM = 1024
N_TILES = N_TOK // TM
P_TILES = N_P // TM
S_TILES_PER_SEQ = DEC_SEQ // TM
CTX_SEQ_PER_STEP = 4
TQ_S = 256
KEY_CHUNK = 512
TF_S = 512
EBLK = 512
X_SLOTS = 4
RUN_ALIGN = 8
RUN_SHIFT = 3
RUN_BITS = 7
assert RUN_ALIGN << (RUN_BITS - 1) == TM
SORT_ROWS = TOP_K * TM + N_EXPERTS * RUN_ALIGN
DISPATCH_SLOTS = 3
assert N_TILES >= DISPATCH_SLOTS
TILE_BITS = 8
assert SORT_ROWS < RUN_ALIGN << TILE_BITS
N_RUNS = N_TILES * N_EXPERTS
N_EBLK = (N_TOK * TOP_K + N_RUNS * (RUN_ALIGN - 1) + N_EXPERTS * (EBLK - 1) + EBLK - 1) // EBLK
XS_ROWS = N_EBLK * EBLK
MOD_NT = 1536


def _dot(a, b):
    return jnp.dot(a, b, preferred_element_type=F32)


def _rms(x):
    return x * lax.rsqrt(jnp.mean(x * x, axis=-1, keepdims=True) + RMS_EPS)


def _mod_row(i):
    return jnp.where(i < P_TILES, 0, 1 + (i - P_TILES) // S_TILES_PER_SEQ)


def _rope_block(i):
    p_tiles = N_P // PROJ_TM
    per_seq = DEC_SEQ // PROJ_TM
    return jnp.where(i < p_tiles, per_seq, (i - p_tiles) % per_seq)


def _mod_kernel(c_ref, w_ref, b_ref, o_ref):
    c = c_ref[...]
    s = (c * jax.nn.sigmoid(c)).astype(BF16)
    o_ref[...] = _dot(s, w_ref[...].astype(BF16)) + b_ref[...]


def _mod_call(c_all, w_ada, b_ada):
    nt = (N_MOD * D_MODEL) // MOD_NT
    return pl.pallas_call(
        _mod_kernel,
        grid=(DEPTH, nt),
        in_specs=[
            pl.BlockSpec((MOD_ROWS, D_MODEL), lambda l, j: (0, 0)),
            pl.BlockSpec((None, D_MODEL, MOD_NT), lambda l, j: (l, 0, j)),
            pl.BlockSpec((None, 1, MOD_NT), lambda l, j: (l, 0, j)),
        ],
        out_specs=pl.BlockSpec((None, MOD_ROWS, MOD_NT), lambda l, j: (l, 0, j)),
        out_shape=jax.ShapeDtypeStruct((DEPTH, MOD_ROWS, N_MOD * D_MODEL), F32),
        name="adaln_mod",
    )(c_all, w_ada, b_ada.reshape(DEPTH, 1, N_MOD * D_MODEL))


def _prenorm(x, mod, gain):
    return _rms(x) * (gain * (1.0 + mod[:, D_MODEL:2 * D_MODEL])) + mod[:, 0:D_MODEL]


def _prenorm_kernel(xp_ref, xs_ref, mod_ref, gpre_ref, x_ref, hb_ref):
    x = jnp.where(pl.program_id(0) < P_TILES, xp_ref[...], xs_ref[...])
    x_ref[...] = x
    hb_ref[...] = _prenorm(x, mod_ref[...], gpre_ref[...]).astype(BF16)


def _prenorm_call(x_prompt, x_sample, mods, gpre):
    row = lambda i: (i, 0)
    return pl.pallas_call(
        _prenorm_kernel,
        grid=(N_TILES,),
        in_specs=[
            pl.BlockSpec((TM, D_MODEL), lambda i: (jnp.minimum(i, P_TILES - 1), 0)),
            pl.BlockSpec((TM, D_MODEL), lambda i: (jnp.maximum(i - P_TILES, 0), 0)),
            pl.BlockSpec((None, None, 1, 2 * D_MODEL), lambda i: (0, _mod_row(i), 0, 0)),
            pl.BlockSpec((None, 1, D_MODEL), lambda i: (0, 0, 0)),
        ],
        out_specs=[pl.BlockSpec((TM, D_MODEL), row), pl.BlockSpec((TM, D_MODEL), row)],
        out_shape=[jax.ShapeDtypeStruct((N_TOK, D_MODEL), F32),
                   jax.ShapeDtypeStruct((N_TOK, D_MODEL), BF16)],
        name="prenorm",
    )(x_prompt, x_sample, mods, gpre)


def _proj_kernel(hb_ref, w_ref, qg_ref, kg_ref, cos_ref, s1_ref, s2_ref, avg_ref, dft_ref,
                 ck_in_ref, cv_in_ref,
                 q_ref, k_ref, v_ref, xp_ref, xc_ref, xs_ref, g_ref, ck_ref, cv_ref):
    del ck_in_ref, cv_in_ref
    hb = hb_ref[...]

    cos = cos_ref[...]
    s1 = s1_ref[...]
    s2 = s2_ref[...]

    def rope(t):
        return (t * cos + pltpu.roll(t, LANES - ROPE_AXIS_DIM // 2, 1) * s1
                + pltpu.roll(t, ROPE_AXIS_DIM // 2, 1) * s2)

    head = _dot(hb, w_ref[:, 0:OFF_G])
    avg = avg_ref[...]
    for c in range(ATTN_WIDTH // LANES):
        lo = c * LANES
        q = head[:, lo:lo + LANES]
        ms = _dot((q * q).astype(BF16), avg)
        q = q * lax.rsqrt(ms + RMS_EPS) * qg_ref[...]
        q_ref[:, lo:lo + LANES] = (rope(q) * QK_SCALE).astype(BF16)

    k = head[:, OFF_K:OFF_K + KV_WIDTH]
    ms = _dot((k * k).astype(BF16), avg)
    k = rope(k * lax.rsqrt(ms + RMS_EPS) * kg_ref[...])
    v = head[:, OFF_K + KV_WIDTH:OFF_P]
    k_ref[...] = k
    v_ref[...] = v

    @pl.when(pl.program_id(0) < N_P // PROJ_TM)
    def _():
        for s in range(PROJ_TM // SEQ):
            ck_ref[s] = k[s * SEQ:(s + 1) * SEQ]
            cv_ref[s] = v[s * SEQ:(s + 1) * SEQ]

    xp_ref[...] = head[:, OFF_P:OFF_P + POOL_WIDTH]
    cs = _dot(head[:, OFF_P + POOL_WIDTH:OFF_G].astype(BF16), dft_ref[...])
    xc_ref[...] = cs[:, 0:FOURIER_WIDTH].astype(BF16)
    xs_ref[...] = cs[:, FOURIER_WIDTH:].astype(BF16)

    for c in range(N_BRANCHES):
        lo = OFF_G + c * D_MODEL
        g_ref[:, c * D_MODEL:(c + 1) * D_MODEL] = _dot(hb, w_ref[:, lo:lo + D_MODEL]).astype(BF16)


def _proj_call(l, hb, w_in, qg, kg, cos_t, s1_t, s2_t, avg, dft, cache_k, cache_v):
    row = lambda i: (i, 0)
    const2 = lambda i: (0, 0)
    per_layer = lambda i: (l, 0, 0)
    rope_spec = pl.BlockSpec((PROJ_TM, LANES), lambda i: (_rope_block(i), 0))
    cache_spec = pl.BlockSpec((PROJ_TM // SEQ, None, SEQ, KV_WIDTH),
                              lambda i: (jnp.minimum(i, N_P // PROJ_TM - 1), l, 0, 0))
    cache_shape = jax.ShapeDtypeStruct((BATCH, DEPTH, SEQ, KV_WIDTH), F32)
    return pl.pallas_call(
        _proj_kernel,
        grid=(N_TOK // PROJ_TM,),
        in_specs=[
            pl.BlockSpec((PROJ_TM, D_MODEL), row),
            pl.BlockSpec((None, D_MODEL, IN_WIDTH), per_layer),
            pl.BlockSpec((None, 1, LANES), per_layer),
            pl.BlockSpec((None, 1, LANES), per_layer),
            rope_spec, rope_spec, rope_spec,
            pl.BlockSpec((LANES, LANES), const2),
            pl.BlockSpec((FOURIER_WIDTH, 2 * FOURIER_WIDTH), const2),
            pl.BlockSpec(memory_space=pl.ANY),
            pl.BlockSpec(memory_space=pl.ANY),
        ],
        out_specs=[
            pl.BlockSpec((PROJ_TM, ATTN_WIDTH), row),
            pl.BlockSpec((PROJ_TM, KV_WIDTH), row),
            pl.BlockSpec((PROJ_TM, KV_WIDTH), row),
            pl.BlockSpec((PROJ_TM, POOL_WIDTH), row),
            pl.BlockSpec((PROJ_TM, FOURIER_WIDTH), row),
            pl.BlockSpec((PROJ_TM, FOURIER_WIDTH), row),
            pl.BlockSpec((PROJ_TM, N_BRANCHES * D_MODEL), row),
            cache_spec, cache_spec,
        ],
        out_shape=[
            jax.ShapeDtypeStruct((N_TOK, ATTN_WIDTH), BF16),
            jax.ShapeDtypeStruct((N_TOK, KV_WIDTH), F32),
            jax.ShapeDtypeStruct((N_TOK, KV_WIDTH), F32),
            jax.ShapeDtypeStruct((N_TOK, POOL_WIDTH), F32),
            jax.ShapeDtypeStruct((N_TOK, FOURIER_WIDTH), BF16),
            jax.ShapeDtypeStruct((N_TOK, FOURIER_WIDTH), BF16),
            jax.ShapeDtypeStruct((N_TOK, N_BRANCHES * D_MODEL), BF16),
            cache_shape, cache_shape,
        ],
        input_output_aliases={9: 7, 10: 8},
        name="proj",
    )(hb, w_in, qg, kg, cos_t, s1_t, s2_t, avg, dft, cache_k, cache_v)


def _attn_kernel(*refs, n_parts, tq):
    q_ref = refs[0]
    kv_refs = refs[1:1 + 2 * n_parts]
    o_ref = refs[-1]
    single_chunk = n_parts == 1 and kv_refs[0].shape[0] <= KEY_CHUNK
    outs = []
    for j in range(N_KV_HEADS):
        lo = j * HEAD_DIM
        qs = jnp.concatenate(
            [q_ref[:, (KV_GROUP * j + g) * HEAD_DIM:(KV_GROUP * j + g + 1) * HEAD_DIM]
             for g in range(KV_GROUP)], axis=0)
        m = acc = None
        for p in range(n_parts):
            k_ref, v_ref = kv_refs[2 * p], kv_refs[2 * p + 1]
            for c0 in range(0, k_ref.shape[0], KEY_CHUNK):
                c1 = min(c0 + KEY_CHUNK, k_ref.shape[0])
                kc = k_ref[c0:c1, lo:lo + HEAD_DIM].astype(BF16)
                s = lax.dot_general(qs, kc, (((1,), (1,)), ((), ())), preferred_element_type=F32)
                mc = jnp.max(s, axis=1, keepdims=True)
                m_new = mc if m is None else jnp.maximum(m, mc)
                e = jnp.exp2(s - m_new)
                vc = v_ref[c0:c1, lo:lo + HEAD_DIM].astype(BF16)
                if single_chunk:
                    den = jnp.sum(e, axis=1, keepdims=True)
                else:
                    vc = jnp.concatenate([vc, jnp.ones((c1 - c0, HEAD_DIM), BF16)], axis=1)
                pv = _dot(e.astype(BF16), vc)
                acc = pv if m is None else acc * jnp.exp2(m - m_new) + pv
                m = m_new
        o = acc / den if single_chunk else acc[:, 0:HEAD_DIM] / acc[:, HEAD_DIM:2 * HEAD_DIM]
        outs.extend(o[g * tq:(g + 1) * tq] for g in range(KV_GROUP))
    o_ref[...] = jnp.concatenate(outs, axis=1).astype(BF16)


def _attn_prompt_call(q, k, v):
    blk = lambda b: (b, 0)
    return pl.pallas_call(
        functools.partial(_attn_kernel, n_parts=1, tq=SEQ),
        grid=(BATCH,),
        in_specs=[
            pl.BlockSpec((SEQ, ATTN_WIDTH), blk),
            pl.BlockSpec((SEQ, KV_WIDTH), blk),
            pl.BlockSpec((SEQ, KV_WIDTH), blk),
        ],
        out_specs=pl.BlockSpec((SEQ, ATTN_WIDTH), blk),
        out_shape=jax.ShapeDtypeStruct((N_P, ATTN_WIDTH), BF16),
        name="attn_context",
    )(q, k, v)


def _attn_sample_call(l, q, k, v, cache_k, cache_v):
    nq = DEC_SEQ // TQ_S
    qrow = lambda b, i: (N_P // TQ_S + b * nq + i, 0)
    seq = lambda b, i: (N_P // DEC_SEQ + b, 0)
    cache = lambda b, i: (b, l, 0, 0)
    return pl.pallas_call(
        functools.partial(_attn_kernel, n_parts=2, tq=TQ_S),
        grid=(DEC_BATCH, nq),
        in_specs=[
            pl.BlockSpec((TQ_S, ATTN_WIDTH), qrow),
            pl.BlockSpec((None, None, PAST_LEN, KV_WIDTH), cache),
            pl.BlockSpec((None, None, PAST_LEN, KV_WIDTH), cache),
            pl.BlockSpec((DEC_SEQ, KV_WIDTH), seq),
            pl.BlockSpec((DEC_SEQ, KV_WIDTH), seq),
        ],
        out_specs=pl.BlockSpec((TQ_S, ATTN_WIDTH), lambda b, i: (b * nq + i, 0)),
        out_shape=jax.ShapeDtypeStruct((N_S, ATTN_WIDTH), BF16),
        name="attn_latent",
    )(q, cache_k, cache_v, k, v)


def _pool_kernel(xp_ref, bdw_ref, sc_ref, o_ref, pad_ref, *, seq_len, n_seq):
    for s in range(n_seq):
        _pool_sequence(xp_ref, bdw_ref, sc_ref, o_ref, pad_ref, s * seq_len, seq_len)


def _pool_sequence(xp_ref, bdw_ref, sc_ref, o_ref, pad_ref, row0, seq_len):
    half = POOL_WIDTH // 2
    zeros = jnp.zeros((POOL_PAD, POOL_WIDTH), F32)
    pad_ref[0:POOL_PAD, :] = zeros
    pad_ref[POOL_PAD + seq_len:, :] = zeros
    pad_ref[POOL_PAD:POOL_PAD + seq_len, :] = xp_ref[row0:row0 + seq_len, :]
    chunk = min(seq_len, 256)
    lane = lax.broadcasted_iota(jnp.int32, (chunk, half), 1)
    first = lane < POOL_GROUP_DIM
    for c in range(seq_len // chunk):
        base = c * chunk
        t = lax.broadcasted_iota(jnp.int32, (chunk, half), 0) + base

        def sh(j, lo):
            return pad_ref[POOL_PAD + base + j:POOL_PAD + base + j + chunk, lo:lo + half]

        def cnt(w):
            return (jnp.minimum(t + w // 2, seq_len) - jnp.maximum(t - w // 2, 0)).astype(F32)

        xa = sh(0, 0)
        w2 = sh(-1, 0) + xa
        w4 = w2 + sh(-2, 0) + sh(1, 0)
        xb = sh(0, half)
        w8 = xb
        for j in (-4, -3, -2, -1, 1, 2, 3):
            w8 = w8 + sh(j, half)
        w16 = w8
        for j in (-8, -7, -6, -5, 4, 5, 6, 7):
            w16 = w16 + sh(j, half)
        pa = jnp.where(first, w2 / cnt(2), w4 / cnt(4)) - xa
        pb = jnp.where(first, w8 / cnt(8), w16 / cnt(16)) - xb
        pooled = jnp.concatenate([pa, pb], axis=1).astype(BF16)
        o_ref[row0 + base:row0 + base + chunk, :] = (
            _dot(pooled, bdw_ref[...]) * sc_ref[...]).astype(BF16)


def _pool_call(l, xp, bdw, scale, seq_len, n_seq, blk0, seq_per_step):
    per_layer = lambda b: (l, 0, 0)
    rows = seq_len * seq_per_step
    return pl.pallas_call(
        functools.partial(_pool_kernel, seq_len=seq_len, n_seq=seq_per_step),
        grid=(n_seq // seq_per_step,),
        in_specs=[
            pl.BlockSpec((rows, POOL_WIDTH), lambda b: (blk0 + b, 0)),
            pl.BlockSpec((None, POOL_WIDTH, POOL_WIDTH), per_layer),
            pl.BlockSpec((None, 1, POOL_WIDTH), per_layer),
        ],
        out_specs=pl.BlockSpec((rows, POOL_WIDTH), lambda b: (b, 0)),
        out_shape=jax.ShapeDtypeStruct((n_seq * seq_len, POOL_WIDTH), BF16),
        scratch_shapes=[pltpu.VMEM((seq_len + 2 * POOL_PAD, POOL_WIDTH), F32)],
        name="pool_%d" % seq_len,
    )(xp, bdw, scale)


def _fourier_kernel(c_ref, s_ref, xc_ref, xs_ref, o_ref, *, scale, n_seq):
    seq_len = xc_ref.shape[0] // n_seq
    for s in range(n_seq):
        rows = slice(s * seq_len, (s + 1) * seq_len)
        y = _dot(c_ref[...], xc_ref[rows, :]) - _dot(s_ref[...], xs_ref[rows, :])
        o_ref[s] = (y * scale).astype(BF16)


def _fourier_prompt_call(cmat, smat, xc, xs):
    blk = lambda b: (b, 0)
    const2 = lambda b: (0, 0)
    rows = SEQ * CTX_SEQ_PER_STEP
    return pl.pallas_call(
        functools.partial(_fourier_kernel, scale=1.0 / math.sqrt(SEQ * FOURIER_GROUP_DIM),
                          n_seq=CTX_SEQ_PER_STEP),
        grid=(BATCH // CTX_SEQ_PER_STEP,),
        in_specs=[
            pl.BlockSpec((SEQ, SEQ), const2),
            pl.BlockSpec((SEQ, SEQ), const2),
            pl.BlockSpec((rows, FOURIER_WIDTH), blk),
            pl.BlockSpec((rows, FOURIER_WIDTH), blk),
        ],
        out_specs=pl.BlockSpec((CTX_SEQ_PER_STEP, SEQ, FOURIER_WIDTH), lambda b: (b, 0, 0)),
        out_shape=jax.ShapeDtypeStruct((BATCH, SEQ, FOURIER_WIDTH), BF16),
        name="fourier_context",
    )(cmat, smat, xc, xs).reshape(N_P, FOURIER_WIDTH)


def _fourier_sample_call(cmat, smat, xc, xs):
    rows = lambda i: (i, 0)
    latent = lambda i: (N_P // N_S, 0)
    return pl.pallas_call(
        functools.partial(_fourier_kernel, scale=1.0 / math.sqrt(DEC_SEQ * FOURIER_GROUP_DIM),
                          n_seq=DEC_BATCH),
        grid=(DEC_SEQ // TF_S,),
        in_specs=[
            pl.BlockSpec((TF_S, DEC_SEQ), rows),
            pl.BlockSpec((TF_S, DEC_SEQ), rows),
            pl.BlockSpec((N_S, FOURIER_WIDTH), latent),
            pl.BlockSpec((N_S, FOURIER_WIDTH), latent),
        ],
        out_specs=pl.BlockSpec((DEC_BATCH, TF_S, FOURIER_WIDTH), lambda i: (0, i, 0)),
        out_shape=jax.ShapeDtypeStruct((DEC_BATCH, DEC_SEQ, FOURIER_WIDTH), BF16),
        name="fourier_latent",
    )(cmat, smat, xc, xs).reshape(N_S, FOURIER_WIDTH)


def _merge_kernel(x_ref, attn_p_ref, attn_s_ref, pool_p_ref, pool_s_ref, four_p_ref, four_s_ref,
                  g_ref, mod_ref, gpost_ref, gffn_ref,
                  wa_ref, wp_ref, wf_ref, wo_ref, wrt_ref, brt_ref, triu_ref, lower_ref,
                  x1_ref, h2_ref, pos_ref, post_ref, mw_ref, cnt_ref):
    i = pl.program_id(0)
    mod = mod_ref[...]
    g1 = mod[:, 2 * D_MODEL:3 * D_MODEL]
    sh2 = mod[:, 3 * D_MODEL:4 * D_MODEL]
    sc2 = mod[:, 4 * D_MODEL:5 * D_MODEL]

    def gate2(c):
        return jnp.tanh(g_ref[:, c * D_MODEL:(c + 1) * D_MODEL].astype(F32)) + 1.0

    def branch(p_ref, s_ref):
        return jnp.where(i < P_TILES, p_ref[...], s_ref[...])

    merged = gate2(0) * _dot(branch(attn_p_ref, attn_s_ref), wa_ref[...])
    merged = merged + gate2(1) * _dot(branch(pool_p_ref, pool_s_ref), wp_ref[...])
    merged = merged + gate2(2) * _dot(branch(four_p_ref, four_s_ref), wf_ref[...])
    mix = _dot((0.5 * merged).astype(BF16), wo_ref[...])
    x1 = x_ref[...] + (g1 * gpost_ref[...]) * _rms(mix)
    x1_ref[...] = x1
    h2 = _rms(x1) * (gffn_ref[...] * (1.0 + sc2)) + sh2
    h2b = h2.astype(BF16)
    h2_ref[...] = h2b

    logits = lax.dot_general(wrt_ref[...], h2b, (((1,), (1,)), ((), ())),
                             preferred_element_type=F32) + brt_ref[...]
    top = logits[0:ROUTER_ROWS, :]
    row = lax.broadcasted_iota(jnp.int32, (ROUTER_ROWS, TM), 0)
    rowf = row.astype(F32)
    neg = jnp.float32(-3e38)
    big = jnp.float32(1e9)
    is_g = (row >= N_EXPERTS) & (row < N_EXPERTS + N_EXPERT_GROUPS)
    lg = jnp.where(is_g, top, neg)
    gmax = jnp.max(lg, axis=0, keepdims=True)
    g_sel = jnp.min(jnp.where(lg == gmax, rowf - N_EXPERTS, big), axis=0, keepdims=True)
    p_g = 1.0 / jnp.sum(jnp.where(is_g, jnp.exp(top - gmax), 0.0), axis=0, keepdims=True)
    grp = lax.shift_right_logical(row, int(math.log2(EXPERTS_PER_GROUP))).astype(F32)
    in_grp = (row < N_EXPERTS) & (grp == g_sel)
    le = jnp.where(in_grp, top, neg)
    v1 = jnp.max(le, axis=0, keepdims=True)
    i1 = jnp.min(jnp.where(le == v1, rowf, big), axis=0, keepdims=True)
    le2 = jnp.where(rowf == i1, neg, le)
    v2 = jnp.max(le2, axis=0, keepdims=True)
    i2 = jnp.min(jnp.where(le2 == v2, rowf, big), axis=0, keepdims=True)
    e21 = jnp.exp(v2 - v1)
    w1 = p_g / (1.0 + e21)
    w2 = p_g * e21 / (1.0 + e21)

    rows_all = lax.broadcasted_iota(jnp.int32, (LANES, TM), 0)
    rows_allf = rows_all.astype(F32)
    oh1 = (rows_allf == i1).astype(F32)
    oh2 = (rows_allf == i2).astype(F32)
    ohb = (oh1 + oh2).astype(BF16)
    before = _dot(ohb, triu_ref[...])
    cnt = _dot(ohb, jnp.ones((TM, LANES), BF16))
    cnt_pad = (lax.shift_right_logical(cnt.astype(jnp.int32) + (RUN_ALIGN - 1), RUN_SHIFT)
               * RUN_ALIGN).astype(F32)
    run_off = _dot(lower_ref[...], cnt_pad.astype(BF16))
    slot = jnp.concatenate([run_off] * (TM // LANES), axis=1) + before
    p1 = jnp.sum(slot * oh1, axis=0, keepdims=True)
    p2 = jnp.sum(slot * oh2, axis=0, keepdims=True)
    cnt_ref[...] = cnt

    meta = jnp.where(rows_all == 0, p1, jnp.where(rows_all == 1, p2,
                     jnp.where(rows_all == 2, w1, jnp.where(rows_all == 3, w2, 0.0))))
    post_ref[...] = meta[0:MOD_ROWS, :].astype(jnp.int32)
    meta_t = meta.T
    pos_ref[...] = meta_t.astype(jnp.int32)
    mw_ref[...] = meta_t


def _merge_call(l, x, attn, pool, four, gates, mods, gpost, gffn, wa, wp, wf, wo, wrt, brt, triu,
                lower):
    row = lambda i: (i, 0)
    prow = lambda i: (jnp.minimum(i, P_TILES - 1), 0)
    srow = lambda i: (jnp.maximum(i - P_TILES, 0), 0)
    const2 = lambda i: (0, 0)
    per_layer = lambda i: (l, 0, 0)
    return pl.pallas_call(
        _merge_kernel,
        grid=(N_TILES,),
        in_specs=[
            pl.BlockSpec((TM, D_MODEL), row),
            pl.BlockSpec((TM, ATTN_WIDTH), prow),
            pl.BlockSpec((TM, ATTN_WIDTH), srow),
            pl.BlockSpec((TM, POOL_WIDTH), prow),
            pl.BlockSpec((TM, POOL_WIDTH), srow),
            pl.BlockSpec((TM, FOURIER_WIDTH), prow),
            pl.BlockSpec((TM, FOURIER_WIDTH), srow),
            pl.BlockSpec((TM, N_BRANCHES * D_MODEL), row),
            pl.BlockSpec((None, None, 1, N_MOD * D_MODEL), lambda i: (l, _mod_row(i), 0, 0)),
            pl.BlockSpec((None, 1, D_MODEL), per_layer),
            pl.BlockSpec((None, 1, D_MODEL), per_layer),
            pl.BlockSpec((None, ATTN_WIDTH, D_MODEL), per_layer),
            pl.BlockSpec((None, POOL_WIDTH, D_MODEL), per_layer),
            pl.BlockSpec((None, FOURIER_WIDTH, D_MODEL), per_layer),
            pl.BlockSpec((None, D_MODEL, D_MODEL), per_layer),
            pl.BlockSpec((None, LANES, D_MODEL), per_layer),
            pl.BlockSpec((None, LANES, TM), per_layer),
            pl.BlockSpec((TM, TM), const2),
            pl.BlockSpec((LANES, LANES), const2),
        ],
        out_specs=[
            pl.BlockSpec((TM, D_MODEL), row),
            pl.BlockSpec((TM, D_MODEL), row),
            pl.BlockSpec((TM, LANES), row),
            pl.BlockSpec((MOD_ROWS, TM), lambda i: (0, i)),
            pl.BlockSpec((TM, LANES), row),
            pl.BlockSpec((LANES, LANES), row),
        ],
        out_shape=[
            jax.ShapeDtypeStruct((N_TOK, D_MODEL), F32),
            jax.ShapeDtypeStruct((N_TOK, D_MODEL), BF16),
            jax.ShapeDtypeStruct((N_TOK, LANES), jnp.int32),
            jax.ShapeDtypeStruct((MOD_ROWS, N_TOK), jnp.int32),
            jax.ShapeDtypeStruct((N_TOK, LANES), F32),
            jax.ShapeDtypeStruct((N_TILES * LANES, LANES), F32),
        ],
        name="merge_router",
    )(x, attn[0], attn[1], pool[0], pool[1], four[0], four[1], gates, mods, gpost, gffn,
      wa, wp, wf, wo, wrt, brt, triu, lower)


HALF = D_MODEL // 2
U32 = jnp.uint32
HI_MASK = 0xFFFF0000


def _pack_rows(x):
    lo = lax.bitcast_convert_type(x[:, :HALF], U32)
    hi = lax.bitcast_convert_type(x[:, HALF:], U32)
    return lax.shift_right_logical(lo, U32(16)) | (hi & U32(HI_MASK))


def _unpack_rows(u):
    lo = lax.bitcast_convert_type(lax.shift_left(u, U32(16)), F32)
    hi = lax.bitcast_convert_type(u & U32(HI_MASK), F32)
    return jnp.concatenate([lo.astype(BF16), hi.astype(BF16)], axis=1)


def _rows(ref, start, size):
    return ref.at[pl.ds(pl.multiple_of(start, RUN_ALIGN), size)]


def _start_tile_copies(tile, n_ref, trow_ref, xrow_ref, make_copy):
    for bit in range(RUN_BITS):
        lst = bit * N_TILES + tile

        def body(k, carry, bit=bit, lst=lst):
            j = lst * N_EXPERTS + k
            make_copy(trow_ref[j], xrow_ref[j], RUN_ALIGN << bit).start()
            return carry

        lax.fori_loop(0, n_ref[lst], body, 0)


def _wait_tile_copies(total, src_ref, dst_ref, sem):
    for b in range(TILE_BITS):
        size = RUN_ALIGN << b

        @pl.when((lax.shift_right_logical(total, b) & 1) == 1)
        def _(size=size):
            pltpu.make_async_copy(_rows(src_ref, 0, size), _rows(dst_ref, 0, size), sem).wait()


def _dispatch_kernel(n_ref, trow_ref, xrow_ref, tsum_ref, h_ref, post_ref, xs_in_ref, xs_ref,
                     sorted_ref, sem):
    del xs_in_ref
    i = pl.program_id(0)
    slot = lax.rem(i, DISPATCH_SLOTS)
    rows = lax.broadcasted_iota(jnp.int32, (SORT_ROWS, TM), 0)
    p = post_ref[...]
    perm = jnp.where(rows == p[0:1, :], 1.0, jnp.where(rows == p[1:2, :], 1.0, 0.0)).astype(BF16)
    sorted_ref[slot] = _pack_rows(_dot(perm, h_ref[...]))

    def wait(tile, slot):
        _wait_tile_copies(tsum_ref[tile], sorted_ref.at[slot], xs_ref, sem.at[slot])

    lag = DISPATCH_SLOTS - 1

    @pl.when(i >= lag)
    def _():
        wait(i - lag, lax.rem(i + 1, DISPATCH_SLOTS))

    _start_tile_copies(i, n_ref, trow_ref, xrow_ref, lambda t, x, size: pltpu.make_async_copy(
        _rows(sorted_ref.at[slot], t, size), _rows(xs_ref, x, size), sem.at[slot]))

    @pl.when(i == pl.num_programs(0) - 1)
    def _():
        for back in reversed(range(lag)):
            wait(i - back, lax.rem(i - back, DISPATCH_SLOTS))


def _dispatch_call(n_list, trow, xrow, tsum, h2, post, xs):
    return pl.pallas_call(
        _dispatch_kernel,
        grid_spec=pltpu.PrefetchScalarGridSpec(
            num_scalar_prefetch=4,
            grid=(N_TILES,),
            in_specs=[
                pl.BlockSpec((TM, D_MODEL), lambda i, *_: (i, 0)),
                pl.BlockSpec((MOD_ROWS, TM), lambda i, *_: (0, i)),
                pl.BlockSpec(memory_space=pl.ANY),
            ],
            out_specs=pl.BlockSpec(memory_space=pl.ANY),
            scratch_shapes=[pltpu.VMEM((DISPATCH_SLOTS, SORT_ROWS, HALF), U32),
                            pltpu.SemaphoreType.DMA((DISPATCH_SLOTS,))],
        ),
        out_shape=jax.ShapeDtypeStruct((XS_ROWS, HALF), U32),
        input_output_aliases={6: 0},
        compiler_params=pltpu.CompilerParams(dimension_semantics=("arbitrary",)),
        name="moe_dispatch",
    )(n_list, trow, xrow, tsum, h2, post, xs)


def _expert_kernel(bstart_ref, bcnt_ref, nu_ref, wg_ref, wu_ref, wd_ref, xs_ref, y_prev_ref, y_ref,
                   wgub, wdb, xbuf, ybuf, xsem, ysem):
    del y_prev_ref
    e = pl.program_id(0)
    n_used = nu_ref[0]

    def x_copy(g, slot):
        rows = pl.ds(pl.multiple_of(g * EBLK, EBLK), EBLK)
        return pltpu.make_async_copy(xs_ref.at[rows], xbuf.at[slot], xsem.at[slot])

    def y_copy(g, slot):
        rows = pl.ds(pl.multiple_of(g * EBLK, EBLK), EBLK)
        return pltpu.make_async_copy(ybuf.at[slot], y_ref.at[rows], ysem.at[slot])

    @pl.when(e == 0)
    def _():
        for g in range(X_SLOTS - 1):
            @pl.when(g < n_used)
            def _(g=g):
                x_copy(g, g).start()

    wgub[:, 0:EXPERT_FF] = wg_ref[...].astype(BF16)
    wgub[:, EXPERT_FF:] = wu_ref[...].astype(BF16)
    wdb[...] = wd_ref[...].astype(BF16)
    first = bstart_ref[e]

    def block(j, carry):
        g = first + j
        slot = lax.rem(g, 2)
        xslot = lax.rem(g, X_SLOTS)
        x_copy(g, xslot).wait()

        @pl.when(g + (X_SLOTS - 1) < n_used)
        def _():
            x_copy(g + (X_SLOTS - 1), lax.rem(g + (X_SLOTS - 1), X_SLOTS)).start()

        xb = _unpack_rows(xbuf[xslot])
        gu = _dot(xb, wgub[...])
        gate = gu[:, 0:EXPERT_FF]
        act = (gate * jax.nn.sigmoid(gate)) * gu[:, EXPERT_FF:]
        y = _dot(act.astype(BF16), wdb[...])

        @pl.when(g >= 2)
        def _():
            y_copy(g - 2, slot).wait()

        ybuf[slot] = _pack_rows(y.astype(BF16).astype(F32))
        y_copy(g, slot).start()
        return carry

    lax.fori_loop(0, bcnt_ref[e], block, 0)

    @pl.when(e == pl.num_programs(0) - 1)
    def _():
        @pl.when(n_used >= 2)
        def _():
            y_copy(n_used - 2, lax.rem(n_used, 2)).wait()

        y_copy(n_used - 1, lax.rem(n_used - 1, 2)).wait()


def _expert_call(l, blk_start, blk_cnt, n_used, xs, y_prev, w_gate, w_up, w_down):
    wsel = lambda e, *_: (l, e, 0, 0)
    return pl.pallas_call(
        _expert_kernel,
        grid_spec=pltpu.PrefetchScalarGridSpec(
            num_scalar_prefetch=3,
            grid=(N_EXPERTS,),
            in_specs=[
                pl.BlockSpec((None, None, D_MODEL, EXPERT_FF), wsel),
                pl.BlockSpec((None, None, D_MODEL, EXPERT_FF), wsel),
                pl.BlockSpec((None, None, EXPERT_FF, D_MODEL), wsel),
                pl.BlockSpec(memory_space=pl.ANY),
                pl.BlockSpec(memory_space=pl.ANY),
            ],
            out_specs=pl.BlockSpec(memory_space=pl.ANY),
            scratch_shapes=[
                pltpu.VMEM((D_MODEL, 2 * EXPERT_FF), BF16),
                pltpu.VMEM((EXPERT_FF, D_MODEL), BF16),
                pltpu.VMEM((X_SLOTS, EBLK, HALF), U32),
                pltpu.VMEM((2, EBLK, HALF), U32),
                pltpu.SemaphoreType.DMA((X_SLOTS,)),
                pltpu.SemaphoreType.DMA((2,)),
            ],
        ),
        out_shape=jax.ShapeDtypeStruct((XS_ROWS, HALF), U32),
        input_output_aliases={7: 0},
        compiler_params=pltpu.CompilerParams(dimension_semantics=("arbitrary",)),
        name="moe_experts",
    )(blk_start, blk_cnt, n_used, w_gate, w_up, w_down, xs, y_prev)


def _combine_kernel(n_ref, trow_ref, xrow_ref, tsum_ref, y_ref, x1_ref, pos_ref, mw_ref, g2_ref,
                    gpost_ref, *rest, has_next):
    if has_next:
        nmod_ref, ngpre_ref, o_ref, hb_ref, ybuf, sem = rest
    else:
        o_ref, os_ref, ybuf, sem = rest
    i = pl.program_id(0)
    slot = lax.rem(i, 2)

    def start(tile, slot):
        _start_tile_copies(tile, n_ref, trow_ref, xrow_ref, lambda t, x, size: pltpu.make_async_copy(
            _rows(y_ref, x, size), _rows(ybuf.at[slot], t, size), sem.at[slot]))

    @pl.when(i == 0)
    def _():
        ybuf[...] = jnp.zeros_like(ybuf)
        start(0, 0)

    @pl.when(i + 1 < pl.num_programs(0))
    def _():
        start(i + 1, 1 - slot)

    _wait_tile_copies(tsum_ref[i], y_ref, ybuf.at[slot], sem.at[slot])

    pos = pos_ref[...]
    mw = mw_ref[...]
    cols = lax.broadcasted_iota(jnp.int32, (TM, SORT_ROWS), 1)
    qw = jnp.where(cols == pos[:, 0:1], mw[:, 2:3],
                   jnp.where(cols == pos[:, 1:2], mw[:, 3:4], 0.0)).astype(BF16)
    ffn = _dot(qw, _unpack_rows(ybuf[slot]))
    x2 = x1_ref[...] + (g2_ref[...] * gpost_ref[...]) * _rms(ffn)
    if has_next:
        o_ref[...] = x2
        hb_ref[...] = _prenorm(x2, nmod_ref[...], ngpre_ref[...]).astype(BF16)
    else:
        @pl.when(i < P_TILES)
        def _():
            o_ref[...] = x2

        os_ref[...] = x2


def _combine_call(l, n_list, trow, xrow, tsum, y, x1, pos, mw, mods, gpost, gpre):
    has_next = l + 1 < DEPTH
    row = lambda i, *_: (i, 0)
    in_specs = [
        pl.BlockSpec(memory_space=pl.ANY),
        pl.BlockSpec((TM, D_MODEL), row),
        pl.BlockSpec((TM, LANES), row),
        pl.BlockSpec((TM, LANES), row),
        pl.BlockSpec((None, None, 1, D_MODEL), lambda i, *_: (l, _mod_row(i), 0, N_MOD - 1)),
        pl.BlockSpec((None, 1, D_MODEL), lambda i, *_: (l, 0, 0)),
    ]
    args = [n_list, trow, xrow, tsum, y, x1, pos, mw, mods, gpost]
    if has_next:
        in_specs += [
            pl.BlockSpec((None, None, 1, 2 * D_MODEL), lambda i, *_: (l + 1, _mod_row(i), 0, 0)),
            pl.BlockSpec((None, 1, D_MODEL), lambda i, *_: (l + 1, 0, 0)),
        ]
        args += [mods, gpre]
        out_specs = [pl.BlockSpec((TM, D_MODEL), row), pl.BlockSpec((TM, D_MODEL), row)]
        out_shape = [jax.ShapeDtypeStruct((N_TOK, D_MODEL), F32),
                     jax.ShapeDtypeStruct((N_TOK, D_MODEL), BF16)]
    else:
        out_specs = [
            pl.BlockSpec((TM, D_MODEL), lambda i, *_: (jnp.minimum(i, P_TILES - 1), 0)),
            pl.BlockSpec((TM, D_MODEL), lambda i, *_: (jnp.maximum(i - P_TILES, 0), 0)),
        ]
        out_shape = [jax.ShapeDtypeStruct((N_P, D_MODEL), F32),
                     jax.ShapeDtypeStruct((N_S, D_MODEL), F32)]
    return pl.pallas_call(
        functools.partial(_combine_kernel, has_next=has_next),
        grid_spec=pltpu.PrefetchScalarGridSpec(
            num_scalar_prefetch=4,
            grid=(N_TILES,),
            in_specs=in_specs,
            out_specs=out_specs,
            scratch_shapes=[pltpu.VMEM((2, SORT_ROWS, HALF), U32), pltpu.SemaphoreType.DMA((2,))],
        ),
        out_shape=out_shape,
        compiler_params=pltpu.CompilerParams(dimension_semantics=("arbitrary",)),
        name="moe_combine",
    )(*args)


def _dft_mats(n):
    k = np.arange(n, dtype=np.int64)
    ang = 2.0 * np.pi * ((k[:, None] * k[None, :]) % n).astype(np.float64) / n
    return np.cos(ang), np.sin(ang)


def _block_diag(m, reps):
    n = m.shape[0]
    out = np.zeros((n * reps, n * reps), m.dtype)
    for r in range(reps):
        out[r * n:(r + 1) * n, r * n:(r + 1) * n] = m
    return out


def _rope_tables():
    t = np.arange(DEC_SEQ)
    pos = np.stack([t // GRID_W, t % GRID_W], axis=1).astype(np.float64)
    n_freq = ROPE_AXIS_DIM // 2
    inv = ROPE_BASE ** (-np.arange(n_freq, dtype=np.float64) * 2.0 / ROPE_AXIS_DIM)
    ang = pos[:, :, None] * inv[None, None, :]
    cos = np.cos(ang)
    sin = np.sin(ang)
    zero = np.zeros_like(sin[:, 0])
    cos_h = np.concatenate([cos[:, 0], cos[:, 0], cos[:, 1], cos[:, 1]], axis=1)
    s1_h = np.concatenate([-sin[:, 0], zero, -sin[:, 1], zero], axis=1)
    s2_h = np.concatenate([zero, sin[:, 0], zero, sin[:, 1]], axis=1)
    reps = LANES // HEAD_DIM

    def table(a, ident):
        a = np.tile(a, (1, reps))
        pad = np.full((PROJ_TM, LANES), ident, np.float64)
        return jnp.asarray(np.concatenate([a, pad], axis=0), F32)

    return table(cos_h, 1.0), table(s1_h, 0.0), table(s2_h, 0.0)


def _copy_lists(run_cnt, tile_off, xs_off):
    bit = jnp.arange(RUN_BITS, dtype=jnp.int32)[:, None, None]
    has = (run_cnt[None] >> bit) & 1
    before = ((run_cnt[None] >> (bit + 1)) << (bit + 1)) * RUN_ALIGN
    slot = jnp.cumsum(has, axis=-1) - has
    hit = (has[..., None] == 1) & (slot[..., None] == jnp.arange(N_EXPERTS, dtype=jnp.int32))

    def compact(rows):
        return jnp.sum(jnp.where(hit, rows[..., None], 0), axis=-2).reshape(-1)

    return (jnp.sum(has, axis=-1).reshape(-1), compact(tile_off[None] + before),
            compact(xs_off[None] + before))


def kernel(x_prompt, x_sample, cache_k, cache_v, c, c_ctx, w_ada, b_ada, norm_mix_pre,
           norm_mix_post, norm_ffn_pre, norm_ffn_post, w_in, q_norm, k_norm, w_attn_out,
           w_pool_group, pool_scale, w_pool_out, w_fourier_out, w_out, w_router_group,
           b_router_group, w_router_expert, b_router_expert, w_expert_gate, w_expert_up,
           w_expert_down):
    cos_t, s1_t, s2_t = _rope_tables()
    avg = jnp.asarray(_block_diag(np.full((HEAD_DIM, HEAD_DIM), 1.0 / HEAD_DIM), LANES // HEAD_DIM), BF16)
    c64, s64 = _dft_mats(FOURIER_GROUP_DIM)
    n_fg = FOURIER_WIDTH // FOURIER_GROUP_DIM
    dft_ch = jnp.asarray(np.concatenate([_block_diag(c64, n_fg), _block_diag(s64, n_fg)], axis=1), BF16)
    cp, sp = _dft_mats(SEQ)
    cp, sp = jnp.asarray(cp, BF16), jnp.asarray(sp, BF16)
    cl, sl = _dft_mats(DEC_SEQ)
    cl, sl = jnp.asarray(cl, BF16), jnp.asarray(sl, BF16)
    triu = jnp.asarray(np.triu(np.ones((TM, TM)), 1), BF16)
    lower = jnp.asarray(np.tril(np.ones((LANES, LANES)), -1), BF16)

    in_scale = np.ones((IN_WIDTH,), np.float32)
    in_scale[OFF_G:] = 0.5
    w_in_b = (w_in * in_scale).astype(BF16)
    wa_b = w_attn_out.astype(BF16)
    wp_b = w_pool_out.astype(BF16)
    wf_b = w_fourier_out.astype(BF16)
    wo_b = w_out.astype(BF16)
    pad_r = jnp.zeros((DEPTH, D_MODEL, LANES - N_EXPERTS - N_EXPERT_GROUPS), F32)
    wrt = jnp.concatenate([w_router_expert, w_router_group, pad_r], axis=2).astype(BF16)
    wrt = jnp.transpose(wrt, (0, 2, 1))
    br = jnp.concatenate([b_router_expert, b_router_group,
                          jnp.zeros((DEPTH, LANES - N_EXPERTS - N_EXPERT_GROUPS), F32)], axis=1)
    brt = jnp.broadcast_to(br[:, :, None], (DEPTH, LANES, TM))
    n_pg = POOL_WIDTH // POOL_GROUP_DIM
    bdw = jnp.zeros((DEPTH, POOL_WIDTH, POOL_WIDTH), F32)
    for g in range(n_pg):
        lo = g * POOL_GROUP_DIM
        bdw = bdw.at[:, lo:lo + POOL_GROUP_DIM, lo:lo + POOL_GROUP_DIM].set(w_pool_group[:, g])
    bdw = bdw.astype(BF16)
    pscale = pool_scale.reshape(DEPTH, 1, POOL_WIDTH)
    qg = jnp.tile(q_norm, (1, LANES // HEAD_DIM)).reshape(DEPTH, 1, LANES)
    kg = jnp.tile(k_norm, (1, LANES // HEAD_DIM)).reshape(DEPTH, 1, LANES)
    gpre = norm_mix_pre.reshape(DEPTH, 1, D_MODEL)
    gpost = norm_mix_post.reshape(DEPTH, 1, D_MODEL)
    gffn = norm_ffn_pre.reshape(DEPTH, 1, D_MODEL)
    gfpost = norm_ffn_post.reshape(DEPTH, 1, D_MODEL)
    ck = cache_k.reshape(DEC_BATCH, DEPTH, PAST_LEN, KV_WIDTH)
    cv = cache_v.reshape(DEC_BATCH, DEPTH, PAST_LEN, KV_WIDTH)

    c_all = jnp.concatenate([c_ctx[None, :], c, jnp.zeros((MOD_ROWS - 1 - DEC_BATCH, D_MODEL), F32)], axis=0)
    mods = _mod_call(c_all, w_ada, b_ada).reshape(DEPTH, MOD_ROWS, 1, N_MOD * D_MODEL)

    xs_buf = jnp.zeros((XS_ROWS, HALF), U32)
    y = jnp.zeros((XS_ROWS, HALF), U32)
    new_k = jnp.zeros((BATCH, DEPTH, SEQ, KV_WIDTH), F32)
    new_v = jnp.zeros((BATCH, DEPTH, SEQ, KV_WIDTH), F32)
    x, hb = _prenorm_call(x_prompt.reshape(N_P, D_MODEL), x_sample.reshape(N_S, D_MODEL), mods, gpre)
    for l in range(DEPTH):
        q, k, v, xp, xc, xsn, gates, new_k, new_v = _proj_call(
            l, hb, w_in_b, qg, kg, cos_t, s1_t, s2_t, avg, dft_ch, new_k, new_v)
        attn = (_attn_prompt_call(q, k, v), _attn_sample_call(l, q, k, v, ck, cv))
        pool = (_pool_call(l, xp, bdw, pscale, SEQ, BATCH, 0, CTX_SEQ_PER_STEP),
                _pool_call(l, xp, bdw, pscale, DEC_SEQ, DEC_BATCH, N_P // DEC_SEQ, 1))
        four = (_fourier_prompt_call(cp, sp, xc, xsn), _fourier_sample_call(cl, sl, xc, xsn))
        x1, h2, pos, post, mw, cnt = _merge_call(l, x, attn, pool, four, gates, mods, gpost, gffn,
                                                 wa_b, wp_b, wf_b, wo_b, wrt, brt, triu, lower)
        runs = cnt.reshape(N_TILES, LANES, LANES)[:, :N_EXPERTS, 0].astype(jnp.int32)
        runs = ((runs + RUN_ALIGN - 1) // RUN_ALIGN) * RUN_ALIGN
        tile_off = jnp.cumsum(runs, axis=1) - runs
        rows_e = jnp.sum(runs, axis=0)
        padded = ((rows_e + EBLK - 1) // EBLK) * EBLK
        pad_end = jnp.cumsum(padded)
        xs_off = (pad_end - padded)[None, :] + jnp.cumsum(runs, axis=0) - runs
        blk_cnt = padded // EBLK
        blk_start = (pad_end - padded) // EBLK
        n_used = pad_end[-1:] // EBLK
        n_list, trow, xrow = _copy_lists(runs // RUN_ALIGN, tile_off, xs_off)
        tile_cnt = jnp.sum(runs, axis=1) // RUN_ALIGN
        xs_buf = _dispatch_call(n_list, trow, xrow, tile_cnt, h2, post, xs_buf)
        y = _expert_call(l, blk_start, blk_cnt, n_used, xs_buf, y,
                         w_expert_gate, w_expert_up, w_expert_down)
        outs = _combine_call(l, n_list, trow, xrow, tile_cnt, y, x1, pos, mw, mods, gfpost, gpre)
        x, hb = outs

    y_prompt = outs[0].reshape(BATCH, SEQ, D_MODEL)
    y_sample = outs[1].reshape(DEC_BATCH, DEC_SEQ, D_MODEL)
    cache_shape = (BATCH, DEPTH, SEQ, N_KV_HEADS, HEAD_DIM)
    return (y_prompt, y_sample, new_k.reshape(cache_shape), new_v.reshape(cache_shape))
```

```python
import functools
import math

import numpy as np
import jax
import jax.numpy as jnp
from jax import lax
from jax.experimental import pallas as pl
from jax.experimental.pallas import tpu as pltpu

F32 = jnp.float32
BF16 = jnp.bfloat16

D_MODEL = 1024
BATCH = 32
SEQ = 256
DEPTH = 4
DEC_BATCH = 2
DEC_SEQ = 2048
PAST_LEN = 512
GRID_W = 64
N_HEADS = 8
N_KV_HEADS = 2
HEAD_DIM = 64
KV_GROUP = N_HEADS // N_KV_HEADS
ATTN_WIDTH = N_HEADS * HEAD_DIM
KV_WIDTH = N_KV_HEADS * HEAD_DIM
ROPE_AXIS_DIM = HEAD_DIM // 2
ROPE_BASE = 10000.0
POOL_WINDOWS = (2, 4, 8, 16)
POOL_WIDTH = 256
POOL_GROUP_DIM = 64
FOURIER_WIDTH = 256
FOURIER_GROUP_DIM = 64
N_BRANCHES = 3
OFF_K = ATTN_WIDTH
OFF_P = ATTN_WIDTH + 2 * KV_WIDTH
OFF_G = OFF_P + POOL_WIDTH + FOURIER_WIDTH
IN_WIDTH = OFF_G + N_BRANCHES * D_MODEL
N_EXPERT_GROUPS = 4
EXPERTS_PER_GROUP = 8
N_EXPERTS = N_EXPERT_GROUPS * EXPERTS_PER_GROUP
TOP_K = 2
EXPERT_FF = 256
N_MOD = 6
RMS_EPS = 1e-6
QK_SCALE = HEAD_DIM ** -0.5 * math.log2(math.e)

N_P = BATCH * SEQ
N_S = DEC_BATCH * DEC_SEQ
N_TOK = N_P + N_S
LANES = 128
MOD_ROWS = 8
ROUTER_ROWS = 40
POOL_PAD = 16

TM = 512
PROJ_TM = 1024
N_TILES = N_TOK // TM
P_TILES = N_P // TM
S_TILES_PER_SEQ = DEC_SEQ // TM
CTX_SEQ_PER_STEP = 4
TQ_S = 256
KEY_CHUNK = 512
TF_S = 512
EBLK = 512
X_SLOTS = 4
W_SLOTS = 3
RUN_ALIGN = 8
RUN_SHIFT = 3
RUN_BITS = 7
assert RUN_ALIGN << (RUN_BITS - 1) == TM
SORT_ROWS = TOP_K * TM + N_EXPERTS * RUN_ALIGN
DISPATCH_SLOTS = 3
assert N_TILES >= DISPATCH_SLOTS
TILE_BITS = 8
assert SORT_ROWS < RUN_ALIGN << TILE_BITS
N_RUNS = N_TILES * N_EXPERTS
N_EBLK = (N_TOK * TOP_K + N_RUNS * (RUN_ALIGN - 1) + N_EXPERTS * (EBLK - 1) + EBLK - 1) // EBLK
XS_ROWS = N_EBLK * EBLK
MOD_NT = 1536


def _dot(a, b):
    return jnp.dot(a, b, preferred_element_type=F32)


def _rms(x):
    return x * lax.rsqrt(jnp.mean(x * x, axis=-1, keepdims=True) + RMS_EPS)


def _mod_row(i):
    return jnp.where(i < P_TILES, 0, 1 + (i - P_TILES) // S_TILES_PER_SEQ)


def _rope_block(i):
    p_tiles = N_P // PROJ_TM
    per_seq = DEC_SEQ // PROJ_TM
    return jnp.where(i < p_tiles, per_seq, (i - p_tiles) % per_seq)


def _mod_kernel(c_ref, w_ref, b_ref, o_ref):
    c = c_ref[...]
    s = (c * jax.nn.sigmoid(c)).astype(BF16)
    o_ref[...] = _dot(s, w_ref[...].astype(BF16)) + b_ref[...]


def _mod_call(c_all, w_ada, b_ada):
    nt = (N_MOD * D_MODEL) // MOD_NT
    return pl.pallas_call(
        _mod_kernel,
        grid=(DEPTH, nt),
        in_specs=[
            pl.BlockSpec((MOD_ROWS, D_MODEL), lambda l, j: (0, 0)),
            pl.BlockSpec((None, D_MODEL, MOD_NT), lambda l, j: (l, 0, j)),
            pl.BlockSpec((None, 1, MOD_NT), lambda l, j: (l, 0, j)),
        ],
        out_specs=pl.BlockSpec((None, MOD_ROWS, MOD_NT), lambda l, j: (l, 0, j)),
        out_shape=jax.ShapeDtypeStruct((DEPTH, MOD_ROWS, N_MOD * D_MODEL), F32),
        name="adaln_mod",
    )(c_all, w_ada, b_ada.reshape(DEPTH, 1, N_MOD * D_MODEL))


def _prenorm(x, mod, gain):
    return _rms(x) * (gain * (1.0 + mod[:, D_MODEL:2 * D_MODEL])) + mod[:, 0:D_MODEL]


def _prenorm_kernel(xp_ref, xs_ref, mod_ref, gpre_ref, x_ref, hb_ref):
    x = jnp.where(pl.program_id(0) < P_TILES, xp_ref[...], xs_ref[...])
    x_ref[...] = x
    hb_ref[...] = _prenorm(x, mod_ref[...], gpre_ref[...]).astype(BF16)


def _prenorm_call(x_prompt, x_sample, mods, gpre):
    row = lambda i: (i, 0)
    return pl.pallas_call(
        _prenorm_kernel,
        grid=(N_TILES,),
        in_specs=[
            pl.BlockSpec((TM, D_MODEL), lambda i: (jnp.minimum(i, P_TILES - 1), 0)),
            pl.BlockSpec((TM, D_MODEL), lambda i: (jnp.maximum(i - P_TILES, 0), 0)),
            pl.BlockSpec((None, None, 1, 2 * D_MODEL), lambda i: (0, _mod_row(i), 0, 0)),
            pl.BlockSpec((None, 1, D_MODEL), lambda i: (0, 0, 0)),
        ],
        out_specs=[pl.BlockSpec((TM, D_MODEL), row), pl.BlockSpec((TM, D_MODEL), row)],
        out_shape=[jax.ShapeDtypeStruct((N_TOK, D_MODEL), F32),
                   jax.ShapeDtypeStruct((N_TOK, D_MODEL), BF16)],
        name="prenorm",
    )(x_prompt, x_sample, mods, gpre)


def _proj_kernel(hb_ref, w_ref, qg_ref, kg_ref, cos_ref, s1_ref, s2_ref, avg_ref, dft_ref,
                 ck_in_ref, cv_in_ref,
                 q_ref, k_ref, v_ref, xp_ref, xc_ref, xs_ref, g_ref, ck_ref, cv_ref):
    del ck_in_ref, cv_in_ref
    hb = hb_ref[...]

    cos = cos_ref[...]
    s1 = s1_ref[...]
    s2 = s2_ref[...]

    def rope(t):
        return (t * cos + pltpu.roll(t, LANES - ROPE_AXIS_DIM // 2, 1) * s1
                + pltpu.roll(t, ROPE_AXIS_DIM // 2, 1) * s2)

    head = _dot(hb, w_ref[:, 0:OFF_G])
    avg = avg_ref[...]
    for c in range(ATTN_WIDTH // LANES):
        lo = c * LANES
        q = head[:, lo:lo + LANES]
        ms = _dot((q * q).astype(BF16), avg)
        q = q * lax.rsqrt(ms + RMS_EPS) * qg_ref[...]
        q_ref[:, lo:lo + LANES] = (rope(q) * QK_SCALE).astype(BF16)

    k = head[:, OFF_K:OFF_K + KV_WIDTH]
    ms = _dot((k * k).astype(BF16), avg)
    k = rope(k * lax.rsqrt(ms + RMS_EPS) * kg_ref[...])
    v = head[:, OFF_K + KV_WIDTH:OFF_P]
    k_ref[...] = k
    v_ref[...] = v

    @pl.when(pl.program_id(0) < N_P // PROJ_TM)
    def _():
        for s in range(PROJ_TM // SEQ):
            ck_ref[s] = k[s * SEQ:(s + 1) * SEQ]
            cv_ref[s] = v[s * SEQ:(s + 1) * SEQ]

    xp_ref[...] = head[:, OFF_P:OFF_P + POOL_WIDTH]
    cs = _dot(head[:, OFF_P + POOL_WIDTH:OFF_G].astype(BF16), dft_ref[...])
    xc_ref[...] = cs[:, 0:FOURIER_WIDTH].astype(BF16)
    xs_ref[...] = cs[:, FOURIER_WIDTH:].astype(BF16)

    for c in range(N_BRANCHES):
        lo = OFF_G + c * D_MODEL
        g_ref[:, c * D_MODEL:(c + 1) * D_MODEL] = _dot(hb, w_ref[:, lo:lo + D_MODEL]).astype(BF16)


def _proj_call(l, hb, w_in, qg, kg, cos_t, s1_t, s2_t, avg, dft, cache_k, cache_v):
    row = lambda i: (i, 0)
    const2 = lambda i: (0, 0)
    per_layer = lambda i: (l, 0, 0)
    rope_spec = pl.BlockSpec((PROJ_TM, LANES), lambda i: (_rope_block(i), 0))
    cache_spec = pl.BlockSpec((PROJ_TM // SEQ, None, SEQ, KV_WIDTH),
                              lambda i: (jnp.minimum(i, N_P // PROJ_TM - 1), l, 0, 0))
    cache_shape = jax.ShapeDtypeStruct((BATCH, DEPTH, SEQ, KV_WIDTH), F32)
    return pl.pallas_call(
        _proj_kernel,
        grid=(N_TOK // PROJ_TM,),
        in_specs=[
            pl.BlockSpec((PROJ_TM, D_MODEL), row),
            pl.BlockSpec((None, D_MODEL, IN_WIDTH), per_layer),
            pl.BlockSpec((None, 1, LANES), per_layer),
            pl.BlockSpec((None, 1, LANES), per_layer),
            rope_spec, rope_spec, rope_spec,
            pl.BlockSpec((LANES, LANES), const2),
            pl.BlockSpec((FOURIER_WIDTH, 2 * FOURIER_WIDTH), const2),
            pl.BlockSpec(memory_space=pl.ANY),
            pl.BlockSpec(memory_space=pl.ANY),
        ],
        out_specs=[
            pl.BlockSpec((PROJ_TM, ATTN_WIDTH), row),
            pl.BlockSpec((PROJ_TM, KV_WIDTH), row),
            pl.BlockSpec((PROJ_TM, KV_WIDTH), row),
            pl.BlockSpec((PROJ_TM, POOL_WIDTH), row),
            pl.BlockSpec((PROJ_TM, FOURIER_WIDTH), row),
            pl.BlockSpec((PROJ_TM, FOURIER_WIDTH), row),
            pl.BlockSpec((PROJ_TM, N_BRANCHES * D_MODEL), row),
            cache_spec, cache_spec,
        ],
        out_shape=[
            jax.ShapeDtypeStruct((N_TOK, ATTN_WIDTH), BF16),
            jax.ShapeDtypeStruct((N_TOK, KV_WIDTH), F32),
            jax.ShapeDtypeStruct((N_TOK, KV_WIDTH), F32),
            jax.ShapeDtypeStruct((N_TOK, POOL_WIDTH), F32),
            jax.ShapeDtypeStruct((N_TOK, FOURIER_WIDTH), BF16),
            jax.ShapeDtypeStruct((N_TOK, FOURIER_WIDTH), BF16),
            jax.ShapeDtypeStruct((N_TOK, N_BRANCHES * D_MODEL), BF16),
            cache_shape, cache_shape,
        ],
        input_output_aliases={9: 7, 10: 8},
        name="proj",
    )(hb, w_in, qg, kg, cos_t, s1_t, s2_t, avg, dft, cache_k, cache_v)


def _attn_kernel(*refs, n_parts, tq):
    q_ref = refs[0]
    kv_refs = refs[1:1 + 2 * n_parts]
    o_ref = refs[-1]
    single_chunk = n_parts == 1 and kv_refs[0].shape[0] <= KEY_CHUNK
    outs = []
    for j in range(N_KV_HEADS):
        lo = j * HEAD_DIM
        qs = jnp.concatenate(
            [q_ref[:, (KV_GROUP * j + g) * HEAD_DIM:(KV_GROUP * j + g + 1) * HEAD_DIM]
             for g in range(KV_GROUP)], axis=0)
        m = acc = None
        for p in range(n_parts):
            k_ref, v_ref = kv_refs[2 * p], kv_refs[2 * p + 1]
            for c0 in range(0, k_ref.shape[0], KEY_CHUNK):
                c1 = min(c0 + KEY_CHUNK, k_ref.shape[0])
                kc = k_ref[c0:c1, lo:lo + HEAD_DIM].astype(BF16)
                s = lax.dot_general(qs, kc, (((1,), (1,)), ((), ())), preferred_element_type=F32)
                mc = jnp.max(s, axis=1, keepdims=True)
                m_new = mc if m is None else jnp.maximum(m, mc)
                e = jnp.exp2(s - m_new)
                vc = v_ref[c0:c1, lo:lo + HEAD_DIM].astype(BF16)
                if single_chunk:
                    den = jnp.sum(e, axis=1, keepdims=True)
                else:
                    vc = jnp.concatenate([vc, jnp.ones((c1 - c0, HEAD_DIM), BF16)], axis=1)
                pv = _dot(e.astype(BF16), vc)
                acc = pv if m is None else acc * jnp.exp2(m - m_new) + pv
                m = m_new
        o = acc / den if single_chunk else acc[:, 0:HEAD_DIM] / acc[:, HEAD_DIM:2 * HEAD_DIM]
        outs.extend(o[g * tq:(g + 1) * tq] for g in range(KV_GROUP))
    o_ref[...] = jnp.concatenate(outs, axis=1).astype(BF16)


def _attn_prompt_call(q, k, v):
    blk = lambda b: (b, 0)
    return pl.pallas_call(
        functools.partial(_attn_kernel, n_parts=1, tq=SEQ),
        grid=(BATCH,),
        in_specs=[
            pl.BlockSpec((SEQ, ATTN_WIDTH), blk),
            pl.BlockSpec((SEQ, KV_WIDTH), blk),
            pl.BlockSpec((SEQ, KV_WIDTH), blk),
        ],
        out_specs=pl.BlockSpec((SEQ, ATTN_WIDTH), blk),
        out_shape=jax.ShapeDtypeStruct((N_P, ATTN_WIDTH), BF16),
        name="attn_context",
    )(q, k, v)


def _attn_sample_call(l, q, k, v, cache_k, cache_v):
    nq = DEC_SEQ // TQ_S
    qrow = lambda b, i: (N_P // TQ_S + b * nq + i, 0)
    seq = lambda b, i: (N_P // DEC_SEQ + b, 0)
    cache = lambda b, i: (b, l, 0, 0)
    return pl.pallas_call(
        functools.partial(_attn_kernel, n_parts=2, tq=TQ_S),
        grid=(DEC_BATCH, nq),
        in_specs=[
            pl.BlockSpec((TQ_S, ATTN_WIDTH), qrow),
            pl.BlockSpec((None, None, PAST_LEN, KV_WIDTH), cache),
            pl.BlockSpec((None, None, PAST_LEN, KV_WIDTH), cache),
            pl.BlockSpec((DEC_SEQ, KV_WIDTH), seq),
            pl.BlockSpec((DEC_SEQ, KV_WIDTH), seq),
        ],
        out_specs=pl.BlockSpec((TQ_S, ATTN_WIDTH), lambda b, i: (b * nq + i, 0)),
        out_shape=jax.ShapeDtypeStruct((N_S, ATTN_WIDTH), BF16),
        name="attn_latent",
    )(q, cache_k, cache_v, k, v)


def _pool_kernel(xp_ref, bdw_ref, sc_ref, o_ref, pad_ref, *, seq_len, n_seq):
    for s in range(n_seq):
        _pool_sequence(xp_ref, bdw_ref, sc_ref, o_ref, pad_ref, s * seq_len, seq_len)


def _pool_sequence(xp_ref, bdw_ref, sc_ref, o_ref, pad_ref, row0, seq_len):
    half = POOL_WIDTH // 2
    zeros = jnp.zeros((POOL_PAD, POOL_WIDTH), F32)
    pad_ref[0:POOL_PAD, :] = zeros
    pad_ref[POOL_PAD + seq_len:, :] = zeros
    pad_ref[POOL_PAD:POOL_PAD + seq_len, :] = xp_ref[row0:row0 + seq_len, :]
    chunk = min(seq_len, 256)
    lane = lax.broadcasted_iota(jnp.int32, (chunk, half), 1)
    first = lane < POOL_GROUP_DIM
    for c in range(seq_len // chunk):
        base = c * chunk
        t = lax.broadcasted_iota(jnp.int32, (chunk, half), 0) + base

        def sh(j, lo):
            return pad_ref[POOL_PAD + base + j:POOL_PAD + base + j + chunk, lo:lo + half]

        def cnt(w):
            return (jnp.minimum(t + w // 2, seq_len) - jnp.maximum(t - w // 2, 0)).astype(F32)

        xa = sh(0, 0)
        w2 = sh(-1, 0) + xa
        w4 = w2 + sh(-2, 0) + sh(1, 0)
        xb = sh(0, half)
        w8 = xb
        for j in (-4, -3, -2, -1, 1, 2, 3):
            w8 = w8 + sh(j, half)
        w16 = w8
        for j in (-8, -7, -6, -5, 4, 5, 6, 7):
            w16 = w16 + sh(j, half)
        pa = jnp.where(first, w2 / cnt(2), w4 / cnt(4)) - xa
        pb = jnp.where(first, w8 / cnt(8), w16 / cnt(16)) - xb
        pooled = jnp.concatenate([pa, pb], axis=1).astype(BF16)
        o_ref[row0 + base:row0 + base + chunk, :] = (
            _dot(pooled, bdw_ref[...]) * sc_ref[...]).astype(BF16)


def _pool_call(l, xp, bdw, scale, seq_len, n_seq, blk0, seq_per_step):
    per_layer = lambda b: (l, 0, 0)
    rows = seq_len * seq_per_step
    return pl.pallas_call(
        functools.partial(_pool_kernel, seq_len=seq_len, n_seq=seq_per_step),
        grid=(n_seq // seq_per_step,),
        in_specs=[
            pl.BlockSpec((rows, POOL_WIDTH), lambda b: (blk0 + b, 0)),
            pl.BlockSpec((None, POOL_WIDTH, POOL_WIDTH), per_layer),
            pl.BlockSpec((None, 1, POOL_WIDTH), per_layer),
        ],
        out_specs=pl.BlockSpec((rows, POOL_WIDTH), lambda b: (b, 0)),
        out_shape=jax.ShapeDtypeStruct((n_seq * seq_len, POOL_WIDTH), BF16),
        scratch_shapes=[pltpu.VMEM((seq_len + 2 * POOL_PAD, POOL_WIDTH), F32)],
        name="pool_%d" % seq_len,
    )(xp, bdw, scale)


def _fourier_kernel(c_ref, s_ref, xc_ref, xs_ref, o_ref, *, scale, n_seq):
    seq_len = xc_ref.shape[0] // n_seq
    for s in range(n_seq):
        rows = slice(s * seq_len, (s + 1) * seq_len)
        y = _dot(c_ref[...], xc_ref[rows, :]) - _dot(s_ref[...], xs_ref[rows, :])
        o_ref[s] = (y * scale).astype(BF16)


def _fourier_prompt_call(cmat, smat, xc, xs):
    blk = lambda b: (b, 0)
    const2 = lambda b: (0, 0)
    rows = SEQ * CTX_SEQ_PER_STEP
    return pl.pallas_call(
        functools.partial(_fourier_kernel, scale=1.0 / math.sqrt(SEQ * FOURIER_GROUP_DIM),
                          n_seq=CTX_SEQ_PER_STEP),
        grid=(BATCH // CTX_SEQ_PER_STEP,),
        in_specs=[
            pl.BlockSpec((SEQ, SEQ), const2),
            pl.BlockSpec((SEQ, SEQ), const2),
            pl.BlockSpec((rows, FOURIER_WIDTH), blk),
            pl.BlockSpec((rows, FOURIER_WIDTH), blk),
        ],
        out_specs=pl.BlockSpec((CTX_SEQ_PER_STEP, SEQ, FOURIER_WIDTH), lambda b: (b, 0, 0)),
        out_shape=jax.ShapeDtypeStruct((BATCH, SEQ, FOURIER_WIDTH), BF16),
        name="fourier_context",
    )(cmat, smat, xc, xs).reshape(N_P, FOURIER_WIDTH)


def _fourier_sample_call(cmat, smat, xc, xs):
    rows = lambda i: (i, 0)
    latent = lambda i: (N_P // N_S, 0)
    return pl.pallas_call(
        functools.partial(_fourier_kernel, scale=1.0 / math.sqrt(DEC_SEQ * FOURIER_GROUP_DIM),
                          n_seq=DEC_BATCH),
        grid=(DEC_SEQ // TF_S,),
        in_specs=[
            pl.BlockSpec((TF_S, DEC_SEQ), rows),
            pl.BlockSpec((TF_S, DEC_SEQ), rows),
            pl.BlockSpec((N_S, FOURIER_WIDTH), latent),
            pl.BlockSpec((N_S, FOURIER_WIDTH), latent),
        ],
        out_specs=pl.BlockSpec((DEC_BATCH, TF_S, FOURIER_WIDTH), lambda i: (0, i, 0)),
        out_shape=jax.ShapeDtypeStruct((DEC_BATCH, DEC_SEQ, FOURIER_WIDTH), BF16),
        name="fourier_latent",
    )(cmat, smat, xc, xs).reshape(N_S, FOURIER_WIDTH)


def _merge_kernel(x_ref, attn_p_ref, attn_s_ref, pool_p_ref, pool_s_ref, four_p_ref, four_s_ref,
                  g_ref, mod_ref, gpost_ref, gffn_ref,
                  wa_ref, wp_ref, wf_ref, wo_ref, wrt_ref, brt_ref, triu_ref, lower_ref,
                  x1_ref, h2_ref, pos_ref, post_ref, mw_ref, cnt_ref):
    i = pl.program_id(0)
    mod = mod_ref[...]
    g1 = mod[:, 2 * D_MODEL:3 * D_MODEL]
    sh2 = mod[:, 3 * D_MODEL:4 * D_MODEL]
    sc2 = mod[:, 4 * D_MODEL:5 * D_MODEL]

    def gate2(c):
        return jnp.tanh(g_ref[:, c * D_MODEL:(c + 1) * D_MODEL].astype(F32)) + 1.0

    def branch(p_ref, s_ref):
        return jnp.where(i < P_TILES, p_ref[...], s_ref[...])

    merged = gate2(0) * _dot(branch(attn_p_ref, attn_s_ref), wa_ref[...])
    merged = merged + gate2(1) * _dot(branch(pool_p_ref, pool_s_ref), wp_ref[...])
    merged = merged + gate2(2) * _dot(branch(four_p_ref, four_s_ref), wf_ref[...])
    mix = _dot((0.5 * merged).astype(BF16), wo_ref[...])
    x1 = x_ref[...] + (g1 * gpost_ref[...]) * _rms(mix)
    x1_ref[...] = x1
    h2 = _rms(x1) * (gffn_ref[...] * (1.0 + sc2)) + sh2
    h2b = h2.astype(BF16)
    h2_ref[...] = h2b

    logits = lax.dot_general(wrt_ref[...], h2b, (((1,), (1,)), ((), ())),
                             preferred_element_type=F32) + brt_ref[...]
    top = logits[0:ROUTER_ROWS, :]
    row = lax.broadcasted_iota(jnp.int32, (ROUTER_ROWS, TM), 0)
    rowf = row.astype(F32)
    neg = jnp.float32(-3e38)
    big = jnp.float32(1e9)
    is_g = (row >= N_EXPERTS) & (row < N_EXPERTS + N_EXPERT_GROUPS)
    lg = jnp.where(is_g, top, neg)
    gmax = jnp.max(lg, axis=0, keepdims=True)
    g_sel = jnp.min(jnp.where(lg == gmax, rowf - N_EXPERTS, big), axis=0, keepdims=True)
    p_g = 1.0 / jnp.sum(jnp.where(is_g, jnp.exp(top - gmax), 0.0), axis=0, keepdims=True)
    grp = lax.shift_right_logical(row, int(math.log2(EXPERTS_PER_GROUP))).astype(F32)
    in_grp = (row < N_EXPERTS) & (grp == g_sel)
    le = jnp.where(in_grp, top, neg)
    v1 = jnp.max(le, axis=0, keepdims=True)
    i1 = jnp.min(jnp.where(le == v1, rowf, big), axis=0, keepdims=True)
    le2 = jnp.where(rowf == i1, neg, le)
    v2 = jnp.max(le2, axis=0, keepdims=True)
    i2 = jnp.min(jnp.where(le2 == v2, rowf, big), axis=0, keepdims=True)
    e21 = jnp.exp(v2 - v1)
    w1 = p_g / (1.0 + e21)
    w2 = p_g * e21 / (1.0 + e21)

    rows_all = lax.broadcasted_iota(jnp.int32, (LANES, TM), 0)
    rows_allf = rows_all.astype(F32)
    oh1 = (rows_allf == i1).astype(F32)
    oh2 = (rows_allf == i2).astype(F32)
    ohb = (oh1 + oh2).astype(BF16)
    before = _dot(ohb, triu_ref[...])
    cnt = _dot(ohb, jnp.ones((TM, LANES), BF16))
    cnt_pad = (lax.shift_right_logical(cnt.astype(jnp.int32) + (RUN_ALIGN - 1), RUN_SHIFT)
               * RUN_ALIGN).astype(F32)
    run_off = _dot(lower_ref[...], cnt_pad.astype(BF16))
    slot = jnp.concatenate([run_off] * (TM // LANES), axis=1) + before
    p1 = jnp.sum(slot * oh1, axis=0, keepdims=True)
    p2 = jnp.sum(slot * oh2, axis=0, keepdims=True)
    cnt_ref[...] = cnt

    meta = jnp.where(rows_all == 0, p1, jnp.where(rows_all == 1, p2,
                     jnp.where(rows_all == 2, w1, jnp.where(rows_all == 3, w2, 0.0))))
    post_ref[...] = meta[0:MOD_ROWS, :].astype(jnp.int32)
    meta_t = meta.T
    pos_ref[...] = meta_t.astype(jnp.int32)
    mw_ref[...] = meta_t


def _merge_call(l, x, attn, pool, four, gates, mods, gpost, gffn, wa, wp, wf, wo, wrt, brt, triu,
                lower):
    row = lambda i: (i, 0)
    prow = lambda i: (jnp.minimum(i, P_TILES - 1), 0)
    srow = lambda i: (jnp.maximum(i - P_TILES, 0), 0)
    const2 = lambda i: (0, 0)
    per_layer = lambda i: (l, 0, 0)
    return pl.pallas_call(
        _merge_kernel,
        grid=(N_TILES,),
        in_specs=[
            pl.BlockSpec((TM, D_MODEL), row),
            pl.BlockSpec((TM, ATTN_WIDTH), prow),
            pl.BlockSpec((TM, ATTN_WIDTH), srow),
            pl.BlockSpec((TM, POOL_WIDTH), prow),
            pl.BlockSpec((TM, POOL_WIDTH), srow),
            pl.BlockSpec((TM, FOURIER_WIDTH), prow),
            pl.BlockSpec((TM, FOURIER_WIDTH), srow),
            pl.BlockSpec((TM, N_BRANCHES * D_MODEL), row),
            pl.BlockSpec((None, None, 1, N_MOD * D_MODEL), lambda i: (l, _mod_row(i), 0, 0)),
            pl.BlockSpec((None, 1, D_MODEL), per_layer),
            pl.BlockSpec((None, 1, D_MODEL), per_layer),
            pl.BlockSpec((None, ATTN_WIDTH, D_MODEL), per_layer),
            pl.BlockSpec((None, POOL_WIDTH, D_MODEL), per_layer),
            pl.BlockSpec((None, FOURIER_WIDTH, D_MODEL), per_layer),
            pl.BlockSpec((None, D_MODEL, D_MODEL), per_layer),
            pl.BlockSpec((None, LANES, D_MODEL), per_layer),
            pl.BlockSpec((None, LANES, TM), per_layer),
            pl.BlockSpec((TM, TM), const2),
            pl.BlockSpec((LANES, LANES), const2),
        ],
        out_specs=[
            pl.BlockSpec((TM, D_MODEL), row),
            pl.BlockSpec((TM, D_MODEL), row),
            pl.BlockSpec((TM, LANES), row),
            pl.BlockSpec((MOD_ROWS, TM), lambda i: (0, i)),
            pl.BlockSpec((TM, LANES), row),
            pl.BlockSpec((LANES, LANES), row),
        ],
        out_shape=[
            jax.ShapeDtypeStruct((N_TOK, D_MODEL), F32),
            jax.ShapeDtypeStruct((N_TOK, D_MODEL), BF16),
            jax.ShapeDtypeStruct((N_TOK, LANES), jnp.int32),
            jax.ShapeDtypeStruct((MOD_ROWS, N_TOK), jnp.int32),
            jax.ShapeDtypeStruct((N_TOK, LANES), F32),
            jax.ShapeDtypeStruct((N_TILES * LANES, LANES), F32),
        ],
        name="merge_router",
    )(x, attn[0], attn[1], pool[0], pool[1], four[0], four[1], gates, mods, gpost, gffn,
      wa, wp, wf, wo, wrt, brt, triu, lower)


HALF = D_MODEL // 2
U32 = jnp.uint32
HI_MASK = 0xFFFF0000


def _pack_rows(x):
    lo = lax.bitcast_convert_type(x[:, :HALF], U32)
    hi = lax.bitcast_convert_type(x[:, HALF:], U32)
    return lax.shift_right_logical(lo, U32(16)) | (hi & U32(HI_MASK))


def _unpack_rows(u):
    lo = lax.bitcast_convert_type(lax.shift_left(u, U32(16)), F32)
    hi = lax.bitcast_convert_type(u & U32(HI_MASK), F32)
    return jnp.concatenate([lo.astype(BF16), hi.astype(BF16)], axis=1)


def _rows(ref, start, size):
    return ref.at[pl.ds(pl.multiple_of(start, RUN_ALIGN), size)]


def _start_tile_copies(tile, n_ref, trow_ref, xrow_ref, make_copy):
    for bit in range(RUN_BITS):
        lst = bit * N_TILES + tile

        def body(k, carry, bit=bit, lst=lst):
            j = lst * N_EXPERTS + k
            make_copy(trow_ref[j], xrow_ref[j], RUN_ALIGN << bit).start()
            return carry

        lax.fori_loop(0, n_ref[lst], body, 0)


def _wait_tile_copies(total, src_ref, dst_ref, sem):
    for b in range(TILE_BITS):
        size = RUN_ALIGN << b

        @pl.when((lax.shift_right_logical(total, b) & 1) == 1)
        def _(size=size):
            pltpu.make_async_copy(_rows(src_ref, 0, size), _rows(dst_ref, 0, size), sem).wait()


def _dispatch_kernel(n_ref, trow_ref, xrow_ref, tsum_ref, h_ref, post_ref, xs_in_ref, xs_ref,
                     sorted_ref, sem):
    del xs_in_ref
    i = pl.program_id(0)
    slot = lax.rem(i, DISPATCH_SLOTS)
    rows = lax.broadcasted_iota(jnp.int32, (SORT_ROWS, TM), 0)
    p = post_ref[...]
    perm = jnp.where(rows == p[0:1, :], 1.0, jnp.where(rows == p[1:2, :], 1.0, 0.0)).astype(BF16)
    sorted_ref[slot] = _pack_rows(_dot(perm, h_ref[...]))

    def wait(tile, slot):
        _wait_tile_copies(tsum_ref[tile], sorted_ref.at[slot], xs_ref, sem.at[slot])

    lag = DISPATCH_SLOTS - 1

    @pl.when(i >= lag)
    def _():
        wait(i - lag, lax.rem(i + 1, DISPATCH_SLOTS))

    _start_tile_copies(i, n_ref, trow_ref, xrow_ref, lambda t, x, size: pltpu.make_async_copy(
        _rows(sorted_ref.at[slot], t, size), _rows(xs_ref, x, size), sem.at[slot]))

    @pl.when(i == pl.num_programs(0) - 1)
    def _():
        for back in reversed(range(lag)):
            wait(i - back, lax.rem(i - back, DISPATCH_SLOTS))


def _dispatch_call(n_list, trow, xrow, tsum, h2, post, xs):
    return pl.pallas_call(
        _dispatch_kernel,
        grid_spec=pltpu.PrefetchScalarGridSpec(
            num_scalar_prefetch=4,
            grid=(N_TILES,),
            in_specs=[
                pl.BlockSpec((TM, D_MODEL), lambda i, *_: (i, 0)),
                pl.BlockSpec((MOD_ROWS, TM), lambda i, *_: (0, i)),
                pl.BlockSpec(memory_space=pl.ANY),
            ],
            out_specs=pl.BlockSpec(memory_space=pl.ANY),
            scratch_shapes=[pltpu.VMEM((DISPATCH_SLOTS, SORT_ROWS, HALF), U32),
                            pltpu.SemaphoreType.DMA((DISPATCH_SLOTS,))],
        ),
        out_shape=jax.ShapeDtypeStruct((XS_ROWS, HALF), U32),
        input_output_aliases={6: 0},
        compiler_params=pltpu.CompilerParams(dimension_semantics=("arbitrary",)),
        name="moe_dispatch",
    )(n_list, trow, xrow, tsum, h2, post, xs)


def _expert_kernel(bstart_ref, bcnt_ref, nu_ref, wg_ref, wu_ref, wd_ref, xs_ref, y_prev_ref, y_ref,
                   wgub, wdb, wgf, wuf, wdf, xbuf, ybuf, xsem, ysem, wsem, *, layer):
    del y_prev_ref
    e = pl.program_id(0)
    n_used = nu_ref[0]

    def w_copies(ex, slot):
        return [pltpu.make_async_copy(src.at[layer, ex], dst.at[slot], wsem.at[slot])
                for src, dst in ((wg_ref, wgf), (wu_ref, wuf), (wd_ref, wdf))]

    @pl.when(e == 0)
    def _():
        for ex in range(W_SLOTS - 1):
            for cp in w_copies(ex, ex):
                cp.start()

    @pl.when(e + (W_SLOTS - 1) < N_EXPERTS)
    def _():
        for cp in w_copies(e + (W_SLOTS - 1), lax.rem(e + (W_SLOTS - 1), W_SLOTS)):
            cp.start()

    wslot = lax.rem(e, W_SLOTS)
    for cp in w_copies(e, wslot):
        cp.wait()

    def x_copy(g, slot):
        rows = pl.ds(pl.multiple_of(g * EBLK, EBLK), EBLK)
        return pltpu.make_async_copy(xs_ref.at[rows], xbuf.at[slot], xsem.at[slot])

    def y_copy(g, slot):
        rows = pl.ds(pl.multiple_of(g * EBLK, EBLK), EBLK)
        return pltpu.make_async_copy(ybuf.at[slot], y_ref.at[rows], ysem.at[slot])

    @pl.when(e == 0)
    def _():
        for g in range(X_SLOTS - 1):
            @pl.when(g < n_used)
            def _(g=g):
                x_copy(g, g).start()

    wgub[:, 0:EXPERT_FF] = wgf[wslot].astype(BF16)
    wgub[:, EXPERT_FF:] = wuf[wslot].astype(BF16)
    wdb[...] = wdf[wslot].astype(BF16)
    first = bstart_ref[e]

    def block(j, carry):
        g = first + j
        slot = lax.rem(g, 2)
        xslot = lax.rem(g, X_SLOTS)
        x_copy(g, xslot).wait()

        @pl.when(g + (X_SLOTS - 1) < n_used)
        def _():
            x_copy(g + (X_SLOTS - 1), lax.rem(g + (X_SLOTS - 1), X_SLOTS)).start()

        xb = _unpack_rows(xbuf[xslot])
        gu = _dot(xb, wgub[...])
        gate = gu[:, 0:EXPERT_FF]
        act = (gate * jax.nn.sigmoid(gate)) * gu[:, EXPERT_FF:]
        y = _dot(act.astype(BF16), wdb[...])

        @pl.when(g >= 2)
        def _():
            y_copy(g - 2, slot).wait()

        ybuf[slot] = _pack_rows(y.astype(BF16).astype(F32))
        y_copy(g, slot).start()
        return carry

    lax.fori_loop(0, bcnt_ref[e], block, 0)

    @pl.when(e == pl.num_programs(0) - 1)
    def _():
        @pl.when(n_used >= 2)
        def _():
            y_copy(n_used - 2, lax.rem(n_used, 2)).wait()

        y_copy(n_used - 1, lax.rem(n_used - 1, 2)).wait()


def _expert_call(l, blk_start, blk_cnt, n_used, xs, y_prev, w_gate, w_up, w_down):
    return pl.pallas_call(
        functools.partial(_expert_kernel, layer=l),
        grid_spec=pltpu.PrefetchScalarGridSpec(
            num_scalar_prefetch=3,
            grid=(N_EXPERTS,),
            in_specs=[
                pl.BlockSpec(memory_space=pl.ANY),
                pl.BlockSpec(memory_space=pl.ANY),
                pl.BlockSpec(memory_space=pl.ANY),
                pl.BlockSpec(memory_space=pl.ANY),
                pl.BlockSpec(memory_space=pl.ANY),
            ],
            out_specs=pl.BlockSpec(memory_space=pl.ANY),
            scratch_shapes=[
                pltpu.VMEM((D_MODEL, 2 * EXPERT_FF), BF16),
                pltpu.VMEM((EXPERT_FF, D_MODEL), BF16),
                pltpu.VMEM((W_SLOTS, D_MODEL, EXPERT_FF), F32),
                pltpu.VMEM((W_SLOTS, D_MODEL, EXPERT_FF), F32),
                pltpu.VMEM((W_SLOTS, EXPERT_FF, D_MODEL), F32),
                pltpu.VMEM((X_SLOTS, EBLK, HALF), U32),
                pltpu.VMEM((2, EBLK, HALF), U32),
                pltpu.SemaphoreType.DMA((X_SLOTS,)),
                pltpu.SemaphoreType.DMA((2,)),
                pltpu.SemaphoreType.DMA((W_SLOTS,)),
            ],
        ),
        out_shape=jax.ShapeDtypeStruct((XS_ROWS, HALF), U32),
        input_output_aliases={7: 0},
        compiler_params=pltpu.CompilerParams(dimension_semantics=("arbitrary",)),
        name="moe_experts",
    )(blk_start, blk_cnt, n_used, w_gate, w_up, w_down, xs, y_prev)


def _combine_kernel(n_ref, trow_ref, xrow_ref, tsum_ref, y_ref, x1_ref, pos_ref, mw_ref, g2_ref,
                    gpost_ref, *rest, has_next):
    if has_next:
        nmod_ref, ngpre_ref, o_ref, hb_ref, ybuf, sem = rest
    else:
        o_ref, os_ref, ybuf, sem = rest
    i = pl.program_id(0)
    slot = lax.rem(i, 2)

    def start(tile, slot):
        _start_tile_copies(tile, n_ref, trow_ref, xrow_ref, lambda t, x, size: pltpu.make_async_copy(
            _rows(y_ref, x, size), _rows(ybuf.at[slot], t, size), sem.at[slot]))

    @pl.when(i == 0)
    def _():
        ybuf[...] = jnp.zeros_like(ybuf)
        start(0, 0)

    @pl.when(i + 1 < pl.num_programs(0))
    def _():
        start(i + 1, 1 - slot)

    _wait_tile_copies(tsum_ref[i], y_ref, ybuf.at[slot], sem.at[slot])

    pos = pos_ref[...]
    mw = mw_ref[...]
    cols = lax.broadcasted_iota(jnp.int32, (TM, SORT_ROWS), 1)
    qw = jnp.where(cols == pos[:, 0:1], mw[:, 2:3],
                   jnp.where(cols == pos[:, 1:2], mw[:, 3:4], 0.0)).astype(BF16)
    ffn = _dot(qw, _unpack_rows(ybuf[slot]))
    x2 = x1_ref[...] + (g2_ref[...] * gpost_ref[...]) * _rms(ffn)
    if has_next:
        o_ref[...] = x2
        hb_ref[...] = _prenorm(x2, nmod_ref[...], ngpre_ref[...]).astype(BF16)
    else:
        @pl.when(i < P_TILES)
        def _():
            o_ref[...] = x2

        os_ref[...] = x2


def _combine_call(l, n_list, trow, xrow, tsum, y, x1, pos, mw, mods, gpost, gpre):
    has_next = l + 1 < DEPTH
    row = lambda i, *_: (i, 0)
    in_specs = [
        pl.BlockSpec(memory_space=pl.ANY),
        pl.BlockSpec((TM, D_MODEL), row),
        pl.BlockSpec((TM, LANES), row),
        pl.BlockSpec((TM, LANES), row),
        pl.BlockSpec((None, None, 1, D_MODEL), lambda i, *_: (l, _mod_row(i), 0, N_MOD - 1)),
        pl.BlockSpec((None, 1, D_MODEL), lambda i, *_: (l, 0, 0)),
    ]
    args = [n_list, trow, xrow, tsum, y, x1, pos, mw, mods, gpost]
    if has_next:
        in_specs += [
            pl.BlockSpec((None, None, 1, 2 * D_MODEL), lambda i, *_: (l + 1, _mod_row(i), 0, 0)),
            pl.BlockSpec((None, 1, D_MODEL), lambda i, *_: (l + 1, 0, 0)),
        ]
        args += [mods, gpre]
        out_specs = [pl.BlockSpec((TM, D_MODEL), row), pl.BlockSpec((TM, D_MODEL), row)]
        out_shape = [jax.ShapeDtypeStruct((N_TOK, D_MODEL), F32),
                     jax.ShapeDtypeStruct((N_TOK, D_MODEL), BF16)]
    else:
        out_specs = [
            pl.BlockSpec((TM, D_MODEL), lambda i, *_: (jnp.minimum(i, P_TILES - 1), 0)),
            pl.BlockSpec((TM, D_MODEL), lambda i, *_: (jnp.maximum(i - P_TILES, 0), 0)),
        ]
        out_shape = [jax.ShapeDtypeStruct((N_P, D_MODEL), F32),
                     jax.ShapeDtypeStruct((N_S, D_MODEL), F32)]
    return pl.pallas_call(
        functools.partial(_combine_kernel, has_next=has_next),
        grid_spec=pltpu.PrefetchScalarGridSpec(
            num_scalar_prefetch=4,
            grid=(N_TILES,),
            in_specs=in_specs,
            out_specs=out_specs,
            scratch_shapes=[pltpu.VMEM((2, SORT_ROWS, HALF), U32), pltpu.SemaphoreType.DMA((2,))],
        ),
        out_shape=out_shape,
        compiler_params=pltpu.CompilerParams(dimension_semantics=("arbitrary",)),
        name="moe_combine",
    )(*args)


def _dft_mats(n):
    k = np.arange(n, dtype=np.int64)
    ang = 2.0 * np.pi * ((k[:, None] * k[None, :]) % n).astype(np.float64) / n
    return np.cos(ang), np.sin(ang)


def _block_diag(m, reps):
    n = m.shape[0]
    out = np.zeros((n * reps, n * reps), m.dtype)
    for r in range(reps):
        out[r * n:(r + 1) * n, r * n:(r + 1) * n] = m
    return out


def _rope_tables():
    t = np.arange(DEC_SEQ)
    pos = np.stack([t // GRID_W, t % GRID_W], axis=1).astype(np.float64)
    n_freq = ROPE_AXIS_DIM // 2
    inv = ROPE_BASE ** (-np.arange(n_freq, dtype=np.float64) * 2.0 / ROPE_AXIS_DIM)
    ang = pos[:, :, None] * inv[None, None, :]
    cos = np.cos(ang)
    sin = np.sin(ang)
    zero = np.zeros_like(sin[:, 0])
    cos_h = np.concatenate([cos[:, 0], cos[:, 0], cos[:, 1], cos[:, 1]], axis=1)
    s1_h = np.concatenate([-sin[:, 0], zero, -sin[:, 1], zero], axis=1)
    s2_h = np.concatenate([zero, sin[:, 0], zero, sin[:, 1]], axis=1)
    reps = LANES // HEAD_DIM

    def table(a, ident):
        a = np.tile(a, (1, reps))
        pad = np.full((PROJ_TM, LANES), ident, np.float64)
        return jnp.asarray(np.concatenate([a, pad], axis=0), F32)

    return table(cos_h, 1.0), table(s1_h, 0.0), table(s2_h, 0.0)


def _copy_lists(run_cnt, tile_off, xs_off):
    bit = jnp.arange(RUN_BITS, dtype=jnp.int32)[:, None, None]
    has = (run_cnt[None] >> bit) & 1
    before = ((run_cnt[None] >> (bit + 1)) << (bit + 1)) * RUN_ALIGN
    slot = jnp.cumsum(has, axis=-1) - has
    hit = (has[..., None] == 1) & (slot[..., None] == jnp.arange(N_EXPERTS, dtype=jnp.int32))

    def compact(rows):
        return jnp.sum(jnp.where(hit, rows[..., None], 0), axis=-2).reshape(-1)

    return (jnp.sum(has, axis=-1).reshape(-1), compact(tile_off[None] + before),
            compact(xs_off[None] + before))


def kernel(x_prompt, x_sample, cache_k, cache_v, c, c_ctx, w_ada, b_ada, norm_mix_pre,
           norm_mix_post, norm_ffn_pre, norm_ffn_post, w_in, q_norm, k_norm, w_attn_out,
           w_pool_group, pool_scale, w_pool_out, w_fourier_out, w_out, w_router_group,
           b_router_group, w_router_expert, b_router_expert, w_expert_gate, w_expert_up,
           w_expert_down):
    cos_t, s1_t, s2_t = _rope_tables()
    avg = jnp.asarray(_block_diag(np.full((HEAD_DIM, HEAD_DIM), 1.0 / HEAD_DIM), LANES // HEAD_DIM), BF16)
    c64, s64 = _dft_mats(FOURIER_GROUP_DIM)
    n_fg = FOURIER_WIDTH // FOURIER_GROUP_DIM
    dft_ch = jnp.asarray(np.concatenate([_block_diag(c64, n_fg), _block_diag(s64, n_fg)], axis=1), BF16)
    cp, sp = _dft_mats(SEQ)
    cp, sp = jnp.asarray(cp, BF16), jnp.asarray(sp, BF16)
    cl, sl = _dft_mats(DEC_SEQ)
    cl, sl = jnp.asarray(cl, BF16), jnp.asarray(sl, BF16)
    triu = jnp.asarray(np.triu(np.ones((TM, TM)), 1), BF16)
    lower = jnp.asarray(np.tril(np.ones((LANES, LANES)), -1), BF16)

    in_scale = np.ones((IN_WIDTH,), np.float32)
    in_scale[OFF_G:] = 0.5
    w_in_b = (w_in * in_scale).astype(BF16)
    wa_b = w_attn_out.astype(BF16)
    wp_b = w_pool_out.astype(BF16)
    wf_b = w_fourier_out.astype(BF16)
    wo_b = w_out.astype(BF16)
    pad_r = jnp.zeros((DEPTH, D_MODEL, LANES - N_EXPERTS - N_EXPERT_GROUPS), F32)
    wrt = jnp.concatenate([w_router_expert, w_router_group, pad_r], axis=2).astype(BF16)
    wrt = jnp.transpose(wrt, (0, 2, 1))
    br = jnp.concatenate([b_router_expert, b_router_group,
                          jnp.zeros((DEPTH, LANES - N_EXPERTS - N_EXPERT_GROUPS), F32)], axis=1)
    brt = jnp.broadcast_to(br[:, :, None], (DEPTH, LANES, TM))
    n_pg = POOL_WIDTH // POOL_GROUP_DIM
    bdw = jnp.zeros((DEPTH, POOL_WIDTH, POOL_WIDTH), F32)
    for g in range(n_pg):
        lo = g * POOL_GROUP_DIM
        bdw = bdw.at[:, lo:lo + POOL_GROUP_DIM, lo:lo + POOL_GROUP_DIM].set(w_pool_group[:, g])
    bdw = bdw.astype(BF16)
    pscale = pool_scale.reshape(DEPTH, 1, POOL_WIDTH)
    qg = jnp.tile(q_norm, (1, LANES // HEAD_DIM)).reshape(DEPTH, 1, LANES)
    kg = jnp.tile(k_norm, (1, LANES // HEAD_DIM)).reshape(DEPTH, 1, LANES)
    gpre = norm_mix_pre.reshape(DEPTH, 1, D_MODEL)
    gpost = norm_mix_post.reshape(DEPTH, 1, D_MODEL)
    gffn = norm_ffn_pre.reshape(DEPTH, 1, D_MODEL)
    gfpost = norm_ffn_post.reshape(DEPTH, 1, D_MODEL)
    ck = cache_k.reshape(DEC_BATCH, DEPTH, PAST_LEN, KV_WIDTH)
    cv = cache_v.reshape(DEC_BATCH, DEPTH, PAST_LEN, KV_WIDTH)

    c_all = jnp.concatenate([c_ctx[None, :], c, jnp.zeros((MOD_ROWS - 1 - DEC_BATCH, D_MODEL), F32)], axis=0)
    mods = _mod_call(c_all, w_ada, b_ada).reshape(DEPTH, MOD_ROWS, 1, N_MOD * D_MODEL)

    xs_buf = jnp.zeros((XS_ROWS, HALF), U32)
    y = jnp.zeros((XS_ROWS, HALF), U32)
    new_k = jnp.zeros((BATCH, DEPTH, SEQ, KV_WIDTH), F32)
    new_v = jnp.zeros((BATCH, DEPTH, SEQ, KV_WIDTH), F32)
    x, hb = _prenorm_call(x_prompt.reshape(N_P, D_MODEL), x_sample.reshape(N_S, D_MODEL), mods, gpre)
    for l in range(DEPTH):
        q, k, v, xp, xc, xsn, gates, new_k, new_v = _proj_call(
            l, hb, w_in_b, qg, kg, cos_t, s1_t, s2_t, avg, dft_ch, new_k, new_v)
        attn = (_attn_prompt_call(q, k, v), _attn_sample_call(l, q, k, v, ck, cv))
        pool = (_pool_call(l, xp, bdw, pscale, SEQ, BATCH, 0, CTX_SEQ_PER_STEP),
                _pool_call(l, xp, bdw, pscale, DEC_SEQ, DEC_BATCH, N_P // DEC_SEQ, 1))
        four = (_fourier_prompt_call(cp, sp, xc, xsn), _fourier_sample_call(cl, sl, xc, xsn))
        x1, h2, pos, post, mw, cnt = _merge_call(l, x, attn, pool, four, gates, mods, gpost, gffn,
                                                 wa_b, wp_b, wf_b, wo_b, wrt, brt, triu, lower)
        runs = cnt.reshape(N_TILES, LANES, LANES)[:, :N_EXPERTS, 0].astype(jnp.int32)
        runs = ((runs + RUN_ALIGN - 1) // RUN_ALIGN) * RUN_ALIGN
        tile_off = jnp.cumsum(runs, axis=1) - runs
        rows_e = jnp.sum(runs, axis=0)
        padded = ((rows_e + EBLK - 1) // EBLK) * EBLK
        pad_end = jnp.cumsum(padded)
        xs_off = (pad_end - padded)[None, :] + jnp.cumsum(runs, axis=0) - runs
        blk_cnt = padded // EBLK
        blk_start = (pad_end - padded) // EBLK
        n_used = pad_end[-1:] // EBLK
        n_list, trow, xrow = _copy_lists(runs // RUN_ALIGN, tile_off, xs_off)
        tile_cnt = jnp.sum(runs, axis=1) // RUN_ALIGN
        xs_buf = _dispatch_call(n_list, trow, xrow, tile_cnt, h2, post, xs_buf)
        y = _expert_call(l, blk_start, blk_cnt, n_used, xs_buf, y,
                         w_expert_gate, w_expert_up, w_expert_down)
        outs = _combine_call(l, n_list, trow, xrow, tile_cnt, y, x1, pos, mw, mods, gfpost, gpre)
        x, hb = outs

    y_prompt = outs[0].reshape(BATCH, SEQ, D_MODEL)
    y_sample = outs[1].reshape(DEC_BATCH, DEC_SEQ, D_MODEL)
    cache_shape = (BATCH, DEPTH, SEQ, N_KV_HEADS, HEAD_DIM)
    return (y_prompt, y_sample, new_k.reshape(cache_shape), new_v.reshape(cache_shape))
```
